```python
import math
import jax, jax.numpy as jnp
from jax import lax
import numpy as np


D_MODEL = 1024
BATCH = 16
SEQ = 2048
DEPTH = 1

GRID_W = 64
CTX_LEN = 256

GDN_DK = 128
GDN_DV = 128
GDN_HEADS = (D_MODEL // 2) // GDN_DV
GDN_QK = GDN_HEADS * GDN_DK
GDN_WIDTH = GDN_HEADS * GDN_DV
CONV_W = 5
RET_DK = 128
RET_DV = 128
RET_HEADS = (D_MODEL // 2) // RET_DV
RET_QK = RET_HEADS * RET_DK
RET_WIDTH = RET_HEADS * RET_DV
MIX_WIDTH = GDN_WIDTH + RET_WIDTH
CHUNK = 64
ROPE_THETA = 10000.0
IN_SIZES = (GDN_QK, GDN_QK, GDN_WIDTH, GDN_WIDTH, 2 * GDN_HEADS, 2 * GDN_HEADS,
            RET_QK, RET_QK, RET_WIDTH, RET_WIDTH)
IN_WIDTH = sum(IN_SIZES)
N_EXPERTS = 64
TOP_K = 8
N_GROUPS = 8
TOPK_GROUPS = 4
EXPERT_FF = 256
SHARED_FF = 256
ROUTED_SCALE = 2.5
DEEPNORM_ALPHA = (2 * DEPTH) ** 0.25
DEEPNORM_BETA = (8 * DEPTH) ** -0.25
LN_EPS = 1e-6

kernel_name = 'hybrid_gdn_retention_moe_deepnorm_dit'


def _layernorm(x, g=None, b=None):
    xf = x.astype(jnp.float32)
    mu = xf.mean(-1, keepdims=True)
    var = jnp.square(xf - mu).mean(-1, keepdims=True)
    y = (xf - mu) * lax.rsqrt(var + LN_EPS)
    if g is not None:
        y = y * g.astype(jnp.float32) + b.astype(jnp.float32)
    return y.astype(x.dtype)


def _rmsnorm(x, g):
    xf = x.astype(jnp.float32)
    y = xf * lax.rsqrt(jnp.mean(xf * xf, -1, keepdims=True) + LN_EPS) * g.astype(jnp.float32)
    return y.astype(x.dtype)


def _l2norm(u):
    uf = u.astype(jnp.float32)
    return (uf * lax.rsqrt(jnp.sum(uf * uf, -1, keepdims=True) + LN_EPS)).astype(u.dtype)


def _heads(u, h):
    B, L, _ = u.shape
    return u.reshape(B, L, h, -1).transpose(0, 2, 1, 3)


def _merge(u):
    B, h, L, d = u.shape
    return u.transpose(0, 2, 1, 3).reshape(B, L, h * d)


def _split_cols(p):
    offs = np.cumsum(IN_SIZES)[:-1].tolist()
    return jnp.split(p, offs, axis=-1)


def _modulate(x, shift, scale):
    return _layernorm(x) * (1 + scale) + shift


def _short_conv(u, w):
    C = u.shape[-1]
    pad = CONV_W // 2
    return lax.conv_general_dilated(u, w[:, None, :].astype(u.dtype), window_strides=(1,),
                                    padding=[(pad, pad)], dimension_numbers=('NWC', 'WIO', 'NWC'),
                                    feature_group_count=C)


def _axial_rope(rows, dim):
    row = jnp.repeat(jnp.arange(rows, dtype=jnp.float32), GRID_W)
    col = jnp.tile(jnp.arange(GRID_W, dtype=jnp.float32), rows)
    quarter = dim // 4
    inv = ROPE_THETA ** (-jnp.arange(quarter, dtype=jnp.float32) / quarter)
    ang = jnp.concatenate([row[:, None] * inv, col[:, None] * inv], -1)
    return jnp.cos(ang), jnp.sin(ang)


def _apply_rope(u, cos, sin):
    half = u.shape[-1] // 2
    u1, u2 = u[..., :half], u[..., half:]
    return jnp.concatenate([u1 * cos - u2 * sin, u1 * sin + u2 * cos], -1).astype(u.dtype)


def _gated_delta_scan(q, k, v, beta, g, s0, with_output):
    f32 = jnp.float32
    B, H, L, dk = k.shape
    dv = v.shape[-1]
    C = CHUNK
    N = L // C
    kc = k.reshape(B, H, N, C, dk).astype(f32)
    vc = v.reshape(B, H, N, C, dv).astype(f32)
    bc = beta.reshape(B, H, N, C).astype(f32)
    gcum = jnp.cumsum(g.reshape(B, H, N, C).astype(f32), axis=-1)
    causal = jnp.tril(jnp.ones((C, C), bool))
    strict = jnp.tril(jnp.ones((C, C), bool), -1)
    decay = jnp.exp(jnp.where(causal, gcum[..., :, None] - gcum[..., None, :], -jnp.inf))
    kk = jnp.einsum('bhncd,bhnmd->bhncm', kc, kc)
    lower = jnp.where(strict, kk * decay, 0.0) * bc[..., :, None]
    rhs = jnp.concatenate([vc * bc[..., None], kc * (bc * jnp.exp(gcum))[..., None]], -1)
    sol = lax.linalg.triangular_solve(lower + jnp.eye(C, dtype=f32), rhs, left_side=True,
                                      lower=True, unit_diagonal=True)
    u0, w = sol[..., :dv], sol[..., dv:]
    k_end = kc * jnp.exp(gcum[..., -1:] - gcum)[..., None]
    g_end = jnp.exp(gcum[..., -1])
    mv = lambda a: jnp.moveaxis(a, 2, 0)
    if with_output:
        qc = q.reshape(B, H, N, C, dk).astype(f32)
        intra = jnp.einsum('bhncd,bhnmd->bhncm', qc, kc) * decay
        q_dec = qc * jnp.exp(gcum)[..., None]

        def step(s, xs):
            u0_n, w_n, k_n, ge_n, qd_n, a_n = xs
            u = u0_n - jnp.einsum('bhcd,bhde->bhce', w_n, s)
            o = jnp.einsum('bhcd,bhde->bhce', qd_n, s) + jnp.einsum('bhcm,bhme->bhce', a_n, u)
            s = ge_n[..., None, None] * s + jnp.einsum('bhcd,bhce->bhde', k_n, u)
            return s, o

        s_fin, o = lax.scan(step, s0, (mv(u0), mv(w), mv(k_end), mv(g_end), mv(q_dec), mv(intra)))
        return jnp.moveaxis(o, 0, 2).reshape(B, H, L, dv).astype(v.dtype), s_fin

    def step_state(s, xs):
        u0_n, w_n, k_n, ge_n = xs
        u = u0_n - jnp.einsum('bhcd,bhde->bhce', w_n, s)
        return ge_n[..., None, None] * s + jnp.einsum('bhcd,bhce->bhde', k_n, u), None

    s_fin, _ = lax.scan(step_state, s0, (mv(u0), mv(w), mv(k_end), mv(g_end)))
    return None, s_fin


def _retention_scan(q, k, v, log_gamma, r0, with_output):
    f32 = jnp.float32
    B, H, L, dk = k.shape
    dv = v.shape[-1]
    C = CHUNK
    N = L // C
    lg = log_gamma.astype(f32)[:, None]
    pos = jnp.arange(C, dtype=f32)
    kc = k.reshape(B, H, N, C, dk).astype(f32)
    vc = v.reshape(B, H, N, C, dv).astype(f32)
    k_dec = kc * jnp.exp(lg * (C - 1 - pos))[None, :, None, :, None]
    chunk_decay = jnp.exp(lg[:, 0] * C)[None, :, None, None]
    mv = lambda a: jnp.moveaxis(a, 2, 0)
    if with_output:
        qc = q.reshape(B, H, N, C, dk).astype(f32)
        causal = jnp.tril(jnp.ones((C, C), bool))
        dmat = jnp.exp(jnp.where(causal, lg[:, :, None] * (pos[:, None] - pos[None, :]), -jnp.inf))
        intra = jnp.einsum('bhncd,bhnmd->bhncm', qc, kc) * dmat[None, :, None]
        q_dec = qc * jnp.exp(lg * (pos + 1))[None, :, None, :, None]

        def step(r, xs):
            kd_n, v_n, qd_n, a_n = xs
            o = jnp.einsum('bhcd,bhde->bhce', qd_n, r) + jnp.einsum('bhcm,bhme->bhce', a_n, v_n)
            return chunk_decay * r + jnp.einsum('bhcd,bhce->bhde', kd_n, v_n), o

        r_fin, o = lax.scan(step, r0, (mv(k_dec), mv(vc), mv(q_dec), mv(intra)))
        return jnp.moveaxis(o, 0, 2).reshape(B, H, L, dv).astype(v.dtype), r_fin

    def step_state(r, xs):
        kd_n, v_n = xs
        return chunk_decay * r + jnp.einsum('bhcd,bhce->bhde', kd_n, v_n), None

    r_fin, _ = lax.scan(step_state, r0, (mv(k_dec), mv(vc)))
    return None, r_fin


def _flip(t):
    return jnp.flip(t, axis=2)


def _gdn_mixer(parts_ctx, parts_lat, conv_w, a_log, dt_bias, norm_g, ctx_out):
    H = GDN_HEADS
    f32 = jnp.float32

    def prep(parts):
        q, k, v, z, b, a = parts
        qkv = jax.nn.silu(_short_conv(jnp.concatenate([q, k, v], -1), conv_w))
        q, k, v = jnp.split(qkv, [GDN_QK, 2 * GDN_QK], axis=-1)
        q = _l2norm(_heads(q, H)) * GDN_DK ** -0.5
        k = _l2norm(_heads(k, H))
        v = _heads(v, H)
        beta = jax.nn.sigmoid(b.astype(f32)).transpose(0, 2, 1)
        g = (-jnp.exp(a_log.astype(f32)) * jax.nn.softplus(a.astype(f32) + dt_bias.astype(f32))).transpose(0, 2, 1)
        return q, k, v, beta, g, z

    qc, kc, vc, bc, gc, zc = prep(parts_ctx)
    ql, kl, vl, bl, gl, zl = prep(parts_lat)
    s0 = jnp.zeros((kc.shape[0], H, GDN_DK, GDN_DV), f32)
    o_cf, s_f = _gated_delta_scan(qc, kc, vc, bc[:, :H], gc[:, :H], s0, ctx_out)
    o_lf, _ = _gated_delta_scan(ql, kl, vl, bl[:, :H], gl[:, :H], s_f, True)
    o_cb, s_b = _gated_delta_scan(_flip(qc), _flip(kc), _flip(vc), _flip(bc[:, H:]), _flip(gc[:, H:]), s0, ctx_out)
    o_lb, _ = _gated_delta_scan(_flip(ql), _flip(kl), _flip(vl), _flip(bl[:, H:]), _flip(gl[:, H:]), s_b, True)
    finish = lambda o, z: _merge(_rmsnorm(o, norm_g)) * jax.nn.silu(z)
    out_lat = finish(o_lf + _flip(o_lb), zl)
    out_ctx = finish(o_cf + _flip(o_cb), zc) if ctx_out else None
    return out_ctx, out_lat


def _retention_mixer(parts_ctx, parts_lat, log_gamma, norm_g, cos, sin, ctx_out):
    H = RET_HEADS

    def prep(parts, rope):
        q, k, v, gate = parts
        q = _heads(q, H)
        k = _heads(k, H) * RET_DK ** -0.5
        v = _heads(v, H)
        if rope:
            q = _apply_rope(q, cos, sin)
            k = _apply_rope(k, cos, sin)
        return q, k, v, gate

    qc, kc, vc, gtc = prep(parts_ctx, False)
    ql, kl, vl, gtl = prep(parts_lat, True)
    r0 = jnp.zeros((kc.shape[0], H, RET_DK, RET_DV), jnp.float32)
    o_cf, r_f = _retention_scan(qc, kc, vc, log_gamma[0], r0, ctx_out)
    o_lf, _ = _retention_scan(ql, kl, vl, log_gamma[0], r_f, True)
    o_cb, r_b = _retention_scan(_flip(qc), _flip(kc), _flip(vc), log_gamma[1], r0, ctx_out)
    o_lb, _ = _retention_scan(_flip(ql), _flip(kl), _flip(vl), log_gamma[1], r_b, True)
    finish = lambda o, gate: _merge(_layernorm(o)) * norm_g * jax.nn.silu(gate)
    out_lat = finish(o_lf + _flip(o_lb), gtl)
    out_ctx = finish(o_cf + _flip(o_cb), gtc) if ctx_out else None
    return out_ctx, out_lat


def _moe(h, w_router, router_bias, w_gate_up, w_down, w_shared_gate_up, w_shared_down):
    T = h.shape[0]
    s = jax.nn.sigmoid(h.astype(jnp.float32) @ w_router.astype(jnp.float32))
    sb = s + router_bias.astype(jnp.float32)
    per_group = N_EXPERTS // N_GROUPS
    gscore = lax.top_k(sb.reshape(T, N_GROUPS, per_group), 2)[0].sum(-1)
    _, gidx = lax.top_k(gscore, TOPK_GROUPS)
    gmask = jax.nn.one_hot(gidx, N_GROUPS, dtype=jnp.float32).sum(-2) > 0
    emask = jnp.repeat(gmask, per_group, axis=-1)
    _, eidx = lax.top_k(jnp.where(emask, sb, -jnp.inf), TOP_K)
    wsel = jnp.take_along_axis(s, eidx, axis=-1)
    wsel = wsel / wsel.sum(-1, keepdims=True) * ROUTED_SCALE
    gates = (jax.nn.one_hot(eidx, N_EXPERTS, dtype=jnp.float32) * wsel[..., None]).sum(1)
    gates = gates.T.astype(h.dtype)
    a_s, b_s = jnp.split(h @ w_shared_gate_up, 2, axis=-1)
    shared = (jax.nn.silu(a_s) * b_s) @ w_shared_down

    def expert_step(acc, xs):
        wgu, wd, ge = xs
        a, b = jnp.split(h @ wgu, 2, axis=-1)
        return acc + ge[:, None] * ((jax.nn.silu(a) * b) @ wd), None

    out, _ = lax.scan(expert_step, shared, (w_gate_up, w_down, gates))
    return out


def setup_inputs(seed: int = 0) -> dict:
    key = jax.random.key(seed)
    ks = jax.random.split(key, 24)
    f32 = jnp.float32
    nrm = lambda k, shape, scale: jax.random.normal(k, shape, f32) * scale
    D, E, F = D_MODEL, N_EXPERTS, EXPERT_FF
    dt = jnp.exp(jax.random.uniform(ks[9], (DEPTH, 2 * GDN_HEADS), f32, math.log(1e-3), math.log(1e-1)))
    base_gamma = jnp.log(1.0 - 2.0 ** (-5.0 - jnp.arange(RET_HEADS, dtype=f32)))
    return {
        'x': nrm(ks[0], (BATCH, SEQ, D), 1.0),
        'c': nrm(ks[1], (BATCH, D), 1.0),
        'ctx': nrm(ks[2], (BATCH, CTX_LEN, D), 1.0),
        'c_ctx': nrm(ks[3], (D,), 1.0),
        'w_mod': nrm(ks[4], (DEPTH, D, 6 * D), D ** -0.5),
        'b_mod': nrm(ks[5], (DEPTH, 6 * D), 0.02),
        'w_in': nrm(ks[6], (DEPTH, D, IN_WIDTH), D ** -0.5),
        'conv_w': nrm(ks[7], (DEPTH, CONV_W, 2 * GDN_QK + GDN_WIDTH), CONV_W ** -0.5),
        'gdn_a_log': jnp.log(jax.random.uniform(ks[8], (DEPTH, 2 * GDN_HEADS), f32, 1.0, 16.0)),
        'gdn_dt_bias': dt + jnp.log(-jnp.expm1(-dt)),
        'gdn_norm_g': 1.0 + nrm(ks[10], (DEPTH, GDN_DV), 0.02),
        'ret_log_gamma': base_gamma * jnp.exp(nrm(ks[11], (DEPTH, 2, RET_HEADS), 0.05)),
        'ret_norm_g': 1.0 + nrm(ks[12], (DEPTH, RET_WIDTH), 0.02),
        'w_out': nrm(ks[13], (DEPTH, MIX_WIDTH, D), DEEPNORM_BETA * MIX_WIDTH ** -0.5),
        'ln1_g': 1.0 + nrm(ks[14], (DEPTH, D), 0.02),
        'ln1_b': nrm(ks[15], (DEPTH, D), 0.02),
        'w_router': nrm(ks[16], (DEPTH, D, E), D ** -0.5),
        'router_bias': nrm(ks[17], (DEPTH, E), 0.01),
        'w_gate_up': nrm(ks[18], (DEPTH, E, D, 2 * F), D ** -0.5),
        'w_down': nrm(ks[19], (DEPTH, E, F, D), DEEPNORM_BETA * F ** -0.5),
        'w_shared_gate_up': nrm(ks[20], (DEPTH, D, 2 * SHARED_FF), D ** -0.5),
        'w_shared_down': nrm(ks[21], (DEPTH, SHARED_FF, D), DEEPNORM_BETA * SHARED_FF ** -0.5),
        'ln2_g': 1.0 + nrm(ks[22], (DEPTH, D), 0.02),
        'ln2_b': nrm(ks[23], (DEPTH, D), 0.02),
    }


def reference(x, c, ctx, c_ctx, w_mod, b_mod, w_in, conv_w, gdn_a_log, gdn_dt_bias, gdn_norm_g,
              ret_log_gamma, ret_norm_g, w_out, ln1_g, ln1_b, w_router, router_bias, w_gate_up, w_down,
              w_shared_gate_up, w_shared_down, ln2_g, ln2_b):
    B, L, D = x.shape
    Lc = ctx.shape[1]
    ROWS = L // GRID_W
    cos, sin = _axial_rope(ROWS, RET_DK)
    for layer in range(DEPTH):
        last = layer == DEPTH - 1
        mod_lat = (jax.nn.silu(c) @ w_mod[layer] + b_mod[layer])[:, None, :]
        mod_ctx = (jax.nn.silu(c_ctx[None]) @ w_mod[layer] + b_mod[layer])[:, None, :]
        sh1, sc1, g1, sh2, sc2, g2 = jnp.split(mod_lat, 6, axis=-1)
        csh1, csc1, cg1, csh2, csc2, cg2 = jnp.split(mod_ctx, 6, axis=-1)
        p_lat = _split_cols(_modulate(x, sh1, sc1) @ w_in[layer])
        p_ctx = _split_cols(_modulate(ctx, csh1, csc1) @ w_in[layer])
        a_ctx, a_lat = _gdn_mixer(p_ctx[:6], p_lat[:6], conv_w[layer], gdn_a_log[layer], gdn_dt_bias[layer],
                                  gdn_norm_g[layer], not last)
        r_ctx, r_lat = _retention_mixer(p_ctx[6:], p_lat[6:], ret_log_gamma[layer], ret_norm_g[layer],
                                        cos, sin, not last)
        y_lat = jnp.concatenate([a_lat, r_lat], -1) @ w_out[layer]
        x = _layernorm(DEEPNORM_ALPHA * x + g1 * y_lat, ln1_g[layer], ln1_b[layer])
        moe_w = (w_router[layer], router_bias[layer], w_gate_up[layer], w_down[layer],
                 w_shared_gate_up[layer], w_shared_down[layer])
        h_lat = _modulate(x, sh2, sc2).reshape(B * L, D)
        if last:
            f_lat = _moe(h_lat, *moe_w).reshape(B, L, D)
        else:
            y_ctx = jnp.concatenate([a_ctx, r_ctx], -1) @ w_out[layer]
            ctx = _layernorm(DEEPNORM_ALPHA * ctx + cg1 * y_ctx, ln1_g[layer], ln1_b[layer])
            h_ctx = _modulate(ctx, csh2, csc2).reshape(B * Lc, D)
            f_all = _moe(jnp.concatenate([h_ctx, h_lat], 0), *moe_w)
            f_lat = f_all[B * Lc:].reshape(B, L, D)
            ctx = _layernorm(DEEPNORM_ALPHA * ctx + cg2 * f_all[:B * Lc].reshape(B, Lc, D), ln2_g[layer], ln2_b[layer])
        x = _layernorm(DEEPNORM_ALPHA * x + g2 * f_lat, ln2_g[layer], ln2_b[layer])
    return x
```

```python
import functools
import math

import jax
import jax.numpy as jnp
from jax import lax
from jax.experimental import pallas as pl
from jax.experimental.pallas import tpu as pltpu

F32 = jnp.float32
BF16 = jnp.bfloat16

HEAD_DIM = 128
GDN_HEADS = 4
RET_HEADS = 4
CONV_W = 5
GDN_CHUNK = 64
RET_CHUNK = 256
GRID_W = 64
ROPE_THETA = 10000.0
N_EXPERTS = 64
TOP_K = 8
N_GROUPS = 8
TOPK_GROUPS = 4
ROUTED_SCALE = 2.5
LN_EPS = 1e-6
DEPTH = 1
DEEPNORM_ALPHA = (2 * DEPTH) ** 0.25

VMEM_LIMIT = 56 * 1024 * 1024
HIGHEST = lax.Precision.HIGHEST
NT_DIMS = (((1,), (1,)), ((), ()))


def _dot(a, b, precision=None):
    return jnp.dot(a, b, preferred_element_type=F32, precision=precision)


def _dot_nt(a, b, precision=None):
    return lax.dot_general(a, b, NT_DIMS, preferred_element_type=F32, precision=precision)


def _silu(x):
    return x * (1.0 / (1.0 + jnp.exp(-x)))


def _sigmoid(x):
    return 1.0 / (1.0 + jnp.exp(-x))


def _softplus(x):
    return jnp.maximum(x, 0.0) + jnp.log(1.0 + jnp.exp(-jnp.abs(x)))


def _ln_rows(x):
    mu = jnp.mean(x, axis=-1, keepdims=True)
    xc = x - mu
    var = jnp.mean(xc * xc, axis=-1, keepdims=True)
    return xc * lax.rsqrt(var + LN_EPS)


def _params(sem):
    return pltpu.CompilerParams(dimension_semantics=sem, vmem_limit_bytes=VMEM_LIMIT)


def _mod_kernel(c_ref, w_ref, b_ref, o_ref):
    o_ref[...] = _dot(_silu(c_ref[...]), w_ref[...], HIGHEST) + b_ref[...]


def _modulation(cc, w_mod, b_mod):
    rows, d = cc.shape
    n = w_mod.shape[1]
    tn = 1024
    return pl.pallas_call(
        _mod_kernel,
        grid=(n // tn,),
        in_specs=[pl.BlockSpec((rows, d), lambda j: (0, 0)),
                  pl.BlockSpec((d, tn), lambda j: (0, j)),
                  pl.BlockSpec((1, tn), lambda j: (0, j))],
        out_specs=pl.BlockSpec((rows, tn), lambda j: (0, j)),
        out_shape=jax.ShapeDtypeStruct((rows, n), F32),
        compiler_params=_params(("arbitrary",)),
        name="modulation",
    )(cc, w_mod, b_mod.reshape(1, n))


def _inproj_kernel(x_ref, sh_ref, sc_ref, wm_ref, wg_ref, qkv_ref, z_ref, ret_ref, g_ref):
    h = _ln_rows(x_ref[0]) * (1.0 + sc_ref[0]) + sh_ref[0]
    hb = h.astype(BF16)
    col = 0
    for ref in (qkv_ref, z_ref, ret_ref):
        width = ref.shape[-1]
        for n0 in range(0, width, 512):
            ref[0, :, n0:n0 + 512] = _dot(hb, wm_ref[:, col + n0:col + n0 + 512]).astype(BF16)
        col += width
    g_ref[0] = _dot(h, wg_ref[...], HIGHEST)


def _inproj(x, sh, sc, w_main, w_gate, tm):
    b, l, d = x.shape
    widths = (3 * GDN_HEADS * HEAD_DIM, GDN_HEADS * HEAD_DIM, 4 * RET_HEADS * HEAD_DIM)
    n_gate = w_gate.shape[1]
    row = lambda i, j: (i, j, 0)
    per_b = lambda i, j: (i, 0, 0)
    const = lambda i, j: (0, 0)
    return pl.pallas_call(
        _inproj_kernel,
        grid=(b, l // tm),
        in_specs=[pl.BlockSpec((1, tm, d), row),
                  pl.BlockSpec((1, 1, d), per_b),
                  pl.BlockSpec((1, 1, d), per_b),
                  pl.BlockSpec(w_main.shape, const),
                  pl.BlockSpec(w_gate.shape, const)],
        out_specs=[pl.BlockSpec((1, tm, w), row) for w in widths]
                  + [pl.BlockSpec((1, tm, n_gate), row)],
        out_shape=[jax.ShapeDtypeStruct((b, l, w), BF16) for w in widths]
                  + [jax.ShapeDtypeStruct((b, l, n_gate), F32)],
        compiler_params=_params(("arbitrary", "arbitrary")),
        name="inproj",
    )(x, sh, sc, w_main, w_gate)


def _conv_silu(src_ref, lane0, w, pad_ref, dst_ref, row0, length, blk):
    pad_ref[0:8, :] = jnp.zeros((8, HEAD_DIM), F32)
    pad_ref[8:8 + length, :] = src_ref[0, :, lane0:lane0 + HEAD_DIM].astype(F32)
    pad_ref[8 + length:16 + length, :] = jnp.zeros((8, HEAD_DIM), F32)
    half = CONV_W // 2

    def body(i, carry):
        r = pl.multiple_of(i * blk, blk)
        win = pad_ref[pl.ds(r, blk + 16), :]
        acc = jnp.zeros((blk, HEAD_DIM), F32)
        for j in range(CONV_W):
            acc = acc + win[8 - half + j:8 - half + j + blk, :] * w[j:j + 1, :]
        dst_ref[pl.ds(row0 + r, blk), :] = _silu(acc)
        return carry

    lax.fori_loop(0, length // blk, body, 0)


def _l2norm_rows(ref, length, blk, scale):
    def body(i, carry):
        r = pl.multiple_of(i * blk, blk)
        x = ref[pl.ds(r, blk), :]
        ref[pl.ds(r, blk), :] = x * (lax.rsqrt(jnp.sum(x * x, axis=-1, keepdims=True) + LN_EPS) * scale)
        return carry

    lax.fori_loop(0, length // blk, body, 0)


def _inv_unit_triangular(a):
    n = a.shape[0]
    eye = (lax.broadcasted_iota(jnp.int32, (n, n), 0) == lax.broadcasted_iota(jnp.int32, (n, n), 1)).astype(F32)
    p = eye - a
    x = a
    for _ in range(5):
        x = _dot(x, x, HIGHEST)
        p = p + _dot(p, x, HIGHEST)
    return p


def _gdn_kernel(alog_ref, dtb_ref,
                q_ref, k_ref, v_ref, qc_ref, kc_ref, vc_ref,
                cwq_ref, cwk_ref, cwv_ref, g_ref, gt_ref, z_ref, ng_ref,
                o_ref,
                pad_ref, qs_ref, ks_ref, vs_ref, gs_ref, gts_ref,
                wqf_ref, wqb_ref, u0_ref, intra_ref, ket_ref, gef_ref, geb_ref,
                of_ref, ob_ref, *, hb, l_lat, l_ctx):
    c = GDN_CHUNK
    c2 = 2 * c
    lt = l_lat + l_ctx
    n_ctx = l_ctx // c
    nt = lt // c
    hgrp = pl.program_id(1)

    i2 = lax.broadcasted_iota(jnp.int32, (c2, c2), 0)
    j2 = lax.broadcasted_iota(jnp.int32, (c2, c2), 1)
    same_blk = (i2 & c) == (j2 & c)
    sgn_i = jnp.where(i2 < c, 1, -1)
    sgn_j = jnp.where(j2 < c, 1, -1)
    incl = same_blk & ((j2 - i2) * sgn_i <= 0)
    strict = same_blk & ((j2 - i2) * sgn_i < 0)
    tri_col = ((j2 < c) & ((j2 - (i2 & (c - 1))) * sgn_i <= 0)).astype(F32)
    tri_row = ((i2 < c) & ((i2 - (j2 & (c - 1))) * sgn_j <= 0)).astype(F32)
    eye_b = (i2 == j2).astype(BF16)
    lane_lt_c = j2 < c

    for hh in range(hb):
        head = hgrp * hb + hh
        lane0 = hh * HEAD_DIM
        for (src_c, src_l, cw_ref, dst) in ((qc_ref, q_ref, cwq_ref, qs_ref),
                                            (kc_ref, k_ref, cwk_ref, ks_ref),
                                            (vc_ref, v_ref, cwv_ref, vs_ref)):
            w = cw_ref[:, lane0:lane0 + HEAD_DIM]
            _conv_silu(src_c, lane0, w, pad_ref, dst, 0, l_ctx, 256)
            _conv_silu(src_l, lane0, w, pad_ref, dst, l_ctx, l_lat, 256)
        _l2norm_rows(qs_ref, lt, 256, HEAD_DIM ** -0.5)
        _l2norm_rows(ks_ref, lt, 256, 1.0)

        a_f, a_b = alog_ref[head], alog_ref[GDN_HEADS + head]
        d_f, d_b = dtb_ref[head], dtb_ref[GDN_HEADS + head]
        comp = lax.broadcasted_iota(jnp.int32, (1, 4), 1)
        neg_a = -jnp.exp(jnp.where(comp == 2, a_f, a_b))
        dtb = jnp.where(comp == 2, d_f, d_b)
        graw = g_ref[0, hh]
        gcol = jnp.where(comp < 2, _sigmoid(graw), neg_a * _softplus(graw + dtb))
        gs_ref[...] = jnp.zeros(gs_ref.shape, F32)
        gs_ref[:, 0:4] = gcol
        compr = lax.broadcasted_iota(jnp.int32, (nt * 8, 1), 0) & 7
        neg_ar = -jnp.exp(jnp.where(compr == 2, a_f, a_b))
        dtbr = jnp.where(compr == 2, d_f, d_b)
        grow_raw = gt_ref[0, hh]
        grow = jnp.where(compr < 2, _sigmoid(grow_raw), neg_ar * _softplus(grow_raw + dtbr))
        gts_ref[...] = jnp.zeros(gts_ref.shape, F32)
        gts_ref[:, 0:c] = grow

        def chunk_body(ci, carry):
            r0 = pl.multiple_of(ci * c, c)
            ib = jnp.where(ci < n_ctx, n_ctx - 1 - ci, nt + n_ctx - 1 - ci)
            kc_ = ks_ref[pl.ds(r0, c), :]
            vc_ = vs_ref[pl.ds(r0, c), :]
            qc_ = qs_ref[pl.ds(r0, c), :]
            g_c = gs_ref[pl.ds(r0, c), :]
            cs = _dot(tri_col[:, 0:c], g_c, HIGHEST)
            rcs = _dot(gts_ref[pl.ds(pl.multiple_of(ci * 8, 8), 8), :], tri_row, HIGHEST)
            gcc = jnp.concatenate([cs[0:c, 2:3], cs[c:c2, 3:4]], axis=0)
            gcr = jnp.where(lane_lt_c[0:1, :], rcs[2:3, :], rcs[3:4, :])
            beta = jnp.concatenate([g_c[:, 0:1], g_c[:, 1:2]], axis=0)
            gend = jnp.concatenate([jnp.broadcast_to(cs[c - 1:c, 2:3], (c, 1)),
                                    jnp.broadcast_to(cs[c:c + 1, 3:4], (c, 1))], axis=0)
            k2 = jnp.concatenate([kc_, kc_], axis=0)
            v2 = jnp.concatenate([vc_, vc_], axis=0)
            q2 = jnp.concatenate([qc_, qc_], axis=0)
            k2b = k2.astype(BF16)
            dlog = gcc - gcr
            decay = jnp.exp(jnp.where(incl, dlog, -jnp.inf))
            kk = _dot_nt(k2b, k2b)
            a_mat = jnp.where(strict, kk * decay, 0.0) * beta
            t_mat = _inv_unit_triangular(a_mat).astype(BF16)
            egc = jnp.exp(gcc)
            u0 = _dot(t_mat, (v2 * beta).astype(BF16))
            w = _dot(t_mat, (k2 * (beta * egc)).astype(BF16))
            qd = q2 * egc
            ke = (k2 * jnp.exp(gend - gcc)).astype(BF16)
            ket = _dot_nt(eye_b, ke)
            intra = (_dot_nt(q2.astype(BF16), k2b) * decay).astype(BF16)
            ge = jnp.exp(gend)

            wqf_ref[hh, ci, 0:c, :] = w[0:c].astype(BF16)
            wqf_ref[hh, ci, c:c2, :] = qd[0:c].astype(BF16)
            wqb_ref[hh, ib, 0:c, :] = w[c:c2].astype(BF16)
            wqb_ref[hh, ib, c:c2, :] = qd[c:c2].astype(BF16)
            u0_ref[hh, ci, 0:c, :] = u0[0:c]
            u0_ref[hh, ib, c:c2, :] = u0[c:c2]
            intra_ref[hh, ci, 0:c, :] = intra[0:c]
            intra_ref[hh, ib, c:c2, :] = intra[c:c2]
            ket_ref[hh, ci, 0:HEAD_DIM, :] = jnp.where(lane_lt_c, ket, 0.0).astype(BF16)
            ket_ref[hh, ib, HEAD_DIM:2 * HEAD_DIM, :] = jnp.where(lane_lt_c, 0.0, ket).astype(BF16)
            gef_ref[hh, ci] = jnp.broadcast_to(ge[0:1, :], (8, HEAD_DIM))
            geb_ref[hh, ib] = jnp.broadcast_to(ge[c:c + 1, :], (8, HEAD_DIM))
            return carry

        lax.fori_loop(0, nt, chunk_body, 0)

    def rec_body(i, states):
        cb = jnp.where(i < n_ctx, n_ctx - 1 - i, nt + n_ctx - 1 - i)
        rf = pl.multiple_of(i * c, c)
        rb = pl.multiple_of(cb * c, c)
        new_states = []
        for hh in range(hb):
            sf, sb = states[2 * hh], states[2 * hh + 1]
            r_f = _dot(wqf_ref[hh, i], sf.astype(BF16))
            r_b = _dot(wqb_ref[hh, i], sb.astype(BF16))
            u0 = u0_ref[hh, i]
            u2 = jnp.concatenate([u0[0:c] - r_f[0:c], u0[c:c2] - r_b[0:c]], axis=0).astype(BF16)
            iu = _dot(intra_ref[hh, i], u2)
            of_ref[hh, pl.ds(rf, c), :] = r_f[c:c2] + iu[0:c]
            ob_ref[hh, pl.ds(rb, c), :] = r_b[c:c2] + iu[c:c2]
            ket = ket_ref[hh, i]
            sf = gef_ref[hh, i][0:1, :] * sf + _dot(ket[0:HEAD_DIM], u2)
            sb = geb_ref[hh, i][0:1, :] * sb + _dot(ket[HEAD_DIM:2 * HEAD_DIM], u2)
            new_states += [sf, sb]
        return tuple(new_states)

    zero_state = tuple(jnp.zeros((HEAD_DIM, HEAD_DIM), F32) for _ in range(2 * hb))
    lax.fori_loop(0, nt, rec_body, zero_state)

    ng = ng_ref[...]
    blk = 256
    for hh in range(hb):
        lane0 = hh * HEAD_DIM

        def fin_body(i, carry):
            r = pl.multiple_of(i * blk, blk)
            o = of_ref[hh, pl.ds(l_ctx + r, blk), :] + ob_ref[hh, pl.ds(l_ctx + r, blk), :]
            y = o * lax.rsqrt(jnp.mean(o * o, axis=-1, keepdims=True) + LN_EPS) * ng
            z = z_ref[0, pl.ds(r, blk), lane0:lane0 + HEAD_DIM].astype(F32)
            o_ref[0, pl.ds(r, blk), lane0:lane0 + HEAD_DIM] = (y * _silu(z)).astype(BF16)
            return carry

        lax.fori_loop(0, l_lat // blk, fin_body, 0)


def _gdn_mixer(qkv_lat, qkv_ctx, z_lat, gate_cols, gate_rows, conv_w, a_log, dt_bias, norm_g, hb):
    b, l_lat, _ = qkv_lat.shape
    l_ctx = qkv_ctx.shape[1]
    lt = l_lat + l_ctx
    nt = lt // GDN_CHUNK
    hw = hb * HEAD_DIM
    ngrp = GDN_HEADS // hb
    c2 = 2 * GDN_CHUNK

    def seq_spec(length, part):
        return pl.BlockSpec((1, length, hw), lambda i, j, *_: (i, 0, part * ngrp + j))

    def cw_spec(part):
        return pl.BlockSpec((CONV_W, hw), lambda i, j, *_: (0, part * ngrp + j))

    grid_spec = pltpu.PrefetchScalarGridSpec(
        num_scalar_prefetch=2,
        grid=(b, ngrp),
        in_specs=[seq_spec(l_lat, 0), seq_spec(l_lat, 1), seq_spec(l_lat, 2),
                  seq_spec(l_ctx, 0), seq_spec(l_ctx, 1), seq_spec(l_ctx, 2),
                  cw_spec(0), cw_spec(1), cw_spec(2),
                  pl.BlockSpec((1, hb, lt, 4), lambda i, j, *_: (i, j, 0, 0)),
                  pl.BlockSpec((1, hb, nt * 8, GDN_CHUNK), lambda i, j, *_: (i, j, 0, 0)),
                  pl.BlockSpec((1, l_lat, hw), lambda i, j, *_: (i, 0, j)),
                  pl.BlockSpec((1, HEAD_DIM), lambda i, j, *_: (0, 0))],
        out_specs=pl.BlockSpec((1, l_lat, hw), lambda i, j, *_: (i, 0, j)),
        scratch_shapes=[
            pltpu.VMEM((l_lat + 16, HEAD_DIM), F32),
            pltpu.VMEM((lt, HEAD_DIM), F32),
            pltpu.VMEM((lt, HEAD_DIM), F32),
            pltpu.VMEM((lt, HEAD_DIM), F32),
            pltpu.VMEM((lt, HEAD_DIM), F32),
            pltpu.VMEM((nt * 8, c2), F32),
            pltpu.VMEM((hb, nt, c2, HEAD_DIM), BF16),
            pltpu.VMEM((hb, nt, c2, HEAD_DIM), BF16),
            pltpu.VMEM((hb, nt, c2, HEAD_DIM), F32),
            pltpu.VMEM((hb, nt, c2, c2), BF16),
            pltpu.VMEM((hb, nt, 2 * HEAD_DIM, c2), BF16),
            pltpu.VMEM((hb, nt, 8, HEAD_DIM), F32),
            pltpu.VMEM((hb, nt, 8, HEAD_DIM), F32),
            pltpu.VMEM((hb, lt, HEAD_DIM), F32),
            pltpu.VMEM((hb, lt, HEAD_DIM), F32),
        ])
    kern = functools.partial(_gdn_kernel, hb=hb, l_lat=l_lat, l_ctx=l_ctx)
    return pl.pallas_call(
        kern,
        grid_spec=grid_spec,
        out_shape=jax.ShapeDtypeStruct((b, l_lat, GDN_HEADS * HEAD_DIM), BF16),
        compiler_params=_params(("arbitrary", "arbitrary")),
        name="gdn_mixer",
    )(a_log, dt_bias, qkv_lat, qkv_lat, qkv_lat, qkv_ctx, qkv_ctx, qkv_ctx,
      conv_w, conv_w, conv_w, gate_cols, gate_rows, z_lat, norm_g.reshape(1, HEAD_DIM))


def _ret_kernel(lg_ref, q_ref, k_ref, v_ref, gate_ref, qc_ref, kc_ref, vc_ref, cos_ref, sin_ref, ng_ref,
                o_ref, q_s, k_s, rf_s, rb_s, *, l_lat, l_ctx):
    c = RET_CHUNK
    n_lat = l_lat // c
    n_ctx = l_ctx // c
    head = pl.program_id(1)
    lg_f = lg_ref[head]
    lg_b = lg_ref[RET_HEADS + head]
    pos_c = lax.broadcasted_iota(jnp.int32, (c, 1), 0).astype(F32)
    ii = lax.broadcasted_iota(jnp.int32, (c, c), 0)
    jj = lax.broadcasted_iota(jnp.int32, (c, c), 1)
    dif = (ii - jj).astype(F32)
    dmat = (jnp.exp(jnp.where(ii >= jj, lg_f * dif, -jnp.inf))
            + jnp.exp(jnp.where(jj >= ii, -lg_b * dif, -jnp.inf)))
    kdec_f = jnp.exp(lg_f * (c - 1 - pos_c))
    kdec_b = jnp.exp(lg_b * pos_c)
    qdec_f = jnp.exp(lg_f * (pos_c + 1.0))
    qdec_b = jnp.exp(lg_b * (c - pos_c))
    cd_f = jnp.exp(jnp.full((1, HEAD_DIM), lg_f * c, F32))
    cd_b = jnp.exp(jnp.full((1, HEAD_DIM), lg_b * c, F32))
    e2 = lax.broadcasted_iota(jnp.int32, (HEAD_DIM, HEAD_DIM), 0)
    f2 = lax.broadcasted_iota(jnp.int32, (HEAD_DIM, HEAD_DIM), 1)
    eye_b = (e2 == f2).astype(BF16)
    kscale = HEAD_DIM ** -0.5

    def kv_state(kd, v):
        kdt = _dot_nt(eye_b, kd.astype(BF16)).astype(BF16)
        return _dot(kdt, v)

    r_f = jnp.zeros((HEAD_DIM, HEAD_DIM), F32)
    for n in range(n_ctx):
        kc = kc_ref[0, n * c:(n + 1) * c, :].astype(F32) * kscale
        r_f = cd_f * r_f + kv_state(kc * kdec_f, vc_ref[0, n * c:(n + 1) * c, :])
    r_b = jnp.zeros((HEAD_DIM, HEAD_DIM), F32)
    for n in reversed(range(n_ctx)):
        kc = kc_ref[0, n * c:(n + 1) * c, :].astype(F32) * kscale
        r_b = cd_b * r_b + kv_state(kc * kdec_b, vc_ref[0, n * c:(n + 1) * c, :])

    def rope_body(n, carry):
        r = pl.multiple_of(n * c, c)
        cs = cos_ref[pl.ds(r, c), :]
        sn = sin_ref[pl.ds(r, c), :]
        q = q_ref[0, pl.ds(r, c), :].astype(F32)
        k = k_ref[0, pl.ds(r, c), :].astype(F32) * kscale
        q_s[pl.ds(r, c), :] = q * cs + pltpu.roll(q, HEAD_DIM // 2, 1) * sn
        k_s[pl.ds(r, c), :] = k * cs + pltpu.roll(k, HEAD_DIM // 2, 1) * sn
        return carry

    lax.fori_loop(0, n_lat, rope_body, 0)

    def fstate_body(n, r):
        rf_s[n] = r
        rr = pl.multiple_of(n * c, c)
        return cd_f * r + kv_state(k_s[pl.ds(rr, c), :] * kdec_f, v_ref[0, pl.ds(rr, c), :])

    lax.fori_loop(0, n_lat, fstate_body, r_f)

    def bstate_body(m, r):
        n = n_lat - 1 - m
        rb_s[n] = r
        rr = pl.multiple_of(n * c, c)
        return cd_b * r + kv_state(k_s[pl.ds(rr, c), :] * kdec_b, v_ref[0, pl.ds(rr, c), :])

    lax.fori_loop(0, n_lat, bstate_body, r_b)

    ng = ng_ref[...]

    def out_body(n, carry):
        r = pl.multiple_of(n * c, c)
        q = q_s[pl.ds(r, c), :]
        kb = k_s[pl.ds(r, c), :].astype(BF16)
        v = v_ref[0, pl.ds(r, c), :]
        att = (_dot_nt(q.astype(BF16), kb) * dmat).astype(BF16)
        o = _dot(att, v)
        o = o + _dot((q * qdec_f).astype(BF16), rf_s[n].astype(BF16))
        o = o + _dot((q * qdec_b).astype(BF16), rb_s[n].astype(BF16))
        y = _ln_rows(o) * ng
        g = gate_ref[0, pl.ds(r, c), :].astype(F32)
        o_ref[0, pl.ds(r, c), :] = (y * _silu(g)).astype(BF16)
        return carry

    lax.fori_loop(0, n_lat, out_body, 0)


def _ret_mixer(ret_lat, ret_ctx, log_gamma, norm_g, cos_t, sin_t):
    b, l_lat, _ = ret_lat.shape
    l_ctx = ret_ctx.shape[1]
    n_lat = l_lat // RET_CHUNK

    def seq_spec(length, part):
        return pl.BlockSpec((1, length, HEAD_DIM), lambda i, j, *_: (i, 0, part * RET_HEADS + j))

    grid_spec = pltpu.PrefetchScalarGridSpec(
        num_scalar_prefetch=1,
        grid=(b, RET_HEADS),
        in_specs=[seq_spec(l_lat, 0), seq_spec(l_lat, 1), seq_spec(l_lat, 2), seq_spec(l_lat, 3),
                  seq_spec(l_ctx, 0), seq_spec(l_ctx, 1), seq_spec(l_ctx, 2),
                  pl.BlockSpec((l_lat, HEAD_DIM), lambda i, j, *_: (0, 0)),
                  pl.BlockSpec((l_lat, HEAD_DIM), lambda i, j, *_: (0, 0)),
                  pl.BlockSpec((1, HEAD_DIM), lambda i, j, *_: (0, j))],
        out_specs=pl.BlockSpec((1, l_lat, HEAD_DIM), lambda i, j, *_: (i, 0, j)),
        scratch_shapes=[pltpu.VMEM((l_lat, HEAD_DIM), F32),
                        pltpu.VMEM((l_lat, HEAD_DIM), F32),
                        pltpu.VMEM((n_lat, HEAD_DIM, HEAD_DIM), F32),
                        pltpu.VMEM((n_lat, HEAD_DIM, HEAD_DIM), F32)])
    kern = functools.partial(_ret_kernel, l_lat=l_lat, l_ctx=l_ctx)
    return pl.pallas_call(
        kern,
        grid_spec=grid_spec,
        out_shape=jax.ShapeDtypeStruct((b, l_lat, RET_HEADS * HEAD_DIM), BF16),
        compiler_params=_params(("arbitrary", "arbitrary")),
        name="ret_mixer",
    )(log_gamma.reshape(-1), ret_lat, ret_lat, ret_lat, ret_lat, ret_ctx, ret_ctx, ret_ctx,
      cos_t, sin_t, norm_g.reshape(1, -1))


def _rope_tables(l_lat):
    rows = l_lat // GRID_W
    row = jnp.repeat(jnp.arange(rows, dtype=F32), GRID_W)
    col = jnp.tile(jnp.arange(GRID_W, dtype=F32), rows)
    quarter = HEAD_DIM // 4
    inv = ROPE_THETA ** (-jnp.arange(quarter, dtype=F32) / quarter)
    ang = jnp.concatenate([row[:, None] * inv, col[:, None] * inv], -1)
    cos, sin = jnp.cos(ang), jnp.sin(ang)
    return jnp.concatenate([cos, cos], -1), jnp.concatenate([-sin, sin], -1)


def _rank_before(vals, n):
    idx = lax.broadcasted_iota(jnp.int32, vals.shape, 0)
    rank = jnp.zeros(vals.shape, jnp.int32)
    for e in range(n):
        ve = vals[e:e + 1, :]
        before = (ve > vals) | ((ve == vals) & (e < idx))
        rank = rank + before.astype(jnp.int32)
    return rank


def _outproj_kernel(a_ref, r_ref, x_ref, wa_ref, wr_ref, g1_ref, sh2_ref, sc2_ref, lg_ref, lb_ref,
                    wrt_ref, rb_ref, x1_ref, h_ref, gates_ref):
    y = _dot(a_ref[0], wa_ref[...]) + _dot(r_ref[0], wr_ref[...])
    x1 = _ln_rows(DEEPNORM_ALPHA * x_ref[0] + g1_ref[0] * y) * lg_ref[...] + lb_ref[...]
    x1_ref[0] = x1
    h = _ln_rows(x1) * (1.0 + sc2_ref[0]) + sh2_ref[0]
    h_ref[0] = h.astype(BF16)
    s = _sigmoid(_dot_nt(wrt_ref[...], h, HIGHEST))
    sb = s + rb_ref[...]
    tm = s.shape[1]
    per = N_EXPERTS // N_GROUPS
    sub = lax.broadcasted_iota(jnp.int32, (per, tm), 0)
    gs_rows = []
    for g in range(N_GROUPS):
        blk = sb[g * per:(g + 1) * per, :]
        m1 = jnp.max(blk, axis=0, keepdims=True)
        first = jnp.min(jnp.where(blk == m1, sub, per), axis=0, keepdims=True)
        m2 = jnp.max(jnp.where(sub == first, -jnp.inf, blk), axis=0, keepdims=True)
        gs_rows.append(m1 + m2)
    gscore = jnp.concatenate(gs_rows, axis=0)
    gsel = _rank_before(gscore, N_GROUPS) < TOPK_GROUPS
    emask = jnp.concatenate([jnp.broadcast_to(gsel[g:g + 1, :], (per, tm)) for g in range(N_GROUPS)], axis=0)
    masked = jnp.where(emask, sb, -jnp.inf)
    sel = _rank_before(masked, N_EXPERTS) < TOP_K
    wsel = jnp.where(sel, s, 0.0)
    gates_ref[...] = wsel / jnp.sum(wsel, axis=0, keepdims=True) * ROUTED_SCALE


def _outproj(a_lat, r_lat, x, w_a, w_r, g1, sh2, sc2, ln_g, ln_b, w_router_t, router_bias, tm):
    b, l, d = x.shape
    half = a_lat.shape[-1]
    nl = l // tm
    row = lambda i, j: (i, j, 0)
    per_b = lambda i, j: (i, 0, 0)
    const = lambda i, j: (0, 0)
    return pl.pallas_call(
        _outproj_kernel,
        grid=(b, nl),
        in_specs=[pl.BlockSpec((1, tm, half), row),
                  pl.BlockSpec((1, tm, half), row),
                  pl.BlockSpec((1, tm, d), row),
                  pl.BlockSpec((half, d), const),
                  pl.BlockSpec((half, d), const),
                  pl.BlockSpec((1, 1, d), per_b),
                  pl.BlockSpec((1, 1, d), per_b),
                  pl.BlockSpec((1, 1, d), per_b),
                  pl.BlockSpec((1, d), const),
                  pl.BlockSpec((1, d), const),
                  pl.BlockSpec((N_EXPERTS, d), const),
                  pl.BlockSpec((N_EXPERTS, 1), const)],
        out_specs=[pl.BlockSpec((1, tm, d), row),
                   pl.BlockSpec((1, tm, d), row),
                   pl.BlockSpec((N_EXPERTS, tm), lambda i, j: (0, i * nl + j))],
        out_shape=[jax.ShapeDtypeStruct((b, l, d), F32),
                   jax.ShapeDtypeStruct((b, l, d), BF16),
                   jax.ShapeDtypeStruct((N_EXPERTS, b * l), F32)],
        compiler_params=_params(("arbitrary", "arbitrary")),
        name="outproj_router",
    )(a_lat, r_lat, x, w_a, w_r, g1, sh2, sc2, ln_g.reshape(1, d), ln_b.reshape(1, d),
      w_router_t, router_bias.reshape(N_EXPERTS, 1))


def _moe_kernel(h_ref, x1_ref, gates_ref, wsg_ref, wsd_ref, wgu_ref, wd_ref, g2_ref, lg_ref, lb_ref,
                o_ref, acc_ref, *, ff):
    e = pl.program_id(1)
    h = h_ref[...]

    @pl.when(e == 0)
    def _():
        ab = _dot(h, wsg_ref[...])
        act = (_silu(ab[:, :ff]) * ab[:, ff:]).astype(BF16)
        acc_ref[...] = _dot(act, wsd_ref[...])

    ab = _dot(h, wgu_ref[0])
    act = (_silu(ab[:, :ff]) * ab[:, ff:]).astype(BF16)
    lane = lax.broadcasted_iota(jnp.int32, gates_ref.shape, 1)
    gate = jnp.sum(jnp.where(lane == e, gates_ref[...], 0.0), axis=1, keepdims=True)
    acc_ref[...] += gate * _dot(act, wd_ref[0])

    @pl.when(e == pl.num_programs(1) - 1)
    def _():
        o_ref[...] = _ln_rows(DEEPNORM_ALPHA * x1_ref[...] + g2_ref[0] * acc_ref[...]) * lg_ref[...] + lb_ref[...]


def _moe(h, x1, gates_tok, w_sg, w_sd, w_gu, w_d, g2, ln_g, ln_b, tm):
    t, d = h.shape
    tiles_per_batch = t // g2.shape[0] // tm
    n_e, _, ff2 = w_gu.shape
    ff = ff2 // 2
    row = lambda i, e: (i, 0)
    const = lambda i, e: (0, 0)
    kern = functools.partial(_moe_kernel, ff=ff)
    return pl.pallas_call(
        kern,
        grid=(t // tm, n_e),
        in_specs=[pl.BlockSpec((tm, d), row),
                  pl.BlockSpec((tm, d), row),
                  pl.BlockSpec((tm, gates_tok.shape[1]), row),
                  pl.BlockSpec(w_sg.shape, const),
                  pl.BlockSpec(w_sd.shape, const),
                  pl.BlockSpec((1, d, ff2), lambda i, e: (e, 0, 0)),
                  pl.BlockSpec((1, ff, d), lambda i, e: (e, 0, 0)),
                  pl.BlockSpec((1, 1, d), lambda i, e: (i // tiles_per_batch, 0, 0)),
                  pl.BlockSpec((1, d), const),
                  pl.BlockSpec((1, d), const)],
        out_specs=pl.BlockSpec((tm, d), row),
        out_shape=jax.ShapeDtypeStruct((t, d), F32),
        scratch_shapes=[pltpu.VMEM((tm, d), F32)],
        compiler_params=_params(("arbitrary", "arbitrary")),
        name="moe",
    )(h, x1, gates_tok, w_sg, w_sd, w_gu, w_d, g2, ln_g.reshape(1, d), ln_b.reshape(1, d))


def _gate_layouts(g_lat, g_ctx):
    g = jnp.concatenate([g_ctx, g_lat], axis=1)[..., :4 * GDN_HEADS]
    b, lt, _ = g.shape
    g = g.reshape(b, lt, 4, GDN_HEADS)
    cols = jnp.transpose(g, (0, 3, 1, 2))
    nt = lt // GDN_CHUNK
    rows = jnp.transpose(g.reshape(b, nt, GDN_CHUNK, 4, GDN_HEADS), (0, 4, 1, 3, 2))
    rows = jnp.pad(rows, ((0, 0), (0, 0), (0, 0), (0, 4), (0, 0))).reshape(b, GDN_HEADS, nt * 8, GDN_CHUNK)
    return cols, rows


def kernel(x, c, ctx, c_ctx, w_mod, b_mod, w_in, conv_w, gdn_a_log, gdn_dt_bias, gdn_norm_g, ret_log_gamma,
           ret_norm_g, w_out, ln1_g, ln1_b, w_router, router_bias, w_gate_up, w_down, w_shared_gate_up,
           w_shared_down, ln2_g, ln2_b):
    b, l, d = x.shape
    lc = ctx.shape[1]
    gw = GDN_HEADS * HEAD_DIM
    rw = RET_HEADS * HEAD_DIM

    n_mod = -(-(b + 1) // 8) * 8
    cc = jnp.zeros((n_mod, d), F32).at[:b].set(c).at[b].set(c_ctx)
    mod = _modulation(cc, w_mod[0], b_mod[0])
    sh1, sc1, g1, sh2, sc2, g2 = [mod[:b, i * d:(i + 1) * d].reshape(b, 1, d) for i in range(6)]
    csh1 = jnp.broadcast_to(mod[b, 0:d].reshape(1, 1, d), (b, 1, d))
    csc1 = jnp.broadcast_to(mod[b, d:2 * d].reshape(1, 1, d), (b, 1, d))

    w = w_in[0]
    o_gate = 3 * gw + gw
    o_ret = o_gate + 4 * GDN_HEADS
    w_main = jnp.concatenate([w[:, :o_gate], w[:, o_ret:]], axis=1).astype(BF16)
    w_gate = w[:, o_gate:o_ret]
    qkv_lat, z_lat, ret_lat, g_lat = _inproj(x, sh1, sc1, w_main, w_gate, 512)
    qkv_ctx, _, ret_ctx, g_ctx = _inproj(ctx, csh1, csc1, w_main, w_gate, lc)

    gate_cols, gate_rows = _gate_layouts(g_lat, g_ctx)
    a_lat = _gdn_mixer(qkv_lat, qkv_ctx, z_lat, gate_cols, gate_rows, conv_w[0], gdn_a_log[0],
                       gdn_dt_bias[0], gdn_norm_g[0], hb=2)
    cos_t, sin_t = _rope_tables(l)
    r_lat = _ret_mixer(ret_lat, ret_ctx, ret_log_gamma[0], ret_norm_g[0], cos_t, sin_t)

    wo = w_out[0].astype(BF16)
    x1, h, gates_t = _outproj(a_lat, r_lat, x, wo[:gw], wo[gw:], g1, sh2, sc2, ln1_g[0], ln1_b[0],
                              w_router[0].T, router_bias[0], 512)
    out = _moe(h.reshape(b * l, d), x1.reshape(b * l, d), gates_t.T,
               w_shared_gate_up[0].astype(BF16), w_shared_down[0].astype(BF16),
               w_gate_up[0].astype(BF16), w_down[0].astype(BF16),
               g2, ln2_g[0], ln2_b[0], 512)
    return out.reshape(b, l, d)
```

```python
import functools

import jax
import jax.numpy as jnp
from jax import lax
from jax.experimental import pallas as pl
from jax.experimental.pallas import tpu as pltpu
from jax.experimental.pallas import tpu_sc as plsc

F32 = jnp.float32
BF16 = jnp.bfloat16

HEAD_DIM = 128
GDN_HEADS = 4
RET_HEADS = 4
CONV_W = 5
CONV_ROWS = 128
HALO = 16
GDN_CHUNK = 64
INV_BASE = 16
RET_CHUNK = 256
GRID_W = 64
ROPE_THETA = 10000.0
N_EXPERTS = 64
TOP_K = 8
N_GROUPS = 8
TOPK_GROUPS = 4
ROUTED_SCALE = 2.5
SC_WINDOW = 128
PACK_GROUPS = 2
BATCH_PARTS = 2
EXPERT_SUBTILES = 4
EXPERT_TILE = 512
LN_EPS = 1e-6
DEPTH = 1
DEEPNORM_ALPHA = (2 * DEPTH) ** 0.25

VMEM_LIMIT = 56 * 1024 * 1024
HIGHEST = lax.Precision.HIGHEST
NT_DIMS = (((1,), (1,)), ((), ()))


def _dot(a, b, precision=None):
    return jnp.dot(a, b, preferred_element_type=F32, precision=precision)


def _dot_nt(a, b, precision=None):
    return lax.dot_general(a, b, NT_DIMS, preferred_element_type=F32, precision=precision)


def _silu(x):
    return x * (1.0 / (1.0 + jnp.exp(-x)))


def _sigmoid(x):
    return 1.0 / (1.0 + jnp.exp(-x))


def _softplus(x):
    return jnp.maximum(x, 0.0) + jnp.log(1.0 + jnp.exp(-jnp.abs(x)))


def _ln_rows(x):
    mu = jnp.mean(x, axis=-1, keepdims=True)
    xc = x - mu
    var = jnp.mean(xc * xc, axis=-1, keepdims=True)
    return xc * lax.rsqrt(var + LN_EPS)


def _params(sem):
    return pltpu.CompilerParams(dimension_semantics=sem, vmem_limit_bytes=VMEM_LIMIT)


def _mod_kernel(c_ref, w_ref, b_ref, o_ref):
    o_ref[...] = _dot(_silu(c_ref[...]), w_ref[...], HIGHEST) + b_ref[...]


def _modulation(cc, w_mod, b_mod):
    rows, d = cc.shape
    n = w_mod.shape[1]
    tn = 1024
    return pl.pallas_call(
        _mod_kernel,
        grid=(n // tn,),
        in_specs=[pl.BlockSpec((rows, d), lambda j: (0, 0)),
                  pl.BlockSpec((d, tn), lambda j: (0, j)),
                  pl.BlockSpec((1, tn), lambda j: (0, j))],
        out_specs=pl.BlockSpec((rows, tn), lambda j: (0, j)),
        out_shape=jax.ShapeDtypeStruct((rows, n), F32),
        compiler_params=_params(("arbitrary",)),
        name="modulation",
    )(cc, w_mod, b_mod.reshape(1, n))


def _inproj_kernel(x_ref, xp_ref, xn_ref, sh_ref, sc_ref, wm_ref, wg_ref, cw_ref, qkv_ref, z_ref, ret_ref, g_ref,
                   p_ref):
    j, nl = pl.program_id(1), pl.num_programs(1)
    tm = x_ref.shape[1]
    modulate = lambda x: _ln_rows(x) * (1.0 + sc_ref[0]) + sh_ref[0]
    h = modulate(x_ref[0])
    hb = h.astype(BF16)
    h_prev = jnp.where(j > 0, modulate(xp_ref[0]), 0.0).astype(BF16)
    h_next = jnp.where(j < nl - 1, modulate(xn_ref[0]), 0.0).astype(BF16)
    h_ext = jnp.concatenate([h_prev, hb, h_next], axis=0)
    half = CONV_W // 2
    gw = GDN_HEADS * HEAD_DIM
    for part in range(3):
        p_ref[part] = _dot(h_ext, wm_ref[:, part * gw:(part + 1) * gw])

    def conv_part(part):
        c0 = part * gw
        p = p_ref[part]
        acc = p[HALO - half:HALO - half + tm, :] * cw_ref[0:1, c0:c0 + gw]
        for tap in range(1, CONV_W):
            acc = acc + p[HALO - half + tap:HALO - half + tap + tm, :] * cw_ref[tap:tap + 1, c0:c0 + gw]
        y = _silu(acc)
        if part < 2:
            scale = HEAD_DIM ** -0.5 if part == 0 else 1.0
            blocks = [y[:, hd * HEAD_DIM:(hd + 1) * HEAD_DIM] for hd in range(GDN_HEADS)]
            blocks = [blk * (lax.rsqrt(jnp.sum(blk * blk, axis=-1, keepdims=True) + LN_EPS) * scale) for blk in blocks]
            y = jnp.concatenate(blocks, axis=1)
        qkv_ref[0, :, c0:c0 + gw] = y.astype(BF16)

    rest = [(ref, n0) for ref in (z_ref, ret_ref) for n0 in range(0, ref.shape[-1], 512)]
    col = 3 * gw
    for idx, (ref, n0) in enumerate(rest):
        if idx < 3:
            conv_part(idx)
        ref[0, :, n0:n0 + 512] = _dot(hb, wm_ref[:, col:col + 512]).astype(BF16)
        col += 512
    h_lo = (h - hb.astype(F32)).astype(BF16)
    wg = wg_ref[...]
    wg_hi = wg.astype(BF16)
    wg_lo = (wg - wg_hi.astype(F32)).astype(BF16)
    g_ref[0] = (_dot(h_lo, wg_hi) + _dot(hb, wg_lo)) + _dot(hb, wg_hi)


def _inproj(x, sh, sc, w_main, w_gate, conv_w, tm, b0, b):
    _, l, d = x.shape
    widths = (3 * GDN_HEADS * HEAD_DIM, GDN_HEADS * HEAD_DIM, 4 * RET_HEADS * HEAD_DIM)
    n_gate = w_gate.shape[1]
    per_tile, last = tm // HALO, l // HALO - 1
    row = lambda i, j: (i, j, 0)
    const = lambda i, j: (0, 0)
    return pl.pallas_call(
        _inproj_kernel,
        grid=(b, l // tm),
        in_specs=[pl.BlockSpec((1, tm, d), lambda i, j: (i + b0, j, 0)),
                  pl.BlockSpec((1, HALO, d), lambda i, j: (i + b0, jnp.maximum(j * per_tile - 1, 0), 0)),
                  pl.BlockSpec((1, HALO, d), lambda i, j: (i + b0, jnp.minimum((j + 1) * per_tile, last), 0)),
                  pl.BlockSpec((1, 1, d), lambda i, j: (i + b0, 0, 0)),
                  pl.BlockSpec((1, 1, d), lambda i, j: (i + b0, 0, 0)),
                  pl.BlockSpec(w_main.shape, const),
                  pl.BlockSpec(w_gate.shape, const),
                  pl.BlockSpec(conv_w.shape, const)],
        out_specs=[pl.BlockSpec((1, tm, w), row) for w in widths]
                  + [pl.BlockSpec((1, tm, n_gate), row)],
        out_shape=[jax.ShapeDtypeStruct((b, l, w), BF16) for w in widths]
                  + [jax.ShapeDtypeStruct((b, l, n_gate), F32)],
        scratch_shapes=[pltpu.VMEM((3, tm + 2 * HALO, GDN_HEADS * HEAD_DIM), F32)],
        compiler_params=_params(("arbitrary", "arbitrary")),
        name="inproj",
    )(x, x, x, sh, sc, w_main, w_gate, conv_w)


def _bdot(a, b):
    return _dot(a.astype(BF16), b.astype(BF16))


def _pack_bf16_pair(x):
    n = x.shape[1] // 2
    bits = lambda v: pltpu.bitcast(v.astype(BF16).astype(F32), jnp.uint32)
    word = lax.shift_right_logical(bits(x[:, :n]), jnp.uint32(16)) | (bits(x[:, n:]) & jnp.uint32(0xFFFF0000))
    return pltpu.bitcast(word, jnp.int32)


def _unpack_bf16_pair(w):
    u = pltpu.bitcast(w, jnp.uint32)
    lo = pltpu.bitcast(lax.shift_left(u, jnp.uint32(16)), F32)
    hi = pltpu.bitcast(u & jnp.uint32(0xFFFF0000), F32)
    return lo.astype(BF16), hi.astype(BF16)


def _split3(x):
    hi = x.astype(BF16)
    r1 = x - hi.astype(F32)
    mid = r1.astype(BF16)
    lo = (r1 - mid.astype(F32)).astype(BF16)
    return hi, mid, lo


def _dot_split_lhs(x, mask01):
    m = mask01.astype(BF16)
    hi, mid, lo = _split3(x)
    return (_dot(lo, m) + _dot(mid, m)) + _dot(hi, m)


def _each(fn, *lists):
    return [fn(*args) for args in zip(*lists)]


def _inv_unit_triangular(mats, base=INV_BASE):
    n = mats[0].shape[0]
    i = lax.broadcasted_iota(jnp.int32, (n, n), 0)
    j = lax.broadcasted_iota(jnp.int32, (n, n), 1)
    eye = (i == j).astype(F32)
    shift = base.bit_length() - 1
    inside = (i >> shift) == (j >> shift)
    xs = _each(lambda a: jnp.where(inside, a, 0.0), mats)
    ts = _each(lambda d: eye - d, xs)
    for _ in range(shift - 1):
        xs = _each(lambda x: _bdot(x, x), xs)
        yield
        ts = _each(lambda t, x: t + _bdot(t, x), ts, xs)
        yield
    size = base
    while size < GDN_CHUNK:
        shift += 1
        size *= 2
        wider = (i >> shift) == (j >> shift)
        off = wider & jnp.logical_not(inside)
        ots = _each(lambda a, t: _bdot(jnp.where(off, a, 0.0), t), mats, ts)
        yield
        ts = _each(lambda t, ot: t - _bdot(t, ot), ts, ots)
        yield
        inside = wider
    return ts


def _gdn_kernel(alog_ref, dtb_ref,
                q_ref, k_ref, v_ref, qc_ref, kc_ref, vc_ref,
                g_ref, gt_ref, z_ref, ng_ref,
                o_ref,
                qs_ref, ks_ref, vs_ref, gs_ref, gts_ref, gth_ref,
                wqf_ref, wqb_ref, u0_ref, intra_ref, ket_ref, gef_ref, geb_ref,
                of_ref, ob_ref, *, hb, l_lat, l_ctx, slots_per_group):
    c = GDN_CHUNK
    c2 = 2 * c
    lt = l_lat + l_ctx
    n_ctx = l_ctx // c
    nt = lt // c
    hgrp = pl.program_id(1)

    i2 = lax.broadcasted_iota(jnp.int32, (c2, c2), 0)
    j2 = lax.broadcasted_iota(jnp.int32, (c2, c2), 1)
    same_blk = (i2 & c) == (j2 & c)
    sgn_i = jnp.where(i2 < c, 1, -1)
    sgn_j = jnp.where(j2 < c, 1, -1)
    incl = same_blk & ((j2 - i2) * sgn_i <= 0)
    strict = same_blk & ((j2 - i2) * sgn_i < 0)
    tri_row = (same_blk & ((i2 - j2) * sgn_j <= 0)).astype(F32)
    eye_b = (i2 == j2).astype(BF16)
    lane_lt_c = j2 < c

    for hh in range(hb):
        head = hgrp * hb + hh
        lane0 = hh * HEAD_DIM
        for (src_c, src_l, dst) in ((qc_ref, q_ref, qs_ref), (kc_ref, k_ref, ks_ref), (vc_ref, v_ref, vs_ref)):
            dst[hh, 0:l_ctx, :] = src_c[0, :, lane0:lane0 + HEAD_DIM]
            dst[hh, l_ctx:lt, :] = src_l[0, :, lane0:lane0 + HEAD_DIM]

        a_f, a_b = alog_ref[head], alog_ref[GDN_HEADS + head]
        d_f, d_b = dtb_ref[head], dtb_ref[GDN_HEADS + head]
        comp = lax.broadcasted_iota(jnp.int32, (1, 4), 1)
        neg_a = -jnp.exp(jnp.where(comp == 2, a_f, a_b))
        dtb = jnp.where(comp == 2, d_f, d_b)
        graw = g_ref[0, hh]
        gcol = jnp.where(comp < 2, _sigmoid(graw), neg_a * _softplus(graw + dtb))
        gs_ref[hh] = jnp.zeros(gs_ref.shape[1:], F32)
        gs_ref[hh, :, 0:4] = gcol
        compr = lax.broadcasted_iota(jnp.int32, (nt * 8, 1), 0) & 7
        neg_ar = -jnp.exp(jnp.where(compr == 2, a_f, a_b))
        dtbr = jnp.where(compr == 2, d_f, d_b)
        grow_raw = gt_ref[0, hh]
        grow = jnp.where(compr < 2, _sigmoid(grow_raw), neg_ar * _softplus(grow_raw + dtbr))
        gts_ref[hh] = jnp.zeros(gts_ref.shape[1:], F32)
        gts_ref[hh, :, 0:c] = grow
        gth_ref[hh] = jnp.zeros(gth_ref.shape[1:], F32)
        gth_ref[hh, :, c:c2] = grow

    spg = slots_per_group
    heads = list(range(hb))

    def bwd_chunk(i):
        return jnp.where(i < n_ctx, n_ctx - 1 - i, nt + n_ctx - 1 - i)

    def slot_load(i, hh):
        cbk = bwd_chunk(i)
        rf = pl.multiple_of(i * c, c)
        rb = pl.multiple_of(cbk * c, c)
        two = lambda ref: jnp.concatenate([ref[hh, pl.ds(rf, c), :], ref[hh, pl.ds(rb, c), :]], axis=0).astype(F32)
        g_r = (gts_ref[hh, pl.ds(pl.multiple_of(i * 8, 8), 8), :]
               + gth_ref[hh, pl.ds(pl.multiple_of(cbk * 8, 8), 8), :])
        return two(ks_ref), two(vs_ref), two(qs_ref), two(gs_ref), g_r

    def slot_stages(loaded):
        k2, v2, q2, g_c, g_r = [list(col) for col in zip(*loaded)]
        rcs = _each(lambda g: _dot_split_lhs(g, tri_row), g_r)
        yield
        k2b = _each(lambda x: x.astype(BF16), k2)
        kk = _each(_dot_nt, k2b, k2b)
        qk = _each(lambda q, kb: _dot_nt(q.astype(BF16), kb), q2, k2b)
        yield
        g_row = _each(lambda g: jnp.where(lane_lt_c[0:1, :], g[2:3, :], g[3:4, :]), g_r)
        gcc = _each(lambda g: jnp.sum(jnp.where(incl, g, 0.0), axis=1, keepdims=True), g_row)
        gcr = _each(lambda r: jnp.where(lane_lt_c[0:1, :], r[2:3, :], r[3:4, :]), rcs)
        beta = _each(lambda g: jnp.concatenate([g[0:c, 0:1], g[c:c2, 1:2]], axis=0), g_c)
        gend = _each(lambda s: jnp.concatenate([jnp.broadcast_to(s[c - 1:c, :], (c, 1)),
                                                jnp.broadcast_to(s[c:c + 1, :], (c, 1))], axis=0), gcc)
        decay = _each(lambda gc, gr: jnp.exp(jnp.where(incl, gc - gr, -jnp.inf)), gcc, gcr)
        a_mat = _each(lambda m, d, b: jnp.where(strict, m * d, 0.0) * b, kk, decay, beta)
        t_f32 = yield from _inv_unit_triangular(a_mat)
        t_mat = _each(lambda t: t.astype(BF16), t_f32)
        egc = _each(jnp.exp, gcc)
        u0 = _each(lambda t, v, b: _dot(t, (v * b).astype(BF16)), t_mat, v2, beta)
        w = _each(lambda t, k, b, e: _dot(t, (k * (b * e)).astype(BF16)), t_mat, k2, beta, egc)
        ket = _each(lambda k, ge_, gc: _dot_nt(eye_b, (k * jnp.exp(ge_ - gc)).astype(BF16)), k2, gend, gcc)
        yield
        qd = _each(lambda q, e: (q * e).astype(BF16), q2, egc)
        intra = _each(lambda m, d: (m * d).astype(BF16), qk, decay)
        ge = _each(jnp.exp, gend)
        return [(jnp.concatenate([w_[0:c].astype(BF16), qd_[0:c]], axis=0),
                 jnp.concatenate([w_[c:c2].astype(BF16), qd_[c:c2]], axis=0), u0_, in_,
                 jnp.concatenate([jnp.where(lane_lt_c, kt, 0.0), jnp.where(lane_lt_c, 0.0, kt)], axis=0).astype(BF16),
                 jnp.broadcast_to(g_[0:1, :], (8, HEAD_DIM)), jnp.broadcast_to(g_[c:c + 1, :], (8, HEAD_DIM)))
                for w_, qd_, u0_, in_, kt, g_ in zip(w, qd, u0, intra, ket, ge)]

    def transform_group(g):
        par = lax.rem(g, 2)
        jobs = [(s, hh) for s in range(spg) for hh in heads]
        results = yield from slot_stages([slot_load(g * spg + s, hh) for s, hh in jobs])
        for (s, hh), (wq_f, wq_b, u0, intra, ket, ge_f, ge_b) in zip(jobs, results):
            wqf_ref[par, hh, s] = wq_f
            wqb_ref[par, hh, s] = wq_b
            u0_ref[par, hh, s] = u0
            intra_ref[par, hh, s] = intra
            ket_ref[par, hh, s] = ket
            gef_ref[par, hh, s] = ge_f
            geb_ref[par, hh, s] = ge_b

    def recur_group(g, states):
        par = lax.rem(g, 2)
        sf, sb = list(states[0::2]), list(states[1::2])
        for s in range(spg):
            i = g * spg + s
            rf = pl.multiple_of(i * c, c)
            rb = pl.multiple_of(bwd_chunk(i) * c, c)
            r_f = _each(lambda hh, st: _dot(wqf_ref[par, hh, s], st.astype(BF16)), heads, sf)
            r_b = _each(lambda hh, st: _dot(wqb_ref[par, hh, s], st.astype(BF16)), heads, sb)
            u0 = _each(lambda hh: u0_ref[par, hh, s], heads)
            yield
            u2 = _each(lambda u, f, b_: jnp.concatenate([u[0:c] - f[0:c], u[c:c2] - b_[0:c]], axis=0).astype(BF16),
                       u0, r_f, r_b)
            ket = _each(lambda hh: ket_ref[par, hh, s], heads)
            df = _each(lambda kt, u: _dot(kt[0:HEAD_DIM], u), ket, u2)
            db = _each(lambda kt, u: _dot(kt[HEAD_DIM:2 * HEAD_DIM], u), ket, u2)
            iu = _each(lambda hh, u: _dot(intra_ref[par, hh, s], u), heads, u2)
            sf = _each(lambda hh, st, d: gef_ref[par, hh, s][0:1, :] * st + d, heads, sf, df)
            sb = _each(lambda hh, st, d: geb_ref[par, hh, s][0:1, :] * st + d, heads, sb, db)
            for hh in heads:
                of_ref[hh, pl.ds(rf, c), :] = r_f[hh][c:c2] + iu[hh][0:c]
                ob_ref[hh, pl.ds(rb, c), :] = r_b[hh][c:c2] + iu[hh][c:c2]
            yield
        return tuple(x for pair in zip(sf, sb) for x in pair)

    def drive(transform, recur, transforms_per_recur=1):
        states = None
        while transform is not None or recur is not None:
            if transform is not None:
                for _ in range(transforms_per_recur):
                    try:
                        next(transform)
                    except StopIteration:
                        transform = None
                        break
            if recur is not None:
                try:
                    next(recur)
                except StopIteration as stop:
                    states, recur = stop.value, None
        return states

    n_groups = nt // spg
    zero_state = tuple(jnp.zeros((HEAD_DIM, HEAD_DIM), F32) for _ in range(2 * hb))
    drive(transform_group(0), None)
    states = lax.fori_loop(0, n_groups - 1,
                           lambda g, st: drive(transform_group(g + 1), recur_group(g, st)), zero_state)
    drive(None, recur_group(n_groups - 1, states))

    ng = ng_ref[...]
    blk = 256
    for hh in range(hb):
        lane0 = hh * HEAD_DIM

        def fin_body(i, carry):
            r = pl.multiple_of(i * blk, blk)
            o = of_ref[hh, pl.ds(l_ctx + r, blk), :] + ob_ref[hh, pl.ds(l_ctx + r, blk), :]
            y = o * lax.rsqrt(jnp.mean(o * o, axis=-1, keepdims=True) + LN_EPS) * ng
            z = z_ref[0, pl.ds(r, blk), lane0:lane0 + HEAD_DIM].astype(F32)
            o_ref[0, pl.ds(r, blk), lane0:lane0 + HEAD_DIM] = (y * _silu(z)).astype(BF16)
            return carry

        lax.fori_loop(0, l_lat // blk, fin_body, 0)


def _gdn_mixer(qkv_lat, qkv_ctx, z_lat, gate_cols, gate_rows, a_log, dt_bias, norm_g, hb):
    b, l_lat, _ = qkv_lat.shape
    l_ctx = qkv_ctx.shape[1]
    lt = l_lat + l_ctx
    nt = lt // GDN_CHUNK
    hw = hb * HEAD_DIM
    ngrp = GDN_HEADS // hb
    c2 = 2 * GDN_CHUNK
    spg = 6
    assert nt % spg == 0 and nt // spg >= 2 and GDN_HEADS % hb == 0

    def seq_spec(length, part):
        return pl.BlockSpec((1, length, hw), lambda i, j, *_: (i, 0, part * ngrp + j))

    grid_spec = pltpu.PrefetchScalarGridSpec(
        num_scalar_prefetch=2,
        grid=(b, ngrp),
        in_specs=[seq_spec(l_lat, 0), seq_spec(l_lat, 1), seq_spec(l_lat, 2),
                  seq_spec(l_ctx, 0), seq_spec(l_ctx, 1), seq_spec(l_ctx, 2),
                  pl.BlockSpec((1, hb, lt, 4), lambda i, j, *_: (i, j, 0, 0)),
                  pl.BlockSpec((1, hb, nt * 8, GDN_CHUNK), lambda i, j, *_: (i, j, 0, 0)),
                  pl.BlockSpec((1, l_lat, hw), lambda i, j, *_: (i, 0, j)),
                  pl.BlockSpec((1, HEAD_DIM), lambda i, j, *_: (0, 0))],
        out_specs=pl.BlockSpec((1, l_lat, hw), lambda i, j, *_: (i, 0, j)),
        scratch_shapes=[
            pltpu.VMEM((hb, lt, HEAD_DIM), BF16),
            pltpu.VMEM((hb, lt, HEAD_DIM), BF16),
            pltpu.VMEM((hb, lt, HEAD_DIM), BF16),
            pltpu.VMEM((hb, lt, HEAD_DIM), F32),
            pltpu.VMEM((hb, nt * 8, c2), F32),
            pltpu.VMEM((hb, nt * 8, c2), F32),
            pltpu.VMEM((2, hb, spg, c2, HEAD_DIM), BF16),
            pltpu.VMEM((2, hb, spg, c2, HEAD_DIM), BF16),
            pltpu.VMEM((2, hb, spg, c2, HEAD_DIM), F32),
            pltpu.VMEM((2, hb, spg, c2, c2), BF16),
            pltpu.VMEM((2, hb, spg, 2 * HEAD_DIM, c2), BF16),
            pltpu.VMEM((2, hb, spg, 8, HEAD_DIM), F32),
            pltpu.VMEM((2, hb, spg, 8, HEAD_DIM), F32),
            pltpu.VMEM((hb, lt, HEAD_DIM), F32),
            pltpu.VMEM((hb, lt, HEAD_DIM), F32),
        ])
    kern = functools.partial(_gdn_kernel, hb=hb, l_lat=l_lat, l_ctx=l_ctx, slots_per_group=spg)
    return pl.pallas_call(
        kern,
        grid_spec=grid_spec,
        out_shape=jax.ShapeDtypeStruct((b, l_lat, GDN_HEADS * HEAD_DIM), BF16),
        compiler_params=_params(("arbitrary", "arbitrary")),
        name="gdn_mixer",
    )(a_log, dt_bias, qkv_lat, qkv_lat, qkv_lat, qkv_ctx, qkv_ctx, qkv_ctx,
      gate_cols, gate_rows, z_lat, norm_g.reshape(1, HEAD_DIM))


def _ret_kernel(lg_ref, q_ref, k_ref, v_ref, gate_ref, qc_ref, kc_ref, vc_ref, cos_ref, sin_ref, ng_ref,
                o_ref, q_s, k_s, rf_s, rb_s, *, l_lat, l_ctx):
    c = RET_CHUNK
    n_lat = l_lat // c
    n_ctx = l_ctx // c
    head = pl.program_id(1)
    lg_f = lg_ref[head]
    lg_b = lg_ref[RET_HEADS + head]
    pos_c = lax.broadcasted_iota(jnp.int32, (c, 1), 0).astype(F32)
    ii = lax.broadcasted_iota(jnp.int32, (c, c), 0)
    jj = lax.broadcasted_iota(jnp.int32, (c, c), 1)
    dif = (ii - jj).astype(F32)
    dmat = (jnp.exp(jnp.where(ii >= jj, lg_f * dif, -jnp.inf))
            + jnp.exp(jnp.where(jj >= ii, -lg_b * dif, -jnp.inf)))
    kdec_f = jnp.exp(lg_f * (c - 1 - pos_c))
    kdec_b = jnp.exp(lg_b * pos_c)
    qdec_f = jnp.exp(lg_f * (pos_c + 1.0))
    qdec_b = jnp.exp(lg_b * (c - pos_c))
    cd_f = jnp.exp(jnp.full((1, HEAD_DIM), lg_f * c, F32))
    cd_b = jnp.exp(jnp.full((1, HEAD_DIM), lg_b * c, F32))
    e2 = lax.broadcasted_iota(jnp.int32, (HEAD_DIM, HEAD_DIM), 0)
    f2 = lax.broadcasted_iota(jnp.int32, (HEAD_DIM, HEAD_DIM), 1)
    eye_b = (e2 == f2).astype(BF16)
    kscale = HEAD_DIM ** -0.5

    def kv_state(kd, v):
        kdt = _dot_nt(eye_b, kd.astype(BF16)).astype(BF16)
        return _dot(kdt, v)

    r_f = jnp.zeros((HEAD_DIM, HEAD_DIM), F32)
    for n in range(n_ctx):
        kc = kc_ref[0, n * c:(n + 1) * c, :].astype(F32) * kscale
        r_f = cd_f * r_f + kv_state(kc * kdec_f, vc_ref[0, n * c:(n + 1) * c, :])
    r_b = jnp.zeros((HEAD_DIM, HEAD_DIM), F32)
    for n in reversed(range(n_ctx)):
        kc = kc_ref[0, n * c:(n + 1) * c, :].astype(F32) * kscale
        r_b = cd_b * r_b + kv_state(kc * kdec_b, vc_ref[0, n * c:(n + 1) * c, :])

    def rope_body(n, carry):
        r = pl.multiple_of(n * c, c)
        cs = cos_ref[pl.ds(r, c), :]
        sn = sin_ref[pl.ds(r, c), :]
        q = q_ref[0, pl.ds(r, c), :].astype(F32)
        k = k_ref[0, pl.ds(r, c), :].astype(F32) * kscale
        q_s[pl.ds(r, c), :] = q * cs + pltpu.roll(q, HEAD_DIM // 2, 1) * sn
        k_s[pl.ds(r, c), :] = k * cs + pltpu.roll(k, HEAD_DIM // 2, 1) * sn
        return carry

    lax.fori_loop(0, n_lat, rope_body, 0)

    chunks = list(range(n_lat))
    rows = [slice(n * c, (n + 1) * c) for n in chunks]
    k_c = _each(lambda r: k_s[r, :], rows)
    v_c = _each(lambda r: v_ref[0, r, :], rows)
    kft = _each(lambda k: _dot_nt(eye_b, (k * kdec_f).astype(BF16)).astype(BF16), k_c)
    kbt = _each(lambda k: _dot_nt(eye_b, (k * kdec_b).astype(BF16)).astype(BF16), k_c)
    kvf = _each(_dot, kft, v_c)
    kvb = _each(_dot, kbt, v_c)
    r = r_f
    for n in chunks:
        rf_s[n] = r
        r = cd_f * r + kvf[n]
    r = r_b
    for n in reversed(chunks):
        rb_s[n] = r
        r = cd_b * r + kvb[n]

    ng = ng_ref[...]
    group = 4
    for g0 in range(0, n_lat, group):
        ids = chunks[g0:g0 + group]
        rws = rows[g0:g0 + group]
        q_c = _each(lambda r_: q_s[r_, :], rws)
        kb_c = _each(lambda r_: k_s[r_, :].astype(BF16), rws)
        s = _each(lambda q, kb: _dot_nt(q.astype(BF16), kb), q_c, kb_c)
        att = _each(lambda s_: (s_ * dmat).astype(BF16), s)
        o = _each(lambda a, r_: _dot(a, v_ref[0, r_, :]), att, rws)
        qd = _each(lambda q: jnp.concatenate([(q * qdec_f).astype(BF16), (q * qdec_b).astype(BF16)], axis=1), q_c)
        st = _each(lambda n: jnp.concatenate([rf_s[n], rb_s[n]], axis=0).astype(BF16), ids)
        o = _each(lambda o_, qd_, st_: o_ + _dot(qd_, st_), o, qd, st)
        for r_, o_ in zip(rws, o):
            y = _ln_rows(o_) * ng
            g = gate_ref[0, r_, :].astype(F32)
            o_ref[0, r_, :] = (y * _silu(g)).astype(BF16)


def _ret_mixer(ret_lat, ret_ctx, log_gamma, norm_g, cos_t, sin_t):
    b, l_lat, _ = ret_lat.shape
    l_ctx = ret_ctx.shape[1]
    n_lat = l_lat // RET_CHUNK

    def seq_spec(length, part):
        return pl.BlockSpec((1, length, HEAD_DIM), lambda i, j, *_: (i, 0, part * RET_HEADS + j))

    grid_spec = pltpu.PrefetchScalarGridSpec(
        num_scalar_prefetch=1,
        grid=(b, RET_HEADS),
        in_specs=[seq_spec(l_lat, 0), seq_spec(l_lat, 1), seq_spec(l_lat, 2), seq_spec(l_lat, 3),
                  seq_spec(l_ctx, 0), seq_spec(l_ctx, 1), seq_spec(l_ctx, 2),
                  pl.BlockSpec((l_lat, HEAD_DIM), lambda i, j, *_: (0, 0)),
                  pl.BlockSpec((l_lat, HEAD_DIM), lambda i, j, *_: (0, 0)),
                  pl.BlockSpec((1, HEAD_DIM), lambda i, j, *_: (0, j))],
        out_specs=pl.BlockSpec((1, l_lat, HEAD_DIM), lambda i, j, *_: (i, 0, j)),
        scratch_shapes=[pltpu.VMEM((l_lat, HEAD_DIM), F32),
                        pltpu.VMEM((l_lat, HEAD_DIM), F32),
                        pltpu.VMEM((n_lat, HEAD_DIM, HEAD_DIM), F32),
                        pltpu.VMEM((n_lat, HEAD_DIM, HEAD_DIM), F32)])
    kern = functools.partial(_ret_kernel, l_lat=l_lat, l_ctx=l_ctx)
    return pl.pallas_call(
        kern,
        grid_spec=grid_spec,
        out_shape=jax.ShapeDtypeStruct((b, l_lat, RET_HEADS * HEAD_DIM), BF16),
        compiler_params=_params(("arbitrary", "arbitrary")),
        name="ret_mixer",
    )(log_gamma.reshape(-1), ret_lat, ret_lat, ret_lat, ret_lat, ret_ctx, ret_ctx, ret_ctx,
      cos_t, sin_t, norm_g.reshape(1, -1))


def _rope_tables(l_lat):
    rows = l_lat // GRID_W
    row = jnp.repeat(jnp.arange(rows, dtype=F32), GRID_W)
    col = jnp.tile(jnp.arange(GRID_W, dtype=F32), rows)
    quarter = HEAD_DIM // 4
    inv = ROPE_THETA ** (-jnp.arange(quarter, dtype=F32) / quarter)
    ang = jnp.concatenate([row[:, None] * inv, col[:, None] * inv], -1)
    cos, sin = jnp.cos(ang), jnp.sin(ang)
    return jnp.concatenate([cos, cos], -1), jnp.concatenate([-sin, sin], -1)


def _top_rows(vals, k):
    n = vals.shape[0]
    idx = lax.broadcasted_iota(jnp.int32, vals.shape, 0)
    taken = jnp.zeros(vals.shape, jnp.int32)
    firsts = []
    for _ in range(k):
        live = jnp.where(taken == 0, vals, -jnp.inf)
        top = jnp.max(live, axis=0, keepdims=True)
        cand = jnp.where((live == top) & (taken == 0), idx, n)
        first = jnp.min(cand, axis=0, keepdims=True)
        taken = taken + (idx == first).astype(jnp.int32)
        firsts.append(first)
    return firsts, taken


def _outproj_kernel(a_ref, r_ref, x_ref, wa_ref, wr_ref, g1_ref, sh2_ref, sc2_ref, lg_ref, lb_ref,
                    wrt_ref, rb_ref, x1_ref, ha_ref, hb_ref, ek_ref, rk_ref, wk_ref, cnt_ref, carry_ref):
    y = _dot(a_ref[0], wa_ref[...]) + _dot(r_ref[0], wr_ref[...])
    x1 = _ln_rows(DEEPNORM_ALPHA * x_ref[0] + g1_ref[0] * y) * lg_ref[...] + lb_ref[...]
    x1_ref[0] = x1
    h = _ln_rows(x1) * (1.0 + sc2_ref[0]) + sh2_ref[0]
    for ref, part in zip((ha_ref, hb_ref), _pack_rows(h)):
        ref[0] = part
    s = _sigmoid(_dot_nt(wrt_ref[...], h, HIGHEST))
    sb = s + rb_ref[...]
    tm = s.shape[1]
    per = N_EXPERTS // N_GROUPS
    sub = lax.broadcasted_iota(jnp.int32, (per, tm), 0)
    gs_rows = []
    for g in range(N_GROUPS):
        blk = sb[g * per:(g + 1) * per, :]
        m1 = jnp.max(blk, axis=0, keepdims=True)
        first = jnp.min(jnp.where(blk == m1, sub, per), axis=0, keepdims=True)
        m2 = jnp.max(jnp.where(sub == first, -jnp.inf, blk), axis=0, keepdims=True)
        gs_rows.append(m1 + m2)
    gscore = jnp.concatenate(gs_rows, axis=0)
    _, gtaken = _top_rows(gscore, TOPK_GROUPS)
    emask = jnp.concatenate([jnp.broadcast_to(gtaken[g:g + 1, :], (per, tm)) for g in range(N_GROUPS)], axis=0)
    masked = jnp.where(emask > 0, sb, -jnp.inf)
    firsts, taken = _top_rows(masked, TOP_K)
    first_step = (pl.program_id(0) == 0) & (pl.program_id(1) == 0)

    @pl.when(first_step)
    def _():
        carry_ref[...] = jnp.zeros(carry_ref.shape, F32)

    sel_f = taken.astype(F32)
    ti = lax.broadcasted_iota(jnp.int32, (tm, tm), 0)
    tj = lax.broadcasted_iota(jnp.int32, (tm, tm), 1)
    rank = _dot(sel_f.astype(BF16), (ti < tj).astype(BF16)) + carry_ref[:, 0:1]
    carry = carry_ref[...] + jnp.sum(sel_f, axis=1, keepdims=True)
    carry_ref[...] = carry
    cnt_ref[...] = carry
    eidx = lax.broadcasted_iota(jnp.int32, (N_EXPERTS, tm), 0)
    picked = []
    for k in range(TOP_K):
        hit = eidx == firsts[k]
        pick = lambda v: jnp.sum(jnp.where(hit, v, 0.0), axis=0, keepdims=True)
        ek_ref[k:k + 1, :] = firsts[k]
        rk_ref[k:k + 1, :] = pick(rank).astype(jnp.int32)
        picked.append(pick(s))
    total = picked[0]
    for k in range(1, TOP_K):
        total = total + picked[k]
    for k in range(TOP_K):
        wk_ref[k:k + 1, :] = picked[k] / total * ROUTED_SCALE


def _outproj(a_lat, r_lat, x, w_a, w_r, g1, sh2, sc2, ln_g, ln_b, w_router_t, router_bias, tm, b0):
    b, l, half = a_lat.shape
    d = x.shape[-1]
    nl = l // tm
    row = lambda i, j: (i, j, 0)
    per_b = lambda i, j: (i + b0, 0, 0)
    const = lambda i, j: (0, 0)
    return pl.pallas_call(
        _outproj_kernel,
        grid=(b, nl),
        in_specs=[pl.BlockSpec((1, tm, half), row),
                  pl.BlockSpec((1, tm, half), row),
                  pl.BlockSpec((1, tm, d), lambda i, j: (i + b0, j, 0)),
                  pl.BlockSpec((half, d), const),
                  pl.BlockSpec((half, d), const),
                  pl.BlockSpec((1, 1, d), per_b),
                  pl.BlockSpec((1, 1, d), per_b),
                  pl.BlockSpec((1, 1, d), per_b),
                  pl.BlockSpec((1, d), const),
                  pl.BlockSpec((1, d), const),
                  pl.BlockSpec((N_EXPERTS, d), const),
                  pl.BlockSpec((N_EXPERTS, 1), const)],
        out_specs=[pl.BlockSpec((1, tm, d), row),
                   pl.BlockSpec((1, tm, d // (2 * PACK_GROUPS)), row),
                   pl.BlockSpec((1, tm, d // (2 * PACK_GROUPS)), row),
                   pl.BlockSpec((TOP_K, tm), lambda i, j: (0, i * nl + j)),
                   pl.BlockSpec((TOP_K, tm), lambda i, j: (0, i * nl + j)),
                   pl.BlockSpec((TOP_K, tm), lambda i, j: (0, i * nl + j)),
                   pl.BlockSpec((N_EXPERTS, HEAD_DIM), const)],
        out_shape=[jax.ShapeDtypeStruct((b, l, d), F32),
                   jax.ShapeDtypeStruct((b, l, d // (2 * PACK_GROUPS)), jnp.int32),
                   jax.ShapeDtypeStruct((b, l, d // (2 * PACK_GROUPS)), jnp.int32),
                   jax.ShapeDtypeStruct((TOP_K, b * l), jnp.int32),
                   jax.ShapeDtypeStruct((TOP_K, b * l), jnp.int32),
                   jax.ShapeDtypeStruct((TOP_K, b * l), F32),
                   jax.ShapeDtypeStruct((N_EXPERTS, HEAD_DIM), F32)],
        scratch_shapes=[pltpu.VMEM((N_EXPERTS, HEAD_DIM), F32)],
        compiler_params=_params(("arbitrary", "arbitrary")),
        name="outproj_router",
    )(a_lat, r_lat, x, w_a, w_r, g1, sh2, sc2, ln_g.reshape(1, d), ln_b.reshape(1, d),
      w_router_t, router_bias.reshape(N_EXPERTS, 1))


def _pack_rows(x):
    n = x.shape[1] // PACK_GROUPS
    return [_pack_bf16_pair(x[:, g * n:(g + 1) * n]) for g in range(PACK_GROUPS)]


def _unpack_rows(parts):
    cols = []
    for p in parts:
        cols += list(_unpack_bf16_pair(p))
    return cols


def _glu_ffn(cols, w_gate_up, w_down, ff):
    n = cols[0].shape[1]
    ab = _dot(cols[0], w_gate_up[0:n])
    for i in range(1, len(cols)):
        ab = ab + _dot(cols[i], w_gate_up[i * n:(i + 1) * n])
    act = (_silu(ab[:, :ff]) * ab[:, ff:]).astype(BF16)
    return _dot(act, w_down)


def _expert_kernel(te_ref, used_ref, *refs, ff):
    xs_refs, (wgu_ref, wd_ref) = refs[:PACK_GROUPS], refs[PACK_GROUPS:PACK_GROUPS + 2]
    ys_refs, (wgu_bf_ref, wd_bf_ref) = refs[PACK_GROUPS + 2:2 * PACK_GROUPS + 2], refs[2 * PACK_GROUPS + 2:]
    i = pl.program_id(0)

    @pl.when(i < used_ref[0])
    def _():
        @pl.when((i == 0) | (te_ref[i] != te_ref[jnp.maximum(i - 1, 0)]))
        def _():
            wgu_bf_ref[...] = wgu_ref[0].astype(BF16)
            wd_bf_ref[...] = wd_ref[0].astype(BF16)

        sub = EXPERT_TILE // EXPERT_SUBTILES
        blocks = [slice(s * sub, (s + 1) * sub) for s in range(EXPERT_SUBTILES)]
        wgu, wd = wgu_bf_ref[...], wd_bf_ref[...]
        cols = _each(lambda b: _unpack_rows([r[b, :] for r in xs_refs]), blocks)
        n = cols[0][0].shape[1]
        ab = _each(lambda c: _dot(c[0], wgu[0:n]), cols)
        for j in range(1, 2 * PACK_GROUPS):
            ab = _each(lambda acc, c: acc + _dot(c[j], wgu[j * n:(j + 1) * n]), ab, cols)
        act = _each(lambda a: (_silu(a[:, :ff]) * a[:, ff:]).astype(BF16), ab)
        y = _each(lambda a: _dot(a, wd), act)
        for b, y_b in zip(blocks, y):
            for ref, part in zip(ys_refs, _pack_rows(y_b)):
                ref[b, :] = part


def _expert_ffn(xs, tile_expert, used_tiles, w_gu, w_d):
    n_rows, dp = xs[0].shape
    n_e, d, ff2 = w_gu.shape
    ff = ff2 // 2
    tile = lambda i, te, used: (jnp.minimum(i, used[0] - 1), 0)
    grid_spec = pltpu.PrefetchScalarGridSpec(
        num_scalar_prefetch=2,
        grid=(n_rows // EXPERT_TILE,),
        in_specs=[pl.BlockSpec((EXPERT_TILE, dp), tile)] * PACK_GROUPS
                 + [pl.BlockSpec((1, d, ff2), lambda i, te, used: (te[i], 0, 0)),
                    pl.BlockSpec((1, ff, d), lambda i, te, used: (te[i], 0, 0))],
        out_specs=[pl.BlockSpec((EXPERT_TILE, dp), tile)] * PACK_GROUPS,
        scratch_shapes=[pltpu.VMEM((d, ff2), BF16), pltpu.VMEM((ff, d), BF16)])
    return pl.pallas_call(
        functools.partial(_expert_kernel, ff=ff),
        grid_spec=grid_spec,
        out_shape=[jax.ShapeDtypeStruct((n_rows, dp), jnp.int32)] * PACK_GROUPS,
        compiler_params=_params(("arbitrary",)),
        name="expert_ffn",
    )(tile_expert, used_tiles, *xs, w_gu, w_d)


def _combine_kernel(*refs, ff):
    h_refs, y_refs = refs[:PACK_GROUPS], refs[PACK_GROUPS:2 * PACK_GROUPS]
    x1_ref, wk_ref, wsg_ref, wsd_ref, g2_ref, lg_ref, lb_ref, o_ref = refs[2 * PACK_GROUPS:]
    shared = _glu_ffn(_unpack_rows([r[...] for r in h_refs]), wsg_ref[...], wsd_ref[...], ff)
    wk = wk_ref[...]
    n = y_refs[0].shape[2]
    blocks = [shared[:, i * n:(i + 1) * n] for i in range(2 * PACK_GROUPS)]
    for k in range(TOP_K):
        cols = _unpack_rows([r[k] for r in y_refs])
        blocks = [acc + wk[:, k:k + 1] * c.astype(F32) for acc, c in zip(blocks, cols)]
    f = jnp.concatenate(blocks, axis=1)
    o_ref[...] = _ln_rows(DEEPNORM_ALPHA * x1_ref[...] + g2_ref[0] * f) * lg_ref[...] + lb_ref[...]


def _moe_combine(h, x1, y_sel, wk_tok, w_sg, w_sd, g2, ln_g, ln_b, tm, b0, l, prev_out):
    t_part, d = x1.shape
    t = g2.shape[0] * l
    dp = h[0].shape[1]
    tiles_per_batch = l // tm
    off = b0 * tiles_per_batch
    ff = w_sd.shape[0]
    row = lambda i: (i, 0)
    const = lambda i: (0, 0)
    in_specs = ([pl.BlockSpec((tm, dp), row)] * PACK_GROUPS
                + [pl.BlockSpec((TOP_K, tm, dp), lambda i: (0, i, 0))] * PACK_GROUPS
                + [pl.BlockSpec((tm, d), row),
                   pl.BlockSpec((tm, TOP_K), row),
                   pl.BlockSpec(w_sg.shape, const),
                   pl.BlockSpec(w_sd.shape, const),
                   pl.BlockSpec((1, 1, d), lambda i: (i // tiles_per_batch + b0, 0, 0)),
                   pl.BlockSpec((1, d), const),
                   pl.BlockSpec((1, d), const)])
    args = [*h, *y_sel, x1, wk_tok, w_sg, w_sd, g2, ln_g.reshape(1, d), ln_b.reshape(1, d)]
    kern = functools.partial(_combine_kernel, ff=ff)
    aliases = {}
    if prev_out is not None:
        in_specs.append(pl.BlockSpec(memory_space=pl.ANY))
        aliases = {len(args): 0}
        args.append(prev_out)
        kern = lambda *refs: _combine_kernel(*refs[:-2], refs[-1], ff=ff)
    return pl.pallas_call(
        kern,
        grid=(t_part // tm,),
        in_specs=in_specs,
        out_specs=pl.BlockSpec((tm, d), lambda i: (i + off, 0)),
        out_shape=jax.ShapeDtypeStruct((t, d), F32),
        input_output_aliases=aliases,
        compiler_params=_params(("arbitrary",)),
        name="moe_combine",
    )(*args)


def _routing_tables(ek, rk, counts, n_rows):
    padded = (counts + EXPERT_TILE - 1) // EXPERT_TILE * EXPERT_TILE
    ends = jnp.cumsum(padded)
    starts = ends - padded
    experts = jnp.arange(N_EXPERTS, dtype=jnp.int32)
    start_of = jnp.sum(jnp.where(ek[None] == experts[:, None, None], starts[:, None, None], 0), axis=0)
    pos = start_of + rk
    tile_start = jnp.arange(n_rows // EXPERT_TILE, dtype=jnp.int32) * EXPERT_TILE
    tile_expert = jnp.minimum(jnp.sum(ends[None, :] <= tile_start[:, None], axis=1), N_EXPERTS - 1).astype(jnp.int32)
    used_tiles = (ends[-1:] // EXPERT_TILE).astype(jnp.int32)
    return pos.astype(jnp.int32), tile_expert, used_tiles


def _sc_mesh():
    return plsc.VectorSubcoreMesh(core_axis_name="core", subcore_axis_name="subcore")


def _dispatch_rows(rows, pos, n_rows):
    t, dp = rows.shape
    n_k = pos.shape[0]

    @functools.partial(pl.kernel, mesh=_sc_mesh(), out_type=jax.ShapeDtypeStruct((n_rows, dp), rows.dtype),
                       scratch_types=[])
    def scatter(rows_hbm, pos_hbm, out_hbm):
        def body(rows_vmem, idx_vmem):
            pltpu.sync_copy(rows_vmem, out_hbm.at[idx_vmem.at[0]])

        pltpu.emit_pipeline(
            body,
            grid=(t // SC_WINDOW, n_k),
            in_specs=[pl.BlockSpec((SC_WINDOW, dp), lambda i, k: (i, 0)),
                      pl.BlockSpec((1, SC_WINDOW), lambda i, k: (k, i))],
            out_specs=[],
            core_axis_name=("core", "subcore"),
            dimension_semantics=(pltpu.PARALLEL, pltpu.ARBITRARY),
        )(rows_hbm, pos_hbm)

    return scatter(rows, pos)


def _gather_rows(table, pos):
    n_k, t = pos.shape
    dp = table.shape[1]

    @functools.partial(pl.kernel, mesh=_sc_mesh(), out_type=jax.ShapeDtypeStruct((n_k * t, dp), table.dtype),
                       scratch_types=[])
    def gather(table_hbm, pos_hbm, out_hbm):
        def body(idx_vmem, out_vmem):
            pltpu.sync_copy(table_hbm.at[idx_vmem.at[0]], out_vmem)

        pltpu.emit_pipeline(
            body,
            grid=(n_k * t // SC_WINDOW,),
            in_specs=[pl.BlockSpec((1, SC_WINDOW), lambda i: (0, i))],
            out_specs=[pl.BlockSpec((SC_WINDOW, dp), lambda i: (i, 0))],
            core_axis_name=("core", "subcore"),
            dimension_semantics=(pltpu.PARALLEL,),
        )(pos_hbm, out_hbm)

    return gather(table, pos.reshape(1, n_k * t)).reshape(n_k, t, dp)


def _routed_moe(h, x1, ek, rk, wk, cnt, w_sg, w_sd, w_gu, w_d, g2, ln_g, ln_b, b0, l, prev_out):
    t = x1.shape[0]
    n_rows = t * TOP_K + N_EXPERTS * EXPERT_TILE
    pos, tile_expert, used_tiles = _routing_tables(ek, rk, cnt[:, 0].astype(jnp.int32), n_rows)
    xs = [_dispatch_rows(rows, pos, n_rows) for rows in h]
    ys = _expert_ffn(xs, tile_expert, used_tiles, w_gu, w_d)
    y_sel = [_gather_rows(table, pos) for table in ys]
    return _moe_combine(h, x1, y_sel, wk.T, w_sg, w_sd, g2, ln_g, ln_b, 256, b0, l, prev_out)


def _gate_layouts(g_lat, g_ctx):
    g = jnp.concatenate([g_ctx, g_lat], axis=1)[..., :4 * GDN_HEADS]
    b, lt, _ = g.shape
    g = g.reshape(b, lt, 4, GDN_HEADS)
    cols = jnp.transpose(g, (0, 3, 1, 2))
    nt = lt // GDN_CHUNK
    rows = jnp.transpose(g.reshape(b, nt, GDN_CHUNK, 4, GDN_HEADS), (0, 4, 1, 3, 2))
    rows = jnp.pad(rows, ((0, 0), (0, 0), (0, 0), (0, 4), (0, 0))).reshape(b, GDN_HEADS, nt * 8, GDN_CHUNK)
    return cols, rows


def kernel(x, c, ctx, c_ctx, w_mod, b_mod, w_in, conv_w, gdn_a_log, gdn_dt_bias, gdn_norm_g, ret_log_gamma,
           ret_norm_g, w_out, ln1_g, ln1_b, w_router, router_bias, w_gate_up, w_down, w_shared_gate_up,
           w_shared_down, ln2_g, ln2_b):
    b, l, d = x.shape
    lc = ctx.shape[1]
    gw = GDN_HEADS * HEAD_DIM

    n_mod = -(-(b + 1) // 8) * 8
    cc = jnp.zeros((n_mod, d), F32).at[:b].set(c).at[b].set(c_ctx)
    mod = _modulation(cc, w_mod[0], b_mod[0])
    sh1, sc1, g1, sh2, sc2, g2 = [mod[:b, i * d:(i + 1) * d].reshape(b, 1, d) for i in range(6)]
    csh1 = jnp.broadcast_to(mod[b, 0:d].reshape(1, 1, d), (b, 1, d))
    csc1 = jnp.broadcast_to(mod[b, d:2 * d].reshape(1, 1, d), (b, 1, d))

    w = w_in[0]
    o_gate = 3 * gw + gw
    o_ret = o_gate + 4 * GDN_HEADS
    w_main = jnp.concatenate([w[:, :o_gate], w[:, o_ret:]], axis=1).astype(BF16)
    w_gate = w[:, o_gate:o_ret]
    cos_t, sin_t = _rope_tables(l)
    wo = w_out[0].astype(BF16)
    w_sg, w_sd = w_shared_gate_up[0].astype(BF16), w_shared_down[0].astype(BF16)
    w_router_t = w_router[0].T

    nb = b // BATCH_PARTS
    out = None
    for part in range(BATCH_PARTS):
        b0 = part * nb
        qkv_lat, z_lat, ret_lat, g_lat = _inproj(x, sh1, sc1, w_main, w_gate, conv_w[0], 512, b0, nb)
        qkv_ctx, _, ret_ctx, g_ctx = _inproj(ctx, csh1, csc1, w_main, w_gate, conv_w[0], lc, b0, nb)
        gate_cols, gate_rows = _gate_layouts(g_lat, g_ctx)
        a_lat = _gdn_mixer(qkv_lat, qkv_ctx, z_lat, gate_cols, gate_rows, gdn_a_log[0],
                           gdn_dt_bias[0], gdn_norm_g[0], hb=2)
        r_lat = _ret_mixer(ret_lat, ret_ctx, ret_log_gamma[0], ret_norm_g[0], cos_t, sin_t)
        x1, h_a, h_b, ek, rk, wk, cnt = _outproj(a_lat, r_lat, x, wo[:gw], wo[gw:], g1, sh2, sc2, ln1_g[0], ln1_b[0],
                                                 w_router_t, router_bias[0], 512, b0)
        h = [arr.reshape(nb * l, arr.shape[-1]) for arr in (h_a, h_b)]
        out = _routed_moe(h, x1.reshape(nb * l, d), ek, rk, wk, cnt, w_sg, w_sd, w_gate_up[0], w_down[0],
                          g2, ln2_g[0], ln2_b[0], b0, l, out)
    return out.reshape(b, l, d)
```

```python
import functools

import jax
import jax.numpy as jnp
from jax import lax
from jax.experimental import pallas as pl
from jax.experimental.pallas import tpu as pltpu
from jax.experimental.pallas import tpu_sc as plsc

F32 = jnp.float32
BF16 = jnp.bfloat16

HEAD_DIM = 128
GDN_HEADS = 4
RET_HEADS = 4
CONV_W = 5
CONV_ROWS = 128
HALO = 16
GDN_CHUNK = 64
INV_BASE = 16
RET_CHUNK = 256
GRID_W = 64
ROPE_THETA = 10000.0
N_EXPERTS = 64
TOP_K = 8
N_GROUPS = 8
TOPK_GROUPS = 4
ROUTED_SCALE = 2.5
SC_WINDOW = 128
PACK_GROUPS = 2
BATCH_PARTS = 2
EXPERT_SUBTILES = 4
EXPERT_TILE = 512
LN_EPS = 1e-6
DEPTH = 1
DEEPNORM_ALPHA = (2 * DEPTH) ** 0.25

VMEM_LIMIT = 56 * 1024 * 1024
HIGHEST = lax.Precision.HIGHEST
NT_DIMS = (((1,), (1,)), ((), ()))


def _dot(a, b, precision=None):
    return jnp.dot(a, b, preferred_element_type=F32, precision=precision)


def _dot_nt(a, b, precision=None):
    return lax.dot_general(a, b, NT_DIMS, preferred_element_type=F32, precision=precision)


def _silu(x):
    return x * (1.0 / (1.0 + jnp.exp(-x)))


def _sigmoid(x):
    return 1.0 / (1.0 + jnp.exp(-x))


def _softplus(x):
    return jnp.maximum(x, 0.0) + jnp.log(1.0 + jnp.exp(-jnp.abs(x)))


def _ln_rows(x):
    mu = jnp.mean(x, axis=-1, keepdims=True)
    xc = x - mu
    var = jnp.mean(xc * xc, axis=-1, keepdims=True)
    return xc * lax.rsqrt(var + LN_EPS)


def _params(sem):
    return pltpu.CompilerParams(dimension_semantics=sem, vmem_limit_bytes=VMEM_LIMIT)


def _mod_kernel(c_ref, w_ref, b_ref, o_ref):
    o_ref[...] = _dot(_silu(c_ref[...]), w_ref[...], HIGHEST) + b_ref[...]


def _modulation(cc, w_mod, b_mod):
    rows, d = cc.shape
    n = w_mod.shape[1]
    tn = 1024
    return pl.pallas_call(
        _mod_kernel,
        grid=(n // tn,),
        in_specs=[pl.BlockSpec((rows, d), lambda j: (0, 0)),
                  pl.BlockSpec((d, tn), lambda j: (0, j)),
                  pl.BlockSpec((1, tn), lambda j: (0, j))],
        out_specs=pl.BlockSpec((rows, tn), lambda j: (0, j)),
        out_shape=jax.ShapeDtypeStruct((rows, n), F32),
        compiler_params=_params(("arbitrary",)),
        name="modulation",
    )(cc, w_mod, b_mod.reshape(1, n))


def _inproj_kernel(x_ref, xp_ref, xn_ref, sh_ref, sc_ref, wm_ref, wg_ref, cw_ref, qkv_ref, z_ref, ret_ref, g_ref,
                   p_ref):
    j, nl = pl.program_id(1), pl.num_programs(1)
    tm = x_ref.shape[1]
    modulate = lambda x: _ln_rows(x) * (1.0 + sc_ref[0]) + sh_ref[0]
    h = modulate(x_ref[0])
    hb = h.astype(BF16)
    h_prev = jnp.where(j > 0, modulate(xp_ref[0]), 0.0).astype(BF16)
    h_next = jnp.where(j < nl - 1, modulate(xn_ref[0]), 0.0).astype(BF16)
    h_ext = jnp.concatenate([h_prev, hb, h_next], axis=0)
    half = CONV_W // 2
    gw = GDN_HEADS * HEAD_DIM
    for part in range(3):
        p_ref[part] = _dot(h_ext, wm_ref[:, part * gw:(part + 1) * gw])

    def conv_part(part):
        c0 = part * gw
        p = p_ref[part]
        acc = p[HALO - half:HALO - half + tm, :] * cw_ref[0:1, c0:c0 + gw]
        for tap in range(1, CONV_W):
            acc = acc + p[HALO - half + tap:HALO - half + tap + tm, :] * cw_ref[tap:tap + 1, c0:c0 + gw]
        y = _silu(acc)
        if part < 2:
            scale = HEAD_DIM ** -0.5 if part == 0 else 1.0
            blocks = [y[:, hd * HEAD_DIM:(hd + 1) * HEAD_DIM] for hd in range(GDN_HEADS)]
            blocks = [blk * (lax.rsqrt(jnp.sum(blk * blk, axis=-1, keepdims=True) + LN_EPS) * scale) for blk in blocks]
            y = jnp.concatenate(blocks, axis=1)
        qkv_ref[0, :, c0:c0 + gw] = y.astype(BF16)

    rest = [(ref, n0) for ref in (z_ref, ret_ref) for n0 in range(0, ref.shape[-1], 512)]
    col = 3 * gw
    for idx, (ref, n0) in enumerate(rest):
        if idx < 3:
            conv_part(idx)
        ref[0, :, n0:n0 + 512] = _dot(hb, wm_ref[:, col:col + 512]).astype(BF16)
        col += 512
    h_lo = (h - hb.astype(F32)).astype(BF16)
    wg = wg_ref[...]
    wg_hi = wg.astype(BF16)
    wg_lo = (wg - wg_hi.astype(F32)).astype(BF16)
    g_ref[0] = (_dot(h_lo, wg_hi) + _dot(hb, wg_lo)) + _dot(hb, wg_hi)


def _inproj(x, sh, sc, w_main, w_gate, conv_w, tm, b0, b):
    _, l, d = x.shape
    widths = (3 * GDN_HEADS * HEAD_DIM, GDN_HEADS * HEAD_DIM, 4 * RET_HEADS * HEAD_DIM)
    n_gate = w_gate.shape[1]
    per_tile, last = tm // HALO, l // HALO - 1
    row = lambda i, j: (i, j, 0)
    const = lambda i, j: (0, 0)
    return pl.pallas_call(
        _inproj_kernel,
        grid=(b, l // tm),
        in_specs=[pl.BlockSpec((1, tm, d), lambda i, j: (i + b0, j, 0)),
                  pl.BlockSpec((1, HALO, d), lambda i, j: (i + b0, jnp.maximum(j * per_tile - 1, 0), 0)),
                  pl.BlockSpec((1, HALO, d), lambda i, j: (i + b0, jnp.minimum((j + 1) * per_tile, last), 0)),
                  pl.BlockSpec((1, 1, d), lambda i, j: (i + b0, 0, 0)),
                  pl.BlockSpec((1, 1, d), lambda i, j: (i + b0, 0, 0)),
                  pl.BlockSpec(w_main.shape, const),
                  pl.BlockSpec(w_gate.shape, const),
                  pl.BlockSpec(conv_w.shape, const)],
        out_specs=[pl.BlockSpec((1, tm, w), row) for w in widths]
                  + [pl.BlockSpec((1, tm, n_gate), row)],
        out_shape=[jax.ShapeDtypeStruct((b, l, w), BF16) for w in widths]
                  + [jax.ShapeDtypeStruct((b, l, n_gate), F32)],
        scratch_shapes=[pltpu.VMEM((3, tm + 2 * HALO, GDN_HEADS * HEAD_DIM), F32)],
        compiler_params=_params(("arbitrary", "arbitrary")),
        name="inproj",
    )(x, x, x, sh, sc, w_main, w_gate, conv_w)


def _bdot(a, b):
    return _dot(a.astype(BF16), b.astype(BF16))


def _pack_bf16_pair(x):
    n = x.shape[1] // 2
    bits = lambda v: pltpu.bitcast(v.astype(BF16).astype(F32), jnp.uint32)
    word = lax.shift_right_logical(bits(x[:, :n]), jnp.uint32(16)) | (bits(x[:, n:]) & jnp.uint32(0xFFFF0000))
    return pltpu.bitcast(word, jnp.int32)


def _unpack_bf16_pair(w):
    u = pltpu.bitcast(w, jnp.uint32)
    lo = pltpu.bitcast(lax.shift_left(u, jnp.uint32(16)), F32)
    hi = pltpu.bitcast(u & jnp.uint32(0xFFFF0000), F32)
    return lo.astype(BF16), hi.astype(BF16)


def _split3(x):
    hi = x.astype(BF16)
    r1 = x - hi.astype(F32)
    mid = r1.astype(BF16)
    lo = (r1 - mid.astype(F32)).astype(BF16)
    return hi, mid, lo


def _dot_split_lhs(x, mask01):
    m = mask01.astype(BF16)
    hi, mid, lo = _split3(x)
    return (_dot(lo, m) + _dot(mid, m)) + _dot(hi, m)


def _each(fn, *lists):
    return [fn(*args) for args in zip(*lists)]


def _inv_unit_triangular(mats, base=INV_BASE):
    n = mats[0].shape[0]
    i = lax.broadcasted_iota(jnp.int32, (n, n), 0)
    j = lax.broadcasted_iota(jnp.int32, (n, n), 1)
    eye = (i == j).astype(F32)
    shift = base.bit_length() - 1
    inside = (i >> shift) == (j >> shift)
    xs = _each(lambda a: jnp.where(inside, a, 0.0), mats)
    ts = _each(lambda d: eye - d, xs)
    for _ in range(shift - 1):
        xs = _each(lambda x: _bdot(x, x), xs)
        yield
        ts = _each(lambda t, x: t + _bdot(t, x), ts, xs)
        yield
    size = base
    while size < GDN_CHUNK:
        shift += 1
        size *= 2
        wider = (i >> shift) == (j >> shift)
        off = wider & jnp.logical_not(inside)
        ots = _each(lambda a, t: _bdot(jnp.where(off, a, 0.0), t), mats, ts)
        yield
        ts = _each(lambda t, ot: t - _bdot(t, ot), ts, ots)
        yield
        inside = wider
    return ts


def _gdn_kernel(alog_ref, dtb_ref,
                q_ref, k_ref, v_ref, qc_ref, kc_ref, vc_ref,
                gt_ref, z_ref, ng_ref,
                o_ref,
                qs_ref, ks_ref, vs_ref, gts_ref, gth_ref,
                wqf_ref, wqb_ref, u0_ref, intra_ref, ket_ref, gef_ref, geb_ref,
                of_ref, ob_ref, *, hb, l_lat, l_ctx, slots_per_group):
    c = GDN_CHUNK
    c2 = 2 * c
    lt = l_lat + l_ctx
    n_ctx = l_ctx // c
    nt = lt // c
    hgrp = pl.program_id(1)

    i2 = lax.broadcasted_iota(jnp.int32, (c2, c2), 0)
    j2 = lax.broadcasted_iota(jnp.int32, (c2, c2), 1)
    same_blk = (i2 & c) == (j2 & c)
    sgn_i = jnp.where(i2 < c, 1, -1)
    sgn_j = jnp.where(j2 < c, 1, -1)
    incl = same_blk & ((j2 - i2) * sgn_i <= 0)
    strict = same_blk & ((j2 - i2) * sgn_i < 0)
    tri_row = (same_blk & ((i2 - j2) * sgn_j <= 0)).astype(F32)
    eye_m = i2 == j2
    eye_b = eye_m.astype(BF16)
    lane_lt_c = j2 < c

    for hh in range(hb):
        head = hgrp * hb + hh
        lane0 = hh * HEAD_DIM
        for (src_c, src_l, dst) in ((qc_ref, q_ref, qs_ref), (kc_ref, k_ref, ks_ref), (vc_ref, v_ref, vs_ref)):
            dst[hh, 0:l_ctx, :] = src_c[0, :, lane0:lane0 + HEAD_DIM]
            dst[hh, l_ctx:lt, :] = src_l[0, :, lane0:lane0 + HEAD_DIM]

        a_f, a_b = alog_ref[head], alog_ref[GDN_HEADS + head]
        d_f, d_b = dtb_ref[head], dtb_ref[GDN_HEADS + head]
        compr = lax.broadcasted_iota(jnp.int32, (nt * 8, 1), 0) & 7
        neg_ar = -jnp.exp(jnp.where(compr == 2, a_f, a_b))
        dtbr = jnp.where(compr == 2, d_f, d_b)
        grow_raw = gt_ref[0, hh]
        grow = jnp.where(compr < 2, _sigmoid(grow_raw), neg_ar * _softplus(grow_raw + dtbr))
        gts_ref[hh] = jnp.zeros(gts_ref.shape[1:], F32)
        gts_ref[hh, :, 0:c] = grow
        gth_ref[hh] = jnp.zeros(gth_ref.shape[1:], F32)
        gth_ref[hh, :, c:c2] = grow

    spg = slots_per_group
    heads = list(range(hb))

    def bwd_chunk(i):
        return jnp.where(i < n_ctx, n_ctx - 1 - i, nt + n_ctx - 1 - i)

    def slot_load(i, hh):
        cbk = bwd_chunk(i)
        rf = pl.multiple_of(i * c, c)
        rb = pl.multiple_of(cbk * c, c)
        two = lambda ref: jnp.concatenate([ref[hh, pl.ds(rf, c), :], ref[hh, pl.ds(rb, c), :]], axis=0).astype(F32)
        g_r = (gts_ref[hh, pl.ds(pl.multiple_of(i * 8, 8), 8), :]
               + gth_ref[hh, pl.ds(pl.multiple_of(cbk * 8, 8), 8), :])
        return two(ks_ref), two(vs_ref), two(qs_ref), g_r

    def slot_stages(loaded):
        k2, v2, q2, g_r = [list(col) for col in zip(*loaded)]
        rcs = _each(lambda g: _dot_split_lhs(g, tri_row), g_r)
        yield
        k2b = _each(lambda x: x.astype(BF16), k2)
        kk = _each(_dot_nt, k2b, k2b)
        qk = _each(lambda q, kb: _dot_nt(q.astype(BF16), kb), q2, k2b)
        yield
        by_dir = lambda g, k: jnp.where(lane_lt_c[0:1, :], g[k:k + 1, :], g[k + 1:k + 2, :])
        lane_sum = lambda mask, row: jnp.sum(jnp.where(mask, row, 0.0), axis=1, keepdims=True)
        gcc = _each(lambda g: lane_sum(incl, by_dir(g, 2)), g_r)
        gcr = _each(lambda r: by_dir(r, 2), rcs)
        beta = _each(lambda g: lane_sum(eye_m, by_dir(g, 0)), g_r)
        gend = _each(lambda s: jnp.concatenate([jnp.broadcast_to(s[c - 1:c, :], (c, 1)),
                                                jnp.broadcast_to(s[c:c + 1, :], (c, 1))], axis=0), gcc)
        decay = _each(lambda gc, gr: jnp.exp(jnp.where(incl, gc - gr, -jnp.inf)), gcc, gcr)
        a_mat = _each(lambda m, d, b: jnp.where(strict, m * d, 0.0) * b, kk, decay, beta)
        t_f32 = yield from _inv_unit_triangular(a_mat)
        t_mat = _each(lambda t: t.astype(BF16), t_f32)
        egc = _each(jnp.exp, gcc)
        u0 = _each(lambda t, v, b: _dot(t, (v * b).astype(BF16)), t_mat, v2, beta)
        w = _each(lambda t, k, b, e: _dot(t, (k * (b * e)).astype(BF16)), t_mat, k2, beta, egc)
        ket = _each(lambda k, ge_, gc: _dot_nt(eye_b, (k * jnp.exp(ge_ - gc)).astype(BF16)), k2, gend, gcc)
        yield
        qd = _each(lambda q, e: (q * e).astype(BF16), q2, egc)
        intra = _each(lambda m, d: (m * d).astype(BF16), qk, decay)
        ge = _each(jnp.exp, gend)
        return [(jnp.concatenate([w_[0:c].astype(BF16), qd_[0:c]], axis=0),
                 jnp.concatenate([w_[c:c2].astype(BF16), qd_[c:c2]], axis=0), u0_, in_,
                 jnp.concatenate([jnp.where(lane_lt_c, kt, 0.0), jnp.where(lane_lt_c, 0.0, kt)], axis=0).astype(BF16),
                 jnp.broadcast_to(g_[0:1, :], (8, HEAD_DIM)), jnp.broadcast_to(g_[c:c + 1, :], (8, HEAD_DIM)))
                for w_, qd_, u0_, in_, kt, g_ in zip(w, qd, u0, intra, ket, ge)]

    def transform_group(g):
        par = lax.rem(g, 2)
        jobs = [(s, hh) for s in range(spg) for hh in heads]
        results = yield from slot_stages([slot_load(g * spg + s, hh) for s, hh in jobs])
        for (s, hh), (wq_f, wq_b, u0, intra, ket, ge_f, ge_b) in zip(jobs, results):
            wqf_ref[par, hh, s] = wq_f
            wqb_ref[par, hh, s] = wq_b
            u0_ref[par, hh, s] = u0
            intra_ref[par, hh, s] = intra
            ket_ref[par, hh, s] = ket
            gef_ref[par, hh, s] = ge_f
            geb_ref[par, hh, s] = ge_b

    def recur_group(g, states):
        par = lax.rem(g, 2)
        sf, sb = list(states[0::2]), list(states[1::2])
        for s in range(spg):
            i = g * spg + s
            rf = pl.multiple_of(i * c, c)
            rb = pl.multiple_of(bwd_chunk(i) * c, c)
            r_f = _each(lambda hh, st: _dot(wqf_ref[par, hh, s], st.astype(BF16)), heads, sf)
            r_b = _each(lambda hh, st: _dot(wqb_ref[par, hh, s], st.astype(BF16)), heads, sb)
            u0 = _each(lambda hh: u0_ref[par, hh, s], heads)
            yield
            u2 = _each(lambda u, f, b_: jnp.concatenate([u[0:c] - f[0:c], u[c:c2] - b_[0:c]], axis=0).astype(BF16),
                       u0, r_f, r_b)
            ket = _each(lambda hh: ket_ref[par, hh, s], heads)
            df = _each(lambda kt, u: _dot(kt[0:HEAD_DIM], u), ket, u2)
            db = _each(lambda kt, u: _dot(kt[HEAD_DIM:2 * HEAD_DIM], u), ket, u2)
            iu = _each(lambda hh, u: _dot(intra_ref[par, hh, s], u), heads, u2)
            sf = _each(lambda hh, st, d: gef_ref[par, hh, s][0:1, :] * st + d, heads, sf, df)
            sb = _each(lambda hh, st, d: geb_ref[par, hh, s][0:1, :] * st + d, heads, sb, db)
            for hh in heads:
                of_ref[hh, pl.ds(rf, c), :] = r_f[hh][c:c2] + iu[hh][0:c]
                ob_ref[hh, pl.ds(rb, c), :] = r_b[hh][c:c2] + iu[hh][c:c2]
            yield
        return tuple(x for pair in zip(sf, sb) for x in pair)

    def drive(transform, recur, transforms_per_recur=1):
        states = None
        while transform is not None or recur is not None:
            if transform is not None:
                for _ in range(transforms_per_recur):
                    try:
                        next(transform)
                    except StopIteration:
                        transform = None
                        break
            if recur is not None:
                try:
                    next(recur)
                except StopIteration as stop:
                    states, recur = stop.value, None
        return states

    n_groups = nt // spg
    zero_state = tuple(jnp.zeros((HEAD_DIM, HEAD_DIM), F32) for _ in range(2 * hb))
    drive(transform_group(0), None)
    states = lax.fori_loop(0, n_groups - 1,
                           lambda g, st: drive(transform_group(g + 1), recur_group(g, st)), zero_state)
    drive(None, recur_group(n_groups - 1, states))

    ng = ng_ref[...]
    blk = 256
    for hh in range(hb):
        lane0 = hh * HEAD_DIM

        def fin_body(i, carry):
            r = pl.multiple_of(i * blk, blk)
            o = of_ref[hh, pl.ds(l_ctx + r, blk), :] + ob_ref[hh, pl.ds(l_ctx + r, blk), :]
            y = o * lax.rsqrt(jnp.mean(o * o, axis=-1, keepdims=True) + LN_EPS) * ng
            z = z_ref[0, pl.ds(r, blk), lane0:lane0 + HEAD_DIM].astype(F32)
            o_ref[0, pl.ds(r, blk), lane0:lane0 + HEAD_DIM] = (y * _silu(z)).astype(BF16)
            return carry

        lax.fori_loop(0, l_lat // blk, fin_body, 0)


def _gdn_mixer(qkv_lat, qkv_ctx, z_lat, gate_rows, a_log, dt_bias, norm_g, hb):
    b, l_lat, _ = qkv_lat.shape
    l_ctx = qkv_ctx.shape[1]
    lt = l_lat + l_ctx
    nt = lt // GDN_CHUNK
    hw = hb * HEAD_DIM
    ngrp = GDN_HEADS // hb
    c2 = 2 * GDN_CHUNK
    spg = 6
    assert nt % spg == 0 and nt // spg >= 2 and GDN_HEADS % hb == 0

    def seq_spec(length, part):
        return pl.BlockSpec((1, length, hw), lambda i, j, *_: (i, 0, part * ngrp + j))

    grid_spec = pltpu.PrefetchScalarGridSpec(
        num_scalar_prefetch=2,
        grid=(b, ngrp),
        in_specs=[seq_spec(l_lat, 0), seq_spec(l_lat, 1), seq_spec(l_lat, 2),
                  seq_spec(l_ctx, 0), seq_spec(l_ctx, 1), seq_spec(l_ctx, 2),
                  pl.BlockSpec((1, hb, nt * 8, GDN_CHUNK), lambda i, j, *_: (i, j, 0, 0)),
                  pl.BlockSpec((1, l_lat, hw), lambda i, j, *_: (i, 0, j)),
                  pl.BlockSpec((1, HEAD_DIM), lambda i, j, *_: (0, 0))],
        out_specs=pl.BlockSpec((1, l_lat, hw), lambda i, j, *_: (i, 0, j)),
        scratch_shapes=[
            pltpu.VMEM((hb, lt, HEAD_DIM), BF16),
            pltpu.VMEM((hb, lt, HEAD_DIM), BF16),
            pltpu.VMEM((hb, lt, HEAD_DIM), BF16),
            pltpu.VMEM((hb, nt * 8, c2), F32),
            pltpu.VMEM((hb, nt * 8, c2), F32),
            pltpu.VMEM((2, hb, spg, c2, HEAD_DIM), BF16),
            pltpu.VMEM((2, hb, spg, c2, HEAD_DIM), BF16),
            pltpu.VMEM((2, hb, spg, c2, HEAD_DIM), F32),
            pltpu.VMEM((2, hb, spg, c2, c2), BF16),
            pltpu.VMEM((2, hb, spg, 2 * HEAD_DIM, c2), BF16),
            pltpu.VMEM((2, hb, spg, 8, HEAD_DIM), F32),
            pltpu.VMEM((2, hb, spg, 8, HEAD_DIM), F32),
            pltpu.VMEM((hb, lt, HEAD_DIM), F32),
            pltpu.VMEM((hb, lt, HEAD_DIM), F32),
        ])
    kern = functools.partial(_gdn_kernel, hb=hb, l_lat=l_lat, l_ctx=l_ctx, slots_per_group=spg)
    return pl.pallas_call(
        kern,
        grid_spec=grid_spec,
        out_shape=jax.ShapeDtypeStruct((b, l_lat, GDN_HEADS * HEAD_DIM), BF16),
        compiler_params=_params(("arbitrary", "arbitrary")),
        name="gdn_mixer",
    )(a_log, dt_bias, qkv_lat, qkv_lat, qkv_lat, qkv_ctx, qkv_ctx, qkv_ctx,
      gate_rows, z_lat, norm_g.reshape(1, HEAD_DIM))


def _ret_kernel(lg_ref, q_ref, k_ref, v_ref, gate_ref, qc_ref, kc_ref, vc_ref, cos_ref, sin_ref, ng_ref,
                o_ref, q_s, k_s, rf_s, rb_s, *, l_lat, l_ctx):
    c = RET_CHUNK
    n_lat = l_lat // c
    n_ctx = l_ctx // c
    head = pl.program_id(1)
    lg_f = lg_ref[head]
    lg_b = lg_ref[RET_HEADS + head]
    pos_c = lax.broadcasted_iota(jnp.int32, (c, 1), 0).astype(F32)
    ii = lax.broadcasted_iota(jnp.int32, (c, c), 0)
    jj = lax.broadcasted_iota(jnp.int32, (c, c), 1)
    dif = (ii - jj).astype(F32)
    dmat = (jnp.exp(jnp.where(ii >= jj, lg_f * dif, -jnp.inf))
            + jnp.exp(jnp.where(jj >= ii, -lg_b * dif, -jnp.inf)))
    kdec_f = jnp.exp(lg_f * (c - 1 - pos_c))
    kdec_b = jnp.exp(lg_b * pos_c)
    qdec_f = jnp.exp(lg_f * (pos_c + 1.0))
    qdec_b = jnp.exp(lg_b * (c - pos_c))
    cd_f = jnp.exp(jnp.full((1, HEAD_DIM), lg_f * c, F32))
    cd_b = jnp.exp(jnp.full((1, HEAD_DIM), lg_b * c, F32))
    e2 = lax.broadcasted_iota(jnp.int32, (HEAD_DIM, HEAD_DIM), 0)
    f2 = lax.broadcasted_iota(jnp.int32, (HEAD_DIM, HEAD_DIM), 1)
    eye_b = (e2 == f2).astype(BF16)
    kscale = HEAD_DIM ** -0.5

    def kv_state(kd, v):
        kdt = _dot_nt(eye_b, kd.astype(BF16)).astype(BF16)
        return _dot(kdt, v)

    r_f = jnp.zeros((HEAD_DIM, HEAD_DIM), F32)
    for n in range(n_ctx):
        kc = kc_ref[0, n * c:(n + 1) * c, :].astype(F32) * kscale
        r_f = cd_f * r_f + kv_state(kc * kdec_f, vc_ref[0, n * c:(n + 1) * c, :])
    r_b = jnp.zeros((HEAD_DIM, HEAD_DIM), F32)
    for n in reversed(range(n_ctx)):
        kc = kc_ref[0, n * c:(n + 1) * c, :].astype(F32) * kscale
        r_b = cd_b * r_b + kv_state(kc * kdec_b, vc_ref[0, n * c:(n + 1) * c, :])

    def rope_body(n, carry):
        r = pl.multiple_of(n * c, c)
        cs = cos_ref[pl.ds(r, c), :]
        sn = sin_ref[pl.ds(r, c), :]
        q = q_ref[0, pl.ds(r, c), :].astype(F32)
        k = k_ref[0, pl.ds(r, c), :].astype(F32) * kscale
        q_s[pl.ds(r, c), :] = q * cs + pltpu.roll(q, HEAD_DIM // 2, 1) * sn
        k_s[pl.ds(r, c), :] = k * cs + pltpu.roll(k, HEAD_DIM // 2, 1) * sn
        return carry

    lax.fori_loop(0, n_lat, rope_body, 0)

    chunks = list(range(n_lat))
    rows = [slice(n * c, (n + 1) * c) for n in chunks]
    k_c = _each(lambda r: k_s[r, :], rows)
    v_c = _each(lambda r: v_ref[0, r, :], rows)
    kft = _each(lambda k: _dot_nt(eye_b, (k * kdec_f).astype(BF16)).astype(BF16), k_c)
    kbt = _each(lambda k: _dot_nt(eye_b, (k * kdec_b).astype(BF16)).astype(BF16), k_c)
    kvf = _each(_dot, kft, v_c)
    kvb = _each(_dot, kbt, v_c)
    r = r_f
    for n in chunks:
        rf_s[n] = r
        r = cd_f * r + kvf[n]
    r = r_b
    for n in reversed(chunks):
        rb_s[n] = r
        r = cd_b * r + kvb[n]

    ng = ng_ref[...]
    group = 4
    for g0 in range(0, n_lat, group):
        ids = chunks[g0:g0 + group]
        rws = rows[g0:g0 + group]
        q_c = _each(lambda r_: q_s[r_, :], rws)
        kb_c = _each(lambda r_: k_s[r_, :].astype(BF16), rws)
        s = _each(lambda q, kb: _dot_nt(q.astype(BF16), kb), q_c, kb_c)
        att = _each(lambda s_: (s_ * dmat).astype(BF16), s)
        o = _each(lambda a, r_: _dot(a, v_ref[0, r_, :]), att, rws)
        qd = _each(lambda q: jnp.concatenate([(q * qdec_f).astype(BF16), (q * qdec_b).astype(BF16)], axis=1), q_c)
        st = _each(lambda n: jnp.concatenate([rf_s[n], rb_s[n]], axis=0).astype(BF16), ids)
        o = _each(lambda o_, qd_, st_: o_ + _dot(qd_, st_), o, qd, st)
        for r_, o_ in zip(rws, o):
            y = _ln_rows(o_) * ng
            g = gate_ref[0, r_, :].astype(F32)
            o_ref[0, r_, :] = (y * _silu(g)).astype(BF16)


def _ret_mixer(ret_lat, ret_ctx, log_gamma, norm_g, cos_t, sin_t):
    b, l_lat, _ = ret_lat.shape
    l_ctx = ret_ctx.shape[1]
    n_lat = l_lat // RET_CHUNK

    def seq_spec(length, part):
        return pl.BlockSpec((1, length, HEAD_DIM), lambda i, j, *_: (i, 0, part * RET_HEADS + j))

    grid_spec = pltpu.PrefetchScalarGridSpec(
        num_scalar_prefetch=1,
        grid=(b, RET_HEADS),
        in_specs=[seq_spec(l_lat, 0), seq_spec(l_lat, 1), seq_spec(l_lat, 2), seq_spec(l_lat, 3),
                  seq_spec(l_ctx, 0), seq_spec(l_ctx, 1), seq_spec(l_ctx, 2),
                  pl.BlockSpec((l_lat, HEAD_DIM), lambda i, j, *_: (0, 0)),
                  pl.BlockSpec((l_lat, HEAD_DIM), lambda i, j, *_: (0, 0)),
                  pl.BlockSpec((1, HEAD_DIM), lambda i, j, *_: (0, j))],
        out_specs=pl.BlockSpec((1, l_lat, HEAD_DIM), lambda i, j, *_: (i, 0, j)),
        scratch_shapes=[pltpu.VMEM((l_lat, HEAD_DIM), F32),
                        pltpu.VMEM((l_lat, HEAD_DIM), F32),
                        pltpu.VMEM((n_lat, HEAD_DIM, HEAD_DIM), F32),
                        pltpu.VMEM((n_lat, HEAD_DIM, HEAD_DIM), F32)])
    kern = functools.partial(_ret_kernel, l_lat=l_lat, l_ctx=l_ctx)
    return pl.pallas_call(
        kern,
        grid_spec=grid_spec,
        out_shape=jax.ShapeDtypeStruct((b, l_lat, RET_HEADS * HEAD_DIM), BF16),
        compiler_params=_params(("arbitrary", "arbitrary")),
        name="ret_mixer",
    )(log_gamma.reshape(-1), ret_lat, ret_lat, ret_lat, ret_lat, ret_ctx, ret_ctx, ret_ctx,
      cos_t, sin_t, norm_g.reshape(1, -1))


def _rope_tables(l_lat):
    rows = l_lat // GRID_W
    row = jnp.repeat(jnp.arange(rows, dtype=F32), GRID_W)
    col = jnp.tile(jnp.arange(GRID_W, dtype=F32), rows)
    quarter = HEAD_DIM // 4
    inv = ROPE_THETA ** (-jnp.arange(quarter, dtype=F32) / quarter)
    ang = jnp.concatenate([row[:, None] * inv, col[:, None] * inv], -1)
    cos, sin = jnp.cos(ang), jnp.sin(ang)
    return jnp.concatenate([cos, cos], -1), jnp.concatenate([-sin, sin], -1)


def _top_rows(vals, k):
    n = vals.shape[0]
    idx = lax.broadcasted_iota(jnp.int32, vals.shape, 0)
    taken = jnp.zeros(vals.shape, jnp.int32)
    firsts = []
    for _ in range(k):
        live = jnp.where(taken == 0, vals, -jnp.inf)
        top = jnp.max(live, axis=0, keepdims=True)
        cand = jnp.where((live == top) & (taken == 0), idx, n)
        first = jnp.min(cand, axis=0, keepdims=True)
        taken = taken + (idx == first).astype(jnp.int32)
        firsts.append(first)
    return firsts, taken


def _outproj_kernel(a_ref, r_ref, x_ref, wa_ref, wr_ref, g1_ref, sh2_ref, sc2_ref, lg_ref, lb_ref,
                    wrt_ref, rb_ref, x1_ref, ha_ref, hb_ref, ek_ref, rk_ref, wk_ref, cnt_ref, carry_ref):
    y = _dot(a_ref[0], wa_ref[...]) + _dot(r_ref[0], wr_ref[...])
    x1 = _ln_rows(DEEPNORM_ALPHA * x_ref[0] + g1_ref[0] * y) * lg_ref[...] + lb_ref[...]
    x1_ref[0] = x1
    h = _ln_rows(x1) * (1.0 + sc2_ref[0]) + sh2_ref[0]
    for ref, part in zip((ha_ref, hb_ref), _pack_rows(h)):
        ref[0] = part
    s = _sigmoid(_dot_nt(wrt_ref[...], h, HIGHEST))
    sb = s + rb_ref[...]
    tm = s.shape[1]
    per = N_EXPERTS // N_GROUPS
    sub = lax.broadcasted_iota(jnp.int32, (per, tm), 0)
    gs_rows = []
    for g in range(N_GROUPS):
        blk = sb[g * per:(g + 1) * per, :]
        m1 = jnp.max(blk, axis=0, keepdims=True)
        first = jnp.min(jnp.where(blk == m1, sub, per), axis=0, keepdims=True)
        m2 = jnp.max(jnp.where(sub == first, -jnp.inf, blk), axis=0, keepdims=True)
        gs_rows.append(m1 + m2)
    gscore = jnp.concatenate(gs_rows, axis=0)
    _, gtaken = _top_rows(gscore, TOPK_GROUPS)
    emask = jnp.concatenate([jnp.broadcast_to(gtaken[g:g + 1, :], (per, tm)) for g in range(N_GROUPS)], axis=0)
    masked = jnp.where(emask > 0, sb, -jnp.inf)
    firsts, taken = _top_rows(masked, TOP_K)
    first_step = (pl.program_id(0) == 0) & (pl.program_id(1) == 0)

    @pl.when(first_step)
    def _():
        carry_ref[...] = jnp.zeros(carry_ref.shape, F32)

    sel_f = taken.astype(F32)
    ti = lax.broadcasted_iota(jnp.int32, (tm, tm), 0)
    tj = lax.broadcasted_iota(jnp.int32, (tm, tm), 1)
    rank = _dot(sel_f.astype(BF16), (ti < tj).astype(BF16)) + carry_ref[:, 0:1]
    carry = carry_ref[...] + jnp.sum(sel_f, axis=1, keepdims=True)
    carry_ref[...] = carry
    cnt_ref[...] = carry
    eidx = lax.broadcasted_iota(jnp.int32, (N_EXPERTS, tm), 0)
    picked = []
    for k in range(TOP_K):
        hit = eidx == firsts[k]
        pick = lambda v: jnp.sum(jnp.where(hit, v, 0.0), axis=0, keepdims=True)
        ek_ref[k:k + 1, :] = firsts[k]
        rk_ref[k:k + 1, :] = pick(rank).astype(jnp.int32)
        picked.append(pick(s))
    total = picked[0]
    for k in range(1, TOP_K):
        total = total + picked[k]
    for k in range(TOP_K):
        wk_ref[k:k + 1, :] = picked[k] / total * ROUTED_SCALE


def _outproj(a_lat, r_lat, x, w_a, w_r, g1, sh2, sc2, ln_g, ln_b, w_router_t, router_bias, tm, b0):
    b, l, half = a_lat.shape
    d = x.shape[-1]
    nl = l // tm
    row = lambda i, j: (i, j, 0)
    per_b = lambda i, j: (i + b0, 0, 0)
    const = lambda i, j: (0, 0)
    return pl.pallas_call(
        _outproj_kernel,
        grid=(b, nl),
        in_specs=[pl.BlockSpec((1, tm, half), row),
                  pl.BlockSpec((1, tm, half), row),
                  pl.BlockSpec((1, tm, d), lambda i, j: (i + b0, j, 0)),
                  pl.BlockSpec((half, d), const),
                  pl.BlockSpec((half, d), const),
                  pl.BlockSpec((1, 1, d), per_b),
                  pl.BlockSpec((1, 1, d), per_b),
                  pl.BlockSpec((1, 1, d), per_b),
                  pl.BlockSpec((1, d), const),
                  pl.BlockSpec((1, d), const),
                  pl.BlockSpec((N_EXPERTS, d), const),
                  pl.BlockSpec((N_EXPERTS, 1), const)],
        out_specs=[pl.BlockSpec((1, tm, d), row),
                   pl.BlockSpec((1, tm, d // (2 * PACK_GROUPS)), row),
                   pl.BlockSpec((1, tm, d // (2 * PACK_GROUPS)), row),
                   pl.BlockSpec((TOP_K, tm), lambda i, j: (0, i * nl + j)),
                   pl.BlockSpec((TOP_K, tm), lambda i, j: (0, i * nl + j)),
                   pl.BlockSpec((TOP_K, tm), lambda i, j: (0, i * nl + j)),
                   pl.BlockSpec((N_EXPERTS, HEAD_DIM), const)],
        out_shape=[jax.ShapeDtypeStruct((b, l, d), F32),
                   jax.ShapeDtypeStruct((b, l, d // (2 * PACK_GROUPS)), jnp.int32),
                   jax.ShapeDtypeStruct((b, l, d // (2 * PACK_GROUPS)), jnp.int32),
                   jax.ShapeDtypeStruct((TOP_K, b * l), jnp.int32),
                   jax.ShapeDtypeStruct((TOP_K, b * l), jnp.int32),
                   jax.ShapeDtypeStruct((TOP_K, b * l), F32),
                   jax.ShapeDtypeStruct((N_EXPERTS, HEAD_DIM), F32)],
        scratch_shapes=[pltpu.VMEM((N_EXPERTS, HEAD_DIM), F32)],
        compiler_params=_params(("arbitrary", "arbitrary")),
        name="outproj_router",
    )(a_lat, r_lat, x, w_a, w_r, g1, sh2, sc2, ln_g.reshape(1, d), ln_b.reshape(1, d),
      w_router_t, router_bias.reshape(N_EXPERTS, 1))


def _pack_rows(x):
    n = x.shape[1] // PACK_GROUPS
    return [_pack_bf16_pair(x[:, g * n:(g + 1) * n]) for g in range(PACK_GROUPS)]


def _unpack_rows(parts):
    cols = []
    for p in parts:
        cols += list(_unpack_bf16_pair(p))
    return cols


def _glu_ffn(cols, w_gate_up, w_down, ff):
    n = cols[0].shape[1]
    ab = _dot(cols[0], w_gate_up[0:n])
    for i in range(1, len(cols)):
        ab = ab + _dot(cols[i], w_gate_up[i * n:(i + 1) * n])
    act = (_silu(ab[:, :ff]) * ab[:, ff:]).astype(BF16)
    return _dot(act, w_down)


def _expert_kernel(te_ref, used_ref, *refs, ff):
    xs_refs, (wgu_ref, wd_ref) = refs[:PACK_GROUPS], refs[PACK_GROUPS:PACK_GROUPS + 2]
    ys_refs, (wgu_bf_ref, wd_bf_ref) = refs[PACK_GROUPS + 2:2 * PACK_GROUPS + 2], refs[2 * PACK_GROUPS + 2:]
    i = pl.program_id(0)

    @pl.when(i < used_ref[0])
    def _():
        @pl.when((i == 0) | (te_ref[i] != te_ref[jnp.maximum(i - 1, 0)]))
        def _():
            wgu_bf_ref[...] = wgu_ref[0].astype(BF16)
            wd_bf_ref[...] = wd_ref[0].astype(BF16)

        sub = EXPERT_TILE // EXPERT_SUBTILES
        blocks = [slice(s * sub, (s + 1) * sub) for s in range(EXPERT_SUBTILES)]
        wgu, wd = wgu_bf_ref[...], wd_bf_ref[...]
        cols = _each(lambda b: _unpack_rows([r[b, :] for r in xs_refs]), blocks)
        n = cols[0][0].shape[1]
        ab = _each(lambda c: _dot(c[0], wgu[0:n]), cols)
        for j in range(1, 2 * PACK_GROUPS):
            ab = _each(lambda acc, c: acc + _dot(c[j], wgu[j * n:(j + 1) * n]), ab, cols)
        act = _each(lambda a: (_silu(a[:, :ff]) * a[:, ff:]).astype(BF16), ab)
        y = _each(lambda a: _dot(a, wd), act)
        for b, y_b in zip(blocks, y):
            for ref, part in zip(ys_refs, _pack_rows(y_b)):
                ref[b, :] = part


def _expert_ffn(xs, tile_expert, used_tiles, w_gu, w_d):
    n_rows, dp = xs[0].shape
    n_e, d, ff2 = w_gu.shape
    ff = ff2 // 2
    tile = lambda i, te, used: (jnp.minimum(i, used[0] - 1), 0)
    grid_spec = pltpu.PrefetchScalarGridSpec(
        num_scalar_prefetch=2,
        grid=(n_rows // EXPERT_TILE,),
        in_specs=[pl.BlockSpec((EXPERT_TILE, dp), tile)] * PACK_GROUPS
                 + [pl.BlockSpec((1, d, ff2), lambda i, te, used: (te[i], 0, 0)),
                    pl.BlockSpec((1, ff, d), lambda i, te, used: (te[i], 0, 0))],
        out_specs=[pl.BlockSpec((EXPERT_TILE, dp), tile)] * PACK_GROUPS,
        scratch_shapes=[pltpu.VMEM((d, ff2), BF16), pltpu.VMEM((ff, d), BF16)])
    return pl.pallas_call(
        functools.partial(_expert_kernel, ff=ff),
        grid_spec=grid_spec,
        out_shape=[jax.ShapeDtypeStruct((n_rows, dp), jnp.int32)] * PACK_GROUPS,
        compiler_params=_params(("arbitrary",)),
        name="expert_ffn",
    )(tile_expert, used_tiles, *xs, w_gu, w_d)


def _combine_kernel(*refs, ff):
    h_refs, y_refs = refs[:PACK_GROUPS], refs[PACK_GROUPS:2 * PACK_GROUPS]
    x1_ref, wk_ref, wsg_ref, wsd_ref, g2_ref, lg_ref, lb_ref, o_ref = refs[2 * PACK_GROUPS:]
    shared = _glu_ffn(_unpack_rows([r[...] for r in h_refs]), wsg_ref[...], wsd_ref[...], ff)
    wk = wk_ref[...]
    n = y_refs[0].shape[2]
    blocks = [shared[:, i * n:(i + 1) * n] for i in range(2 * PACK_GROUPS)]
    for k in range(TOP_K):
        cols = _unpack_rows([r[k] for r in y_refs])
        blocks = [acc + wk[:, k:k + 1] * c.astype(F32) for acc, c in zip(blocks, cols)]
    f = jnp.concatenate(blocks, axis=1)
    o_ref[...] = _ln_rows(DEEPNORM_ALPHA * x1_ref[...] + g2_ref[0] * f) * lg_ref[...] + lb_ref[...]


def _moe_combine(h, x1, y_sel, wk_tok, w_sg, w_sd, g2, ln_g, ln_b, tm, b0, l, prev_out):
    t_part, d = x1.shape
    t = g2.shape[0] * l
    dp = h[0].shape[1]
    tiles_per_batch = l // tm
    off = b0 * tiles_per_batch
    ff = w_sd.shape[0]
    row = lambda i: (i, 0)
    const = lambda i: (0, 0)
    in_specs = ([pl.BlockSpec((tm, dp), row)] * PACK_GROUPS
                + [pl.BlockSpec((TOP_K, tm, dp), lambda i: (0, i, 0))] * PACK_GROUPS
                + [pl.BlockSpec((tm, d), row),
                   pl.BlockSpec((tm, TOP_K), row),
                   pl.BlockSpec(w_sg.shape, const),
                   pl.BlockSpec(w_sd.shape, const),
                   pl.BlockSpec((1, 1, d), lambda i: (i // tiles_per_batch + b0, 0, 0)),
                   pl.BlockSpec((1, d), const),
                   pl.BlockSpec((1, d), const)])
    args = [*h, *y_sel, x1, wk_tok, w_sg, w_sd, g2, ln_g.reshape(1, d), ln_b.reshape(1, d)]
    kern = functools.partial(_combine_kernel, ff=ff)
    aliases = {}
    if prev_out is not None:
        in_specs.append(pl.BlockSpec(memory_space=pl.ANY))
        aliases = {len(args): 0}
        args.append(prev_out)
        kern = lambda *refs: _combine_kernel(*refs[:-2], refs[-1], ff=ff)
    return pl.pallas_call(
        kern,
        grid=(t_part // tm,),
        in_specs=in_specs,
        out_specs=pl.BlockSpec((tm, d), lambda i: (i + off, 0)),
        out_shape=jax.ShapeDtypeStruct((t, d), F32),
        input_output_aliases=aliases,
        compiler_params=_params(("arbitrary",)),
        name="moe_combine",
    )(*args)


def _routing_tables(ek, rk, counts, n_rows):
    padded = (counts + EXPERT_TILE - 1) // EXPERT_TILE * EXPERT_TILE
    ends = jnp.cumsum(padded)
    starts = ends - padded
    experts = jnp.arange(N_EXPERTS, dtype=jnp.int32)
    start_of = jnp.sum(jnp.where(ek[None] == experts[:, None, None], starts[:, None, None], 0), axis=0)
    pos = start_of + rk
    tile_start = jnp.arange(n_rows // EXPERT_TILE, dtype=jnp.int32) * EXPERT_TILE
    tile_expert = jnp.minimum(jnp.sum(ends[None, :] <= tile_start[:, None], axis=1), N_EXPERTS - 1).astype(jnp.int32)
    used_tiles = (ends[-1:] // EXPERT_TILE).astype(jnp.int32)
    return pos.astype(jnp.int32), tile_expert, used_tiles


def _sc_mesh():
    return plsc.VectorSubcoreMesh(core_axis_name="core", subcore_axis_name="subcore")


def _dispatch_rows(rows, pos, n_rows):
    t, dp = rows.shape
    n_k = pos.shape[0]

    @functools.partial(pl.kernel, mesh=_sc_mesh(), out_type=jax.ShapeDtypeStruct((n_rows, dp), rows.dtype),
                       scratch_types=[])
    def scatter(rows_hbm, pos_hbm, out_hbm):
        def body(rows_vmem, idx_vmem):
            pltpu.sync_copy(rows_vmem, out_hbm.at[idx_vmem.at[0]])

        pltpu.emit_pipeline(
            body,
            grid=(t // SC_WINDOW, n_k),
            in_specs=[pl.BlockSpec((SC_WINDOW, dp), lambda i, k: (i, 0)),
                      pl.BlockSpec((1, SC_WINDOW), lambda i, k: (k, i))],
            out_specs=[],
            core_axis_name=("core", "subcore"),
            dimension_semantics=(pltpu.PARALLEL, pltpu.ARBITRARY),
        )(rows_hbm, pos_hbm)

    return scatter(rows, pos)


def _gather_rows(table, pos):
    n_k, t = pos.shape
    dp = table.shape[1]

    @functools.partial(pl.kernel, mesh=_sc_mesh(), out_type=jax.ShapeDtypeStruct((n_k * t, dp), table.dtype),
                       scratch_types=[])
    def gather(table_hbm, pos_hbm, out_hbm):
        def body(idx_vmem, out_vmem):
            pltpu.sync_copy(table_hbm.at[idx_vmem.at[0]], out_vmem)

        pltpu.emit_pipeline(
            body,
            grid=(n_k * t // SC_WINDOW,),
            in_specs=[pl.BlockSpec((1, SC_WINDOW), lambda i: (0, i))],
            out_specs=[pl.BlockSpec((SC_WINDOW, dp), lambda i: (i, 0))],
            core_axis_name=("core", "subcore"),
            dimension_semantics=(pltpu.PARALLEL,),
        )(pos_hbm, out_hbm)

    return gather(table, pos.reshape(1, n_k * t)).reshape(n_k, t, dp)


def _routed_moe(h, x1, ek, rk, wk, cnt, w_sg, w_sd, w_gu, w_d, g2, ln_g, ln_b, b0, l, prev_out):
    t = x1.shape[0]
    n_rows = t * TOP_K + N_EXPERTS * EXPERT_TILE
    pos, tile_expert, used_tiles = _routing_tables(ek, rk, cnt[:, 0].astype(jnp.int32), n_rows)
    xs = [_dispatch_rows(rows, pos, n_rows) for rows in h]
    ys = _expert_ffn(xs, tile_expert, used_tiles, w_gu, w_d)
    y_sel = [_gather_rows(table, pos) for table in ys]
    return _moe_combine(h, x1, y_sel, wk.T, w_sg, w_sd, g2, ln_g, ln_b, 256, b0, l, prev_out)


def _gate_layouts(g_lat, g_ctx):
    g = jnp.concatenate([g_ctx, g_lat], axis=1)[..., :4 * GDN_HEADS]
    b, lt, _ = g.shape
    nt = lt // GDN_CHUNK
    rows = jnp.transpose(g.reshape(b, nt, GDN_CHUNK, 4, GDN_HEADS), (0, 4, 1, 3, 2))
    return jnp.pad(rows, ((0, 0), (0, 0), (0, 0), (0, 4), (0, 0))).reshape(b, GDN_HEADS, nt * 8, GDN_CHUNK)


def kernel(x, c, ctx, c_ctx, w_mod, b_mod, w_in, conv_w, gdn_a_log, gdn_dt_bias, gdn_norm_g, ret_log_gamma,
           ret_norm_g, w_out, ln1_g, ln1_b, w_router, router_bias, w_gate_up, w_down, w_shared_gate_up,
           w_shared_down, ln2_g, ln2_b):
    b, l, d = x.shape
    lc = ctx.shape[1]
    gw = GDN_HEADS * HEAD_DIM

    n_mod = -(-(b + 1) // 8) * 8
    cc = jnp.zeros((n_mod, d), F32).at[:b].set(c).at[b].set(c_ctx)
    mod = _modulation(cc, w_mod[0], b_mod[0])
    sh1, sc1, g1, sh2, sc2, g2 = [mod[:b, i * d:(i + 1) * d].reshape(b, 1, d) for i in range(6)]
    csh1 = jnp.broadcast_to(mod[b, 0:d].reshape(1, 1, d), (b, 1, d))
    csc1 = jnp.broadcast_to(mod[b, d:2 * d].reshape(1, 1, d), (b, 1, d))

    w = w_in[0]
    o_gate = 3 * gw + gw
    o_ret = o_gate + 4 * GDN_HEADS
    w_main = jnp.concatenate([w[:, :o_gate], w[:, o_ret:]], axis=1).astype(BF16)
    w_gate = w[:, o_gate:o_ret]
    cos_t, sin_t = _rope_tables(l)
    wo = w_out[0].astype(BF16)
    w_sg, w_sd = w_shared_gate_up[0].astype(BF16), w_shared_down[0].astype(BF16)
    w_router_t = w_router[0].T

    nb = b // BATCH_PARTS
    out = None
    for part in range(BATCH_PARTS):
        b0 = part * nb
        qkv_lat, z_lat, ret_lat, g_lat = _inproj(x, sh1, sc1, w_main, w_gate, conv_w[0], 512, b0, nb)
        qkv_ctx, _, ret_ctx, g_ctx = _inproj(ctx, csh1, csc1, w_main, w_gate, conv_w[0], lc, b0, nb)
        gate_rows = _gate_layouts(g_lat, g_ctx)
        a_lat = _gdn_mixer(qkv_lat, qkv_ctx, z_lat, gate_rows, gdn_a_log[0],
                           gdn_dt_bias[0], gdn_norm_g[0], hb=2)
        r_lat = _ret_mixer(ret_lat, ret_ctx, ret_log_gamma[0], ret_norm_g[0], cos_t, sin_t)
        x1, h_a, h_b, ek, rk, wk, cnt = _outproj(a_lat, r_lat, x, wo[:gw], wo[gw:], g1, sh2, sc2, ln1_g[0], ln1_b[0],
                                                 w_router_t, router_bias[0], 512, b0)
        h = [arr.reshape(nb * l, arr.shape[-1]) for arr in (h_a, h_b)]
        out = _routed_moe(h, x1.reshape(nb * l, d), ek, rk, wk, cnt, w_sg, w_sd, w_gate_up[0], w_down[0],
                          g2, ln2_g[0], ln2_b[0], b0, l, out)
    return out.reshape(b, l, d)
```

```python
import functools

import jax
import jax.numpy as jnp
from jax import lax
from jax.experimental import pallas as pl
from jax.experimental.pallas import tpu as pltpu
from jax.experimental.pallas import tpu_sc as plsc

F32 = jnp.float32
BF16 = jnp.bfloat16

HEAD_DIM = 128
GDN_HEADS = 4
RET_HEADS = 4
CONV_W = 5
CONV_ROWS = 128
HALO = 16
GDN_CHUNK = 64
INV_BASE = 16
RET_CHUNK = 256
GRID_W = 64
ROPE_THETA = 10000.0
N_EXPERTS = 64
TOP_K = 8
N_GROUPS = 8
TOPK_GROUPS = 4
ROUTED_SCALE = 2.5
SC_WINDOW = 128
PACK_GROUPS = 2
BATCH_PARTS = 2
EXPERT_SUBTILES = 4
EXPERT_TILE = 512
LN_EPS = 1e-6
DEPTH = 1
DEEPNORM_ALPHA = (2 * DEPTH) ** 0.25

VMEM_LIMIT = 56 * 1024 * 1024
HIGHEST = lax.Precision.HIGHEST
NT_DIMS = (((1,), (1,)), ((), ()))


def _dot(a, b, precision=None):
    return jnp.dot(a, b, preferred_element_type=F32, precision=precision)


def _dot_nt(a, b, precision=None):
    return lax.dot_general(a, b, NT_DIMS, preferred_element_type=F32, precision=precision)


def _silu(x):
    return x * (1.0 / (1.0 + jnp.exp(-x)))


def _sigmoid(x):
    return 1.0 / (1.0 + jnp.exp(-x))


def _softplus(x):
    return jnp.maximum(x, 0.0) + jnp.log(1.0 + jnp.exp(-jnp.abs(x)))


def _ln_rows(x):
    mu = jnp.mean(x, axis=-1, keepdims=True)
    xc = x - mu
    var = jnp.mean(xc * xc, axis=-1, keepdims=True)
    return xc * lax.rsqrt(var + LN_EPS)


def _params(sem):
    return pltpu.CompilerParams(dimension_semantics=sem, vmem_limit_bytes=VMEM_LIMIT)


def _mod_kernel(c_ref, w_ref, b_ref, o_ref):
    o_ref[...] = _dot(_silu(c_ref[...]), w_ref[...], HIGHEST) + b_ref[...]


def _modulation(cc, w_mod, b_mod):
    rows, d = cc.shape
    n = w_mod.shape[1]
    tn = 1024
    return pl.pallas_call(
        _mod_kernel,
        grid=(n // tn,),
        in_specs=[pl.BlockSpec((rows, d), lambda j: (0, 0)),
                  pl.BlockSpec((d, tn), lambda j: (0, j)),
                  pl.BlockSpec((1, tn), lambda j: (0, j))],
        out_specs=pl.BlockSpec((rows, tn), lambda j: (0, j)),
        out_shape=jax.ShapeDtypeStruct((rows, n), F32),
        compiler_params=_params(("arbitrary",)),
        name="modulation",
    )(cc, w_mod, b_mod.reshape(1, n))


def _inproj_kernel(x_ref, xp_ref, xn_ref, sh_ref, sc_ref, wm_ref, wg_ref, cw_ref, qkv_ref, z_ref, ret_ref, g_ref,
                   p_ref):
    j, nl = pl.program_id(1), pl.num_programs(1)
    tm = x_ref.shape[1]
    modulate = lambda x: _ln_rows(x) * (1.0 + sc_ref[0]) + sh_ref[0]
    h = modulate(x_ref[0])
    hb = h.astype(BF16)
    h_prev = jnp.where(j > 0, modulate(xp_ref[0]), 0.0).astype(BF16)
    h_next = jnp.where(j < nl - 1, modulate(xn_ref[0]), 0.0).astype(BF16)
    h_ext = jnp.concatenate([h_prev, hb, h_next], axis=0)
    half = CONV_W // 2
    gw = GDN_HEADS * HEAD_DIM
    for part in range(3):
        p_ref[part] = _dot(h_ext, wm_ref[:, part * gw:(part + 1) * gw])

    def conv_part(part):
        c0 = part * gw
        p = p_ref[part]
        acc = p[HALO - half:HALO - half + tm, :] * cw_ref[0:1, c0:c0 + gw]
        for tap in range(1, CONV_W):
            acc = acc + p[HALO - half + tap:HALO - half + tap + tm, :] * cw_ref[tap:tap + 1, c0:c0 + gw]
        y = _silu(acc)
        if part < 2:
            scale = HEAD_DIM ** -0.5 if part == 0 else 1.0
            blocks = [y[:, hd * HEAD_DIM:(hd + 1) * HEAD_DIM] for hd in range(GDN_HEADS)]
            blocks = [blk * (lax.rsqrt(jnp.sum(blk * blk, axis=-1, keepdims=True) + LN_EPS) * scale) for blk in blocks]
            y = jnp.concatenate(blocks, axis=1)
        qkv_ref[0, :, c0:c0 + gw] = y.astype(BF16)

    rest = [(ref, n0) for ref in (z_ref, ret_ref) for n0 in range(0, ref.shape[-1], 512)]
    col = 3 * gw
    for idx, (ref, n0) in enumerate(rest):
        if idx < 3:
            conv_part(idx)
        ref[0, :, n0:n0 + 512] = _dot(hb, wm_ref[:, col:col + 512]).astype(BF16)
        col += 512
    h_lo = (h - hb.astype(F32)).astype(BF16)
    wg = wg_ref[...]
    wg_hi = wg.astype(BF16)
    wg_lo = (wg - wg_hi.astype(F32)).astype(BF16)
    g_ref[0] = (_dot(h_lo, wg_hi) + _dot(hb, wg_lo)) + _dot(hb, wg_hi)


def _inproj(x, sh, sc, w_main, w_gate, conv_w, tm, b0, b):
    _, l, d = x.shape
    widths = (3 * GDN_HEADS * HEAD_DIM, GDN_HEADS * HEAD_DIM, 4 * RET_HEADS * HEAD_DIM)
    n_gate = w_gate.shape[1]
    per_tile, last = tm // HALO, l // HALO - 1
    row = lambda i, j: (i, j, 0)
    const = lambda i, j: (0, 0)
    return pl.pallas_call(
        _inproj_kernel,
        grid=(b, l // tm),
        in_specs=[pl.BlockSpec((1, tm, d), lambda i, j: (i + b0, j, 0)),
                  pl.BlockSpec((1, HALO, d), lambda i, j: (i + b0, jnp.maximum(j * per_tile - 1, 0), 0)),
                  pl.BlockSpec((1, HALO, d), lambda i, j: (i + b0, jnp.minimum((j + 1) * per_tile, last), 0)),
                  pl.BlockSpec((1, 1, d), lambda i, j: (i + b0, 0, 0)),
                  pl.BlockSpec((1, 1, d), lambda i, j: (i + b0, 0, 0)),
                  pl.BlockSpec(w_main.shape, const),
                  pl.BlockSpec(w_gate.shape, const),
                  pl.BlockSpec(conv_w.shape, const)],
        out_specs=[pl.BlockSpec((1, tm, w), row) for w in widths]
                  + [pl.BlockSpec((1, tm, n_gate), row)],
        out_shape=[jax.ShapeDtypeStruct((b, l, w), BF16) for w in widths]
                  + [jax.ShapeDtypeStruct((b, l, n_gate), F32)],
        scratch_shapes=[pltpu.VMEM((3, tm + 2 * HALO, GDN_HEADS * HEAD_DIM), F32)],
        compiler_params=_params(("arbitrary", "arbitrary")),
        name="inproj",
    )(x, x, x, sh, sc, w_main, w_gate, conv_w)


def _bdot(a, b):
    return _dot(a.astype(BF16), b.astype(BF16))


def _pack_bf16_pair(x):
    n = x.shape[1] // 2
    bits = lambda v: pltpu.bitcast(v.astype(BF16).astype(F32), jnp.uint32)
    word = lax.shift_right_logical(bits(x[:, :n]), jnp.uint32(16)) | (bits(x[:, n:]) & jnp.uint32(0xFFFF0000))
    return pltpu.bitcast(word, jnp.int32)


def _unpack_bf16_pair(w):
    u = pltpu.bitcast(w, jnp.uint32)
    lo = pltpu.bitcast(lax.shift_left(u, jnp.uint32(16)), F32)
    hi = pltpu.bitcast(u & jnp.uint32(0xFFFF0000), F32)
    return lo.astype(BF16), hi.astype(BF16)


def _split3(x):
    hi = x.astype(BF16)
    r1 = x - hi.astype(F32)
    mid = r1.astype(BF16)
    lo = (r1 - mid.astype(F32)).astype(BF16)
    return hi, mid, lo


def _dot_split_lhs(x, mask01):
    m = mask01.astype(BF16)
    hi, mid, lo = _split3(x)
    return (_dot(lo, m) + _dot(mid, m)) + _dot(hi, m)


def _each(fn, *lists):
    return [fn(*args) for args in zip(*lists)]


def _inv_unit_triangular(mats, base=INV_BASE):
    n = mats[0].shape[0]
    i = lax.broadcasted_iota(jnp.int32, (n, n), 0)
    j = lax.broadcasted_iota(jnp.int32, (n, n), 1)
    eye = (i == j).astype(F32)
    shift = base.bit_length() - 1
    inside = (i >> shift) == (j >> shift)
    xs = _each(lambda a: jnp.where(inside, a, 0.0), mats)
    ts = _each(lambda d: eye - d, xs)
    for _ in range(shift - 1):
        xs = _each(lambda x: _bdot(x, x), xs)
        yield
        ts = _each(lambda t, x: t + _bdot(t, x), ts, xs)
        yield
    size = base
    while size < GDN_CHUNK:
        shift += 1
        size *= 2
        wider = (i >> shift) == (j >> shift)
        off = wider & jnp.logical_not(inside)
        ots = _each(lambda a, t: _bdot(jnp.where(off, a, 0.0), t), mats, ts)
        yield
        ts = _each(lambda t, ot: t - _bdot(t, ot), ts, ots)
        yield
        inside = wider
    return ts


def _gdn_kernel(alog_ref, dtb_ref,
                q_ref, k_ref, v_ref, qc_ref, kc_ref, vc_ref,
                gt_ref, z_ref, ng_ref,
                o_ref,
                qs_ref, ks_ref, vs_ref, gts_ref, gth_ref,
                wqf_ref, wqb_ref, u0_ref, intra_ref, ket_ref, gef_ref, geb_ref,
                of_ref, ob_ref, *, hb, l_lat, l_ctx, slots_per_group):
    c = GDN_CHUNK
    c2 = 2 * c
    lt = l_lat + l_ctx
    n_ctx = l_ctx // c
    nt = lt // c
    hgrp = pl.program_id(1)

    i2 = lax.broadcasted_iota(jnp.int32, (c2, c2), 0)
    j2 = lax.broadcasted_iota(jnp.int32, (c2, c2), 1)
    same_blk = (i2 & c) == (j2 & c)
    sgn_i = jnp.where(i2 < c, 1, -1)
    sgn_j = jnp.where(j2 < c, 1, -1)
    incl = same_blk & ((j2 - i2) * sgn_i <= 0)
    strict = same_blk & ((j2 - i2) * sgn_i < 0)
    tri_row = (same_blk & ((i2 - j2) * sgn_j <= 0)).astype(F32)
    eye_m = i2 == j2
    eye_b = eye_m.astype(BF16)
    lane_lt_c = j2 < c

    for hh in range(hb):
        head = hgrp * hb + hh
        lane0 = hh * HEAD_DIM
        for (src_c, src_l, dst) in ((qc_ref, q_ref, qs_ref), (kc_ref, k_ref, ks_ref), (vc_ref, v_ref, vs_ref)):
            dst[hh, 0:l_ctx, :] = src_c[0, :, lane0:lane0 + HEAD_DIM]
            dst[hh, l_ctx:lt, :] = src_l[0, :, lane0:lane0 + HEAD_DIM]

        a_f, a_b = alog_ref[head], alog_ref[GDN_HEADS + head]
        d_f, d_b = dtb_ref[head], dtb_ref[GDN_HEADS + head]
        compr = lax.broadcasted_iota(jnp.int32, (nt * 8, 1), 0) & 7
        neg_ar = -jnp.exp(jnp.where(compr == 2, a_f, a_b))
        dtbr = jnp.where(compr == 2, d_f, d_b)
        grow_raw = gt_ref[0, hh]
        grow = jnp.where(compr < 2, _sigmoid(grow_raw), neg_ar * _softplus(grow_raw + dtbr))
        gts_ref[hh] = jnp.zeros(gts_ref.shape[1:], F32)
        gts_ref[hh, :, 0:c] = grow
        gth_ref[hh] = jnp.zeros(gth_ref.shape[1:], F32)
        gth_ref[hh, :, c:c2] = grow

    spg = slots_per_group
    heads = list(range(hb))

    def bwd_chunk(i):
        return jnp.where(i < n_ctx, n_ctx - 1 - i, nt + n_ctx - 1 - i)

    def slot_load(i, hh):
        cbk = bwd_chunk(i)
        rf = pl.multiple_of(i * c, c)
        rb = pl.multiple_of(cbk * c, c)
        two = lambda ref: jnp.concatenate([ref[hh, pl.ds(rf, c), :], ref[hh, pl.ds(rb, c), :]], axis=0).astype(F32)
        g_r = (gts_ref[hh, pl.ds(pl.multiple_of(i * 8, 8), 8), :]
               + gth_ref[hh, pl.ds(pl.multiple_of(cbk * 8, 8), 8), :])
        return two(ks_ref), two(vs_ref), two(qs_ref), g_r

    def slot_stages(loaded):
        k2, v2, q2, g_r = [list(col) for col in zip(*loaded)]
        rcs = _each(lambda g: _dot_split_lhs(g, tri_row), g_r)
        yield
        k2b = _each(lambda x: x.astype(BF16), k2)
        kk = _each(_dot_nt, k2b, k2b)
        qk = _each(lambda q, kb: _dot_nt(q.astype(BF16), kb), q2, k2b)
        yield
        by_dir = lambda g, k: jnp.where(lane_lt_c[0:1, :], g[k:k + 1, :], g[k + 1:k + 2, :])
        lane_sum = lambda mask, row: jnp.sum(jnp.where(mask, row, 0.0), axis=1, keepdims=True)
        gcc = _each(lambda g: lane_sum(incl, by_dir(g, 2)), g_r)
        gcr = _each(lambda r: by_dir(r, 2), rcs)
        beta = _each(lambda g: lane_sum(eye_m, by_dir(g, 0)), g_r)
        gend = _each(lambda s: jnp.concatenate([jnp.broadcast_to(s[c - 1:c, :], (c, 1)),
                                                jnp.broadcast_to(s[c:c + 1, :], (c, 1))], axis=0), gcc)
        decay = _each(lambda gc, gr: jnp.exp(jnp.where(incl, gc - gr, -jnp.inf)), gcc, gcr)
        a_mat = _each(lambda m, d, b: jnp.where(strict, m * d, 0.0) * b, kk, decay, beta)
        t_f32 = yield from _inv_unit_triangular(a_mat)
        t_mat = _each(lambda t: t.astype(BF16), t_f32)
        egc = _each(jnp.exp, gcc)
        u0 = _each(lambda t, v, b: _dot(t, (v * b).astype(BF16)), t_mat, v2, beta)
        w = _each(lambda t, k, b, e: _dot(t, (k * (b * e)).astype(BF16)), t_mat, k2, beta, egc)
        ket = _each(lambda k, ge_, gc: _dot_nt(eye_b, (k * jnp.exp(ge_ - gc)).astype(BF16)), k2, gend, gcc)
        yield
        qd = _each(lambda q, e: (q * e).astype(BF16), q2, egc)
        intra = _each(lambda m, d: (m * d).astype(BF16), qk, decay)
        ge = _each(jnp.exp, gend)
        return [(jnp.concatenate([w_[0:c].astype(BF16), qd_[0:c]], axis=0),
                 jnp.concatenate([w_[c:c2].astype(BF16), qd_[c:c2]], axis=0), u0_, in_,
                 jnp.concatenate([jnp.where(lane_lt_c, kt, 0.0), jnp.where(lane_lt_c, 0.0, kt)], axis=0).astype(BF16),
                 jnp.broadcast_to(g_[0:1, :], (8, HEAD_DIM)), jnp.broadcast_to(g_[c:c + 1, :], (8, HEAD_DIM)))
                for w_, qd_, u0_, in_, kt, g_ in zip(w, qd, u0, intra, ket, ge)]

    def transform_group(g):
        par = lax.rem(g, 2)
        jobs = [(s, hh) for s in range(spg) for hh in heads]
        results = yield from slot_stages([slot_load(g * spg + s, hh) for s, hh in jobs])
        for (s, hh), (wq_f, wq_b, u0, intra, ket, ge_f, ge_b) in zip(jobs, results):
            wqf_ref[par, hh, s] = wq_f
            wqb_ref[par, hh, s] = wq_b
            u0_ref[par, hh, s] = u0
            intra_ref[par, hh, s] = intra
            ket_ref[par, hh, s] = ket
            gef_ref[par, hh, s] = ge_f
            geb_ref[par, hh, s] = ge_b

    def recur_group(g, states):
        par = lax.rem(g, 2)
        sf, sb = list(states[0::2]), list(states[1::2])
        for s in range(spg):
            i = g * spg + s
            rf = pl.multiple_of(i * c, c)
            rb = pl.multiple_of(bwd_chunk(i) * c, c)
            r_f = _each(lambda hh, st: _dot(wqf_ref[par, hh, s], st.astype(BF16)), heads, sf)
            r_b = _each(lambda hh, st: _dot(wqb_ref[par, hh, s], st.astype(BF16)), heads, sb)
            u0 = _each(lambda hh: u0_ref[par, hh, s], heads)
            yield
            u2 = _each(lambda u, f, b_: jnp.concatenate([u[0:c] - f[0:c], u[c:c2] - b_[0:c]], axis=0).astype(BF16),
                       u0, r_f, r_b)
            ket = _each(lambda hh: ket_ref[par, hh, s], heads)
            df = _each(lambda kt, u: _dot(kt[0:HEAD_DIM], u), ket, u2)
            db = _each(lambda kt, u: _dot(kt[HEAD_DIM:2 * HEAD_DIM], u), ket, u2)
            iu = _each(lambda hh, u: _dot(intra_ref[par, hh, s], u), heads, u2)
            sf = _each(lambda hh, st, d: gef_ref[par, hh, s][0:1, :] * st + d, heads, sf, df)
            sb = _each(lambda hh, st, d: geb_ref[par, hh, s][0:1, :] * st + d, heads, sb, db)
            for hh in heads:
                of_ref[hh, pl.ds(rf, c), :] = r_f[hh][c:c2] + iu[hh][0:c]
                ob_ref[hh, pl.ds(rb, c), :] = r_b[hh][c:c2] + iu[hh][c:c2]
            yield
        return tuple(x for pair in zip(sf, sb) for x in pair)

    def drive(transform, recur, transforms_per_recur=1):
        states = None
        while transform is not None or recur is not None:
            if transform is not None:
                for _ in range(transforms_per_recur):
                    try:
                        next(transform)
                    except StopIteration:
                        transform = None
                        break
            if recur is not None:
                try:
                    next(recur)
                except StopIteration as stop:
                    states, recur = stop.value, None
        return states

    n_groups = nt // spg
    zero_state = tuple(jnp.zeros((HEAD_DIM, HEAD_DIM), F32) for _ in range(2 * hb))
    drive(transform_group(0), None)
    states = lax.fori_loop(0, n_groups - 1,
                           lambda g, st: drive(transform_group(g + 1), recur_group(g, st)), zero_state)
    drive(None, recur_group(n_groups - 1, states))

    ng = ng_ref[...]
    blk = 256
    for hh in range(hb):
        lane0 = hh * HEAD_DIM

        def fin_body(i, carry):
            r = pl.multiple_of(i * blk, blk)
            o = of_ref[hh, pl.ds(l_ctx + r, blk), :] + ob_ref[hh, pl.ds(l_ctx + r, blk), :]
            y = o * lax.rsqrt(jnp.mean(o * o, axis=-1, keepdims=True) + LN_EPS) * ng
            z = z_ref[0, pl.ds(r, blk), lane0:lane0 + HEAD_DIM].astype(F32)
            o_ref[0, pl.ds(r, blk), lane0:lane0 + HEAD_DIM] = (y * _silu(z)).astype(BF16)
            return carry

        lax.fori_loop(0, l_lat // blk, fin_body, 0)


def _gdn_mixer(qkv_lat, qkv_ctx, z_lat, gate_rows, a_log, dt_bias, norm_g, hb):
    b, l_lat, _ = qkv_lat.shape
    l_ctx = qkv_ctx.shape[1]
    lt = l_lat + l_ctx
    nt = lt // GDN_CHUNK
    hw = hb * HEAD_DIM
    ngrp = GDN_HEADS // hb
    c2 = 2 * GDN_CHUNK
    spg = 6
    assert nt % spg == 0 and nt // spg >= 2 and GDN_HEADS % hb == 0

    def seq_spec(length, part):
        return pl.BlockSpec((1, length, hw), lambda i, j, *_: (i, 0, part * ngrp + j))

    grid_spec = pltpu.PrefetchScalarGridSpec(
        num_scalar_prefetch=2,
        grid=(b, ngrp),
        in_specs=[seq_spec(l_lat, 0), seq_spec(l_lat, 1), seq_spec(l_lat, 2),
                  seq_spec(l_ctx, 0), seq_spec(l_ctx, 1), seq_spec(l_ctx, 2),
                  pl.BlockSpec((1, hb, nt * 8, GDN_CHUNK), lambda i, j, *_: (i, j, 0, 0)),
                  pl.BlockSpec((1, l_lat, hw), lambda i, j, *_: (i, 0, j)),
                  pl.BlockSpec((1, HEAD_DIM), lambda i, j, *_: (0, 0))],
        out_specs=pl.BlockSpec((1, l_lat, hw), lambda i, j, *_: (i, 0, j)),
        scratch_shapes=[
            pltpu.VMEM((hb, lt, HEAD_DIM), BF16),
            pltpu.VMEM((hb, lt, HEAD_DIM), BF16),
            pltpu.VMEM((hb, lt, HEAD_DIM), BF16),
            pltpu.VMEM((hb, nt * 8, c2), F32),
            pltpu.VMEM((hb, nt * 8, c2), F32),
            pltpu.VMEM((2, hb, spg, c2, HEAD_DIM), BF16),
            pltpu.VMEM((2, hb, spg, c2, HEAD_DIM), BF16),
            pltpu.VMEM((2, hb, spg, c2, HEAD_DIM), F32),
            pltpu.VMEM((2, hb, spg, c2, c2), BF16),
            pltpu.VMEM((2, hb, spg, 2 * HEAD_DIM, c2), BF16),
            pltpu.VMEM((2, hb, spg, 8, HEAD_DIM), F32),
            pltpu.VMEM((2, hb, spg, 8, HEAD_DIM), F32),
            pltpu.VMEM((hb, lt, HEAD_DIM), F32),
            pltpu.VMEM((hb, lt, HEAD_DIM), F32),
        ])
    kern = functools.partial(_gdn_kernel, hb=hb, l_lat=l_lat, l_ctx=l_ctx, slots_per_group=spg)
    return pl.pallas_call(
        kern,
        grid_spec=grid_spec,
        out_shape=jax.ShapeDtypeStruct((b, l_lat, GDN_HEADS * HEAD_DIM), BF16),
        compiler_params=_params(("arbitrary", "arbitrary")),
        name="gdn_mixer",
    )(a_log, dt_bias, qkv_lat, qkv_lat, qkv_lat, qkv_ctx, qkv_ctx, qkv_ctx,
      gate_rows, z_lat, norm_g.reshape(1, HEAD_DIM))


def _ret_kernel(lg_ref, q_ref, k_ref, v_ref, gate_ref, qc_ref, kc_ref, vc_ref, cos_ref, sin_ref, ng_ref,
                o_ref, q_s, k_s, rf_s, rb_s, *, l_lat, l_ctx, hb):
    c = RET_CHUNK
    n_lat = l_lat // c
    n_ctx = l_ctx // c
    heads = list(range(hb))
    lanes = [slice(hh * HEAD_DIM, (hh + 1) * HEAD_DIM) for hh in heads]
    pos_c = lax.broadcasted_iota(jnp.int32, (c, 1), 0).astype(F32)
    ii = lax.broadcasted_iota(jnp.int32, (c, c), 0)
    jj = lax.broadcasted_iota(jnp.int32, (c, c), 1)
    dif = (ii - jj).astype(F32)
    e2 = lax.broadcasted_iota(jnp.int32, (HEAD_DIM, HEAD_DIM), 0)
    f2 = lax.broadcasted_iota(jnp.int32, (HEAD_DIM, HEAD_DIM), 1)
    eye_b = (e2 == f2).astype(BF16)
    kscale = HEAD_DIM ** -0.5
    lg_f = [lg_ref[pl.program_id(1) * hb + hh] for hh in heads]
    lg_b = [lg_ref[RET_HEADS + pl.program_id(1) * hb + hh] for hh in heads]
    dmat = _each(lambda f, b_: (jnp.exp(jnp.where(ii >= jj, f * dif, -jnp.inf))
                                + jnp.exp(jnp.where(jj >= ii, -b_ * dif, -jnp.inf))), lg_f, lg_b)
    kdec_f = _each(lambda f: jnp.exp(f * (c - 1 - pos_c)), lg_f)
    kdec_b = _each(lambda b_: jnp.exp(b_ * pos_c), lg_b)
    qdec_f = _each(lambda f: jnp.exp(f * (pos_c + 1.0)), lg_f)
    qdec_b = _each(lambda b_: jnp.exp(b_ * (c - pos_c)), lg_b)
    cd_f = _each(lambda f: jnp.exp(jnp.full((1, HEAD_DIM), f * c, F32)), lg_f)
    cd_b = _each(lambda b_: jnp.exp(jnp.full((1, HEAD_DIM), b_ * c, F32)), lg_b)

    def transposed(kd):
        return _dot_nt(eye_b, kd.astype(BF16)).astype(BF16)

    r_f = [jnp.zeros((HEAD_DIM, HEAD_DIM), F32) for _ in heads]
    r_b = [jnp.zeros((HEAD_DIM, HEAD_DIM), F32) for _ in heads]
    for n in range(n_ctx):
        m = n_ctx - 1 - n
        kf = _each(lambda ln, d: transposed(kc_ref[0, n * c:(n + 1) * c, ln].astype(F32) * kscale * d), lanes, kdec_f)
        kb = _each(lambda ln, d: transposed(kc_ref[0, m * c:(m + 1) * c, ln].astype(F32) * kscale * d), lanes, kdec_b)
        pf = _each(lambda kt, ln: _dot(kt, vc_ref[0, n * c:(n + 1) * c, ln]), kf, lanes)
        pb = _each(lambda kt, ln: _dot(kt, vc_ref[0, m * c:(m + 1) * c, ln]), kb, lanes)
        r_f = _each(lambda d, r, p: d * r + p, cd_f, r_f, pf)
        r_b = _each(lambda d, r, p: d * r + p, cd_b, r_b, pb)

    def rope_body(n, carry):
        r = pl.multiple_of(n * c, c)
        cs = cos_ref[pl.ds(r, c), :]
        sn = sin_ref[pl.ds(r, c), :]
        for hh, ln in zip(heads, lanes):
            q = q_ref[0, pl.ds(r, c), ln].astype(F32)
            k = k_ref[0, pl.ds(r, c), ln].astype(F32) * kscale
            q_s[hh, pl.ds(r, c), :] = q * cs + pltpu.roll(q, HEAD_DIM // 2, 1) * sn
            k_s[hh, pl.ds(r, c), :] = k * cs + pltpu.roll(k, HEAD_DIM // 2, 1) * sn
        return carry

    lax.fori_loop(0, n_lat, rope_body, 0)

    chunks = list(range(n_lat))
    rows = [slice(n * c, (n + 1) * c) for n in chunks]
    jobs = [(hh, n) for hh in heads for n in chunks]
    k_c = [k_s[hh, rows[n], :] for hh, n in jobs]
    kft = [transposed(k * kdec_f[hh]) for k, (hh, n) in zip(k_c, jobs)]
    kbt = [transposed(k * kdec_b[hh]) for k, (hh, n) in zip(k_c, jobs)]
    kvf = [_dot(kt, v_ref[0, rows[n], lanes[hh]]) for kt, (hh, n) in zip(kft, jobs)]
    kvb = [_dot(kt, v_ref[0, rows[n], lanes[hh]]) for kt, (hh, n) in zip(kbt, jobs)]
    for hh in heads:
        r = r_f[hh]
        for n in chunks:
            rf_s[hh, n] = r
            r = cd_f[hh] * r + kvf[hh * n_lat + n]
        r = r_b[hh]
        for n in reversed(chunks):
            rb_s[hh, n] = r
            r = cd_b[hh] * r + kvb[hh * n_lat + n]

    group = 4
    for g0 in range(0, len(jobs), group):
        part = jobs[g0:g0 + group]
        q_c = [q_s[hh, rows[n], :] for hh, n in part]
        s = [_dot_nt(q.astype(BF16), k_s[hh, rows[n], :].astype(BF16)) for q, (hh, n) in zip(q_c, part)]
        att = [(s_ * dmat[hh]).astype(BF16) for s_, (hh, n) in zip(s, part)]
        o = [_dot(a, v_ref[0, rows[n], lanes[hh]]) for a, (hh, n) in zip(att, part)]
        qd = [jnp.concatenate([(q * qdec_f[hh]).astype(BF16), (q * qdec_b[hh]).astype(BF16)], axis=1)
              for q, (hh, n) in zip(q_c, part)]
        st = [jnp.concatenate([rf_s[hh, n], rb_s[hh, n]], axis=0).astype(BF16) for hh, n in part]
        o = _each(lambda o_, qd_, st_: o_ + _dot(qd_, st_), o, qd, st)
        for o_, (hh, n) in zip(o, part):
            y = _ln_rows(o_) * ng_ref[:, lanes[hh]]
            g = gate_ref[0, rows[n], lanes[hh]].astype(F32)
            o_ref[0, rows[n], lanes[hh]] = (y * _silu(g)).astype(BF16)


def _ret_mixer(ret_lat, ret_ctx, log_gamma, norm_g, cos_t, sin_t):
    b, l_lat, _ = ret_lat.shape
    l_ctx = ret_ctx.shape[1]
    n_lat = l_lat // RET_CHUNK
    hb = 2
    hw = hb * HEAD_DIM
    ngrp = RET_HEADS // hb

    def seq_spec(length, part):
        return pl.BlockSpec((1, length, hw), lambda i, j, *_: (i, 0, part * ngrp + j))

    grid_spec = pltpu.PrefetchScalarGridSpec(
        num_scalar_prefetch=1,
        grid=(b, ngrp),
        in_specs=[seq_spec(l_lat, 0), seq_spec(l_lat, 1), seq_spec(l_lat, 2), seq_spec(l_lat, 3),
                  seq_spec(l_ctx, 0), seq_spec(l_ctx, 1), seq_spec(l_ctx, 2),
                  pl.BlockSpec((l_lat, HEAD_DIM), lambda i, j, *_: (0, 0)),
                  pl.BlockSpec((l_lat, HEAD_DIM), lambda i, j, *_: (0, 0)),
                  pl.BlockSpec((1, hw), lambda i, j, *_: (0, j))],
        out_specs=pl.BlockSpec((1, l_lat, hw), lambda i, j, *_: (i, 0, j)),
        scratch_shapes=[pltpu.VMEM((hb, l_lat, HEAD_DIM), F32),
                        pltpu.VMEM((hb, l_lat, HEAD_DIM), F32),
                        pltpu.VMEM((hb, n_lat, HEAD_DIM, HEAD_DIM), F32),
                        pltpu.VMEM((hb, n_lat, HEAD_DIM, HEAD_DIM), F32)])
    kern = functools.partial(_ret_kernel, l_lat=l_lat, l_ctx=l_ctx, hb=hb)
    return pl.pallas_call(
        kern,
        grid_spec=grid_spec,
        out_shape=jax.ShapeDtypeStruct((b, l_lat, RET_HEADS * HEAD_DIM), BF16),
        compiler_params=_params(("arbitrary", "arbitrary")),
        name="ret_mixer",
    )(log_gamma.reshape(-1), ret_lat, ret_lat, ret_lat, ret_lat, ret_ctx, ret_ctx, ret_ctx,
      cos_t, sin_t, norm_g.reshape(1, -1))


def _rope_tables(l_lat):
    rows = l_lat // GRID_W
    row = jnp.repeat(jnp.arange(rows, dtype=F32), GRID_W)
    col = jnp.tile(jnp.arange(GRID_W, dtype=F32), rows)
    quarter = HEAD_DIM // 4
    inv = ROPE_THETA ** (-jnp.arange(quarter, dtype=F32) / quarter)
    ang = jnp.concatenate([row[:, None] * inv, col[:, None] * inv], -1)
    cos, sin = jnp.cos(ang), jnp.sin(ang)
    return jnp.concatenate([cos, cos], -1), jnp.concatenate([-sin, sin], -1)


def _top_rows(vals, k):
    n = vals.shape[0]
    idx = lax.broadcasted_iota(jnp.int32, vals.shape, 0)
    taken = jnp.zeros(vals.shape, jnp.int32)
    firsts = []
    for _ in range(k):
        live = jnp.where(taken == 0, vals, -jnp.inf)
        top = jnp.max(live, axis=0, keepdims=True)
        cand = jnp.where((live == top) & (taken == 0), idx, n)
        first = jnp.min(cand, axis=0, keepdims=True)
        taken = taken + (idx == first).astype(jnp.int32)
        firsts.append(first)
    return firsts, taken


def _outproj_kernel(a_ref, r_ref, x_ref, wa_ref, wr_ref, g1_ref, sh2_ref, sc2_ref, lg_ref, lb_ref,
                    wrt_ref, rb_ref, x1_ref, ha_ref, hb_ref, ek_ref, rk_ref, wk_ref, cnt_ref, carry_ref):
    y = _dot(a_ref[0], wa_ref[...]) + _dot(r_ref[0], wr_ref[...])
    x1 = _ln_rows(DEEPNORM_ALPHA * x_ref[0] + g1_ref[0] * y) * lg_ref[...] + lb_ref[...]
    x1_ref[0] = x1
    h = _ln_rows(x1) * (1.0 + sc2_ref[0]) + sh2_ref[0]
    for ref, part in zip((ha_ref, hb_ref), _pack_rows(h)):
        ref[0] = part
    s = _sigmoid(_dot_nt(wrt_ref[...], h, HIGHEST))
    sb = s + rb_ref[...]
    tm = s.shape[1]
    per = N_EXPERTS // N_GROUPS
    sub = lax.broadcasted_iota(jnp.int32, (per, tm), 0)
    gs_rows = []
    for g in range(N_GROUPS):
        blk = sb[g * per:(g + 1) * per, :]
        m1 = jnp.max(blk, axis=0, keepdims=True)
        first = jnp.min(jnp.where(blk == m1, sub, per), axis=0, keepdims=True)
        m2 = jnp.max(jnp.where(sub == first, -jnp.inf, blk), axis=0, keepdims=True)
        gs_rows.append(m1 + m2)
    gscore = jnp.concatenate(gs_rows, axis=0)
    _, gtaken = _top_rows(gscore, TOPK_GROUPS)
    emask = jnp.concatenate([jnp.broadcast_to(gtaken[g:g + 1, :], (per, tm)) for g in range(N_GROUPS)], axis=0)
    masked = jnp.where(emask > 0, sb, -jnp.inf)
    firsts, taken = _top_rows(masked, TOP_K)
    first_step = (pl.program_id(0) == 0) & (pl.program_id(1) == 0)

    @pl.when(first_step)
    def _():
        carry_ref[...] = jnp.zeros(carry_ref.shape, F32)

    sel_f = taken.astype(F32)
    ti = lax.broadcasted_iota(jnp.int32, (tm, tm), 0)
    tj = lax.broadcasted_iota(jnp.int32, (tm, tm), 1)
    rank = _dot(sel_f.astype(BF16), (ti < tj).astype(BF16)) + carry_ref[:, 0:1]
    carry = carry_ref[...] + jnp.sum(sel_f, axis=1, keepdims=True)
    carry_ref[...] = carry
    cnt_ref[...] = carry
    eidx = lax.broadcasted_iota(jnp.int32, (N_EXPERTS, tm), 0)
    picked = []
    for k in range(TOP_K):
        hit = eidx == firsts[k]
        pick = lambda v: jnp.sum(jnp.where(hit, v, 0.0), axis=0, keepdims=True)
        ek_ref[k:k + 1, :] = firsts[k]
        rk_ref[k:k + 1, :] = pick(rank).astype(jnp.int32)
        picked.append(pick(s))
    total = picked[0]
    for k in range(1, TOP_K):
        total = total + picked[k]
    for k in range(TOP_K):
        wk_ref[k:k + 1, :] = picked[k] / total * ROUTED_SCALE


def _outproj(a_lat, r_lat, x, w_a, w_r, g1, sh2, sc2, ln_g, ln_b, w_router_t, router_bias, tm, b0):
    b, l, half = a_lat.shape
    d = x.shape[-1]
    nl = l // tm
    row = lambda i, j: (i, j, 0)
    per_b = lambda i, j: (i + b0, 0, 0)
    const = lambda i, j: (0, 0)
    return pl.pallas_call(
        _outproj_kernel,
        grid=(b, nl),
        in_specs=[pl.BlockSpec((1, tm, half), row),
                  pl.BlockSpec((1, tm, half), row),
                  pl.BlockSpec((1, tm, d), lambda i, j: (i + b0, j, 0)),
                  pl.BlockSpec((half, d), const),
                  pl.BlockSpec((half, d), const),
                  pl.BlockSpec((1, 1, d), per_b),
                  pl.BlockSpec((1, 1, d), per_b),
                  pl.BlockSpec((1, 1, d), per_b),
                  pl.BlockSpec((1, d), const),
                  pl.BlockSpec((1, d), const),
                  pl.BlockSpec((N_EXPERTS, d), const),
                  pl.BlockSpec((N_EXPERTS, 1), const)],
        out_specs=[pl.BlockSpec((1, tm, d), row),
                   pl.BlockSpec((1, tm, d // (2 * PACK_GROUPS)), row),
                   pl.BlockSpec((1, tm, d // (2 * PACK_GROUPS)), row),
                   pl.BlockSpec((TOP_K, tm), lambda i, j: (0, i * nl + j)),
                   pl.BlockSpec((TOP_K, tm), lambda i, j: (0, i * nl + j)),
                   pl.BlockSpec((TOP_K, tm), lambda i, j: (0, i * nl + j)),
                   pl.BlockSpec((N_EXPERTS, HEAD_DIM), const)],
        out_shape=[jax.ShapeDtypeStruct((b, l, d), F32),
                   jax.ShapeDtypeStruct((b, l, d // (2 * PACK_GROUPS)), jnp.int32),
                   jax.ShapeDtypeStruct((b, l, d // (2 * PACK_GROUPS)), jnp.int32),
                   jax.ShapeDtypeStruct((TOP_K, b * l), jnp.int32),
                   jax.ShapeDtypeStruct((TOP_K, b * l), jnp.int32),
                   jax.ShapeDtypeStruct((TOP_K, b * l), F32),
                   jax.ShapeDtypeStruct((N_EXPERTS, HEAD_DIM), F32)],
        scratch_shapes=[pltpu.VMEM((N_EXPERTS, HEAD_DIM), F32)],
        compiler_params=_params(("arbitrary", "arbitrary")),
        name="outproj_router",
    )(a_lat, r_lat, x, w_a, w_r, g1, sh2, sc2, ln_g.reshape(1, d), ln_b.reshape(1, d),
      w_router_t, router_bias.reshape(N_EXPERTS, 1))


def _pack_rows(x):
    n = x.shape[1] // PACK_GROUPS
    return [_pack_bf16_pair(x[:, g * n:(g + 1) * n]) for g in range(PACK_GROUPS)]


def _unpack_rows(parts):
    cols = []
    for p in parts:
        cols += list(_unpack_bf16_pair(p))
    return cols


def _glu_ffn(cols, w_gate_up, w_down, ff):
    n = cols[0].shape[1]
    ab = _dot(cols[0], w_gate_up[0:n])
    for i in range(1, len(cols)):
        ab = ab + _dot(cols[i], w_gate_up[i * n:(i + 1) * n])
    act = (_silu(ab[:, :ff]) * ab[:, ff:]).astype(BF16)
    return _dot(act, w_down)


def _expert_kernel(te_ref, used_ref, *refs, ff):
    xs_refs, (wgu_ref, wd_ref) = refs[:PACK_GROUPS], refs[PACK_GROUPS:PACK_GROUPS + 2]
    ys_refs, (wgu_bf_ref, wd_bf_ref) = refs[PACK_GROUPS + 2:2 * PACK_GROUPS + 2], refs[2 * PACK_GROUPS + 2:]
    i = pl.program_id(0)

    @pl.when(i < used_ref[0])
    def _():
        @pl.when((i == 0) | (te_ref[i] != te_ref[jnp.maximum(i - 1, 0)]))
        def _():
            wgu_bf_ref[...] = wgu_ref[0].astype(BF16)
            wd_bf_ref[...] = wd_ref[0].astype(BF16)

        sub = EXPERT_TILE // EXPERT_SUBTILES
        blocks = [slice(s * sub, (s + 1) * sub) for s in range(EXPERT_SUBTILES)]
        wgu, wd = wgu_bf_ref[...], wd_bf_ref[...]
        cols = _each(lambda b: _unpack_rows([r[b, :] for r in xs_refs]), blocks)
        n = cols[0][0].shape[1]
        ab = _each(lambda c: _dot(c[0], wgu[0:n]), cols)
        for j in range(1, 2 * PACK_GROUPS):
            ab = _each(lambda acc, c: acc + _dot(c[j], wgu[j * n:(j + 1) * n]), ab, cols)
        act = _each(lambda a: (_silu(a[:, :ff]) * a[:, ff:]).astype(BF16), ab)
        y = _each(lambda a: _dot(a, wd), act)
        for b, y_b in zip(blocks, y):
            for ref, part in zip(ys_refs, _pack_rows(y_b)):
                ref[b, :] = part


def _expert_ffn(xs, tile_expert, used_tiles, w_gu, w_d):
    n_rows, dp = xs[0].shape
    n_e, d, ff2 = w_gu.shape
    ff = ff2 // 2
    tile = lambda i, te, used: (jnp.minimum(i, used[0] - 1), 0)
    grid_spec = pltpu.PrefetchScalarGridSpec(
        num_scalar_prefetch=2,
        grid=(n_rows // EXPERT_TILE,),
        in_specs=[pl.BlockSpec((EXPERT_TILE, dp), tile)] * PACK_GROUPS
                 + [pl.BlockSpec((1, d, ff2), lambda i, te, used: (te[i], 0, 0)),
                    pl.BlockSpec((1, ff, d), lambda i, te, used: (te[i], 0, 0))],
        out_specs=[pl.BlockSpec((EXPERT_TILE, dp), tile)] * PACK_GROUPS,
        scratch_shapes=[pltpu.VMEM((d, ff2), BF16), pltpu.VMEM((ff, d), BF16)])
    return pl.pallas_call(
        functools.partial(_expert_kernel, ff=ff),
        grid_spec=grid_spec,
        out_shape=[jax.ShapeDtypeStruct((n_rows, dp), jnp.int32)] * PACK_GROUPS,
        compiler_params=_params(("arbitrary",)),
        name="expert_ffn",
    )(tile_expert, used_tiles, *xs, w_gu, w_d)


def _combine_kernel(*refs, ff):
    h_refs, y_refs = refs[:PACK_GROUPS], refs[PACK_GROUPS:2 * PACK_GROUPS]
    x1_ref, wk_ref, wsg_ref, wsd_ref, g2_ref, lg_ref, lb_ref, o_ref = refs[2 * PACK_GROUPS:]
    shared = _glu_ffn(_unpack_rows([r[...] for r in h_refs]), wsg_ref[...], wsd_ref[...], ff)
    wk = wk_ref[...]
    n = y_refs[0].shape[2]
    blocks = [shared[:, i * n:(i + 1) * n] for i in range(2 * PACK_GROUPS)]
    for k in range(TOP_K):
        cols = _unpack_rows([r[k] for r in y_refs])
        blocks = [acc + wk[:, k:k + 1] * c.astype(F32) for acc, c in zip(blocks, cols)]
    f = jnp.concatenate(blocks, axis=1)
    o_ref[...] = _ln_rows(DEEPNORM_ALPHA * x1_ref[...] + g2_ref[0] * f) * lg_ref[...] + lb_ref[...]


def _moe_combine(h, x1, y_sel, wk_tok, w_sg, w_sd, g2, ln_g, ln_b, tm, b0, l, prev_out):
    t_part, d = x1.shape
    t = g2.shape[0] * l
    dp = h[0].shape[1]
    tiles_per_batch = l // tm
    off = b0 * tiles_per_batch
    ff = w_sd.shape[0]
    row = lambda i: (i, 0)
    const = lambda i: (0, 0)
    in_specs = ([pl.BlockSpec((tm, dp), row)] * PACK_GROUPS
                + [pl.BlockSpec((TOP_K, tm, dp), lambda i: (0, i, 0))] * PACK_GROUPS
                + [pl.BlockSpec((tm, d), row),
                   pl.BlockSpec((tm, TOP_K), row),
                   pl.BlockSpec(w_sg.shape, const),
                   pl.BlockSpec(w_sd.shape, const),
                   pl.BlockSpec((1, 1, d), lambda i: (i // tiles_per_batch + b0, 0, 0)),
                   pl.BlockSpec((1, d), const),
                   pl.BlockSpec((1, d), const)])
    args = [*h, *y_sel, x1, wk_tok, w_sg, w_sd, g2, ln_g.reshape(1, d), ln_b.reshape(1, d)]
    kern = functools.partial(_combine_kernel, ff=ff)
    aliases = {}
    if prev_out is not None:
        in_specs.append(pl.BlockSpec(memory_space=pl.ANY))
        aliases = {len(args): 0}
        args.append(prev_out)
        kern = lambda *refs: _combine_kernel(*refs[:-2], refs[-1], ff=ff)
    return pl.pallas_call(
        kern,
        grid=(t_part // tm,),
        in_specs=in_specs,
        out_specs=pl.BlockSpec((tm, d), lambda i: (i + off, 0)),
        out_shape=jax.ShapeDtypeStruct((t, d), F32),
        input_output_aliases=aliases,
        compiler_params=_params(("arbitrary",)),
        name="moe_combine",
    )(*args)


def _routing_tables(ek, rk, counts, n_rows):
    padded = (counts + EXPERT_TILE - 1) // EXPERT_TILE * EXPERT_TILE
    ends = jnp.cumsum(padded)
    starts = ends - padded
    experts = jnp.arange(N_EXPERTS, dtype=jnp.int32)
    start_of = jnp.sum(jnp.where(ek[None] == experts[:, None, None], starts[:, None, None], 0), axis=0)
    pos = start_of + rk
    tile_start = jnp.arange(n_rows // EXPERT_TILE, dtype=jnp.int32) * EXPERT_TILE
    tile_expert = jnp.minimum(jnp.sum(ends[None, :] <= tile_start[:, None], axis=1), N_EXPERTS - 1).astype(jnp.int32)
    used_tiles = (ends[-1:] // EXPERT_TILE).astype(jnp.int32)
    return pos.astype(jnp.int32), tile_expert, used_tiles


def _sc_mesh():
    return plsc.VectorSubcoreMesh(core_axis_name="core", subcore_axis_name="subcore")


def _dispatch_rows(rows, pos, n_rows):
    t, dp = rows.shape
    n_k = pos.shape[0]

    @functools.partial(pl.kernel, mesh=_sc_mesh(), out_type=jax.ShapeDtypeStruct((n_rows, dp), rows.dtype),
                       scratch_types=[])
    def scatter(rows_hbm, pos_hbm, out_hbm):
        def body(rows_vmem, idx_vmem):
            pltpu.sync_copy(rows_vmem, out_hbm.at[idx_vmem.at[0]])

        pltpu.emit_pipeline(
            body,
            grid=(t // SC_WINDOW, n_k),
            in_specs=[pl.BlockSpec((SC_WINDOW, dp), lambda i, k: (i, 0)),
                      pl.BlockSpec((1, SC_WINDOW), lambda i, k: (k, i))],
            out_specs=[],
            core_axis_name=("core", "subcore"),
            dimension_semantics=(pltpu.PARALLEL, pltpu.ARBITRARY),
        )(rows_hbm, pos_hbm)

    return scatter(rows, pos)


def _gather_rows(table, pos):
    n_k, t = pos.shape
    dp = table.shape[1]

    @functools.partial(pl.kernel, mesh=_sc_mesh(), out_type=jax.ShapeDtypeStruct((n_k * t, dp), table.dtype),
                       scratch_types=[])
    def gather(table_hbm, pos_hbm, out_hbm):
        def body(idx_vmem, out_vmem):
            pltpu.sync_copy(table_hbm.at[idx_vmem.at[0]], out_vmem)

        pltpu.emit_pipeline(
            body,
            grid=(n_k * t // SC_WINDOW,),
            in_specs=[pl.BlockSpec((1, SC_WINDOW), lambda i: (0, i))],
            out_specs=[pl.BlockSpec((SC_WINDOW, dp), lambda i: (i, 0))],
            core_axis_name=("core", "subcore"),
            dimension_semantics=(pltpu.PARALLEL,),
        )(pos_hbm, out_hbm)

    return gather(table, pos.reshape(1, n_k * t)).reshape(n_k, t, dp)


def _routed_moe(h, x1, ek, rk, wk, cnt, w_sg, w_sd, w_gu, w_d, g2, ln_g, ln_b, b0, l, prev_out):
    t = x1.shape[0]
    n_rows = t * TOP_K + N_EXPERTS * EXPERT_TILE
    pos, tile_expert, used_tiles = _routing_tables(ek, rk, cnt[:, 0].astype(jnp.int32), n_rows)
    xs = [_dispatch_rows(rows, pos, n_rows) for rows in h]
    ys = _expert_ffn(xs, tile_expert, used_tiles, w_gu, w_d)
    y_sel = [_gather_rows(table, pos) for table in ys]
    return _moe_combine(h, x1, y_sel, wk.T, w_sg, w_sd, g2, ln_g, ln_b, 256, b0, l, prev_out)


def _gate_layouts(g_lat, g_ctx):
    g = jnp.concatenate([g_ctx, g_lat], axis=1)[..., :4 * GDN_HEADS]
    b, lt, _ = g.shape
    nt = lt // GDN_CHUNK
    rows = jnp.transpose(g.reshape(b, nt, GDN_CHUNK, 4, GDN_HEADS), (0, 4, 1, 3, 2))
    return jnp.pad(rows, ((0, 0), (0, 0), (0, 0), (0, 4), (0, 0))).reshape(b, GDN_HEADS, nt * 8, GDN_CHUNK)


def kernel(x, c, ctx, c_ctx, w_mod, b_mod, w_in, conv_w, gdn_a_log, gdn_dt_bias, gdn_norm_g, ret_log_gamma,
           ret_norm_g, w_out, ln1_g, ln1_b, w_router, router_bias, w_gate_up, w_down, w_shared_gate_up,
           w_shared_down, ln2_g, ln2_b):
    b, l, d = x.shape
    lc = ctx.shape[1]
    gw = GDN_HEADS * HEAD_DIM

    n_mod = -(-(b + 1) // 8) * 8
    cc = jnp.zeros((n_mod, d), F32).at[:b].set(c).at[b].set(c_ctx)
    mod = _modulation(cc, w_mod[0], b_mod[0])
    sh1, sc1, g1, sh2, sc2, g2 = [mod[:b, i * d:(i + 1) * d].reshape(b, 1, d) for i in range(6)]
    csh1 = jnp.broadcast_to(mod[b, 0:d].reshape(1, 1, d), (b, 1, d))
    csc1 = jnp.broadcast_to(mod[b, d:2 * d].reshape(1, 1, d), (b, 1, d))

    w = w_in[0]
    o_gate = 3 * gw + gw
    o_ret = o_gate + 4 * GDN_HEADS
    w_main = jnp.concatenate([w[:, :o_gate], w[:, o_ret:]], axis=1).astype(BF16)
    w_gate = w[:, o_gate:o_ret]
    cos_t, sin_t = _rope_tables(l)
    wo = w_out[0].astype(BF16)
    w_sg, w_sd = w_shared_gate_up[0].astype(BF16), w_shared_down[0].astype(BF16)
    w_router_t = w_router[0].T

    nb = b // BATCH_PARTS
    out = None
    for part in range(BATCH_PARTS):
        b0 = part * nb
        qkv_lat, z_lat, ret_lat, g_lat = _inproj(x, sh1, sc1, w_main, w_gate, conv_w[0], 512, b0, nb)
        qkv_ctx, _, ret_ctx, g_ctx = _inproj(ctx, csh1, csc1, w_main, w_gate, conv_w[0], lc, b0, nb)
        gate_rows = _gate_layouts(g_lat, g_ctx)
        a_lat = _gdn_mixer(qkv_lat, qkv_ctx, z_lat, gate_rows, gdn_a_log[0],
                           gdn_dt_bias[0], gdn_norm_g[0], hb=2)
        r_lat = _ret_mixer(ret_lat, ret_ctx, ret_log_gamma[0], ret_norm_g[0], cos_t, sin_t)
        x1, h_a, h_b, ek, rk, wk, cnt = _outproj(a_lat, r_lat, x, wo[:gw], wo[gw:], g1, sh2, sc2, ln1_g[0], ln1_b[0],
                                                 w_router_t, router_bias[0], 512, b0)
        h = [arr.reshape(nb * l, arr.shape[-1]) for arr in (h_a, h_b)]
        out = _routed_moe(h, x1.reshape(nb * l, d), ek, rk, wk, cnt, w_sg, w_sd, w_gate_up[0], w_down[0],
                          g2, ln2_g[0], ln2_b[0], b0, l, out)
    return out.reshape(b, l, d)
```

```python
import functools

import jax
import jax.numpy as jnp
from jax import lax
from jax.experimental import pallas as pl
from jax.experimental.pallas import tpu as pltpu
from jax.experimental.pallas import tpu_sc as plsc

F32 = jnp.float32
BF16 = jnp.bfloat16

HEAD_DIM = 128
GDN_HEADS = 4
RET_HEADS = 4
CONV_W = 5
CONV_ROWS = 128
HALO = 16
GDN_CHUNK = 64
INV_BASE = 16
RET_CHUNK = 256
GRID_W = 64
ROPE_THETA = 10000.0
N_EXPERTS = 64
TOP_K = 8
N_GROUPS = 8
TOPK_GROUPS = 4
ROUTED_SCALE = 2.5
SC_WINDOW = 128
PACK_GROUPS = 2
BATCH_PARTS = 2
EXPERT_SUBTILES = 4
EXPERT_TILE = 512
LN_EPS = 1e-6
DEPTH = 1
DEEPNORM_ALPHA = (2 * DEPTH) ** 0.25

VMEM_LIMIT = 56 * 1024 * 1024
HIGHEST = lax.Precision.HIGHEST
NT_DIMS = (((1,), (1,)), ((), ()))


def _dot(a, b, precision=None):
    return jnp.dot(a, b, preferred_element_type=F32, precision=precision)


def _dot_nt(a, b, precision=None):
    return lax.dot_general(a, b, NT_DIMS, preferred_element_type=F32, precision=precision)


def _silu(x):
    return x * (1.0 / (1.0 + jnp.exp(-x)))


def _sigmoid(x):
    return 1.0 / (1.0 + jnp.exp(-x))


def _softplus(x):
    return jnp.maximum(x, 0.0) + jnp.log(1.0 + jnp.exp(-jnp.abs(x)))


def _ln_rows(x):
    mu = jnp.mean(x, axis=-1, keepdims=True)
    xc = x - mu
    var = jnp.mean(xc * xc, axis=-1, keepdims=True)
    return xc * lax.rsqrt(var + LN_EPS)


def _params(sem):
    return pltpu.CompilerParams(dimension_semantics=sem, vmem_limit_bytes=VMEM_LIMIT)


def _mod_kernel(c_ref, w_ref, b_ref, o_ref):
    o_ref[...] = _dot(_silu(c_ref[...]), w_ref[...], HIGHEST) + b_ref[...]


def _modulation(cc, w_mod, b_mod):
    rows, d = cc.shape
    n = w_mod.shape[1]
    tn = 1024
    return pl.pallas_call(
        _mod_kernel,
        grid=(n // tn,),
        in_specs=[pl.BlockSpec((rows, d), lambda j: (0, 0)),
                  pl.BlockSpec((d, tn), lambda j: (0, j)),
                  pl.BlockSpec((1, tn), lambda j: (0, j))],
        out_specs=pl.BlockSpec((rows, tn), lambda j: (0, j)),
        out_shape=jax.ShapeDtypeStruct((rows, n), F32),
        compiler_params=_params(("arbitrary",)),
        name="modulation",
    )(cc, w_mod, b_mod.reshape(1, n))


def _inproj_kernel(x_ref, xp_ref, xn_ref, sh_ref, sc_ref, wm_ref, wg_ref, cw_ref, qkv_ref, z_ref, ret_ref, g_ref,
                   p_ref):
    j, nl = pl.program_id(1), pl.num_programs(1)
    tm = x_ref.shape[1]
    modulate = lambda x: _ln_rows(x) * (1.0 + sc_ref[0]) + sh_ref[0]
    h = modulate(x_ref[0])
    hb = h.astype(BF16)
    h_prev = jnp.where(j > 0, modulate(xp_ref[0]), 0.0).astype(BF16)
    h_next = jnp.where(j < nl - 1, modulate(xn_ref[0]), 0.0).astype(BF16)
    h_ext = jnp.concatenate([h_prev, hb, h_next], axis=0)
    half = CONV_W // 2
    gw = GDN_HEADS * HEAD_DIM
    for part in range(3):
        p_ref[part] = _dot(h_ext, wm_ref[:, part * gw:(part + 1) * gw])

    def conv_part(part):
        c0 = part * gw
        p = p_ref[part]
        acc = p[HALO - half:HALO - half + tm, :] * cw_ref[0:1, c0:c0 + gw]
        for tap in range(1, CONV_W):
            acc = acc + p[HALO - half + tap:HALO - half + tap + tm, :] * cw_ref[tap:tap + 1, c0:c0 + gw]
        y = _silu(acc)
        if part < 2:
            scale = HEAD_DIM ** -0.5 if part == 0 else 1.0
            blocks = [y[:, hd * HEAD_DIM:(hd + 1) * HEAD_DIM] for hd in range(GDN_HEADS)]
            blocks = [blk * (lax.rsqrt(jnp.sum(blk * blk, axis=-1, keepdims=True) + LN_EPS) * scale) for blk in blocks]
            y = jnp.concatenate(blocks, axis=1)
        qkv_ref[0, :, c0:c0 + gw] = y.astype(BF16)

    rest = [(ref, n0) for ref in (z_ref, ret_ref) for n0 in range(0, ref.shape[-1], 512)]
    col = 3 * gw
    for idx, (ref, n0) in enumerate(rest):
        if idx < 3:
            conv_part(idx)
        ref[0, :, n0:n0 + 512] = _dot(hb, wm_ref[:, col:col + 512]).astype(BF16)
        col += 512
    h_lo = (h - hb.astype(F32)).astype(BF16)
    wg = wg_ref[...]
    wg_hi = wg.astype(BF16)
    wg_lo = (wg - wg_hi.astype(F32)).astype(BF16)
    g_ref[0] = (_dot(h_lo, wg_hi) + _dot(hb, wg_lo)) + _dot(hb, wg_hi)


def _inproj(x, sh, sc, w_main, w_gate, conv_w, tm, b0, b):
    _, l, d = x.shape
    widths = (3 * GDN_HEADS * HEAD_DIM, GDN_HEADS * HEAD_DIM, 4 * RET_HEADS * HEAD_DIM)
    n_gate = w_gate.shape[1]
    per_tile, last = tm // HALO, l // HALO - 1
    row = lambda i, j: (i, j, 0)
    const = lambda i, j: (0, 0)
    return pl.pallas_call(
        _inproj_kernel,
        grid=(b, l // tm),
        in_specs=[pl.BlockSpec((1, tm, d), lambda i, j: (i + b0, j, 0)),
                  pl.BlockSpec((1, HALO, d), lambda i, j: (i + b0, jnp.maximum(j * per_tile - 1, 0), 0)),
                  pl.BlockSpec((1, HALO, d), lambda i, j: (i + b0, jnp.minimum((j + 1) * per_tile, last), 0)),
                  pl.BlockSpec((1, 1, d), lambda i, j: (i + b0, 0, 0)),
                  pl.BlockSpec((1, 1, d), lambda i, j: (i + b0, 0, 0)),
                  pl.BlockSpec(w_main.shape, const),
                  pl.BlockSpec(w_gate.shape, const),
                  pl.BlockSpec(conv_w.shape, const)],
        out_specs=[pl.BlockSpec((1, tm, w), row) for w in widths]
                  + [pl.BlockSpec((1, tm, n_gate), row)],
        out_shape=[jax.ShapeDtypeStruct((b, l, w), BF16) for w in widths]
                  + [jax.ShapeDtypeStruct((b, l, n_gate), F32)],
        scratch_shapes=[pltpu.VMEM((3, tm + 2 * HALO, GDN_HEADS * HEAD_DIM), F32)],
        compiler_params=_params(("arbitrary", "arbitrary")),
        name="inproj",
    )(x, x, x, sh, sc, w_main, w_gate, conv_w)


def _bdot(a, b):
    return _dot(a.astype(BF16), b.astype(BF16))


def _pack_bf16_pair(x):
    n = x.shape[1] // 2
    bits = lambda v: pltpu.bitcast(v.astype(BF16).astype(F32), jnp.uint32)
    word = lax.shift_right_logical(bits(x[:, :n]), jnp.uint32(16)) | (bits(x[:, n:]) & jnp.uint32(0xFFFF0000))
    return pltpu.bitcast(word, jnp.int32)


def _unpack_bf16_pair(w):
    u = pltpu.bitcast(w, jnp.uint32)
    lo = pltpu.bitcast(lax.shift_left(u, jnp.uint32(16)), F32)
    hi = pltpu.bitcast(u & jnp.uint32(0xFFFF0000), F32)
    return lo.astype(BF16), hi.astype(BF16)


def _split3(x):
    hi = x.astype(BF16)
    r1 = x - hi.astype(F32)
    mid = r1.astype(BF16)
    lo = (r1 - mid.astype(F32)).astype(BF16)
    return hi, mid, lo


def _dot_split_lhs(x, mask01):
    m = mask01.astype(BF16)
    hi, mid, lo = _split3(x)
    return (_dot(lo, m) + _dot(mid, m)) + _dot(hi, m)


def _each(fn, *lists):
    return [fn(*args) for args in zip(*lists)]


def _inv_unit_triangular(mats, base=INV_BASE):
    n = mats[0].shape[0]
    i = lax.broadcasted_iota(jnp.int32, (n, n), 0)
    j = lax.broadcasted_iota(jnp.int32, (n, n), 1)
    eye = (i == j).astype(F32)
    shift = base.bit_length() - 1
    inside = (i >> shift) == (j >> shift)
    xs = _each(lambda a: jnp.where(inside, a, 0.0), mats)
    ts = _each(lambda d: eye - d, xs)
    for _ in range(shift - 1):
        xs = _each(lambda x: _bdot(x, x), xs)
        yield
        ts = _each(lambda t, x: t + _bdot(t, x), ts, xs)
        yield
    size = base
    while size < GDN_CHUNK:
        shift += 1
        size *= 2
        wider = (i >> shift) == (j >> shift)
        off = wider & jnp.logical_not(inside)
        ots = _each(lambda a, t: _bdot(jnp.where(off, a, 0.0), t), mats, ts)
        yield
        ts = _each(lambda t, ot: t - _bdot(t, ot), ts, ots)
        yield
        inside = wider
    return ts


def _gdn_kernel(alog_ref, dtb_ref,
                q_ref, k_ref, v_ref, qc_ref, kc_ref, vc_ref,
                gt_ref, z_ref, ng_ref,
                o_ref,
                qs_ref, ks_ref, vs_ref, gts_ref, gth_ref,
                wqf_ref, wqb_ref, u0_ref, intra_ref, ket_ref, gef_ref, geb_ref,
                of_ref, ob_ref, *, hb, l_lat, l_ctx, slots_per_group):
    c = GDN_CHUNK
    c2 = 2 * c
    lt = l_lat + l_ctx
    n_ctx = l_ctx // c
    nt = lt // c
    hgrp = pl.program_id(1)

    i2 = lax.broadcasted_iota(jnp.int32, (c2, c2), 0)
    j2 = lax.broadcasted_iota(jnp.int32, (c2, c2), 1)
    same_blk = (i2 & c) == (j2 & c)
    sgn_i = jnp.where(i2 < c, 1, -1)
    sgn_j = jnp.where(j2 < c, 1, -1)
    incl = same_blk & ((j2 - i2) * sgn_i <= 0)
    strict = same_blk & ((j2 - i2) * sgn_i < 0)
    tri_row = (same_blk & ((i2 - j2) * sgn_j <= 0)).astype(F32)
    eye_m = i2 == j2
    eye_b = eye_m.astype(BF16)
    lane_lt_c = j2 < c

    for hh in range(hb):
        head = hgrp * hb + hh
        lane0 = hh * HEAD_DIM
        for (src_c, src_l, dst) in ((qc_ref, q_ref, qs_ref), (kc_ref, k_ref, ks_ref), (vc_ref, v_ref, vs_ref)):
            dst[hh, 0:l_ctx, :] = src_c[0, :, lane0:lane0 + HEAD_DIM]
            dst[hh, l_ctx:lt, :] = src_l[0, :, lane0:lane0 + HEAD_DIM]

        a_f, a_b = alog_ref[head], alog_ref[GDN_HEADS + head]
        d_f, d_b = dtb_ref[head], dtb_ref[GDN_HEADS + head]
        compr = lax.broadcasted_iota(jnp.int32, (nt * 8, 1), 0) & 7
        neg_ar = -jnp.exp(jnp.where(compr == 2, a_f, a_b))
        dtbr = jnp.where(compr == 2, d_f, d_b)
        grow_raw = gt_ref[0, hh]
        grow = jnp.where(compr < 2, _sigmoid(grow_raw), neg_ar * _softplus(grow_raw + dtbr))
        gts_ref[hh] = jnp.zeros(gts_ref.shape[1:], F32)
        gts_ref[hh, :, 0:c] = grow
        gth_ref[hh] = jnp.zeros(gth_ref.shape[1:], F32)
        gth_ref[hh, :, c:c2] = grow

    spg = slots_per_group
    heads = list(range(hb))

    def bwd_chunk(i):
        return jnp.where(i < n_ctx, n_ctx - 1 - i, nt + n_ctx - 1 - i)

    def slot_load(i, hh):
        cbk = bwd_chunk(i)
        rf = pl.multiple_of(i * c, c)
        rb = pl.multiple_of(cbk * c, c)
        two = lambda ref: jnp.concatenate([ref[hh, pl.ds(rf, c), :], ref[hh, pl.ds(rb, c), :]], axis=0).astype(F32)
        g_r = (gts_ref[hh, pl.ds(pl.multiple_of(i * 8, 8), 8), :]
               + gth_ref[hh, pl.ds(pl.multiple_of(cbk * 8, 8), 8), :])
        return two(ks_ref), two(vs_ref), two(qs_ref), g_r

    def slot_stages(loaded):
        k2, v2, q2, g_r = [list(col) for col in zip(*loaded)]
        rcs = _each(lambda g: _dot_split_lhs(g, tri_row), g_r)
        yield
        k2b = _each(lambda x: x.astype(BF16), k2)
        kk = _each(_dot_nt, k2b, k2b)
        qk = _each(lambda q, kb: _dot_nt(q.astype(BF16), kb), q2, k2b)
        yield
        by_dir = lambda g, k: jnp.where(lane_lt_c[0:1, :], g[k:k + 1, :], g[k + 1:k + 2, :])
        lane_sum = lambda mask, row: jnp.sum(jnp.where(mask, row, 0.0), axis=1, keepdims=True)
        gcc = _each(lambda g: lane_sum(incl, by_dir(g, 2)), g_r)
        gcr = _each(lambda r: by_dir(r, 2), rcs)
        beta = _each(lambda g: lane_sum(eye_m, by_dir(g, 0)), g_r)
        gend = _each(lambda s: jnp.concatenate([jnp.broadcast_to(s[c - 1:c, :], (c, 1)),
                                                jnp.broadcast_to(s[c:c + 1, :], (c, 1))], axis=0), gcc)
        decay = _each(lambda gc, gr: jnp.exp(jnp.where(incl, gc - gr, -jnp.inf)), gcc, gcr)
        a_mat = _each(lambda m, d, b: jnp.where(strict, m * d, 0.0) * b, kk, decay, beta)
        t_f32 = yield from _inv_unit_triangular(a_mat)
        t_mat = _each(lambda t: t.astype(BF16), t_f32)
        egc = _each(jnp.exp, gcc)
        u0 = _each(lambda t, v, b: _dot(t, (v * b).astype(BF16)), t_mat, v2, beta)
        w = _each(lambda t, k, b, e: _dot(t, (k * (b * e)).astype(BF16)), t_mat, k2, beta, egc)
        ket = _each(lambda k, ge_, gc: _dot_nt(eye_b, (k * jnp.exp(ge_ - gc)).astype(BF16)), k2, gend, gcc)
        yield
        qd = _each(lambda q, e: (q * e).astype(BF16), q2, egc)
        intra = _each(lambda m, d: (m * d).astype(BF16), qk, decay)
        ge = _each(jnp.exp, gend)
        return [(jnp.concatenate([w_[0:c].astype(BF16), qd_[0:c]], axis=0),
                 jnp.concatenate([w_[c:c2].astype(BF16), qd_[c:c2]], axis=0), u0_, in_,
                 jnp.concatenate([jnp.where(lane_lt_c, kt, 0.0), jnp.where(lane_lt_c, 0.0, kt)], axis=0).astype(BF16),
                 jnp.broadcast_to(g_[0:1, :], (8, HEAD_DIM)), jnp.broadcast_to(g_[c:c + 1, :], (8, HEAD_DIM)))
                for w_, qd_, u0_, in_, kt, g_ in zip(w, qd, u0, intra, ket, ge)]

    def transform_group(g):
        par = lax.rem(g, 2)
        jobs = [(s, hh) for s in range(spg) for hh in heads]
        results = yield from slot_stages([slot_load(g * spg + s, hh) for s, hh in jobs])
        for (s, hh), (wq_f, wq_b, u0, intra, ket, ge_f, ge_b) in zip(jobs, results):
            wqf_ref[par, hh, s] = wq_f
            wqb_ref[par, hh, s] = wq_b
            u0_ref[par, hh, s] = u0
            intra_ref[par, hh, s] = intra
            ket_ref[par, hh, s] = ket
            gef_ref[par, hh, s] = ge_f
            geb_ref[par, hh, s] = ge_b

    def recur_group(g, states):
        par = lax.rem(g, 2)
        sf, sb = list(states[0::2]), list(states[1::2])
        for s in range(spg):
            i = g * spg + s
            rf = pl.multiple_of(i * c, c)
            rb = pl.multiple_of(bwd_chunk(i) * c, c)
            r_f = _each(lambda hh, st: _dot(wqf_ref[par, hh, s], st.astype(BF16)), heads, sf)
            r_b = _each(lambda hh, st: _dot(wqb_ref[par, hh, s], st.astype(BF16)), heads, sb)
            u0 = _each(lambda hh: u0_ref[par, hh, s], heads)
            yield
            u2 = _each(lambda u, f, b_: jnp.concatenate([u[0:c] - f[0:c], u[c:c2] - b_[0:c]], axis=0).astype(BF16),
                       u0, r_f, r_b)
            ket = _each(lambda hh: ket_ref[par, hh, s], heads)
            df = _each(lambda kt, u: _dot(kt[0:HEAD_DIM], u), ket, u2)
            db = _each(lambda kt, u: _dot(kt[HEAD_DIM:2 * HEAD_DIM], u), ket, u2)
            iu = _each(lambda hh, u: _dot(intra_ref[par, hh, s], u), heads, u2)
            sf = _each(lambda hh, st, d: gef_ref[par, hh, s][0:1, :] * st + d, heads, sf, df)
            sb = _each(lambda hh, st, d: geb_ref[par, hh, s][0:1, :] * st + d, heads, sb, db)
            for hh in heads:
                of_ref[hh, pl.ds(rf, c), :] = r_f[hh][c:c2] + iu[hh][0:c]
                ob_ref[hh, pl.ds(rb, c), :] = r_b[hh][c:c2] + iu[hh][c:c2]
            yield
        return tuple(x for pair in zip(sf, sb) for x in pair)

    def drive(transform, recur, transforms_per_recur=1):
        states = None
        while transform is not None or recur is not None:
            if transform is not None:
                for _ in range(transforms_per_recur):
                    try:
                        next(transform)
                    except StopIteration:
                        transform = None
                        break
            if recur is not None:
                try:
                    next(recur)
                except StopIteration as stop:
                    states, recur = stop.value, None
        return states

    n_groups = nt // spg
    zero_state = tuple(jnp.zeros((HEAD_DIM, HEAD_DIM), F32) for _ in range(2 * hb))
    drive(transform_group(0), None)
    states = lax.fori_loop(0, n_groups - 1,
                           lambda g, st: drive(transform_group(g + 1), recur_group(g, st)), zero_state)
    drive(None, recur_group(n_groups - 1, states))

    ng = ng_ref[...]
    blk = 256
    for hh in range(hb):
        lane0 = hh * HEAD_DIM

        def fin_body(i, carry):
            r = pl.multiple_of(i * blk, blk)
            o = of_ref[hh, pl.ds(l_ctx + r, blk), :] + ob_ref[hh, pl.ds(l_ctx + r, blk), :]
            y = o * lax.rsqrt(jnp.mean(o * o, axis=-1, keepdims=True) + LN_EPS) * ng
            z = z_ref[0, pl.ds(r, blk), lane0:lane0 + HEAD_DIM].astype(F32)
            o_ref[0, pl.ds(r, blk), lane0:lane0 + HEAD_DIM] = (y * _silu(z)).astype(BF16)
            return carry

        lax.fori_loop(0, l_lat // blk, fin_body, 0)


def _gdn_mixer(qkv_lat, qkv_ctx, z_lat, gate_rows, a_log, dt_bias, norm_g, hb):
    b, l_lat, _ = qkv_lat.shape
    l_ctx = qkv_ctx.shape[1]
    lt = l_lat + l_ctx
    nt = lt // GDN_CHUNK
    hw = hb * HEAD_DIM
    ngrp = GDN_HEADS // hb
    c2 = 2 * GDN_CHUNK
    spg = 6
    assert nt % spg == 0 and nt // spg >= 2 and GDN_HEADS % hb == 0

    def seq_spec(length, part):
        return pl.BlockSpec((1, length, hw), lambda i, j, *_: (i, 0, part * ngrp + j))

    grid_spec = pltpu.PrefetchScalarGridSpec(
        num_scalar_prefetch=2,
        grid=(b, ngrp),
        in_specs=[seq_spec(l_lat, 0), seq_spec(l_lat, 1), seq_spec(l_lat, 2),
                  seq_spec(l_ctx, 0), seq_spec(l_ctx, 1), seq_spec(l_ctx, 2),
                  pl.BlockSpec((1, hb, nt * 8, GDN_CHUNK), lambda i, j, *_: (i, j, 0, 0)),
                  pl.BlockSpec((1, l_lat, hw), lambda i, j, *_: (i, 0, j)),
                  pl.BlockSpec((1, HEAD_DIM), lambda i, j, *_: (0, 0))],
        out_specs=pl.BlockSpec((1, l_lat, hw), lambda i, j, *_: (i, 0, j)),
        scratch_shapes=[
            pltpu.VMEM((hb, lt, HEAD_DIM), BF16),
            pltpu.VMEM((hb, lt, HEAD_DIM), BF16),
            pltpu.VMEM((hb, lt, HEAD_DIM), BF16),
            pltpu.VMEM((hb, nt * 8, c2), F32),
            pltpu.VMEM((hb, nt * 8, c2), F32),
            pltpu.VMEM((2, hb, spg, c2, HEAD_DIM), BF16),
            pltpu.VMEM((2, hb, spg, c2, HEAD_DIM), BF16),
            pltpu.VMEM((2, hb, spg, c2, HEAD_DIM), F32),
            pltpu.VMEM((2, hb, spg, c2, c2), BF16),
            pltpu.VMEM((2, hb, spg, 2 * HEAD_DIM, c2), BF16),
            pltpu.VMEM((2, hb, spg, 8, HEAD_DIM), F32),
            pltpu.VMEM((2, hb, spg, 8, HEAD_DIM), F32),
            pltpu.VMEM((hb, lt, HEAD_DIM), F32),
            pltpu.VMEM((hb, lt, HEAD_DIM), F32),
        ])
    kern = functools.partial(_gdn_kernel, hb=hb, l_lat=l_lat, l_ctx=l_ctx, slots_per_group=spg)
    return pl.pallas_call(
        kern,
        grid_spec=grid_spec,
        out_shape=jax.ShapeDtypeStruct((b, l_lat, GDN_HEADS * HEAD_DIM), BF16),
        compiler_params=_params(("arbitrary", "arbitrary")),
        name="gdn_mixer",
    )(a_log, dt_bias, qkv_lat, qkv_lat, qkv_lat, qkv_ctx, qkv_ctx, qkv_ctx,
      gate_rows, z_lat, norm_g.reshape(1, HEAD_DIM))


def _ret_kernel(lg_ref, q_ref, k_ref, v_ref, gate_ref, qc_ref, kc_ref, vc_ref, cos_ref, sin_ref, ng_ref,
                o_ref, q_s, k_s, rf_s, rb_s, *, l_lat, l_ctx, hb):
    c = RET_CHUNK
    n_lat = l_lat // c
    n_ctx = l_ctx // c
    heads = list(range(hb))
    lanes = [slice(hh * HEAD_DIM, (hh + 1) * HEAD_DIM) for hh in heads]
    pos_c = lax.broadcasted_iota(jnp.int32, (c, 1), 0).astype(F32)
    ii = lax.broadcasted_iota(jnp.int32, (c, c), 0)
    jj = lax.broadcasted_iota(jnp.int32, (c, c), 1)
    dif = (ii - jj).astype(F32)
    e2 = lax.broadcasted_iota(jnp.int32, (HEAD_DIM, HEAD_DIM), 0)
    f2 = lax.broadcasted_iota(jnp.int32, (HEAD_DIM, HEAD_DIM), 1)
    eye_b = (e2 == f2).astype(BF16)
    kscale = HEAD_DIM ** -0.5
    lg_f = [lg_ref[pl.program_id(1) * hb + hh] for hh in heads]
    lg_b = [lg_ref[RET_HEADS + pl.program_id(1) * hb + hh] for hh in heads]
    dmat = _each(lambda f, b_: (jnp.exp(jnp.where(ii >= jj, f * dif, -jnp.inf))
                                + jnp.exp(jnp.where(jj >= ii, -b_ * dif, -jnp.inf))), lg_f, lg_b)
    kdec_f = _each(lambda f: jnp.exp(f * (c - 1 - pos_c)), lg_f)
    kdec_b = _each(lambda b_: jnp.exp(b_ * pos_c), lg_b)
    qdec_f = _each(lambda f: jnp.exp(f * (pos_c + 1.0)), lg_f)
    qdec_b = _each(lambda b_: jnp.exp(b_ * (c - pos_c)), lg_b)
    cd_f = _each(lambda f: jnp.exp(jnp.full((1, HEAD_DIM), f * c, F32)), lg_f)
    cd_b = _each(lambda b_: jnp.exp(jnp.full((1, HEAD_DIM), b_ * c, F32)), lg_b)

    def transposed(kd):
        return _dot_nt(eye_b, kd.astype(BF16)).astype(BF16)

    r_f = [jnp.zeros((HEAD_DIM, HEAD_DIM), F32) for _ in heads]
    r_b = [jnp.zeros((HEAD_DIM, HEAD_DIM), F32) for _ in heads]
    for n in range(n_ctx):
        m = n_ctx - 1 - n
        kf = _each(lambda ln, d: transposed(kc_ref[0, n * c:(n + 1) * c, ln].astype(F32) * kscale * d), lanes, kdec_f)
        kb = _each(lambda ln, d: transposed(kc_ref[0, m * c:(m + 1) * c, ln].astype(F32) * kscale * d), lanes, kdec_b)
        pf = _each(lambda kt, ln: _dot(kt, vc_ref[0, n * c:(n + 1) * c, ln]), kf, lanes)
        pb = _each(lambda kt, ln: _dot(kt, vc_ref[0, m * c:(m + 1) * c, ln]), kb, lanes)
        r_f = _each(lambda d, r, p: d * r + p, cd_f, r_f, pf)
        r_b = _each(lambda d, r, p: d * r + p, cd_b, r_b, pb)

    def rope_body(n, carry):
        r = pl.multiple_of(n * c, c)
        cs = cos_ref[pl.ds(r, c), :]
        sn = sin_ref[pl.ds(r, c), :]
        for hh, ln in zip(heads, lanes):
            q = q_ref[0, pl.ds(r, c), ln].astype(F32)
            k = k_ref[0, pl.ds(r, c), ln].astype(F32) * kscale
            q_s[hh, pl.ds(r, c), :] = q * cs + pltpu.roll(q, HEAD_DIM // 2, 1) * sn
            k_s[hh, pl.ds(r, c), :] = k * cs + pltpu.roll(k, HEAD_DIM // 2, 1) * sn
        return carry

    lax.fori_loop(0, n_lat, rope_body, 0)

    chunks = list(range(n_lat))
    rows = [slice(n * c, (n + 1) * c) for n in chunks]
    jobs = [(hh, n) for hh in heads for n in chunks]
    k_c = [k_s[hh, rows[n], :] for hh, n in jobs]
    kft = [transposed(k * kdec_f[hh]) for k, (hh, n) in zip(k_c, jobs)]
    kbt = [transposed(k * kdec_b[hh]) for k, (hh, n) in zip(k_c, jobs)]
    kvf = [_dot(kt, v_ref[0, rows[n], lanes[hh]]) for kt, (hh, n) in zip(kft, jobs)]
    kvb = [_dot(kt, v_ref[0, rows[n], lanes[hh]]) for kt, (hh, n) in zip(kbt, jobs)]
    for hh in heads:
        r = r_f[hh]
        for n in chunks:
            rf_s[hh, n] = r
            r = cd_f[hh] * r + kvf[hh * n_lat + n]
        r = r_b[hh]
        for n in reversed(chunks):
            rb_s[hh, n] = r
            r = cd_b[hh] * r + kvb[hh * n_lat + n]

    group = 4
    for g0 in range(0, len(jobs), group):
        part = jobs[g0:g0 + group]
        q_c = [q_s[hh, rows[n], :] for hh, n in part]
        s = [_dot_nt(q.astype(BF16), k_s[hh, rows[n], :].astype(BF16)) for q, (hh, n) in zip(q_c, part)]
        att = [(s_ * dmat[hh]).astype(BF16) for s_, (hh, n) in zip(s, part)]
        o = [_dot(a, v_ref[0, rows[n], lanes[hh]]) for a, (hh, n) in zip(att, part)]
        qd = [jnp.concatenate([(q * qdec_f[hh]).astype(BF16), (q * qdec_b[hh]).astype(BF16)], axis=1)
              for q, (hh, n) in zip(q_c, part)]
        st = [jnp.concatenate([rf_s[hh, n], rb_s[hh, n]], axis=0).astype(BF16) for hh, n in part]
        o = _each(lambda o_, qd_, st_: o_ + _dot(qd_, st_), o, qd, st)
        for o_, (hh, n) in zip(o, part):
            y = _ln_rows(o_) * ng_ref[:, lanes[hh]]
            g = gate_ref[0, rows[n], lanes[hh]].astype(F32)
            o_ref[0, rows[n], lanes[hh]] = (y * _silu(g)).astype(BF16)


def _ret_mixer(ret_lat, ret_ctx, log_gamma, norm_g, cos_t, sin_t):
    b, l_lat, _ = ret_lat.shape
    l_ctx = ret_ctx.shape[1]
    n_lat = l_lat // RET_CHUNK
    hb = RET_HEADS
    hw = hb * HEAD_DIM
    ngrp = RET_HEADS // hb

    def seq_spec(length, part):
        return pl.BlockSpec((1, length, hw), lambda i, j, *_: (i, 0, part * ngrp + j))

    grid_spec = pltpu.PrefetchScalarGridSpec(
        num_scalar_prefetch=1,
        grid=(b, ngrp),
        in_specs=[seq_spec(l_lat, 0), seq_spec(l_lat, 1), seq_spec(l_lat, 2), seq_spec(l_lat, 3),
                  seq_spec(l_ctx, 0), seq_spec(l_ctx, 1), seq_spec(l_ctx, 2),
                  pl.BlockSpec((l_lat, HEAD_DIM), lambda i, j, *_: (0, 0)),
                  pl.BlockSpec((l_lat, HEAD_DIM), lambda i, j, *_: (0, 0)),
                  pl.BlockSpec((1, hw), lambda i, j, *_: (0, j))],
        out_specs=pl.BlockSpec((1, l_lat, hw), lambda i, j, *_: (i, 0, j)),
        scratch_shapes=[pltpu.VMEM((hb, l_lat, HEAD_DIM), F32),
                        pltpu.VMEM((hb, l_lat, HEAD_DIM), F32),
                        pltpu.VMEM((hb, n_lat, HEAD_DIM, HEAD_DIM), F32),
                        pltpu.VMEM((hb, n_lat, HEAD_DIM, HEAD_DIM), F32)])
    kern = functools.partial(_ret_kernel, l_lat=l_lat, l_ctx=l_ctx, hb=hb)
    return pl.pallas_call(
        kern,
        grid_spec=grid_spec,
        out_shape=jax.ShapeDtypeStruct((b, l_lat, RET_HEADS * HEAD_DIM), BF16),
        compiler_params=_params(("arbitrary", "arbitrary")),
        name="ret_mixer",
    )(log_gamma.reshape(-1), ret_lat, ret_lat, ret_lat, ret_lat, ret_ctx, ret_ctx, ret_ctx,
      cos_t, sin_t, norm_g.reshape(1, -1))


def _rope_tables(l_lat):
    rows = l_lat // GRID_W
    row = jnp.repeat(jnp.arange(rows, dtype=F32), GRID_W)
    col = jnp.tile(jnp.arange(GRID_W, dtype=F32), rows)
    quarter = HEAD_DIM // 4
    inv = ROPE_THETA ** (-jnp.arange(quarter, dtype=F32) / quarter)
    ang = jnp.concatenate([row[:, None] * inv, col[:, None] * inv], -1)
    cos, sin = jnp.cos(ang), jnp.sin(ang)
    return jnp.concatenate([cos, cos], -1), jnp.concatenate([-sin, sin], -1)


def _top_rows(vals, k):
    n = vals.shape[0]
    idx = lax.broadcasted_iota(jnp.int32, vals.shape, 0)
    taken = jnp.zeros(vals.shape, jnp.int32)
    firsts = []
    for _ in range(k):
        live = jnp.where(taken == 0, vals, -jnp.inf)
        top = jnp.max(live, axis=0, keepdims=True)
        cand = jnp.where((live == top) & (taken == 0), idx, n)
        first = jnp.min(cand, axis=0, keepdims=True)
        taken = taken + (idx == first).astype(jnp.int32)
        firsts.append(first)
    return firsts, taken


def _outproj_kernel(a_ref, r_ref, x_ref, wa_ref, wr_ref, g1_ref, sh2_ref, sc2_ref, lg_ref, lb_ref,
                    wrt_ref, rb_ref, x1_ref, ha_ref, hb_ref, ek_ref, rk_ref, wk_ref, cnt_ref, carry_ref):
    y = _dot(a_ref[0], wa_ref[...]) + _dot(r_ref[0], wr_ref[...])
    x1 = _ln_rows(DEEPNORM_ALPHA * x_ref[0] + g1_ref[0] * y) * lg_ref[...] + lb_ref[...]
    x1_ref[0] = x1
    h = _ln_rows(x1) * (1.0 + sc2_ref[0]) + sh2_ref[0]
    for ref, part in zip((ha_ref, hb_ref), _pack_rows(h)):
        ref[0] = part
    s = _sigmoid(_dot_nt(wrt_ref[...], h, HIGHEST))
    sb = s + rb_ref[...]
    tm = s.shape[1]
    per = N_EXPERTS // N_GROUPS
    sub = lax.broadcasted_iota(jnp.int32, (per, tm), 0)
    gs_rows = []
    for g in range(N_GROUPS):
        blk = sb[g * per:(g + 1) * per, :]
        m1 = jnp.max(blk, axis=0, keepdims=True)
        first = jnp.min(jnp.where(blk == m1, sub, per), axis=0, keepdims=True)
        m2 = jnp.max(jnp.where(sub == first, -jnp.inf, blk), axis=0, keepdims=True)
        gs_rows.append(m1 + m2)
    gscore = jnp.concatenate(gs_rows, axis=0)
    _, gtaken = _top_rows(gscore, TOPK_GROUPS)
    emask = jnp.concatenate([jnp.broadcast_to(gtaken[g:g + 1, :], (per, tm)) for g in range(N_GROUPS)], axis=0)
    masked = jnp.where(emask > 0, sb, -jnp.inf)
    firsts, taken = _top_rows(masked, TOP_K)
    first_step = (pl.program_id(0) == 0) & (pl.program_id(1) == 0)

    @pl.when(first_step)
    def _():
        carry_ref[...] = jnp.zeros(carry_ref.shape, F32)

    sel_f = taken.astype(F32)
    ti = lax.broadcasted_iota(jnp.int32, (tm, tm), 0)
    tj = lax.broadcasted_iota(jnp.int32, (tm, tm), 1)
    rank = _dot(sel_f.astype(BF16), (ti < tj).astype(BF16)) + carry_ref[:, 0:1]
    carry = carry_ref[...] + jnp.sum(sel_f, axis=1, keepdims=True)
    carry_ref[...] = carry
    cnt_ref[...] = carry
    eidx = lax.broadcasted_iota(jnp.int32, (N_EXPERTS, tm), 0)
    picked = []
    for k in range(TOP_K):
        hit = eidx == firsts[k]
        pick = lambda v: jnp.sum(jnp.where(hit, v, 0.0), axis=0, keepdims=True)
        ek_ref[k:k + 1, :] = firsts[k]
        rk_ref[k:k + 1, :] = pick(rank).astype(jnp.int32)
        picked.append(pick(s))
    total = picked[0]
    for k in range(1, TOP_K):
        total = total + picked[k]
    for k in range(TOP_K):
        wk_ref[k:k + 1, :] = picked[k] / total * ROUTED_SCALE


def _outproj(a_lat, r_lat, x, w_a, w_r, g1, sh2, sc2, ln_g, ln_b, w_router_t, router_bias, tm, b0):
    b, l, half = a_lat.shape
    d = x.shape[-1]
    nl = l // tm
    row = lambda i, j: (i, j, 0)
    per_b = lambda i, j: (i + b0, 0, 0)
    const = lambda i, j: (0, 0)
    return pl.pallas_call(
        _outproj_kernel,
        grid=(b, nl),
        in_specs=[pl.BlockSpec((1, tm, half), row),
                  pl.BlockSpec((1, tm, half), row),
                  pl.BlockSpec((1, tm, d), lambda i, j: (i + b0, j, 0)),
                  pl.BlockSpec((half, d), const),
                  pl.BlockSpec((half, d), const),
                  pl.BlockSpec((1, 1, d), per_b),
                  pl.BlockSpec((1, 1, d), per_b),
                  pl.BlockSpec((1, 1, d), per_b),
                  pl.BlockSpec((1, d), const),
                  pl.BlockSpec((1, d), const),
                  pl.BlockSpec((N_EXPERTS, d), const),
                  pl.BlockSpec((N_EXPERTS, 1), const)],
        out_specs=[pl.BlockSpec((1, tm, d), row),
                   pl.BlockSpec((1, tm, d // (2 * PACK_GROUPS)), row),
                   pl.BlockSpec((1, tm, d // (2 * PACK_GROUPS)), row),
                   pl.BlockSpec((TOP_K, tm), lambda i, j: (0, i * nl + j)),
                   pl.BlockSpec((TOP_K, tm), lambda i, j: (0, i * nl + j)),
                   pl.BlockSpec((TOP_K, tm), lambda i, j: (0, i * nl + j)),
                   pl.BlockSpec((N_EXPERTS, HEAD_DIM), const)],
        out_shape=[jax.ShapeDtypeStruct((b, l, d), F32),
                   jax.ShapeDtypeStruct((b, l, d // (2 * PACK_GROUPS)), jnp.int32),
                   jax.ShapeDtypeStruct((b, l, d // (2 * PACK_GROUPS)), jnp.int32),
                   jax.ShapeDtypeStruct((TOP_K, b * l), jnp.int32),
                   jax.ShapeDtypeStruct((TOP_K, b * l), jnp.int32),
                   jax.ShapeDtypeStruct((TOP_K, b * l), F32),
                   jax.ShapeDtypeStruct((N_EXPERTS, HEAD_DIM), F32)],
        scratch_shapes=[pltpu.VMEM((N_EXPERTS, HEAD_DIM), F32)],
        compiler_params=_params(("arbitrary", "arbitrary")),
        name="outproj_router",
    )(a_lat, r_lat, x, w_a, w_r, g1, sh2, sc2, ln_g.reshape(1, d), ln_b.reshape(1, d),
      w_router_t, router_bias.reshape(N_EXPERTS, 1))


def _pack_rows(x):
    n = x.shape[1] // PACK_GROUPS
    return [_pack_bf16_pair(x[:, g * n:(g + 1) * n]) for g in range(PACK_GROUPS)]


def _unpack_rows(parts):
    cols = []
    for p in parts:
        cols += list(_unpack_bf16_pair(p))
    return cols


def _glu_ffn(cols, w_gate_up, w_down, ff):
    n = cols[0].shape[1]
    ab = _dot(cols[0], w_gate_up[0:n])
    for i in range(1, len(cols)):
        ab = ab + _dot(cols[i], w_gate_up[i * n:(i + 1) * n])
    act = (_silu(ab[:, :ff]) * ab[:, ff:]).astype(BF16)
    return _dot(act, w_down)


def _expert_kernel(te_ref, used_ref, *refs, ff):
    xs_refs, (wgu_ref, wd_ref) = refs[:PACK_GROUPS], refs[PACK_GROUPS:PACK_GROUPS + 2]
    ys_refs, (wgu_bf_ref, wd_bf_ref) = refs[PACK_GROUPS + 2:2 * PACK_GROUPS + 2], refs[2 * PACK_GROUPS + 2:]
    i = pl.program_id(0)

    @pl.when(i < used_ref[0])
    def _():
        @pl.when((i == 0) | (te_ref[i] != te_ref[jnp.maximum(i - 1, 0)]))
        def _():
            wgu_bf_ref[...] = wgu_ref[0].astype(BF16)
            wd_bf_ref[...] = wd_ref[0].astype(BF16)

        sub = EXPERT_TILE // EXPERT_SUBTILES
        blocks = [slice(s * sub, (s + 1) * sub) for s in range(EXPERT_SUBTILES)]
        wgu, wd = wgu_bf_ref[...], wd_bf_ref[...]
        cols = _each(lambda b: _unpack_rows([r[b, :] for r in xs_refs]), blocks)
        n = cols[0][0].shape[1]
        ab = _each(lambda c: _dot(c[0], wgu[0:n]), cols)
        for j in range(1, 2 * PACK_GROUPS):
            ab = _each(lambda acc, c: acc + _dot(c[j], wgu[j * n:(j + 1) * n]), ab, cols)
        act = _each(lambda a: (_silu(a[:, :ff]) * a[:, ff:]).astype(BF16), ab)
        y = _each(lambda a: _dot(a, wd), act)
        for b, y_b in zip(blocks, y):
            for ref, part in zip(ys_refs, _pack_rows(y_b)):
                ref[b, :] = part


def _expert_ffn(xs, tile_expert, used_tiles, w_gu, w_d):
    n_rows, dp = xs[0].shape
    n_e, d, ff2 = w_gu.shape
    ff = ff2 // 2
    tile = lambda i, te, used: (jnp.minimum(i, used[0] - 1), 0)
    grid_spec = pltpu.PrefetchScalarGridSpec(
        num_scalar_prefetch=2,
        grid=(n_rows // EXPERT_TILE,),
        in_specs=[pl.BlockSpec((EXPERT_TILE, dp), tile)] * PACK_GROUPS
                 + [pl.BlockSpec((1, d, ff2), lambda i, te, used: (te[i], 0, 0)),
                    pl.BlockSpec((1, ff, d), lambda i, te, used: (te[i], 0, 0))],
        out_specs=[pl.BlockSpec((EXPERT_TILE, dp), tile)] * PACK_GROUPS,
        scratch_shapes=[pltpu.VMEM((d, ff2), BF16), pltpu.VMEM((ff, d), BF16)])
    return pl.pallas_call(
        functools.partial(_expert_kernel, ff=ff),
        grid_spec=grid_spec,
        out_shape=[jax.ShapeDtypeStruct((n_rows, dp), jnp.int32)] * PACK_GROUPS,
        compiler_params=_params(("arbitrary",)),
        name="expert_ffn",
    )(tile_expert, used_tiles, *xs, w_gu, w_d)


def _combine_kernel(*refs, ff):
    h_refs, y_refs = refs[:PACK_GROUPS], refs[PACK_GROUPS:2 * PACK_GROUPS]
    x1_ref, wk_ref, wsg_ref, wsd_ref, g2_ref, lg_ref, lb_ref, o_ref = refs[2 * PACK_GROUPS:]
    shared = _glu_ffn(_unpack_rows([r[...] for r in h_refs]), wsg_ref[...], wsd_ref[...], ff)
    wk = wk_ref[...]
    n = y_refs[0].shape[2]
    blocks = [shared[:, i * n:(i + 1) * n] for i in range(2 * PACK_GROUPS)]
    for k in range(TOP_K):
        cols = _unpack_rows([r[k] for r in y_refs])
        blocks = [acc + wk[:, k:k + 1] * c.astype(F32) for acc, c in zip(blocks, cols)]
    f = jnp.concatenate(blocks, axis=1)
    o_ref[...] = _ln_rows(DEEPNORM_ALPHA * x1_ref[...] + g2_ref[0] * f) * lg_ref[...] + lb_ref[...]


def _moe_combine(h, x1, y_sel, wk_tok, w_sg, w_sd, g2, ln_g, ln_b, tm, b0, l, prev_out):
    t_part, d = x1.shape
    t = g2.shape[0] * l
    dp = h[0].shape[1]
    tiles_per_batch = l // tm
    off = b0 * tiles_per_batch
    ff = w_sd.shape[0]
    row = lambda i: (i, 0)
    const = lambda i: (0, 0)
    in_specs = ([pl.BlockSpec((tm, dp), row)] * PACK_GROUPS
                + [pl.BlockSpec((TOP_K, tm, dp), lambda i: (0, i, 0))] * PACK_GROUPS
                + [pl.BlockSpec((tm, d), row),
                   pl.BlockSpec((tm, TOP_K), row),
                   pl.BlockSpec(w_sg.shape, const),
                   pl.BlockSpec(w_sd.shape, const),
                   pl.BlockSpec((1, 1, d), lambda i: (i // tiles_per_batch + b0, 0, 0)),
                   pl.BlockSpec((1, d), const),
                   pl.BlockSpec((1, d), const)])
    args = [*h, *y_sel, x1, wk_tok, w_sg, w_sd, g2, ln_g.reshape(1, d), ln_b.reshape(1, d)]
    kern = functools.partial(_combine_kernel, ff=ff)
    aliases = {}
    if prev_out is not None:
        in_specs.append(pl.BlockSpec(memory_space=pl.ANY))
        aliases = {len(args): 0}
        args.append(prev_out)
        kern = lambda *refs: _combine_kernel(*refs[:-2], refs[-1], ff=ff)
    return pl.pallas_call(
        kern,
        grid=(t_part // tm,),
        in_specs=in_specs,
        out_specs=pl.BlockSpec((tm, d), lambda i: (i + off, 0)),
        out_shape=jax.ShapeDtypeStruct((t, d), F32),
        input_output_aliases=aliases,
        compiler_params=_params(("arbitrary",)),
        name="moe_combine",
    )(*args)


def _routing_tables(ek, rk, counts, n_rows):
    padded = (counts + EXPERT_TILE - 1) // EXPERT_TILE * EXPERT_TILE
    ends = jnp.cumsum(padded)
    starts = ends - padded
    experts = jnp.arange(N_EXPERTS, dtype=jnp.int32)
    start_of = jnp.sum(jnp.where(ek[None] == experts[:, None, None], starts[:, None, None], 0), axis=0)
    pos = start_of + rk
    tile_start = jnp.arange(n_rows // EXPERT_TILE, dtype=jnp.int32) * EXPERT_TILE
    tile_expert = jnp.minimum(jnp.sum(ends[None, :] <= tile_start[:, None], axis=1), N_EXPERTS - 1).astype(jnp.int32)
    used_tiles = (ends[-1:] // EXPERT_TILE).astype(jnp.int32)
    return pos.astype(jnp.int32), tile_expert, used_tiles


def _sc_mesh():
    return plsc.VectorSubcoreMesh(core_axis_name="core", subcore_axis_name="subcore")


def _dispatch_rows(rows, pos, n_rows):
    t, dp = rows.shape
    n_k = pos.shape[0]

    @functools.partial(pl.kernel, mesh=_sc_mesh(), out_type=jax.ShapeDtypeStruct((n_rows, dp), rows.dtype),
                       scratch_types=[])
    def scatter(rows_hbm, pos_hbm, out_hbm):
        def body(rows_vmem, idx_vmem):
            pltpu.sync_copy(rows_vmem, out_hbm.at[idx_vmem.at[0]])

        pltpu.emit_pipeline(
            body,
            grid=(t // SC_WINDOW, n_k),
            in_specs=[pl.BlockSpec((SC_WINDOW, dp), lambda i, k: (i, 0)),
                      pl.BlockSpec((1, SC_WINDOW), lambda i, k: (k, i))],
            out_specs=[],
            core_axis_name=("core", "subcore"),
            dimension_semantics=(pltpu.PARALLEL, pltpu.ARBITRARY),
        )(rows_hbm, pos_hbm)

    return scatter(rows, pos)


def _gather_rows(table, pos):
    n_k, t = pos.shape
    dp = table.shape[1]

    @functools.partial(pl.kernel, mesh=_sc_mesh(), out_type=jax.ShapeDtypeStruct((n_k * t, dp), table.dtype),
                       scratch_types=[])
    def gather(table_hbm, pos_hbm, out_hbm):
        def body(idx_vmem, out_vmem):
            pltpu.sync_copy(table_hbm.at[idx_vmem.at[0]], out_vmem)

        pltpu.emit_pipeline(
            body,
            grid=(n_k * t // SC_WINDOW,),
            in_specs=[pl.BlockSpec((1, SC_WINDOW), lambda i: (0, i))],
            out_specs=[pl.BlockSpec((SC_WINDOW, dp), lambda i: (i, 0))],
            core_axis_name=("core", "subcore"),
            dimension_semantics=(pltpu.PARALLEL,),
        )(pos_hbm, out_hbm)

    return gather(table, pos.reshape(1, n_k * t)).reshape(n_k, t, dp)


def _routed_moe(h, x1, ek, rk, wk, cnt, w_sg, w_sd, w_gu, w_d, g2, ln_g, ln_b, b0, l, prev_out):
    t = x1.shape[0]
    n_rows = t * TOP_K + N_EXPERTS * EXPERT_TILE
    pos, tile_expert, used_tiles = _routing_tables(ek, rk, cnt[:, 0].astype(jnp.int32), n_rows)
    xs = [_dispatch_rows(rows, pos, n_rows) for rows in h]
    ys = _expert_ffn(xs, tile_expert, used_tiles, w_gu, w_d)
    y_sel = [_gather_rows(table, pos) for table in ys]
    return _moe_combine(h, x1, y_sel, wk.T, w_sg, w_sd, g2, ln_g, ln_b, 512, b0, l, prev_out)


def _gate_layouts(g_lat, g_ctx):
    g = jnp.concatenate([g_ctx, g_lat], axis=1)[..., :4 * GDN_HEADS]
    b, lt, _ = g.shape
    nt = lt // GDN_CHUNK
    rows = jnp.transpose(g.reshape(b, nt, GDN_CHUNK, 4, GDN_HEADS), (0, 4, 1, 3, 2))
    return jnp.pad(rows, ((0, 0), (0, 0), (0, 0), (0, 4), (0, 0))).reshape(b, GDN_HEADS, nt * 8, GDN_CHUNK)


def kernel(x, c, ctx, c_ctx, w_mod, b_mod, w_in, conv_w, gdn_a_log, gdn_dt_bias, gdn_norm_g, ret_log_gamma,
           ret_norm_g, w_out, ln1_g, ln1_b, w_router, router_bias, w_gate_up, w_down, w_shared_gate_up,
           w_shared_down, ln2_g, ln2_b):
    b, l, d = x.shape
    lc = ctx.shape[1]
    gw = GDN_HEADS * HEAD_DIM

    n_mod = -(-(b + 1) // 8) * 8
    cc = jnp.zeros((n_mod, d), F32).at[:b].set(c).at[b].set(c_ctx)
    mod = _modulation(cc, w_mod[0], b_mod[0])
    sh1, sc1, g1, sh2, sc2, g2 = [mod[:b, i * d:(i + 1) * d].reshape(b, 1, d) for i in range(6)]
    csh1 = jnp.broadcast_to(mod[b, 0:d].reshape(1, 1, d), (b, 1, d))
    csc1 = jnp.broadcast_to(mod[b, d:2 * d].reshape(1, 1, d), (b, 1, d))

    w = w_in[0]
    o_gate = 3 * gw + gw
    o_ret = o_gate + 4 * GDN_HEADS
    w_main = jnp.concatenate([w[:, :o_gate], w[:, o_ret:]], axis=1).astype(BF16)
    w_gate = w[:, o_gate:o_ret]
    cos_t, sin_t = _rope_tables(l)
    wo = w_out[0].astype(BF16)
    w_sg, w_sd = w_shared_gate_up[0].astype(BF16), w_shared_down[0].astype(BF16)
    w_router_t = w_router[0].T

    nb = b // BATCH_PARTS
    out = None
    for part in range(BATCH_PARTS):
        b0 = part * nb
        qkv_lat, z_lat, ret_lat, g_lat = _inproj(x, sh1, sc1, w_main, w_gate, conv_w[0], 512, b0, nb)
        qkv_ctx, _, ret_ctx, g_ctx = _inproj(ctx, csh1, csc1, w_main, w_gate, conv_w[0], lc, b0, nb)
        gate_rows = _gate_layouts(g_lat, g_ctx)
        a_lat = _gdn_mixer(qkv_lat, qkv_ctx, z_lat, gate_rows, gdn_a_log[0],
                           gdn_dt_bias[0], gdn_norm_g[0], hb=2)
        r_lat = _ret_mixer(ret_lat, ret_ctx, ret_log_gamma[0], ret_norm_g[0], cos_t, sin_t)
        x1, h_a, h_b, ek, rk, wk, cnt = _outproj(a_lat, r_lat, x, wo[:gw], wo[gw:], g1, sh2, sc2, ln1_g[0], ln1_b[0],
                                                 w_router_t, router_bias[0], 512, b0)
        h = [arr.reshape(nb * l, arr.shape[-1]) for arr in (h_a, h_b)]
        out = _routed_moe(h, x1.reshape(nb * l, d), ek, rk, wk, cnt, w_sg, w_sd, w_gate_up[0], w_down[0],
                          g2, ln2_g[0], ln2_b[0], b0, l, out)
    return out.reshape(b, l, d)
```

```python
import functools

import jax
import jax.numpy as jnp
from jax import lax
from jax.experimental import pallas as pl
from jax.experimental.pallas import tpu as pltpu
from jax.experimental.pallas import tpu_sc as plsc

F32 = jnp.float32
BF16 = jnp.bfloat16

HEAD_DIM = 128
GDN_HEADS = 4
RET_HEADS = 4
CONV_W = 5
HALO = 16
GDN_CHUNK = 64
INV_BASE = 16
RET_CHUNK = 256
GRID_W = 64
ROPE_THETA = 10000.0
N_EXPERTS = 64
TOP_K = 8
N_GROUPS = 8
TOPK_GROUPS = 4
ROUTED_SCALE = 2.5
SC_WINDOW = 128
PACK_GROUPS = 2
ROW_TILE = 512
GDN_HEADS_PER_STEP = 2
GDN_SLOTS_PER_GROUP = 6
BATCH_PARTS = 2
EXPERT_SUBTILES = 4
EXPERT_TILE = 512
LN_EPS = 1e-6
DEPTH = 1
DEEPNORM_ALPHA = (2 * DEPTH) ** 0.25

VMEM_LIMIT = 56 * 1024 * 1024
HIGHEST = lax.Precision.HIGHEST
NT_DIMS = (((1,), (1,)), ((), ()))


def _dot(a, b, precision=None):
    return jnp.dot(a, b, preferred_element_type=F32, precision=precision)


def _dot_nt(a, b, precision=None):
    return lax.dot_general(a, b, NT_DIMS, preferred_element_type=F32, precision=precision)


def _silu(x):
    return x * (1.0 / (1.0 + jnp.exp(-x)))


def _sigmoid(x):
    return 1.0 / (1.0 + jnp.exp(-x))


def _softplus(x):
    return jnp.maximum(x, 0.0) + jnp.log(1.0 + jnp.exp(-jnp.abs(x)))


def _ln_rows(x):
    mu = jnp.mean(x, axis=-1, keepdims=True)
    xc = x - mu
    var = jnp.mean(xc * xc, axis=-1, keepdims=True)
    return xc * lax.rsqrt(var + LN_EPS)


def _params(sem):
    return pltpu.CompilerParams(dimension_semantics=sem, vmem_limit_bytes=VMEM_LIMIT)


def _mod_kernel(c_ref, w_ref, b_ref, o_ref):
    o_ref[...] = _dot(_silu(c_ref[...]), w_ref[...], HIGHEST) + b_ref[...]


def _modulation(cc, w_mod, b_mod):
    rows, d = cc.shape
    n = w_mod.shape[1]
    tn = 1024
    return pl.pallas_call(
        _mod_kernel,
        grid=(n // tn,),
        in_specs=[pl.BlockSpec((rows, d), lambda j: (0, 0)),
                  pl.BlockSpec((d, tn), lambda j: (0, j)),
                  pl.BlockSpec((1, tn), lambda j: (0, j))],
        out_specs=pl.BlockSpec((rows, tn), lambda j: (0, j)),
        out_shape=jax.ShapeDtypeStruct((rows, n), F32),
        compiler_params=_params(("arbitrary",)),
        name="modulation",
    )(cc, w_mod, b_mod.reshape(1, n))


def _inproj_kernel(x_ref, xp_ref, xn_ref, sh_ref, sc_ref, wm_ref, wg_ref, cw_ref, qkv_ref, z_ref, ret_ref, g_ref,
                   p_ref):
    j, nl = pl.program_id(1), pl.num_programs(1)
    tm = x_ref.shape[1]
    modulate = lambda x: _ln_rows(x) * (1.0 + sc_ref[0]) + sh_ref[0]
    h = modulate(x_ref[0])
    hb = h.astype(BF16)
    h_prev = jnp.where(j > 0, modulate(xp_ref[0]), 0.0).astype(BF16)
    h_next = jnp.where(j < nl - 1, modulate(xn_ref[0]), 0.0).astype(BF16)
    h_ext = jnp.concatenate([h_prev, hb, h_next], axis=0)
    half = CONV_W // 2
    gw = GDN_HEADS * HEAD_DIM
    for part in range(3):
        p_ref[part] = _dot(h_ext, wm_ref[:, part * gw:(part + 1) * gw])

    def conv_part(part):
        c0 = part * gw
        p = p_ref[part]
        acc = p[HALO - half:HALO - half + tm, :] * cw_ref[0:1, c0:c0 + gw]
        for tap in range(1, CONV_W):
            acc = acc + p[HALO - half + tap:HALO - half + tap + tm, :] * cw_ref[tap:tap + 1, c0:c0 + gw]
        y = _silu(acc)
        if part < 2:
            scale = HEAD_DIM ** -0.5 if part == 0 else 1.0
            blocks = [y[:, hd * HEAD_DIM:(hd + 1) * HEAD_DIM] for hd in range(GDN_HEADS)]
            blocks = [blk * (lax.rsqrt(jnp.sum(blk * blk, axis=-1, keepdims=True) + LN_EPS) * scale) for blk in blocks]
            y = jnp.concatenate(blocks, axis=1)
        qkv_ref[0, :, c0:c0 + gw] = y.astype(BF16)

    rest = [(ref, n0) for ref in (z_ref, ret_ref) for n0 in range(0, ref.shape[-1], 512)]
    col = 3 * gw
    for idx, (ref, n0) in enumerate(rest):
        if idx < 3:
            conv_part(idx)
        ref[0, :, n0:n0 + 512] = _dot(hb, wm_ref[:, col:col + 512]).astype(BF16)
        col += 512
    h_lo = (h - hb.astype(F32)).astype(BF16)
    wg = wg_ref[...]
    wg_hi = wg.astype(BF16)
    wg_lo = (wg - wg_hi.astype(F32)).astype(BF16)
    g_ref[0] = (_dot(h_lo, wg_hi) + _dot(hb, wg_lo)) + _dot(hb, wg_hi)


def _inproj(x, sh, sc, w_main, w_gate, conv_w, tm, b0, b):
    _, l, d = x.shape
    widths = (3 * GDN_HEADS * HEAD_DIM, GDN_HEADS * HEAD_DIM, 4 * RET_HEADS * HEAD_DIM)
    n_gate = w_gate.shape[1]
    per_tile, last = tm // HALO, l // HALO - 1
    row = lambda i, j: (i, j, 0)
    const = lambda i, j: (0, 0)
    return pl.pallas_call(
        _inproj_kernel,
        grid=(b, l // tm),
        in_specs=[pl.BlockSpec((1, tm, d), lambda i, j: (i + b0, j, 0)),
                  pl.BlockSpec((1, HALO, d), lambda i, j: (i + b0, jnp.maximum(j * per_tile - 1, 0), 0)),
                  pl.BlockSpec((1, HALO, d), lambda i, j: (i + b0, jnp.minimum((j + 1) * per_tile, last), 0)),
                  pl.BlockSpec((1, 1, d), lambda i, j: (i + b0, 0, 0)),
                  pl.BlockSpec((1, 1, d), lambda i, j: (i + b0, 0, 0)),
                  pl.BlockSpec(w_main.shape, const),
                  pl.BlockSpec(w_gate.shape, const),
                  pl.BlockSpec(conv_w.shape, const)],
        out_specs=[pl.BlockSpec((1, tm, w), row) for w in widths]
                  + [pl.BlockSpec((1, tm, n_gate), row)],
        out_shape=[jax.ShapeDtypeStruct((b, l, w), BF16) for w in widths]
                  + [jax.ShapeDtypeStruct((b, l, n_gate), F32)],
        scratch_shapes=[pltpu.VMEM((3, tm + 2 * HALO, GDN_HEADS * HEAD_DIM), F32)],
        compiler_params=_params(("arbitrary", "arbitrary")),
        name="inproj",
    )(x, x, x, sh, sc, w_main, w_gate, conv_w)


def _bdot(a, b):
    return _dot(a.astype(BF16), b.astype(BF16))


def _pack_bf16_pair(x):
    n = x.shape[1] // 2
    bits = lambda v: pltpu.bitcast(v.astype(BF16).astype(F32), jnp.uint32)
    word = lax.shift_right_logical(bits(x[:, :n]), jnp.uint32(16)) | (bits(x[:, n:]) & jnp.uint32(0xFFFF0000))
    return pltpu.bitcast(word, jnp.int32)


def _unpack_bf16_pair(w):
    u = pltpu.bitcast(w, jnp.uint32)
    lo = pltpu.bitcast(lax.shift_left(u, jnp.uint32(16)), F32)
    hi = pltpu.bitcast(u & jnp.uint32(0xFFFF0000), F32)
    return lo.astype(BF16), hi.astype(BF16)


def _split3(x):
    hi = x.astype(BF16)
    r1 = x - hi.astype(F32)
    mid = r1.astype(BF16)
    lo = (r1 - mid.astype(F32)).astype(BF16)
    return hi, mid, lo


def _dot_split_lhs(x, mask01):
    m = mask01.astype(BF16)
    hi, mid, lo = _split3(x)
    return (_dot(lo, m) + _dot(mid, m)) + _dot(hi, m)


def _each(fn, *lists):
    return [fn(*args) for args in zip(*lists)]


def _inv_unit_triangular(mats, base=INV_BASE):
    n = mats[0].shape[0]
    i = lax.broadcasted_iota(jnp.int32, (n, n), 0)
    j = lax.broadcasted_iota(jnp.int32, (n, n), 1)
    eye = (i == j).astype(F32)
    shift = base.bit_length() - 1
    inside = (i >> shift) == (j >> shift)
    xs = _each(lambda a: jnp.where(inside, a, 0.0), mats)
    ts = _each(lambda d: eye - d, xs)
    for _ in range(shift - 1):
        xs = _each(lambda x: _bdot(x, x), xs)
        yield
        ts = _each(lambda t, x: t + _bdot(t, x), ts, xs)
        yield
    size = base
    while size < GDN_CHUNK:
        shift += 1
        size *= 2
        wider = (i >> shift) == (j >> shift)
        off = wider & jnp.logical_not(inside)
        ots = _each(lambda a, t: _bdot(jnp.where(off, a, 0.0), t), mats, ts)
        yield
        ts = _each(lambda t, ot: t - _bdot(t, ot), ts, ots)
        yield
        inside = wider
    return ts


def _gdn_kernel(alog_ref, dtb_ref,
                q_ref, k_ref, v_ref, qc_ref, kc_ref, vc_ref,
                gt_ref, z_ref, ng_ref,
                o_ref,
                qs_ref, ks_ref, vs_ref, gts_ref, gth_ref,
                wqf_ref, wqb_ref, u0_ref, intra_ref, ket_ref, gef_ref, geb_ref,
                of_ref, ob_ref, *, hb, l_lat, l_ctx, slots_per_group):
    c = GDN_CHUNK
    c2 = 2 * c
    lt = l_lat + l_ctx
    n_ctx = l_ctx // c
    nt = lt // c
    hgrp = pl.program_id(1)

    i2 = lax.broadcasted_iota(jnp.int32, (c2, c2), 0)
    j2 = lax.broadcasted_iota(jnp.int32, (c2, c2), 1)
    same_blk = (i2 & c) == (j2 & c)
    sgn_i = jnp.where(i2 < c, 1, -1)
    sgn_j = jnp.where(j2 < c, 1, -1)
    incl = same_blk & ((j2 - i2) * sgn_i <= 0)
    strict = same_blk & ((j2 - i2) * sgn_i < 0)
    tri_row = (same_blk & ((i2 - j2) * sgn_j <= 0)).astype(F32)
    eye_m = i2 == j2
    eye_b = eye_m.astype(BF16)
    lane_lt_c = j2 < c

    for hh in range(hb):
        head = hgrp * hb + hh
        lane0 = hh * HEAD_DIM
        for (src_c, src_l, dst) in ((qc_ref, q_ref, qs_ref), (kc_ref, k_ref, ks_ref), (vc_ref, v_ref, vs_ref)):
            dst[hh, 0:l_ctx, :] = src_c[0, :, lane0:lane0 + HEAD_DIM]
            dst[hh, l_ctx:lt, :] = src_l[0, :, lane0:lane0 + HEAD_DIM]

        a_f, a_b = alog_ref[head], alog_ref[GDN_HEADS + head]
        d_f, d_b = dtb_ref[head], dtb_ref[GDN_HEADS + head]
        compr = lax.broadcasted_iota(jnp.int32, (nt * 8, 1), 0) & 7
        neg_ar = -jnp.exp(jnp.where(compr == 2, a_f, a_b))
        dtbr = jnp.where(compr == 2, d_f, d_b)
        grow_raw = gt_ref[0, hh]
        grow = jnp.where(compr < 2, _sigmoid(grow_raw), neg_ar * _softplus(grow_raw + dtbr))
        gts_ref[hh] = jnp.zeros(gts_ref.shape[1:], F32)
        gts_ref[hh, :, 0:c] = grow
        gth_ref[hh] = jnp.zeros(gth_ref.shape[1:], F32)
        gth_ref[hh, :, c:c2] = grow

    spg = slots_per_group
    heads = list(range(hb))

    def bwd_chunk(i):
        return jnp.where(i < n_ctx, n_ctx - 1 - i, nt + n_ctx - 1 - i)

    def slot_load(i, hh):
        cbk = bwd_chunk(i)
        rf = pl.multiple_of(i * c, c)
        rb = pl.multiple_of(cbk * c, c)
        two = lambda ref: jnp.concatenate([ref[hh, pl.ds(rf, c), :], ref[hh, pl.ds(rb, c), :]], axis=0).astype(F32)
        g_r = (gts_ref[hh, pl.ds(pl.multiple_of(i * 8, 8), 8), :]
               + gth_ref[hh, pl.ds(pl.multiple_of(cbk * 8, 8), 8), :])
        return two(ks_ref), two(vs_ref), two(qs_ref), g_r

    def slot_stages(loaded):
        k2, v2, q2, g_r = [list(col) for col in zip(*loaded)]
        rcs = _each(lambda g: _dot_split_lhs(g, tri_row), g_r)
        yield
        k2b = _each(lambda x: x.astype(BF16), k2)
        kk = _each(_dot_nt, k2b, k2b)
        qk = _each(lambda q, kb: _dot_nt(q.astype(BF16), kb), q2, k2b)
        yield
        by_dir = lambda g, k: jnp.where(lane_lt_c[0:1, :], g[k:k + 1, :], g[k + 1:k + 2, :])
        lane_sum = lambda mask, row: jnp.sum(jnp.where(mask, row, 0.0), axis=1, keepdims=True)
        gcc = _each(lambda g: lane_sum(incl, by_dir(g, 2)), g_r)
        gcr = _each(lambda r: by_dir(r, 2), rcs)
        beta = _each(lambda g: lane_sum(eye_m, by_dir(g, 0)), g_r)
        gend = _each(lambda s: jnp.concatenate([jnp.broadcast_to(s[c - 1:c, :], (c, 1)),
                                                jnp.broadcast_to(s[c:c + 1, :], (c, 1))], axis=0), gcc)
        decay = _each(lambda gc, gr: jnp.exp(jnp.where(incl, gc - gr, -jnp.inf)), gcc, gcr)
        a_mat = _each(lambda m, d, b: jnp.where(strict, m * d, 0.0) * b, kk, decay, beta)
        t_f32 = yield from _inv_unit_triangular(a_mat)
        t_mat = _each(lambda t: t.astype(BF16), t_f32)
        egc = _each(jnp.exp, gcc)
        u0 = _each(lambda t, v, b: _dot(t, (v * b).astype(BF16)), t_mat, v2, beta)
        w = _each(lambda t, k, b, e: _dot(t, (k * (b * e)).astype(BF16)), t_mat, k2, beta, egc)
        ket = _each(lambda k, ge_, gc: _dot_nt(eye_b, (k * jnp.exp(ge_ - gc)).astype(BF16)), k2, gend, gcc)
        yield
        qd = _each(lambda q, e: (q * e).astype(BF16), q2, egc)
        intra = _each(lambda m, d: (m * d).astype(BF16), qk, decay)
        ge = _each(jnp.exp, gend)
        return [(jnp.concatenate([w_[0:c].astype(BF16), qd_[0:c]], axis=0),
                 jnp.concatenate([w_[c:c2].astype(BF16), qd_[c:c2]], axis=0), u0_, in_,
                 jnp.concatenate([jnp.where(lane_lt_c, kt, 0.0), jnp.where(lane_lt_c, 0.0, kt)], axis=0).astype(BF16),
                 jnp.broadcast_to(g_[0:1, :], (8, HEAD_DIM)), jnp.broadcast_to(g_[c:c + 1, :], (8, HEAD_DIM)))
                for w_, qd_, u0_, in_, kt, g_ in zip(w, qd, u0, intra, ket, ge)]

    def transform_group(g):
        par = lax.rem(g, 2)
        jobs = [(s, hh) for s in range(spg) for hh in heads]
        results = yield from slot_stages([slot_load(g * spg + s, hh) for s, hh in jobs])
        for (s, hh), (wq_f, wq_b, u0, intra, ket, ge_f, ge_b) in zip(jobs, results):
            wqf_ref[par, hh, s] = wq_f
            wqb_ref[par, hh, s] = wq_b
            u0_ref[par, hh, s] = u0
            intra_ref[par, hh, s] = intra
            ket_ref[par, hh, s] = ket
            gef_ref[par, hh, s] = ge_f
            geb_ref[par, hh, s] = ge_b

    def recur_group(g, states):
        par = lax.rem(g, 2)
        sf, sb = list(states[0::2]), list(states[1::2])
        for s in range(spg):
            i = g * spg + s
            rf = pl.multiple_of(i * c, c)
            rb = pl.multiple_of(bwd_chunk(i) * c, c)
            r_f = _each(lambda hh, st: _dot(wqf_ref[par, hh, s], st.astype(BF16)), heads, sf)
            r_b = _each(lambda hh, st: _dot(wqb_ref[par, hh, s], st.astype(BF16)), heads, sb)
            u0 = _each(lambda hh: u0_ref[par, hh, s], heads)
            yield
            u2 = _each(lambda u, f, b_: jnp.concatenate([u[0:c] - f[0:c], u[c:c2] - b_[0:c]], axis=0).astype(BF16),
                       u0, r_f, r_b)
            ket = _each(lambda hh: ket_ref[par, hh, s], heads)
            df = _each(lambda kt, u: _dot(kt[0:HEAD_DIM], u), ket, u2)
            db = _each(lambda kt, u: _dot(kt[HEAD_DIM:2 * HEAD_DIM], u), ket, u2)
            iu = _each(lambda hh, u: _dot(intra_ref[par, hh, s], u), heads, u2)
            sf = _each(lambda hh, st, d: gef_ref[par, hh, s][0:1, :] * st + d, heads, sf, df)
            sb = _each(lambda hh, st, d: geb_ref[par, hh, s][0:1, :] * st + d, heads, sb, db)
            for hh in heads:
                of_ref[hh, pl.ds(rf, c), :] = r_f[hh][c:c2] + iu[hh][0:c]
                ob_ref[hh, pl.ds(rb, c), :] = r_b[hh][c:c2] + iu[hh][c:c2]
            yield
        return tuple(x for pair in zip(sf, sb) for x in pair)

    def drive(transform, recur, transforms_per_recur=1):
        states = None
        while transform is not None or recur is not None:
            if transform is not None:
                for _ in range(transforms_per_recur):
                    try:
                        next(transform)
                    except StopIteration:
                        transform = None
                        break
            if recur is not None:
                try:
                    next(recur)
                except StopIteration as stop:
                    states, recur = stop.value, None
        return states

    n_groups = nt // spg
    zero_state = tuple(jnp.zeros((HEAD_DIM, HEAD_DIM), F32) for _ in range(2 * hb))
    drive(transform_group(0), None)
    states = lax.fori_loop(0, n_groups - 1,
                           lambda g, st: drive(transform_group(g + 1), recur_group(g, st)), zero_state)
    drive(None, recur_group(n_groups - 1, states))

    ng = ng_ref[...]
    blk = 256
    for hh in range(hb):
        lane0 = hh * HEAD_DIM

        def fin_body(i, carry):
            r = pl.multiple_of(i * blk, blk)
            o = of_ref[hh, pl.ds(l_ctx + r, blk), :] + ob_ref[hh, pl.ds(l_ctx + r, blk), :]
            y = o * lax.rsqrt(jnp.mean(o * o, axis=-1, keepdims=True) + LN_EPS) * ng
            z = z_ref[0, pl.ds(r, blk), lane0:lane0 + HEAD_DIM].astype(F32)
            o_ref[0, pl.ds(r, blk), lane0:lane0 + HEAD_DIM] = (y * _silu(z)).astype(BF16)
            return carry

        lax.fori_loop(0, l_lat // blk, fin_body, 0)


def _gdn_mixer(qkv_lat, qkv_ctx, z_lat, gate_rows, a_log, dt_bias, norm_g, hb):
    b, l_lat, _ = qkv_lat.shape
    l_ctx = qkv_ctx.shape[1]
    lt = l_lat + l_ctx
    nt = lt // GDN_CHUNK
    hw = hb * HEAD_DIM
    ngrp = GDN_HEADS // hb
    c2 = 2 * GDN_CHUNK
    spg = GDN_SLOTS_PER_GROUP
    assert nt % spg == 0 and nt // spg >= 2 and GDN_HEADS % hb == 0

    def seq_spec(length, part):
        return pl.BlockSpec((1, length, hw), lambda i, j, *_: (i, 0, part * ngrp + j))

    grid_spec = pltpu.PrefetchScalarGridSpec(
        num_scalar_prefetch=2,
        grid=(b, ngrp),
        in_specs=[seq_spec(l_lat, 0), seq_spec(l_lat, 1), seq_spec(l_lat, 2),
                  seq_spec(l_ctx, 0), seq_spec(l_ctx, 1), seq_spec(l_ctx, 2),
                  pl.BlockSpec((1, hb, nt * 8, GDN_CHUNK), lambda i, j, *_: (i, j, 0, 0)),
                  pl.BlockSpec((1, l_lat, hw), lambda i, j, *_: (i, 0, j)),
                  pl.BlockSpec((1, HEAD_DIM), lambda i, j, *_: (0, 0))],
        out_specs=pl.BlockSpec((1, l_lat, hw), lambda i, j, *_: (i, 0, j)),
        scratch_shapes=[
            pltpu.VMEM((hb, lt, HEAD_DIM), BF16),
            pltpu.VMEM((hb, lt, HEAD_DIM), BF16),
            pltpu.VMEM((hb, lt, HEAD_DIM), BF16),
            pltpu.VMEM((hb, nt * 8, c2), F32),
            pltpu.VMEM((hb, nt * 8, c2), F32),
            pltpu.VMEM((2, hb, spg, c2, HEAD_DIM), BF16),
            pltpu.VMEM((2, hb, spg, c2, HEAD_DIM), BF16),
            pltpu.VMEM((2, hb, spg, c2, HEAD_DIM), F32),
            pltpu.VMEM((2, hb, spg, c2, c2), BF16),
            pltpu.VMEM((2, hb, spg, 2 * HEAD_DIM, c2), BF16),
            pltpu.VMEM((2, hb, spg, 8, HEAD_DIM), F32),
            pltpu.VMEM((2, hb, spg, 8, HEAD_DIM), F32),
            pltpu.VMEM((hb, lt, HEAD_DIM), F32),
            pltpu.VMEM((hb, lt, HEAD_DIM), F32),
        ])
    kern = functools.partial(_gdn_kernel, hb=hb, l_lat=l_lat, l_ctx=l_ctx, slots_per_group=spg)
    return pl.pallas_call(
        kern,
        grid_spec=grid_spec,
        out_shape=jax.ShapeDtypeStruct((b, l_lat, GDN_HEADS * HEAD_DIM), BF16),
        compiler_params=_params(("arbitrary", "arbitrary")),
        name="gdn_mixer",
    )(a_log, dt_bias, qkv_lat, qkv_lat, qkv_lat, qkv_ctx, qkv_ctx, qkv_ctx,
      gate_rows, z_lat, norm_g.reshape(1, HEAD_DIM))


def _ret_kernel(lg_ref, q_ref, k_ref, v_ref, gate_ref, qc_ref, kc_ref, vc_ref, cos_ref, sin_ref, ng_ref,
                o_ref, q_s, k_s, rf_s, rb_s, *, l_lat, l_ctx, hb):
    c = RET_CHUNK
    n_lat = l_lat // c
    n_ctx = l_ctx // c
    heads = list(range(hb))
    lanes = [slice(hh * HEAD_DIM, (hh + 1) * HEAD_DIM) for hh in heads]
    pos_c = lax.broadcasted_iota(jnp.int32, (c, 1), 0).astype(F32)
    ii = lax.broadcasted_iota(jnp.int32, (c, c), 0)
    jj = lax.broadcasted_iota(jnp.int32, (c, c), 1)
    dif = (ii - jj).astype(F32)
    e2 = lax.broadcasted_iota(jnp.int32, (HEAD_DIM, HEAD_DIM), 0)
    f2 = lax.broadcasted_iota(jnp.int32, (HEAD_DIM, HEAD_DIM), 1)
    eye_b = (e2 == f2).astype(BF16)
    kscale = HEAD_DIM ** -0.5
    lg_f = [lg_ref[pl.program_id(1) * hb + hh] for hh in heads]
    lg_b = [lg_ref[RET_HEADS + pl.program_id(1) * hb + hh] for hh in heads]
    dmat = _each(lambda f, b_: (jnp.exp(jnp.where(ii >= jj, f * dif, -jnp.inf))
                                + jnp.exp(jnp.where(jj >= ii, -b_ * dif, -jnp.inf))), lg_f, lg_b)
    kdec_f = _each(lambda f: jnp.exp(f * (c - 1 - pos_c)), lg_f)
    kdec_b = _each(lambda b_: jnp.exp(b_ * pos_c), lg_b)
    qdec_f = _each(lambda f: jnp.exp(f * (pos_c + 1.0)), lg_f)
    qdec_b = _each(lambda b_: jnp.exp(b_ * (c - pos_c)), lg_b)
    cd_f = _each(lambda f: jnp.exp(jnp.full((1, HEAD_DIM), f * c, F32)), lg_f)
    cd_b = _each(lambda b_: jnp.exp(jnp.full((1, HEAD_DIM), b_ * c, F32)), lg_b)

    def transposed(kd):
        return _dot_nt(eye_b, kd.astype(BF16)).astype(BF16)

    r_f = [jnp.zeros((HEAD_DIM, HEAD_DIM), F32) for _ in heads]
    r_b = [jnp.zeros((HEAD_DIM, HEAD_DIM), F32) for _ in heads]
    for n in range(n_ctx):
        m = n_ctx - 1 - n
        kf = _each(lambda ln, d: transposed(kc_ref[0, n * c:(n + 1) * c, ln].astype(F32) * kscale * d), lanes, kdec_f)
        kb = _each(lambda ln, d: transposed(kc_ref[0, m * c:(m + 1) * c, ln].astype(F32) * kscale * d), lanes, kdec_b)
        pf = _each(lambda kt, ln: _dot(kt, vc_ref[0, n * c:(n + 1) * c, ln]), kf, lanes)
        pb = _each(lambda kt, ln: _dot(kt, vc_ref[0, m * c:(m + 1) * c, ln]), kb, lanes)
        r_f = _each(lambda d, r, p: d * r + p, cd_f, r_f, pf)
        r_b = _each(lambda d, r, p: d * r + p, cd_b, r_b, pb)

    def rope_body(n, carry):
        r = pl.multiple_of(n * c, c)
        cs = cos_ref[pl.ds(r, c), :]
        sn = sin_ref[pl.ds(r, c), :]
        for hh, ln in zip(heads, lanes):
            q = q_ref[0, pl.ds(r, c), ln].astype(F32)
            k = k_ref[0, pl.ds(r, c), ln].astype(F32) * kscale
            q_s[hh, pl.ds(r, c), :] = q * cs + pltpu.roll(q, HEAD_DIM // 2, 1) * sn
            k_s[hh, pl.ds(r, c), :] = k * cs + pltpu.roll(k, HEAD_DIM // 2, 1) * sn
        return carry

    lax.fori_loop(0, n_lat, rope_body, 0)

    chunks = list(range(n_lat))
    rows = [slice(n * c, (n + 1) * c) for n in chunks]
    jobs = [(hh, n) for hh in heads for n in chunks]
    k_c = [k_s[hh, rows[n], :] for hh, n in jobs]
    kft = [transposed(k * kdec_f[hh]) for k, (hh, n) in zip(k_c, jobs)]
    kbt = [transposed(k * kdec_b[hh]) for k, (hh, n) in zip(k_c, jobs)]
    kvf = [_dot(kt, v_ref[0, rows[n], lanes[hh]]) for kt, (hh, n) in zip(kft, jobs)]
    kvb = [_dot(kt, v_ref[0, rows[n], lanes[hh]]) for kt, (hh, n) in zip(kbt, jobs)]
    for hh in heads:
        r = r_f[hh]
        for n in chunks:
            rf_s[hh, n] = r
            r = cd_f[hh] * r + kvf[hh * n_lat + n]
        r = r_b[hh]
        for n in reversed(chunks):
            rb_s[hh, n] = r
            r = cd_b[hh] * r + kvb[hh * n_lat + n]

    group = 4
    for g0 in range(0, len(jobs), group):
        part = jobs[g0:g0 + group]
        q_c = [q_s[hh, rows[n], :] for hh, n in part]
        s = [_dot_nt(q.astype(BF16), k_s[hh, rows[n], :].astype(BF16)) for q, (hh, n) in zip(q_c, part)]
        att = [(s_ * dmat[hh]).astype(BF16) for s_, (hh, n) in zip(s, part)]
        o = [_dot(a, v_ref[0, rows[n], lanes[hh]]) for a, (hh, n) in zip(att, part)]
        qd = [jnp.concatenate([(q * qdec_f[hh]).astype(BF16), (q * qdec_b[hh]).astype(BF16)], axis=1)
              for q, (hh, n) in zip(q_c, part)]
        st = [jnp.concatenate([rf_s[hh, n], rb_s[hh, n]], axis=0).astype(BF16) for hh, n in part]
        o = _each(lambda o_, qd_, st_: o_ + _dot(qd_, st_), o, qd, st)
        for o_, (hh, n) in zip(o, part):
            y = _ln_rows(o_) * ng_ref[:, lanes[hh]]
            g = gate_ref[0, rows[n], lanes[hh]].astype(F32)
            o_ref[0, rows[n], lanes[hh]] = (y * _silu(g)).astype(BF16)


def _ret_mixer(ret_lat, ret_ctx, log_gamma, norm_g, cos_t, sin_t):
    b, l_lat, _ = ret_lat.shape
    l_ctx = ret_ctx.shape[1]
    n_lat = l_lat // RET_CHUNK
    hb = RET_HEADS
    hw = hb * HEAD_DIM
    ngrp = RET_HEADS // hb

    def seq_spec(length, part):
        return pl.BlockSpec((1, length, hw), lambda i, j, *_: (i, 0, part * ngrp + j))

    grid_spec = pltpu.PrefetchScalarGridSpec(
        num_scalar_prefetch=1,
        grid=(b, ngrp),
        in_specs=[seq_spec(l_lat, 0), seq_spec(l_lat, 1), seq_spec(l_lat, 2), seq_spec(l_lat, 3),
                  seq_spec(l_ctx, 0), seq_spec(l_ctx, 1), seq_spec(l_ctx, 2),
                  pl.BlockSpec((l_lat, HEAD_DIM), lambda i, j, *_: (0, 0)),
                  pl.BlockSpec((l_lat, HEAD_DIM), lambda i, j, *_: (0, 0)),
                  pl.BlockSpec((1, hw), lambda i, j, *_: (0, j))],
        out_specs=pl.BlockSpec((1, l_lat, hw), lambda i, j, *_: (i, 0, j)),
        scratch_shapes=[pltpu.VMEM((hb, l_lat, HEAD_DIM), F32),
                        pltpu.VMEM((hb, l_lat, HEAD_DIM), F32),
                        pltpu.VMEM((hb, n_lat, HEAD_DIM, HEAD_DIM), F32),
                        pltpu.VMEM((hb, n_lat, HEAD_DIM, HEAD_DIM), F32)])
    kern = functools.partial(_ret_kernel, l_lat=l_lat, l_ctx=l_ctx, hb=hb)
    return pl.pallas_call(
        kern,
        grid_spec=grid_spec,
        out_shape=jax.ShapeDtypeStruct((b, l_lat, RET_HEADS * HEAD_DIM), BF16),
        compiler_params=_params(("arbitrary", "arbitrary")),
        name="ret_mixer",
    )(log_gamma.reshape(-1), ret_lat, ret_lat, ret_lat, ret_lat, ret_ctx, ret_ctx, ret_ctx,
      cos_t, sin_t, norm_g.reshape(1, -1))


def _rope_tables(l_lat):
    rows = l_lat // GRID_W
    row = jnp.repeat(jnp.arange(rows, dtype=F32), GRID_W)
    col = jnp.tile(jnp.arange(GRID_W, dtype=F32), rows)
    quarter = HEAD_DIM // 4
    inv = ROPE_THETA ** (-jnp.arange(quarter, dtype=F32) / quarter)
    ang = jnp.concatenate([row[:, None] * inv, col[:, None] * inv], -1)
    cos, sin = jnp.cos(ang), jnp.sin(ang)
    return jnp.concatenate([cos, cos], -1), jnp.concatenate([-sin, sin], -1)


def _top_rows(vals, k):
    n = vals.shape[0]
    idx = lax.broadcasted_iota(jnp.int32, vals.shape, 0)
    taken = jnp.zeros(vals.shape, jnp.int32)
    firsts = []
    for _ in range(k):
        live = jnp.where(taken == 0, vals, -jnp.inf)
        top = jnp.max(live, axis=0, keepdims=True)
        cand = jnp.where((live == top) & (taken == 0), idx, n)
        first = jnp.min(cand, axis=0, keepdims=True)
        taken = taken + (idx == first).astype(jnp.int32)
        firsts.append(first)
    return firsts, taken


def _outproj_kernel(a_ref, r_ref, x_ref, wa_ref, wr_ref, g1_ref, sh2_ref, sc2_ref, lg_ref, lb_ref,
                    wrt_ref, rb_ref, x1_ref, ha_ref, hb_ref, ek_ref, rk_ref, wk_ref, cnt_ref, carry_ref):
    y = _dot(a_ref[0], wa_ref[...]) + _dot(r_ref[0], wr_ref[...])
    x1 = _ln_rows(DEEPNORM_ALPHA * x_ref[0] + g1_ref[0] * y) * lg_ref[...] + lb_ref[...]
    x1_ref[0] = x1
    h = _ln_rows(x1) * (1.0 + sc2_ref[0]) + sh2_ref[0]
    for ref, part in zip((ha_ref, hb_ref), _pack_rows(h)):
        ref[0] = part
    s = _sigmoid(_dot_nt(wrt_ref[...], h, HIGHEST))
    sb = s + rb_ref[...]
    tm = s.shape[1]
    per = N_EXPERTS // N_GROUPS
    sub = lax.broadcasted_iota(jnp.int32, (per, tm), 0)
    gs_rows = []
    for g in range(N_GROUPS):
        blk = sb[g * per:(g + 1) * per, :]
        m1 = jnp.max(blk, axis=0, keepdims=True)
        first = jnp.min(jnp.where(blk == m1, sub, per), axis=0, keepdims=True)
        m2 = jnp.max(jnp.where(sub == first, -jnp.inf, blk), axis=0, keepdims=True)
        gs_rows.append(m1 + m2)
    gscore = jnp.concatenate(gs_rows, axis=0)
    _, gtaken = _top_rows(gscore, TOPK_GROUPS)
    emask = jnp.concatenate([jnp.broadcast_to(gtaken[g:g + 1, :], (per, tm)) for g in range(N_GROUPS)], axis=0)
    masked = jnp.where(emask > 0, sb, -jnp.inf)
    firsts, taken = _top_rows(masked, TOP_K)
    first_step = (pl.program_id(0) == 0) & (pl.program_id(1) == 0)

    @pl.when(first_step)
    def _():
        carry_ref[...] = jnp.zeros(carry_ref.shape, F32)

    sel_f = taken.astype(F32)
    ti = lax.broadcasted_iota(jnp.int32, (tm, tm), 0)
    tj = lax.broadcasted_iota(jnp.int32, (tm, tm), 1)
    rank = _dot(sel_f.astype(BF16), (ti < tj).astype(BF16)) + carry_ref[:, 0:1]
    carry = carry_ref[...] + jnp.sum(sel_f, axis=1, keepdims=True)
    carry_ref[...] = carry
    cnt_ref[...] = carry
    eidx = lax.broadcasted_iota(jnp.int32, (N_EXPERTS, tm), 0)
    picked = []
    for k in range(TOP_K):
        hit = eidx == firsts[k]
        pick = lambda v: jnp.sum(jnp.where(hit, v, 0.0), axis=0, keepdims=True)
        ek_ref[k:k + 1, :] = firsts[k]
        rk_ref[k:k + 1, :] = pick(rank).astype(jnp.int32)
        picked.append(pick(s))
    total = picked[0]
    for k in range(1, TOP_K):
        total = total + picked[k]
    for k in range(TOP_K):
        wk_ref[k:k + 1, :] = picked[k] / total * ROUTED_SCALE


def _outproj(a_lat, r_lat, x, w_a, w_r, g1, sh2, sc2, ln_g, ln_b, w_router_t, router_bias, tm, b0):
    b, l, half = a_lat.shape
    d = x.shape[-1]
    nl = l // tm
    row = lambda i, j: (i, j, 0)
    per_b = lambda i, j: (i + b0, 0, 0)
    const = lambda i, j: (0, 0)
    return pl.pallas_call(
        _outproj_kernel,
        grid=(b, nl),
        in_specs=[pl.BlockSpec((1, tm, half), row),
                  pl.BlockSpec((1, tm, half), row),
                  pl.BlockSpec((1, tm, d), lambda i, j: (i + b0, j, 0)),
                  pl.BlockSpec((half, d), const),
                  pl.BlockSpec((half, d), const),
                  pl.BlockSpec((1, 1, d), per_b),
                  pl.BlockSpec((1, 1, d), per_b),
                  pl.BlockSpec((1, 1, d), per_b),
                  pl.BlockSpec((1, d), const),
                  pl.BlockSpec((1, d), const),
                  pl.BlockSpec((N_EXPERTS, d), const),
                  pl.BlockSpec((N_EXPERTS, 1), const)],
        out_specs=[pl.BlockSpec((1, tm, d), row),
                   pl.BlockSpec((1, tm, d // (2 * PACK_GROUPS)), row),
                   pl.BlockSpec((1, tm, d // (2 * PACK_GROUPS)), row),
                   pl.BlockSpec((TOP_K, tm), lambda i, j: (0, i * nl + j)),
                   pl.BlockSpec((TOP_K, tm), lambda i, j: (0, i * nl + j)),
                   pl.BlockSpec((TOP_K, tm), lambda i, j: (0, i * nl + j)),
                   pl.BlockSpec((N_EXPERTS, HEAD_DIM), const)],
        out_shape=[jax.ShapeDtypeStruct((b, l, d), F32),
                   jax.ShapeDtypeStruct((b, l, d // (2 * PACK_GROUPS)), jnp.int32),
                   jax.ShapeDtypeStruct((b, l, d // (2 * PACK_GROUPS)), jnp.int32),
                   jax.ShapeDtypeStruct((TOP_K, b * l), jnp.int32),
                   jax.ShapeDtypeStruct((TOP_K, b * l), jnp.int32),
                   jax.ShapeDtypeStruct((TOP_K, b * l), F32),
                   jax.ShapeDtypeStruct((N_EXPERTS, HEAD_DIM), F32)],
        scratch_shapes=[pltpu.VMEM((N_EXPERTS, HEAD_DIM), F32)],
        compiler_params=_params(("arbitrary", "arbitrary")),
        name="outproj_router",
    )(a_lat, r_lat, x, w_a, w_r, g1, sh2, sc2, ln_g.reshape(1, d), ln_b.reshape(1, d),
      w_router_t, router_bias.reshape(N_EXPERTS, 1))


def _pack_rows(x):
    n = x.shape[1] // PACK_GROUPS
    return [_pack_bf16_pair(x[:, g * n:(g + 1) * n]) for g in range(PACK_GROUPS)]


def _unpack_rows(parts):
    cols = []
    for p in parts:
        cols += list(_unpack_bf16_pair(p))
    return cols


def _glu_ffn(cols, w_gate_up, w_down, ff):
    n = cols[0].shape[1]
    ab = _dot(cols[0], w_gate_up[0:n])
    for i in range(1, len(cols)):
        ab = ab + _dot(cols[i], w_gate_up[i * n:(i + 1) * n])
    act = (_silu(ab[:, :ff]) * ab[:, ff:]).astype(BF16)
    return _dot(act, w_down)


def _expert_kernel(te_ref, used_ref, *refs, ff):
    xs_refs, (wgu_ref, wd_ref) = refs[:PACK_GROUPS], refs[PACK_GROUPS:PACK_GROUPS + 2]
    ys_refs, (wgu_bf_ref, wd_bf_ref) = refs[PACK_GROUPS + 2:2 * PACK_GROUPS + 2], refs[2 * PACK_GROUPS + 2:]
    i = pl.program_id(0)

    @pl.when(i < used_ref[0])
    def _():
        @pl.when((i == 0) | (te_ref[i] != te_ref[jnp.maximum(i - 1, 0)]))
        def _():
            wgu_bf_ref[...] = wgu_ref[0].astype(BF16)
            wd_bf_ref[...] = wd_ref[0].astype(BF16)

        sub = EXPERT_TILE // EXPERT_SUBTILES
        blocks = [slice(s * sub, (s + 1) * sub) for s in range(EXPERT_SUBTILES)]
        wgu, wd = wgu_bf_ref[...], wd_bf_ref[...]
        cols = _each(lambda b: _unpack_rows([r[b, :] for r in xs_refs]), blocks)
        n = cols[0][0].shape[1]
        ab = _each(lambda c: _dot(c[0], wgu[0:n]), cols)
        for j in range(1, 2 * PACK_GROUPS):
            ab = _each(lambda acc, c: acc + _dot(c[j], wgu[j * n:(j + 1) * n]), ab, cols)
        act = _each(lambda a: (_silu(a[:, :ff]) * a[:, ff:]).astype(BF16), ab)
        y = _each(lambda a: _dot(a, wd), act)
        for b, y_b in zip(blocks, y):
            for ref, part in zip(ys_refs, _pack_rows(y_b)):
                ref[b, :] = part


def _expert_ffn(xs, tile_expert, used_tiles, w_gu, w_d):
    n_rows, dp = xs[0].shape
    n_e, d, ff2 = w_gu.shape
    ff = ff2 // 2
    tile = lambda i, te, used: (jnp.minimum(i, used[0] - 1), 0)
    grid_spec = pltpu.PrefetchScalarGridSpec(
        num_scalar_prefetch=2,
        grid=(n_rows // EXPERT_TILE,),
        in_specs=[pl.BlockSpec((EXPERT_TILE, dp), tile)] * PACK_GROUPS
                 + [pl.BlockSpec((1, d, ff2), lambda i, te, used: (te[i], 0, 0)),
                    pl.BlockSpec((1, ff, d), lambda i, te, used: (te[i], 0, 0))],
        out_specs=[pl.BlockSpec((EXPERT_TILE, dp), tile)] * PACK_GROUPS,
        scratch_shapes=[pltpu.VMEM((d, ff2), BF16), pltpu.VMEM((ff, d), BF16)])
    return pl.pallas_call(
        functools.partial(_expert_kernel, ff=ff),
        grid_spec=grid_spec,
        out_shape=[jax.ShapeDtypeStruct((n_rows, dp), jnp.int32)] * PACK_GROUPS,
        compiler_params=_params(("arbitrary",)),
        name="expert_ffn",
    )(tile_expert, used_tiles, *xs, w_gu, w_d)


def _combine_kernel(*refs, ff):
    h_refs, y_refs = refs[:PACK_GROUPS], refs[PACK_GROUPS:2 * PACK_GROUPS]
    x1_ref, wk_ref, wsg_ref, wsd_ref, g2_ref, lg_ref, lb_ref, o_ref = refs[2 * PACK_GROUPS:]
    shared = _glu_ffn(_unpack_rows([r[...] for r in h_refs]), wsg_ref[...], wsd_ref[...], ff)
    wk = wk_ref[...]
    n = y_refs[0].shape[2]
    blocks = [shared[:, i * n:(i + 1) * n] for i in range(2 * PACK_GROUPS)]
    for k in range(TOP_K):
        cols = _unpack_rows([r[k] for r in y_refs])
        blocks = [acc + wk[:, k:k + 1] * c.astype(F32) for acc, c in zip(blocks, cols)]
    f = jnp.concatenate(blocks, axis=1)
    o_ref[...] = _ln_rows(DEEPNORM_ALPHA * x1_ref[...] + g2_ref[0] * f) * lg_ref[...] + lb_ref[...]


def _moe_combine(h, x1, y_sel, wk_tok, w_sg, w_sd, g2, ln_g, ln_b, tm, b0, l, prev_out):
    t_part, d = x1.shape
    t = g2.shape[0] * l
    dp = h[0].shape[1]
    tiles_per_batch = l // tm
    off = b0 * tiles_per_batch
    ff = w_sd.shape[0]
    row = lambda i: (i, 0)
    const = lambda i: (0, 0)
    in_specs = ([pl.BlockSpec((tm, dp), row)] * PACK_GROUPS
                + [pl.BlockSpec((TOP_K, tm, dp), lambda i: (0, i, 0))] * PACK_GROUPS
                + [pl.BlockSpec((tm, d), row),
                   pl.BlockSpec((tm, TOP_K), row),
                   pl.BlockSpec(w_sg.shape, const),
                   pl.BlockSpec(w_sd.shape, const),
                   pl.BlockSpec((1, 1, d), lambda i: (i // tiles_per_batch + b0, 0, 0)),
                   pl.BlockSpec((1, d), const),
                   pl.BlockSpec((1, d), const)])
    args = [*h, *y_sel, x1, wk_tok, w_sg, w_sd, g2, ln_g.reshape(1, d), ln_b.reshape(1, d)]
    kern = functools.partial(_combine_kernel, ff=ff)
    aliases = {}
    if prev_out is not None:
        in_specs.append(pl.BlockSpec(memory_space=pl.ANY))
        aliases = {len(args): 0}
        args.append(prev_out)
        kern = lambda *refs: _combine_kernel(*refs[:-2], refs[-1], ff=ff)
    return pl.pallas_call(
        kern,
        grid=(t_part // tm,),
        in_specs=in_specs,
        out_specs=pl.BlockSpec((tm, d), lambda i: (i + off, 0)),
        out_shape=jax.ShapeDtypeStruct((t, d), F32),
        input_output_aliases=aliases,
        compiler_params=_params(("arbitrary",)),
        name="moe_combine",
    )(*args)


def _routing_tables(ek, rk, counts, n_rows):
    padded = (counts + EXPERT_TILE - 1) // EXPERT_TILE * EXPERT_TILE
    ends = jnp.cumsum(padded)
    starts = ends - padded
    experts = jnp.arange(N_EXPERTS, dtype=jnp.int32)
    start_of = jnp.sum(jnp.where(ek[None] == experts[:, None, None], starts[:, None, None], 0), axis=0)
    pos = start_of + rk
    tile_start = jnp.arange(n_rows // EXPERT_TILE, dtype=jnp.int32) * EXPERT_TILE
    tile_expert = jnp.minimum(jnp.sum(ends[None, :] <= tile_start[:, None], axis=1), N_EXPERTS - 1).astype(jnp.int32)
    used_tiles = (ends[-1:] // EXPERT_TILE).astype(jnp.int32)
    return pos.astype(jnp.int32), tile_expert, used_tiles


def _sc_mesh():
    return plsc.VectorSubcoreMesh(core_axis_name="core", subcore_axis_name="subcore")


def _dispatch_rows(rows, pos, n_rows):
    t, dp = rows.shape
    n_k = pos.shape[0]

    @functools.partial(pl.kernel, mesh=_sc_mesh(), out_type=jax.ShapeDtypeStruct((n_rows, dp), rows.dtype),
                       scratch_types=[])
    def scatter(rows_hbm, pos_hbm, out_hbm):
        def body(rows_vmem, idx_vmem):
            pltpu.sync_copy(rows_vmem, out_hbm.at[idx_vmem.at[0]])

        pltpu.emit_pipeline(
            body,
            grid=(t // SC_WINDOW, n_k),
            in_specs=[pl.BlockSpec((SC_WINDOW, dp), lambda i, k: (i, 0)),
                      pl.BlockSpec((1, SC_WINDOW), lambda i, k: (k, i))],
            out_specs=[],
            core_axis_name=("core", "subcore"),
            dimension_semantics=(pltpu.PARALLEL, pltpu.ARBITRARY),
        )(rows_hbm, pos_hbm)

    return scatter(rows, pos)


def _gather_rows(table, pos):
    n_k, t = pos.shape
    dp = table.shape[1]

    @functools.partial(pl.kernel, mesh=_sc_mesh(), out_type=jax.ShapeDtypeStruct((n_k * t, dp), table.dtype),
                       scratch_types=[])
    def gather(table_hbm, pos_hbm, out_hbm):
        def body(idx_vmem, out_vmem):
            pltpu.sync_copy(table_hbm.at[idx_vmem.at[0]], out_vmem)

        pltpu.emit_pipeline(
            body,
            grid=(n_k * t // SC_WINDOW,),
            in_specs=[pl.BlockSpec((1, SC_WINDOW), lambda i: (0, i))],
            out_specs=[pl.BlockSpec((SC_WINDOW, dp), lambda i: (i, 0))],
            core_axis_name=("core", "subcore"),
            dimension_semantics=(pltpu.PARALLEL,),
        )(pos_hbm, out_hbm)

    return gather(table, pos.reshape(1, n_k * t)).reshape(n_k, t, dp)


def _routed_moe(h, x1, ek, rk, wk, cnt, w_sg, w_sd, w_gu, w_d, g2, ln_g, ln_b, b0, l, prev_out):
    t = x1.shape[0]
    n_rows = t * TOP_K + N_EXPERTS * EXPERT_TILE
    pos, tile_expert, used_tiles = _routing_tables(ek, rk, cnt[:, 0].astype(jnp.int32), n_rows)
    xs = [_dispatch_rows(rows, pos, n_rows) for rows in h]
    ys = _expert_ffn(xs, tile_expert, used_tiles, w_gu, w_d)
    y_sel = [_gather_rows(table, pos) for table in ys]
    return _moe_combine(h, x1, y_sel, wk.T, w_sg, w_sd, g2, ln_g, ln_b, ROW_TILE, b0, l, prev_out)


def _gate_layouts(g_lat, g_ctx):
    g = jnp.concatenate([g_ctx, g_lat], axis=1)[..., :4 * GDN_HEADS]
    b, lt, _ = g.shape
    nt = lt // GDN_CHUNK
    rows = jnp.transpose(g.reshape(b, nt, GDN_CHUNK, 4, GDN_HEADS), (0, 4, 1, 3, 2))
    return jnp.pad(rows, ((0, 0), (0, 0), (0, 0), (0, 4), (0, 0))).reshape(b, GDN_HEADS, nt * 8, GDN_CHUNK)


def kernel(x, c, ctx, c_ctx, w_mod, b_mod, w_in, conv_w, gdn_a_log, gdn_dt_bias, gdn_norm_g, ret_log_gamma,
           ret_norm_g, w_out, ln1_g, ln1_b, w_router, router_bias, w_gate_up, w_down, w_shared_gate_up,
           w_shared_down, ln2_g, ln2_b):
    b, l, d = x.shape
    lc = ctx.shape[1]
    gw = GDN_HEADS * HEAD_DIM
    assert d == 2 * gw and w_in.shape[-1] == 8 * gw + 4 * GDN_HEADS and w_gate_up.shape[1] == N_EXPERTS
    assert b % BATCH_PARTS == 0 and l % ROW_TILE == 0 and l % RET_CHUNK == 0 and l % GRID_W == 0
    assert lc % RET_CHUNK == 0 and lc % HALO == 0 and (b // BATCH_PARTS * l) % EXPERT_TILE == 0

    n_mod = -(-(b + 1) // 8) * 8
    cc = jnp.zeros((n_mod, d), F32).at[:b].set(c).at[b].set(c_ctx)
    mod = _modulation(cc, w_mod[0], b_mod[0])
    sh1, sc1, g1, sh2, sc2, g2 = [mod[:b, i * d:(i + 1) * d].reshape(b, 1, d) for i in range(6)]
    csh1 = jnp.broadcast_to(mod[b, 0:d].reshape(1, 1, d), (b, 1, d))
    csc1 = jnp.broadcast_to(mod[b, d:2 * d].reshape(1, 1, d), (b, 1, d))

    w = w_in[0]
    o_gate = 3 * gw + gw
    o_ret = o_gate + 4 * GDN_HEADS
    w_main = jnp.concatenate([w[:, :o_gate], w[:, o_ret:]], axis=1).astype(BF16)
    w_gate = w[:, o_gate:o_ret]
    cos_t, sin_t = _rope_tables(l)
    wo = w_out[0].astype(BF16)
    w_sg, w_sd = w_shared_gate_up[0].astype(BF16), w_shared_down[0].astype(BF16)
    w_router_t = w_router[0].T

    nb = b // BATCH_PARTS
    out = None
    for part in range(BATCH_PARTS):
        b0 = part * nb
        qkv_lat, z_lat, ret_lat, g_lat = _inproj(x, sh1, sc1, w_main, w_gate, conv_w[0], ROW_TILE, b0, nb)
        qkv_ctx, _, ret_ctx, g_ctx = _inproj(ctx, csh1, csc1, w_main, w_gate, conv_w[0], lc, b0, nb)
        gate_rows = _gate_layouts(g_lat, g_ctx)
        a_lat = _gdn_mixer(qkv_lat, qkv_ctx, z_lat, gate_rows, gdn_a_log[0],
                           gdn_dt_bias[0], gdn_norm_g[0], GDN_HEADS_PER_STEP)
        r_lat = _ret_mixer(ret_lat, ret_ctx, ret_log_gamma[0], ret_norm_g[0], cos_t, sin_t)
        x1, h_a, h_b, ek, rk, wk, cnt = _outproj(a_lat, r_lat, x, wo[:gw], wo[gw:], g1, sh2, sc2, ln1_g[0], ln1_b[0],
                                                 w_router_t, router_bias[0], ROW_TILE, b0)
        h = [arr.reshape(nb * l, arr.shape[-1]) for arr in (h_a, h_b)]
        out = _routed_moe(h, x1.reshape(nb * l, d), ek, rk, wk, cnt, w_sg, w_sd, w_gate_up[0], w_down[0],
                          g2, ln2_g[0], ln2_b[0], b0, l, out)
    return out.reshape(b, l, d)
```

```python
import functools

import jax
import jax.numpy as jnp
from jax import lax
from jax.experimental import pallas as pl
from jax.experimental.pallas import tpu as pltpu
from jax.experimental.pallas import tpu_sc as plsc

F32 = jnp.float32
BF16 = jnp.bfloat16

HEAD_DIM = 128
GDN_HEADS = 4
RET_HEADS = 4
CONV_W = 5
HALO = 16
GDN_CHUNK = 64
INV_BASE = 16
RET_CHUNK = 256
GRID_W = 64
ROPE_THETA = 10000.0
N_EXPERTS = 64
TOP_K = 8
N_GROUPS = 8
TOPK_GROUPS = 4
ROUTED_SCALE = 2.5
SC_WINDOW = 128
PACK_GROUPS = 2
ROW_TILE = 512
GDN_HEADS_PER_STEP = 2
GDN_SLOTS_PER_GROUP = 6
BATCH_PARTS = 2
EXPERT_TILES_PER_STEP = 2
EXPERT_SUBTILES = 4
EXPERT_TILE = 512
LN_EPS = 1e-6
DEPTH = 1
DEEPNORM_ALPHA = (2 * DEPTH) ** 0.25

VMEM_LIMIT = 56 * 1024 * 1024
HIGHEST = lax.Precision.HIGHEST
NT_DIMS = (((1,), (1,)), ((), ()))


def _dot(a, b, precision=None):
    return jnp.dot(a, b, preferred_element_type=F32, precision=precision)


def _dot_nt(a, b, precision=None):
    return lax.dot_general(a, b, NT_DIMS, preferred_element_type=F32, precision=precision)


def _silu(x):
    return x * (1.0 / (1.0 + jnp.exp(-x)))


def _sigmoid(x):
    return 1.0 / (1.0 + jnp.exp(-x))


def _softplus(x):
    return jnp.maximum(x, 0.0) + jnp.log(1.0 + jnp.exp(-jnp.abs(x)))


def _ln_rows(x):
    mu = jnp.mean(x, axis=-1, keepdims=True)
    xc = x - mu
    var = jnp.mean(xc * xc, axis=-1, keepdims=True)
    return xc * lax.rsqrt(var + LN_EPS)


def _params(sem):
    return pltpu.CompilerParams(dimension_semantics=sem, vmem_limit_bytes=VMEM_LIMIT)


def _mod_kernel(c_ref, w_ref, b_ref, o_ref):
    o_ref[...] = _dot(_silu(c_ref[...]), w_ref[...], HIGHEST) + b_ref[...]


def _modulation(cc, w_mod, b_mod):
    rows, d = cc.shape
    n = w_mod.shape[1]
    tn = 1024
    return pl.pallas_call(
        _mod_kernel,
        grid=(n // tn,),
        in_specs=[pl.BlockSpec((rows, d), lambda j: (0, 0)),
                  pl.BlockSpec((d, tn), lambda j: (0, j)),
                  pl.BlockSpec((1, tn), lambda j: (0, j))],
        out_specs=pl.BlockSpec((rows, tn), lambda j: (0, j)),
        out_shape=jax.ShapeDtypeStruct((rows, n), F32),
        compiler_params=_params(("arbitrary",)),
        name="modulation",
    )(cc, w_mod, b_mod.reshape(1, n))


def _inproj_kernel(x_ref, xp_ref, xn_ref, sh_ref, sc_ref, wm_ref, wg_ref, cw_ref, qkv_ref, z_ref, ret_ref, g_ref,
                   p_ref):
    j, nl = pl.program_id(1), pl.num_programs(1)
    tm = x_ref.shape[1]
    modulate = lambda x: _ln_rows(x) * (1.0 + sc_ref[0]) + sh_ref[0]
    h = modulate(x_ref[0])
    hb = h.astype(BF16)
    h_prev = jnp.where(j > 0, modulate(xp_ref[0]), 0.0).astype(BF16)
    h_next = jnp.where(j < nl - 1, modulate(xn_ref[0]), 0.0).astype(BF16)
    h_ext = jnp.concatenate([h_prev, hb, h_next], axis=0)
    half = CONV_W // 2
    gw = GDN_HEADS * HEAD_DIM
    for part in range(3):
        p_ref[part] = _dot(h_ext, wm_ref[:, part * gw:(part + 1) * gw])

    def conv_part(part):
        c0 = part * gw
        p = p_ref[part]
        acc = p[HALO - half:HALO - half + tm, :] * cw_ref[0:1, c0:c0 + gw]
        for tap in range(1, CONV_W):
            acc = acc + p[HALO - half + tap:HALO - half + tap + tm, :] * cw_ref[tap:tap + 1, c0:c0 + gw]
        y = _silu(acc)
        if part < 2:
            scale = HEAD_DIM ** -0.5 if part == 0 else 1.0
            blocks = [y[:, hd * HEAD_DIM:(hd + 1) * HEAD_DIM] for hd in range(GDN_HEADS)]
            blocks = [blk * (lax.rsqrt(jnp.sum(blk * blk, axis=-1, keepdims=True) + LN_EPS) * scale) for blk in blocks]
            y = jnp.concatenate(blocks, axis=1)
        qkv_ref[0, :, c0:c0 + gw] = y.astype(BF16)

    rest = [(ref, n0) for ref in (z_ref, ret_ref) for n0 in range(0, ref.shape[-1], 512)]
    col = 3 * gw
    for idx, (ref, n0) in enumerate(rest):
        if idx < 3:
            conv_part(idx)
        ref[0, :, n0:n0 + 512] = _dot(hb, wm_ref[:, col:col + 512]).astype(BF16)
        col += 512
    h_lo = (h - hb.astype(F32)).astype(BF16)
    wg = wg_ref[...]
    wg_hi = wg.astype(BF16)
    wg_lo = (wg - wg_hi.astype(F32)).astype(BF16)
    g_ref[0] = (_dot(h_lo, wg_hi) + _dot(hb, wg_lo)) + _dot(hb, wg_hi)


def _inproj(x, sh, sc, w_main, w_gate, conv_w, tm, b0, b):
    _, l, d = x.shape
    widths = (3 * GDN_HEADS * HEAD_DIM, GDN_HEADS * HEAD_DIM, 4 * RET_HEADS * HEAD_DIM)
    n_gate = w_gate.shape[1]
    per_tile, last = tm // HALO, l // HALO - 1
    row = lambda i, j: (i, j, 0)
    const = lambda i, j: (0, 0)
    return pl.pallas_call(
        _inproj_kernel,
        grid=(b, l // tm),
        in_specs=[pl.BlockSpec((1, tm, d), lambda i, j: (i + b0, j, 0)),
                  pl.BlockSpec((1, HALO, d), lambda i, j: (i + b0, jnp.maximum(j * per_tile - 1, 0), 0)),
                  pl.BlockSpec((1, HALO, d), lambda i, j: (i + b0, jnp.minimum((j + 1) * per_tile, last), 0)),
                  pl.BlockSpec((1, 1, d), lambda i, j: (i + b0, 0, 0)),
                  pl.BlockSpec((1, 1, d), lambda i, j: (i + b0, 0, 0)),
                  pl.BlockSpec(w_main.shape, const),
                  pl.BlockSpec(w_gate.shape, const),
                  pl.BlockSpec(conv_w.shape, const)],
        out_specs=[pl.BlockSpec((1, tm, w), row) for w in widths]
                  + [pl.BlockSpec((1, tm, n_gate), row)],
        out_shape=[jax.ShapeDtypeStruct((b, l, w), BF16) for w in widths]
                  + [jax.ShapeDtypeStruct((b, l, n_gate), F32)],
        scratch_shapes=[pltpu.VMEM((3, tm + 2 * HALO, GDN_HEADS * HEAD_DIM), F32)],
        compiler_params=_params(("arbitrary", "arbitrary")),
        name="inproj",
    )(x, x, x, sh, sc, w_main, w_gate, conv_w)


def _bdot(a, b):
    return _dot(a.astype(BF16), b.astype(BF16))


def _pack_bf16_pair(x):
    n = x.shape[1] // 2
    bits = lambda v: pltpu.bitcast(v.astype(BF16).astype(F32), jnp.uint32)
    word = lax.shift_right_logical(bits(x[:, :n]), jnp.uint32(16)) | (bits(x[:, n:]) & jnp.uint32(0xFFFF0000))
    return pltpu.bitcast(word, jnp.int32)


def _unpack_bf16_pair(w):
    u = pltpu.bitcast(w, jnp.uint32)
    lo = pltpu.bitcast(lax.shift_left(u, jnp.uint32(16)), F32)
    hi = pltpu.bitcast(u & jnp.uint32(0xFFFF0000), F32)
    return lo.astype(BF16), hi.astype(BF16)


def _split3(x):
    hi = x.astype(BF16)
    r1 = x - hi.astype(F32)
    mid = r1.astype(BF16)
    lo = (r1 - mid.astype(F32)).astype(BF16)
    return hi, mid, lo


def _dot_split_lhs(x, mask01):
    m = mask01.astype(BF16)
    hi, mid, lo = _split3(x)
    return (_dot(lo, m) + _dot(mid, m)) + _dot(hi, m)


def _each(fn, *lists):
    return [fn(*args) for args in zip(*lists)]


def _inv_unit_triangular(mats, base=INV_BASE):
    n = mats[0].shape[0]
    i = lax.broadcasted_iota(jnp.int32, (n, n), 0)
    j = lax.broadcasted_iota(jnp.int32, (n, n), 1)
    eye = (i == j).astype(F32)
    shift = base.bit_length() - 1
    inside = (i >> shift) == (j >> shift)
    xs = _each(lambda a: jnp.where(inside, a, 0.0), mats)
    ts = _each(lambda d: eye - d, xs)
    for _ in range(shift - 1):
        xs = _each(lambda x: _bdot(x, x), xs)
        yield
        ts = _each(lambda t, x: t + _bdot(t, x), ts, xs)
        yield
    size = base
    while size < GDN_CHUNK:
        shift += 1
        size *= 2
        wider = (i >> shift) == (j >> shift)
        off = wider & jnp.logical_not(inside)
        ots = _each(lambda a, t: _bdot(jnp.where(off, a, 0.0), t), mats, ts)
        yield
        ts = _each(lambda t, ot: t - _bdot(t, ot), ts, ots)
        yield
        inside = wider
    return ts


def _gdn_kernel(alog_ref, dtb_ref,
                q_ref, k_ref, v_ref, qc_ref, kc_ref, vc_ref,
                gt_ref, z_ref, ng_ref,
                o_ref,
                qs_ref, ks_ref, vs_ref, gts_ref, gth_ref,
                wqf_ref, wqb_ref, u0_ref, intra_ref, ket_ref, gef_ref, geb_ref,
                of_ref, ob_ref, *, hb, l_lat, l_ctx, slots_per_group):
    c = GDN_CHUNK
    c2 = 2 * c
    lt = l_lat + l_ctx
    n_ctx = l_ctx // c
    nt = lt // c
    hgrp = pl.program_id(1)

    i2 = lax.broadcasted_iota(jnp.int32, (c2, c2), 0)
    j2 = lax.broadcasted_iota(jnp.int32, (c2, c2), 1)
    same_blk = (i2 & c) == (j2 & c)
    sgn_i = jnp.where(i2 < c, 1, -1)
    sgn_j = jnp.where(j2 < c, 1, -1)
    incl = same_blk & ((j2 - i2) * sgn_i <= 0)
    strict = same_blk & ((j2 - i2) * sgn_i < 0)
    tri_row = (same_blk & ((i2 - j2) * sgn_j <= 0)).astype(F32)
    eye_m = i2 == j2
    eye_b = eye_m.astype(BF16)
    lane_lt_c = j2 < c

    for hh in range(hb):
        head = hgrp * hb + hh
        lane0 = hh * HEAD_DIM
        for (src_c, src_l, dst) in ((qc_ref, q_ref, qs_ref), (kc_ref, k_ref, ks_ref), (vc_ref, v_ref, vs_ref)):
            dst[hh, 0:l_ctx, :] = src_c[0, :, lane0:lane0 + HEAD_DIM]
            dst[hh, l_ctx:lt, :] = src_l[0, :, lane0:lane0 + HEAD_DIM]

        a_f, a_b = alog_ref[head], alog_ref[GDN_HEADS + head]
        d_f, d_b = dtb_ref[head], dtb_ref[GDN_HEADS + head]
        compr = lax.broadcasted_iota(jnp.int32, (nt * 8, 1), 0) & 7
        neg_ar = -jnp.exp(jnp.where(compr == 2, a_f, a_b))
        dtbr = jnp.where(compr == 2, d_f, d_b)
        grow_raw = gt_ref[0, hh]
        grow = jnp.where(compr < 2, _sigmoid(grow_raw), neg_ar * _softplus(grow_raw + dtbr))
        gts_ref[hh] = jnp.zeros(gts_ref.shape[1:], F32)
        gts_ref[hh, :, 0:c] = grow
        gth_ref[hh] = jnp.zeros(gth_ref.shape[1:], F32)
        gth_ref[hh, :, c:c2] = grow

    spg = slots_per_group
    heads = list(range(hb))

    def bwd_chunk(i):
        return jnp.where(i < n_ctx, n_ctx - 1 - i, nt + n_ctx - 1 - i)

    def slot_load(i, hh):
        cbk = bwd_chunk(i)
        rf = pl.multiple_of(i * c, c)
        rb = pl.multiple_of(cbk * c, c)
        two = lambda ref: jnp.concatenate([ref[hh, pl.ds(rf, c), :], ref[hh, pl.ds(rb, c), :]], axis=0).astype(F32)
        g_r = (gts_ref[hh, pl.ds(pl.multiple_of(i * 8, 8), 8), :]
               + gth_ref[hh, pl.ds(pl.multiple_of(cbk * 8, 8), 8), :])
        return two(ks_ref), two(vs_ref), two(qs_ref), g_r

    def slot_stages(loaded):
        k2, v2, q2, g_r = [list(col) for col in zip(*loaded)]
        rcs = _each(lambda g: _dot_split_lhs(g, tri_row), g_r)
        yield
        k2b = _each(lambda x: x.astype(BF16), k2)
        kk = _each(_dot_nt, k2b, k2b)
        qk = _each(lambda q, kb: _dot_nt(q.astype(BF16), kb), q2, k2b)
        yield
        by_dir = lambda g, k: jnp.where(lane_lt_c[0:1, :], g[k:k + 1, :], g[k + 1:k + 2, :])
        lane_sum = lambda mask, row: jnp.sum(jnp.where(mask, row, 0.0), axis=1, keepdims=True)
        gcc = _each(lambda g: lane_sum(incl, by_dir(g, 2)), g_r)
        gcr = _each(lambda r: by_dir(r, 2), rcs)
        beta = _each(lambda g: lane_sum(eye_m, by_dir(g, 0)), g_r)
        gend = _each(lambda s: jnp.concatenate([jnp.broadcast_to(s[c - 1:c, :], (c, 1)),
                                                jnp.broadcast_to(s[c:c + 1, :], (c, 1))], axis=0), gcc)
        decay = _each(lambda gc, gr: jnp.exp(jnp.where(incl, gc - gr, -jnp.inf)), gcc, gcr)
        a_mat = _each(lambda m, d, b: jnp.where(strict, m * d, 0.0) * b, kk, decay, beta)
        t_f32 = yield from _inv_unit_triangular(a_mat)
        t_mat = _each(lambda t: t.astype(BF16), t_f32)
        egc = _each(jnp.exp, gcc)
        u0 = _each(lambda t, v, b: _dot(t, (v * b).astype(BF16)), t_mat, v2, beta)
        w = _each(lambda t, k, b, e: _dot(t, (k * (b * e)).astype(BF16)), t_mat, k2, beta, egc)
        ket = _each(lambda k, ge_, gc: _dot_nt(eye_b, (k * jnp.exp(ge_ - gc)).astype(BF16)), k2, gend, gcc)
        yield
        qd = _each(lambda q, e: (q * e).astype(BF16), q2, egc)
        intra = _each(lambda m, d: (m * d).astype(BF16), qk, decay)
        ge = _each(jnp.exp, gend)
        return [(jnp.concatenate([w_[0:c].astype(BF16), qd_[0:c]], axis=0),
                 jnp.concatenate([w_[c:c2].astype(BF16), qd_[c:c2]], axis=0), u0_, in_,
                 jnp.concatenate([jnp.where(lane_lt_c, kt, 0.0), jnp.where(lane_lt_c, 0.0, kt)], axis=0).astype(BF16),
                 jnp.broadcast_to(g_[0:1, :], (8, HEAD_DIM)), jnp.broadcast_to(g_[c:c + 1, :], (8, HEAD_DIM)))
                for w_, qd_, u0_, in_, kt, g_ in zip(w, qd, u0, intra, ket, ge)]

    def transform_group(g):
        par = lax.rem(g, 2)
        jobs = [(s, hh) for s in range(spg) for hh in heads]
        results = yield from slot_stages([slot_load(g * spg + s, hh) for s, hh in jobs])
        for (s, hh), (wq_f, wq_b, u0, intra, ket, ge_f, ge_b) in zip(jobs, results):
            wqf_ref[par, hh, s] = wq_f
            wqb_ref[par, hh, s] = wq_b
            u0_ref[par, hh, s] = u0
            intra_ref[par, hh, s] = intra
            ket_ref[par, hh, s] = ket
            gef_ref[par, hh, s] = ge_f
            geb_ref[par, hh, s] = ge_b

    def recur_group(g, states):
        par = lax.rem(g, 2)
        sf, sb = list(states[0::2]), list(states[1::2])
        for s in range(spg):
            i = g * spg + s
            rf = pl.multiple_of(i * c, c)
            rb = pl.multiple_of(bwd_chunk(i) * c, c)
            r_f = _each(lambda hh, st: _dot(wqf_ref[par, hh, s], st.astype(BF16)), heads, sf)
            r_b = _each(lambda hh, st: _dot(wqb_ref[par, hh, s], st.astype(BF16)), heads, sb)
            u0 = _each(lambda hh: u0_ref[par, hh, s], heads)
            yield
            u2 = _each(lambda u, f, b_: jnp.concatenate([u[0:c] - f[0:c], u[c:c2] - b_[0:c]], axis=0).astype(BF16),
                       u0, r_f, r_b)
            ket = _each(lambda hh: ket_ref[par, hh, s], heads)
            df = _each(lambda kt, u: _dot(kt[0:HEAD_DIM], u), ket, u2)
            db = _each(lambda kt, u: _dot(kt[HEAD_DIM:2 * HEAD_DIM], u), ket, u2)
            iu = _each(lambda hh, u: _dot(intra_ref[par, hh, s], u), heads, u2)
            sf = _each(lambda hh, st, d: gef_ref[par, hh, s][0:1, :] * st + d, heads, sf, df)
            sb = _each(lambda hh, st, d: geb_ref[par, hh, s][0:1, :] * st + d, heads, sb, db)
            for hh in heads:
                of_ref[hh, pl.ds(rf, c), :] = r_f[hh][c:c2] + iu[hh][0:c]
                ob_ref[hh, pl.ds(rb, c), :] = r_b[hh][c:c2] + iu[hh][c:c2]
            yield
        return tuple(x for pair in zip(sf, sb) for x in pair)

    def drive(transform, recur, transforms_per_recur=1):
        states = None
        while transform is not None or recur is not None:
            if transform is not None:
                for _ in range(transforms_per_recur):
                    try:
                        next(transform)
                    except StopIteration:
                        transform = None
                        break
            if recur is not None:
                try:
                    next(recur)
                except StopIteration as stop:
                    states, recur = stop.value, None
        return states

    n_groups = nt // spg
    zero_state = tuple(jnp.zeros((HEAD_DIM, HEAD_DIM), F32) for _ in range(2 * hb))
    drive(transform_group(0), None)
    states = lax.fori_loop(0, n_groups - 1,
                           lambda g, st: drive(transform_group(g + 1), recur_group(g, st)), zero_state)
    drive(None, recur_group(n_groups - 1, states))

    ng = ng_ref[...]
    blk = 256
    for hh in range(hb):
        lane0 = hh * HEAD_DIM

        def fin_body(i, carry):
            r = pl.multiple_of(i * blk, blk)
            o = of_ref[hh, pl.ds(l_ctx + r, blk), :] + ob_ref[hh, pl.ds(l_ctx + r, blk), :]
            y = o * lax.rsqrt(jnp.mean(o * o, axis=-1, keepdims=True) + LN_EPS) * ng
            z = z_ref[0, pl.ds(r, blk), lane0:lane0 + HEAD_DIM].astype(F32)
            o_ref[0, pl.ds(r, blk), lane0:lane0 + HEAD_DIM] = (y * _silu(z)).astype(BF16)
            return carry

        lax.fori_loop(0, l_lat // blk, fin_body, 0)


def _gdn_mixer(qkv_lat, qkv_ctx, z_lat, gate_rows, a_log, dt_bias, norm_g, hb):
    b, l_lat, _ = qkv_lat.shape
    l_ctx = qkv_ctx.shape[1]
    lt = l_lat + l_ctx
    nt = lt // GDN_CHUNK
    hw = hb * HEAD_DIM
    ngrp = GDN_HEADS // hb
    c2 = 2 * GDN_CHUNK
    spg = GDN_SLOTS_PER_GROUP
    assert nt % spg == 0 and nt // spg >= 2 and GDN_HEADS % hb == 0

    def seq_spec(length, part):
        return pl.BlockSpec((1, length, hw), lambda i, j, *_: (i, 0, part * ngrp + j))

    grid_spec = pltpu.PrefetchScalarGridSpec(
        num_scalar_prefetch=2,
        grid=(b, ngrp),
        in_specs=[seq_spec(l_lat, 0), seq_spec(l_lat, 1), seq_spec(l_lat, 2),
                  seq_spec(l_ctx, 0), seq_spec(l_ctx, 1), seq_spec(l_ctx, 2),
                  pl.BlockSpec((1, hb, nt * 8, GDN_CHUNK), lambda i, j, *_: (i, j, 0, 0)),
                  pl.BlockSpec((1, l_lat, hw), lambda i, j, *_: (i, 0, j)),
                  pl.BlockSpec((1, HEAD_DIM), lambda i, j, *_: (0, 0))],
        out_specs=pl.BlockSpec((1, l_lat, hw), lambda i, j, *_: (i, 0, j)),
        scratch_shapes=[
            pltpu.VMEM((hb, lt, HEAD_DIM), BF16),
            pltpu.VMEM((hb, lt, HEAD_DIM), BF16),
            pltpu.VMEM((hb, lt, HEAD_DIM), BF16),
            pltpu.VMEM((hb, nt * 8, c2), F32),
            pltpu.VMEM((hb, nt * 8, c2), F32),
            pltpu.VMEM((2, hb, spg, c2, HEAD_DIM), BF16),
            pltpu.VMEM((2, hb, spg, c2, HEAD_DIM), BF16),
            pltpu.VMEM((2, hb, spg, c2, HEAD_DIM), F32),
            pltpu.VMEM((2, hb, spg, c2, c2), BF16),
            pltpu.VMEM((2, hb, spg, 2 * HEAD_DIM, c2), BF16),
            pltpu.VMEM((2, hb, spg, 8, HEAD_DIM), F32),
            pltpu.VMEM((2, hb, spg, 8, HEAD_DIM), F32),
            pltpu.VMEM((hb, lt, HEAD_DIM), F32),
            pltpu.VMEM((hb, lt, HEAD_DIM), F32),
        ])
    kern = functools.partial(_gdn_kernel, hb=hb, l_lat=l_lat, l_ctx=l_ctx, slots_per_group=spg)
    return pl.pallas_call(
        kern,
        grid_spec=grid_spec,
        out_shape=jax.ShapeDtypeStruct((b, l_lat, GDN_HEADS * HEAD_DIM), BF16),
        compiler_params=_params(("arbitrary", "arbitrary")),
        name="gdn_mixer",
    )(a_log, dt_bias, qkv_lat, qkv_lat, qkv_lat, qkv_ctx, qkv_ctx, qkv_ctx,
      gate_rows, z_lat, norm_g.reshape(1, HEAD_DIM))


def _ret_kernel(lg_ref, q_ref, k_ref, v_ref, gate_ref, qc_ref, kc_ref, vc_ref, cos_ref, sin_ref, ng_ref,
                o_ref, q_s, k_s, rf_s, rb_s, *, l_lat, l_ctx, hb):
    c = RET_CHUNK
    n_lat = l_lat // c
    n_ctx = l_ctx // c
    heads = list(range(hb))
    lanes = [slice(hh * HEAD_DIM, (hh + 1) * HEAD_DIM) for hh in heads]
    pos_c = lax.broadcasted_iota(jnp.int32, (c, 1), 0).astype(F32)
    ii = lax.broadcasted_iota(jnp.int32, (c, c), 0)
    jj = lax.broadcasted_iota(jnp.int32, (c, c), 1)
    dif = (ii - jj).astype(F32)
    e2 = lax.broadcasted_iota(jnp.int32, (HEAD_DIM, HEAD_DIM), 0)
    f2 = lax.broadcasted_iota(jnp.int32, (HEAD_DIM, HEAD_DIM), 1)
    eye_b = (e2 == f2).astype(BF16)
    kscale = HEAD_DIM ** -0.5
    lg_f = [lg_ref[pl.program_id(1) * hb + hh] for hh in heads]
    lg_b = [lg_ref[RET_HEADS + pl.program_id(1) * hb + hh] for hh in heads]
    dmat = _each(lambda f, b_: (jnp.exp(jnp.where(ii >= jj, f * dif, -jnp.inf))
                                + jnp.exp(jnp.where(jj >= ii, -b_ * dif, -jnp.inf))), lg_f, lg_b)
    kdec_f = _each(lambda f: jnp.exp(f * (c - 1 - pos_c)), lg_f)
    kdec_b = _each(lambda b_: jnp.exp(b_ * pos_c), lg_b)
    qdec_f = _each(lambda f: jnp.exp(f * (pos_c + 1.0)), lg_f)
    qdec_b = _each(lambda b_: jnp.exp(b_ * (c - pos_c)), lg_b)
    cd_f = _each(lambda f: jnp.exp(jnp.full((1, HEAD_DIM), f * c, F32)), lg_f)
    cd_b = _each(lambda b_: jnp.exp(jnp.full((1, HEAD_DIM), b_ * c, F32)), lg_b)

    def transposed(kd):
        return _dot_nt(eye_b, kd.astype(BF16)).astype(BF16)

    r_f = [jnp.zeros((HEAD_DIM, HEAD_DIM), F32) for _ in heads]
    r_b = [jnp.zeros((HEAD_DIM, HEAD_DIM), F32) for _ in heads]
    for n in range(n_ctx):
        m = n_ctx - 1 - n
        kf = _each(lambda ln, d: transposed(kc_ref[0, n * c:(n + 1) * c, ln].astype(F32) * kscale * d), lanes, kdec_f)
        kb = _each(lambda ln, d: transposed(kc_ref[0, m * c:(m + 1) * c, ln].astype(F32) * kscale * d), lanes, kdec_b)
        pf = _each(lambda kt, ln: _dot(kt, vc_ref[0, n * c:(n + 1) * c, ln]), kf, lanes)
        pb = _each(lambda kt, ln: _dot(kt, vc_ref[0, m * c:(m + 1) * c, ln]), kb, lanes)
        r_f = _each(lambda d, r, p: d * r + p, cd_f, r_f, pf)
        r_b = _each(lambda d, r, p: d * r + p, cd_b, r_b, pb)

    def rope_body(n, carry):
        r = pl.multiple_of(n * c, c)
        cs = cos_ref[pl.ds(r, c), :]
        sn = sin_ref[pl.ds(r, c), :]
        for hh, ln in zip(heads, lanes):
            q = q_ref[0, pl.ds(r, c), ln].astype(F32)
            k = k_ref[0, pl.ds(r, c), ln].astype(F32) * kscale
            q_s[hh, pl.ds(r, c), :] = q * cs + pltpu.roll(q, HEAD_DIM // 2, 1) * sn
            k_s[hh, pl.ds(r, c), :] = k * cs + pltpu.roll(k, HEAD_DIM // 2, 1) * sn
        return carry

    lax.fori_loop(0, n_lat, rope_body, 0)

    chunks = list(range(n_lat))
    rows = [slice(n * c, (n + 1) * c) for n in chunks]
    jobs = [(hh, n) for hh in heads for n in chunks]
    k_c = [k_s[hh, rows[n], :] for hh, n in jobs]
    kft = [transposed(k * kdec_f[hh]) for k, (hh, n) in zip(k_c, jobs)]
    kbt = [transposed(k * kdec_b[hh]) for k, (hh, n) in zip(k_c, jobs)]
    kvf = [_dot(kt, v_ref[0, rows[n], lanes[hh]]) for kt, (hh, n) in zip(kft, jobs)]
    kvb = [_dot(kt, v_ref[0, rows[n], lanes[hh]]) for kt, (hh, n) in zip(kbt, jobs)]
    for hh in heads:
        r = r_f[hh]
        for n in chunks:
            rf_s[hh, n] = r
            r = cd_f[hh] * r + kvf[hh * n_lat + n]
        r = r_b[hh]
        for n in reversed(chunks):
            rb_s[hh, n] = r
            r = cd_b[hh] * r + kvb[hh * n_lat + n]

    group = 4
    for g0 in range(0, len(jobs), group):
        part = jobs[g0:g0 + group]
        q_c = [q_s[hh, rows[n], :] for hh, n in part]
        s = [_dot_nt(q.astype(BF16), k_s[hh, rows[n], :].astype(BF16)) for q, (hh, n) in zip(q_c, part)]
        att = [(s_ * dmat[hh]).astype(BF16) for s_, (hh, n) in zip(s, part)]
        o = [_dot(a, v_ref[0, rows[n], lanes[hh]]) for a, (hh, n) in zip(att, part)]
        qd = [jnp.concatenate([(q * qdec_f[hh]).astype(BF16), (q * qdec_b[hh]).astype(BF16)], axis=1)
              for q, (hh, n) in zip(q_c, part)]
        st = [jnp.concatenate([rf_s[hh, n], rb_s[hh, n]], axis=0).astype(BF16) for hh, n in part]
        o = _each(lambda o_, qd_, st_: o_ + _dot(qd_, st_), o, qd, st)
        for o_, (hh, n) in zip(o, part):
            y = _ln_rows(o_) * ng_ref[:, lanes[hh]]
            g = gate_ref[0, rows[n], lanes[hh]].astype(F32)
            o_ref[0, rows[n], lanes[hh]] = (y * _silu(g)).astype(BF16)


def _ret_mixer(ret_lat, ret_ctx, log_gamma, norm_g, cos_t, sin_t):
    b, l_lat, _ = ret_lat.shape
    l_ctx = ret_ctx.shape[1]
    n_lat = l_lat // RET_CHUNK
    hb = RET_HEADS
    hw = hb * HEAD_DIM
    ngrp = RET_HEADS // hb

    def seq_spec(length, part):
        return pl.BlockSpec((1, length, hw), lambda i, j, *_: (i, 0, part * ngrp + j))

    grid_spec = pltpu.PrefetchScalarGridSpec(
        num_scalar_prefetch=1,
        grid=(b, ngrp),
        in_specs=[seq_spec(l_lat, 0), seq_spec(l_lat, 1), seq_spec(l_lat, 2), seq_spec(l_lat, 3),
                  seq_spec(l_ctx, 0), seq_spec(l_ctx, 1), seq_spec(l_ctx, 2),
                  pl.BlockSpec((l_lat, HEAD_DIM), lambda i, j, *_: (0, 0)),
                  pl.BlockSpec((l_lat, HEAD_DIM), lambda i, j, *_: (0, 0)),
                  pl.BlockSpec((1, hw), lambda i, j, *_: (0, j))],
        out_specs=pl.BlockSpec((1, l_lat, hw), lambda i, j, *_: (i, 0, j)),
        scratch_shapes=[pltpu.VMEM((hb, l_lat, HEAD_DIM), F32),
                        pltpu.VMEM((hb, l_lat, HEAD_DIM), F32),
                        pltpu.VMEM((hb, n_lat, HEAD_DIM, HEAD_DIM), F32),
                        pltpu.VMEM((hb, n_lat, HEAD_DIM, HEAD_DIM), F32)])
    kern = functools.partial(_ret_kernel, l_lat=l_lat, l_ctx=l_ctx, hb=hb)
    return pl.pallas_call(
        kern,
        grid_spec=grid_spec,
        out_shape=jax.ShapeDtypeStruct((b, l_lat, RET_HEADS * HEAD_DIM), BF16),
        compiler_params=_params(("arbitrary", "arbitrary")),
        name="ret_mixer",
    )(log_gamma.reshape(-1), ret_lat, ret_lat, ret_lat, ret_lat, ret_ctx, ret_ctx, ret_ctx,
      cos_t, sin_t, norm_g.reshape(1, -1))


def _rope_tables(l_lat):
    rows = l_lat // GRID_W
    row = jnp.repeat(jnp.arange(rows, dtype=F32), GRID_W)
    col = jnp.tile(jnp.arange(GRID_W, dtype=F32), rows)
    quarter = HEAD_DIM // 4
    inv = ROPE_THETA ** (-jnp.arange(quarter, dtype=F32) / quarter)
    ang = jnp.concatenate([row[:, None] * inv, col[:, None] * inv], -1)
    cos, sin = jnp.cos(ang), jnp.sin(ang)
    return jnp.concatenate([cos, cos], -1), jnp.concatenate([-sin, sin], -1)


def _top_rows(vals, k):
    n = vals.shape[0]
    idx = lax.broadcasted_iota(jnp.int32, vals.shape, 0)
    taken = jnp.zeros(vals.shape, jnp.int32)
    firsts = []
    for _ in range(k):
        live = jnp.where(taken == 0, vals, -jnp.inf)
        top = jnp.max(live, axis=0, keepdims=True)
        cand = jnp.where((live == top) & (taken == 0), idx, n)
        first = jnp.min(cand, axis=0, keepdims=True)
        taken = taken + (idx == first).astype(jnp.int32)
        firsts.append(first)
    return firsts, taken


def _outproj_kernel(a_ref, r_ref, x_ref, wa_ref, wr_ref, g1_ref, sh2_ref, sc2_ref, lg_ref, lb_ref,
                    wrt_ref, rb_ref, x1_ref, ha_ref, hb_ref, ek_ref, rk_ref, wk_ref, cnt_ref, carry_ref):
    y = _dot(a_ref[0], wa_ref[...]) + _dot(r_ref[0], wr_ref[...])
    x1 = _ln_rows(DEEPNORM_ALPHA * x_ref[0] + g1_ref[0] * y) * lg_ref[...] + lb_ref[...]
    x1_ref[0] = x1
    h = _ln_rows(x1) * (1.0 + sc2_ref[0]) + sh2_ref[0]
    for ref, part in zip((ha_ref, hb_ref), _pack_rows(h)):
        ref[0] = part
    s = _sigmoid(_dot_nt(wrt_ref[...], h, HIGHEST))
    sb = s + rb_ref[...]
    tm = s.shape[1]
    per = N_EXPERTS // N_GROUPS
    sub = lax.broadcasted_iota(jnp.int32, (per, tm), 0)
    gs_rows = []
    for g in range(N_GROUPS):
        blk = sb[g * per:(g + 1) * per, :]
        m1 = jnp.max(blk, axis=0, keepdims=True)
        first = jnp.min(jnp.where(blk == m1, sub, per), axis=0, keepdims=True)
        m2 = jnp.max(jnp.where(sub == first, -jnp.inf, blk), axis=0, keepdims=True)
        gs_rows.append(m1 + m2)
    gscore = jnp.concatenate(gs_rows, axis=0)
    _, gtaken = _top_rows(gscore, TOPK_GROUPS)
    emask = jnp.concatenate([jnp.broadcast_to(gtaken[g:g + 1, :], (per, tm)) for g in range(N_GROUPS)], axis=0)
    masked = jnp.where(emask > 0, sb, -jnp.inf)
    firsts, taken = _top_rows(masked, TOP_K)
    first_step = (pl.program_id(0) == 0) & (pl.program_id(1) == 0)

    @pl.when(first_step)
    def _():
        carry_ref[...] = jnp.zeros(carry_ref.shape, F32)

    sel_f = taken.astype(F32)
    ti = lax.broadcasted_iota(jnp.int32, (tm, tm), 0)
    tj = lax.broadcasted_iota(jnp.int32, (tm, tm), 1)
    rank = _dot(sel_f.astype(BF16), (ti < tj).astype(BF16)) + carry_ref[:, 0:1]
    carry = carry_ref[...] + jnp.sum(sel_f, axis=1, keepdims=True)
    carry_ref[...] = carry
    cnt_ref[...] = carry
    eidx = lax.broadcasted_iota(jnp.int32, (N_EXPERTS, tm), 0)
    picked = []
    for k in range(TOP_K):
        hit = eidx == firsts[k]
        pick = lambda v: jnp.sum(jnp.where(hit, v, 0.0), axis=0, keepdims=True)
        ek_ref[k:k + 1, :] = firsts[k]
        rk_ref[k:k + 1, :] = pick(rank).astype(jnp.int32)
        picked.append(pick(s))
    total = picked[0]
    for k in range(1, TOP_K):
        total = total + picked[k]
    for k in range(TOP_K):
        wk_ref[k:k + 1, :] = picked[k] / total * ROUTED_SCALE


def _outproj(a_lat, r_lat, x, w_a, w_r, g1, sh2, sc2, ln_g, ln_b, w_router_t, router_bias, tm, b0):
    b, l, half = a_lat.shape
    d = x.shape[-1]
    nl = l // tm
    row = lambda i, j: (i, j, 0)
    per_b = lambda i, j: (i + b0, 0, 0)
    const = lambda i, j: (0, 0)
    return pl.pallas_call(
        _outproj_kernel,
        grid=(b, nl),
        in_specs=[pl.BlockSpec((1, tm, half), row),
                  pl.BlockSpec((1, tm, half), row),
                  pl.BlockSpec((1, tm, d), lambda i, j: (i + b0, j, 0)),
                  pl.BlockSpec((half, d), const),
                  pl.BlockSpec((half, d), const),
                  pl.BlockSpec((1, 1, d), per_b),
                  pl.BlockSpec((1, 1, d), per_b),
                  pl.BlockSpec((1, 1, d), per_b),
                  pl.BlockSpec((1, d), const),
                  pl.BlockSpec((1, d), const),
                  pl.BlockSpec((N_EXPERTS, d), const),
                  pl.BlockSpec((N_EXPERTS, 1), const)],
        out_specs=[pl.BlockSpec((1, tm, d), row),
                   pl.BlockSpec((1, tm, d // (2 * PACK_GROUPS)), row),
                   pl.BlockSpec((1, tm, d // (2 * PACK_GROUPS)), row),
                   pl.BlockSpec((TOP_K, tm), lambda i, j: (0, i * nl + j)),
                   pl.BlockSpec((TOP_K, tm), lambda i, j: (0, i * nl + j)),
                   pl.BlockSpec((TOP_K, tm), lambda i, j: (0, i * nl + j)),
                   pl.BlockSpec((N_EXPERTS, HEAD_DIM), const)],
        out_shape=[jax.ShapeDtypeStruct((b, l, d), F32),
                   jax.ShapeDtypeStruct((b, l, d // (2 * PACK_GROUPS)), jnp.int32),
                   jax.ShapeDtypeStruct((b, l, d // (2 * PACK_GROUPS)), jnp.int32),
                   jax.ShapeDtypeStruct((TOP_K, b * l), jnp.int32),
                   jax.ShapeDtypeStruct((TOP_K, b * l), jnp.int32),
                   jax.ShapeDtypeStruct((TOP_K, b * l), F32),
                   jax.ShapeDtypeStruct((N_EXPERTS, HEAD_DIM), F32)],
        scratch_shapes=[pltpu.VMEM((N_EXPERTS, HEAD_DIM), F32)],
        compiler_params=_params(("arbitrary", "arbitrary")),
        name="outproj_router",
    )(a_lat, r_lat, x, w_a, w_r, g1, sh2, sc2, ln_g.reshape(1, d), ln_b.reshape(1, d),
      w_router_t, router_bias.reshape(N_EXPERTS, 1))


def _pack_rows(x):
    n = x.shape[1] // PACK_GROUPS
    return [_pack_bf16_pair(x[:, g * n:(g + 1) * n]) for g in range(PACK_GROUPS)]


def _unpack_rows(parts):
    cols = []
    for p in parts:
        cols += list(_unpack_bf16_pair(p))
    return cols


def _glu_ffn(cols, w_gate_up, w_down, ff):
    n = cols[0].shape[1]
    ab = _dot(cols[0], w_gate_up[0:n])
    for i in range(1, len(cols)):
        ab = ab + _dot(cols[i], w_gate_up[i * n:(i + 1) * n])
    act = (_silu(ab[:, :ff]) * ab[:, ff:]).astype(BF16)
    return _dot(act, w_down)


def _expert_kernel(te_ref, used_ref, *refs, ff):
    tps = EXPERT_TILES_PER_STEP
    xs_refs, w_refs = refs[:PACK_GROUPS], refs[PACK_GROUPS:PACK_GROUPS + 2 * tps]
    ys_refs = refs[PACK_GROUPS + 2 * tps:2 * PACK_GROUPS + 2 * tps]
    wbf_refs = refs[2 * PACK_GROUPS + 2 * tps:]
    i = pl.program_id(0)
    used = used_ref[0]

    def run(n_active):
        for t in range(n_active):
            tile = i * tps + t
            prev = jnp.maximum(tile - tps, 0)

            @pl.when((i == 0) | (te_ref[tile] != te_ref[prev]))
            def _(t=t):
                wbf_refs[2 * t][...] = w_refs[2 * t][0].astype(BF16)
                wbf_refs[2 * t + 1][...] = w_refs[2 * t + 1][0].astype(BF16)

        sub = EXPERT_TILE // EXPERT_SUBTILES
        blocks = [(t, slice(t * EXPERT_TILE + s * sub, t * EXPERT_TILE + (s + 1) * sub))
                  for t in range(n_active) for s in range(EXPERT_SUBTILES)]
        wgu = [wbf_refs[2 * t][...] for t in range(n_active)]
        wd = [wbf_refs[2 * t + 1][...] for t in range(n_active)]
        cols = [_unpack_rows([r[b, :] for r in xs_refs]) for t, b in blocks]
        n = cols[0][0].shape[1]
        ab = [_dot(c[0], wgu[t][0:n]) for c, (t, b) in zip(cols, blocks)]
        for j in range(1, 2 * PACK_GROUPS):
            ab = [acc + _dot(c[j], wgu[t][j * n:(j + 1) * n]) for acc, c, (t, b) in zip(ab, cols, blocks)]
        act = [(_silu(a[:, :ff]) * a[:, ff:]).astype(BF16) for a in ab]
        y = [_dot(a, wd[t]) for a, (t, b) in zip(act, blocks)]
        for (t, b), y_b in zip(blocks, y):
            for ref, part in zip(ys_refs, _pack_rows(y_b)):
                ref[b, :] = part

    for n_active in range(1, tps + 1):
        full = (i * tps + n_active <= used) if n_active == tps else (i * tps + n_active == used)
        pl.when(full)(functools.partial(run, n_active))


def _expert_ffn(xs, tile_expert, used_tiles, w_gu, w_d):
    n_rows, dp = xs[0].shape
    n_e, d, ff2 = w_gu.shape
    ff = ff2 // 2
    tps = EXPERT_TILES_PER_STEP
    step_rows = tps * EXPERT_TILE
    assert n_rows % step_rows == 0
    rows_spec = pl.BlockSpec((step_rows, dp), lambda i, te, used: (jnp.minimum(i, (used[0] - 1) // tps), 0))
    w_specs = []
    for t in range(tps):
        w_specs += [pl.BlockSpec((1, d, ff2), lambda i, te, used, t=t: (te[i * tps + t], 0, 0)),
                    pl.BlockSpec((1, ff, d), lambda i, te, used, t=t: (te[i * tps + t], 0, 0))]
    grid_spec = pltpu.PrefetchScalarGridSpec(
        num_scalar_prefetch=2,
        grid=(n_rows // step_rows,),
        in_specs=[rows_spec] * PACK_GROUPS + w_specs,
        out_specs=[rows_spec] * PACK_GROUPS,
        scratch_shapes=[pltpu.VMEM((d, ff2), BF16), pltpu.VMEM((ff, d), BF16)] * tps)
    return pl.pallas_call(
        functools.partial(_expert_kernel, ff=ff),
        grid_spec=grid_spec,
        out_shape=[jax.ShapeDtypeStruct((n_rows, dp), jnp.int32)] * PACK_GROUPS,
        compiler_params=_params(("arbitrary",)),
        name="expert_ffn",
    )(tile_expert, used_tiles, *xs, *([w_gu, w_d] * tps))


def _combine_kernel(*refs, ff):
    h_refs, y_refs = refs[:PACK_GROUPS], refs[PACK_GROUPS:2 * PACK_GROUPS]
    x1_ref, wk_ref, wsg_ref, wsd_ref, g2_ref, lg_ref, lb_ref, o_ref = refs[2 * PACK_GROUPS:]
    shared = _glu_ffn(_unpack_rows([r[...] for r in h_refs]), wsg_ref[...], wsd_ref[...], ff)
    wk = wk_ref[...]
    n = y_refs[0].shape[2]
    blocks = [shared[:, i * n:(i + 1) * n] for i in range(2 * PACK_GROUPS)]
    for k in range(TOP_K):
        cols = _unpack_rows([r[k] for r in y_refs])
        blocks = [acc + wk[:, k:k + 1] * c.astype(F32) for acc, c in zip(blocks, cols)]
    f = jnp.concatenate(blocks, axis=1)
    o_ref[...] = _ln_rows(DEEPNORM_ALPHA * x1_ref[...] + g2_ref[0] * f) * lg_ref[...] + lb_ref[...]


def _moe_combine(h, x1, y_sel, wk_tok, w_sg, w_sd, g2, ln_g, ln_b, tm, b0, l, prev_out):
    t_part, d = x1.shape
    t = g2.shape[0] * l
    dp = h[0].shape[1]
    tiles_per_batch = l // tm
    off = b0 * tiles_per_batch
    ff = w_sd.shape[0]
    row = lambda i: (i, 0)
    const = lambda i: (0, 0)
    in_specs = ([pl.BlockSpec((tm, dp), row)] * PACK_GROUPS
                + [pl.BlockSpec((TOP_K, tm, dp), lambda i: (0, i, 0))] * PACK_GROUPS
                + [pl.BlockSpec((tm, d), row),
                   pl.BlockSpec((tm, TOP_K), row),
                   pl.BlockSpec(w_sg.shape, const),
                   pl.BlockSpec(w_sd.shape, const),
                   pl.BlockSpec((1, 1, d), lambda i: (i // tiles_per_batch + b0, 0, 0)),
                   pl.BlockSpec((1, d), const),
                   pl.BlockSpec((1, d), const)])
    args = [*h, *y_sel, x1, wk_tok, w_sg, w_sd, g2, ln_g.reshape(1, d), ln_b.reshape(1, d)]
    kern = functools.partial(_combine_kernel, ff=ff)
    aliases = {}
    if prev_out is not None:
        in_specs.append(pl.BlockSpec(memory_space=pl.ANY))
        aliases = {len(args): 0}
        args.append(prev_out)
        kern = lambda *refs: _combine_kernel(*refs[:-2], refs[-1], ff=ff)
    return pl.pallas_call(
        kern,
        grid=(t_part // tm,),
        in_specs=in_specs,
        out_specs=pl.BlockSpec((tm, d), lambda i: (i + off, 0)),
        out_shape=jax.ShapeDtypeStruct((t, d), F32),
        input_output_aliases=aliases,
        compiler_params=_params(("arbitrary",)),
        name="moe_combine",
    )(*args)


def _routing_tables(ek, rk, counts, n_rows):
    padded = (counts + EXPERT_TILE - 1) // EXPERT_TILE * EXPERT_TILE
    ends = jnp.cumsum(padded)
    starts = ends - padded
    experts = jnp.arange(N_EXPERTS, dtype=jnp.int32)
    start_of = jnp.sum(jnp.where(ek[None] == experts[:, None, None], starts[:, None, None], 0), axis=0)
    pos = start_of + rk
    tile_start = jnp.arange(n_rows // EXPERT_TILE, dtype=jnp.int32) * EXPERT_TILE
    tile_expert = jnp.minimum(jnp.sum(ends[None, :] <= tile_start[:, None], axis=1), N_EXPERTS - 1).astype(jnp.int32)
    used_tiles = (ends[-1:] // EXPERT_TILE).astype(jnp.int32)
    return pos.astype(jnp.int32), tile_expert, used_tiles


def _sc_mesh():
    return plsc.VectorSubcoreMesh(core_axis_name="core", subcore_axis_name="subcore")


def _dispatch_rows(rows, pos, n_rows):
    t, dp = rows.shape
    n_k = pos.shape[0]

    @functools.partial(pl.kernel, mesh=_sc_mesh(), out_type=jax.ShapeDtypeStruct((n_rows, dp), rows.dtype),
                       scratch_types=[])
    def scatter(rows_hbm, pos_hbm, out_hbm):
        def body(rows_vmem, idx_vmem):
            pltpu.sync_copy(rows_vmem, out_hbm.at[idx_vmem.at[0]])

        pltpu.emit_pipeline(
            body,
            grid=(t // SC_WINDOW, n_k),
            in_specs=[pl.BlockSpec((SC_WINDOW, dp), lambda i, k: (i, 0)),
                      pl.BlockSpec((1, SC_WINDOW), lambda i, k: (k, i))],
            out_specs=[],
            core_axis_name=("core", "subcore"),
            dimension_semantics=(pltpu.PARALLEL, pltpu.ARBITRARY),
        )(rows_hbm, pos_hbm)

    return scatter(rows, pos)


def _gather_rows(table, pos):
    n_k, t = pos.shape
    dp = table.shape[1]

    @functools.partial(pl.kernel, mesh=_sc_mesh(), out_type=jax.ShapeDtypeStruct((n_k * t, dp), table.dtype),
                       scratch_types=[])
    def gather(table_hbm, pos_hbm, out_hbm):
        def body(idx_vmem, out_vmem):
            pltpu.sync_copy(table_hbm.at[idx_vmem.at[0]], out_vmem)

        pltpu.emit_pipeline(
            body,
            grid=(n_k * t // SC_WINDOW,),
            in_specs=[pl.BlockSpec((1, SC_WINDOW), lambda i: (0, i))],
            out_specs=[pl.BlockSpec((SC_WINDOW, dp), lambda i: (i, 0))],
            core_axis_name=("core", "subcore"),
            dimension_semantics=(pltpu.PARALLEL,),
        )(pos_hbm, out_hbm)

    return gather(table, pos.reshape(1, n_k * t)).reshape(n_k, t, dp)


def _routed_moe(h, x1, ek, rk, wk, cnt, w_sg, w_sd, w_gu, w_d, g2, ln_g, ln_b, b0, l, prev_out):
    t = x1.shape[0]
    n_rows = t * TOP_K + N_EXPERTS * EXPERT_TILE
    pos, tile_expert, used_tiles = _routing_tables(ek, rk, cnt[:, 0].astype(jnp.int32), n_rows)
    xs = [_dispatch_rows(rows, pos, n_rows) for rows in h]
    ys = _expert_ffn(xs, tile_expert, used_tiles, w_gu, w_d)
    y_sel = [_gather_rows(table, pos) for table in ys]
    return _moe_combine(h, x1, y_sel, wk.T, w_sg, w_sd, g2, ln_g, ln_b, ROW_TILE, b0, l, prev_out)


def _gate_layouts(g_lat, g_ctx):
    g = jnp.concatenate([g_ctx, g_lat], axis=1)[..., :4 * GDN_HEADS]
    b, lt, _ = g.shape
    nt = lt // GDN_CHUNK
    rows = jnp.transpose(g.reshape(b, nt, GDN_CHUNK, 4, GDN_HEADS), (0, 4, 1, 3, 2))
    return jnp.pad(rows, ((0, 0), (0, 0), (0, 0), (0, 4), (0, 0))).reshape(b, GDN_HEADS, nt * 8, GDN_CHUNK)


def kernel(x, c, ctx, c_ctx, w_mod, b_mod, w_in, conv_w, gdn_a_log, gdn_dt_bias, gdn_norm_g, ret_log_gamma,
           ret_norm_g, w_out, ln1_g, ln1_b, w_router, router_bias, w_gate_up, w_down, w_shared_gate_up,
           w_shared_down, ln2_g, ln2_b):
    b, l, d = x.shape
    lc = ctx.shape[1]
    gw = GDN_HEADS * HEAD_DIM
    assert d == 2 * gw and w_in.shape[-1] == 8 * gw + 4 * GDN_HEADS and w_gate_up.shape[1] == N_EXPERTS
    assert b % BATCH_PARTS == 0 and l % ROW_TILE == 0 and l % RET_CHUNK == 0 and l % GRID_W == 0
    assert lc % RET_CHUNK == 0 and lc % HALO == 0 and (b // BATCH_PARTS * l) % EXPERT_TILE == 0

    n_mod = -(-(b + 1) // 8) * 8
    cc = jnp.zeros((n_mod, d), F32).at[:b].set(c).at[b].set(c_ctx)
    mod = _modulation(cc, w_mod[0], b_mod[0])
    sh1, sc1, g1, sh2, sc2, g2 = [mod[:b, i * d:(i + 1) * d].reshape(b, 1, d) for i in range(6)]
    csh1 = jnp.broadcast_to(mod[b, 0:d].reshape(1, 1, d), (b, 1, d))
    csc1 = jnp.broadcast_to(mod[b, d:2 * d].reshape(1, 1, d), (b, 1, d))

    w = w_in[0]
    o_gate = 3 * gw + gw
    o_ret = o_gate + 4 * GDN_HEADS
    w_main = jnp.concatenate([w[:, :o_gate], w[:, o_ret:]], axis=1).astype(BF16)
    w_gate = w[:, o_gate:o_ret]
    cos_t, sin_t = _rope_tables(l)
    wo = w_out[0].astype(BF16)
    w_sg, w_sd = w_shared_gate_up[0].astype(BF16), w_shared_down[0].astype(BF16)
    w_router_t = w_router[0].T

    nb = b // BATCH_PARTS
    out = None
    for part in range(BATCH_PARTS):
        b0 = part * nb
        qkv_lat, z_lat, ret_lat, g_lat = _inproj(x, sh1, sc1, w_main, w_gate, conv_w[0], ROW_TILE, b0, nb)
        qkv_ctx, _, ret_ctx, g_ctx = _inproj(ctx, csh1, csc1, w_main, w_gate, conv_w[0], lc, b0, nb)
        gate_rows = _gate_layouts(g_lat, g_ctx)
        a_lat = _gdn_mixer(qkv_lat, qkv_ctx, z_lat, gate_rows, gdn_a_log[0],
                           gdn_dt_bias[0], gdn_norm_g[0], GDN_HEADS_PER_STEP)
        r_lat = _ret_mixer(ret_lat, ret_ctx, ret_log_gamma[0], ret_norm_g[0], cos_t, sin_t)
        x1, h_a, h_b, ek, rk, wk, cnt = _outproj(a_lat, r_lat, x, wo[:gw], wo[gw:], g1, sh2, sc2, ln1_g[0], ln1_b[0],
                                                 w_router_t, router_bias[0], ROW_TILE, b0)
        h = [arr.reshape(nb * l, arr.shape[-1]) for arr in (h_a, h_b)]
        out = _routed_moe(h, x1.reshape(nb * l, d), ek, rk, wk, cnt, w_sg, w_sd, w_gate_up[0], w_down[0],
                          g2, ln2_g[0], ln2_b[0], b0, l, out)
    return out.reshape(b, l, d)
```

```python
import functools

import jax
import jax.numpy as jnp
from jax import lax
from jax.experimental import pallas as pl
from jax.experimental.pallas import tpu as pltpu
from jax.experimental.pallas import tpu_sc as plsc

F32 = jnp.float32
BF16 = jnp.bfloat16

HEAD_DIM = 128
GDN_HEADS = 4
RET_HEADS = 4
CONV_W = 5
HALO = 16
GDN_CHUNK = 64
INV_BASE = 16
RET_CHUNK = 256
GRID_W = 64
ROPE_THETA = 10000.0
N_EXPERTS = 64
TOP_K = 8
N_GROUPS = 8
TOPK_GROUPS = 4
ROUTED_SCALE = 2.5
SC_WINDOW = 128
PACK_GROUPS = 2
ROW_TILE = 512
GDN_HEADS_PER_STEP = 2
GDN_SLOTS_PER_GROUP = 6
BATCH_PARTS = 2
EXPERT_TILES_PER_STEP = 2
EXPERT_SUBTILES = 4
EXPERT_TILE = 512
LN_EPS = 1e-6
DEPTH = 1
DEEPNORM_ALPHA = (2 * DEPTH) ** 0.25

VMEM_LIMIT = 56 * 1024 * 1024
HIGHEST = lax.Precision.HIGHEST
NT_DIMS = (((1,), (1,)), ((), ()))


def _dot(a, b, precision=None):
    return jnp.dot(a, b, preferred_element_type=F32, precision=precision)


def _dot_nt(a, b, precision=None):
    return lax.dot_general(a, b, NT_DIMS, preferred_element_type=F32, precision=precision)


def _silu(x):
    return x * (1.0 / (1.0 + jnp.exp(-x)))


def _sigmoid(x):
    return 1.0 / (1.0 + jnp.exp(-x))


def _softplus(x):
    return jnp.maximum(x, 0.0) + jnp.log(1.0 + jnp.exp(-jnp.abs(x)))


def _ln_rows(x):
    mu = jnp.mean(x, axis=-1, keepdims=True)
    xc = x - mu
    var = jnp.mean(xc * xc, axis=-1, keepdims=True)
    return xc * lax.rsqrt(var + LN_EPS)


def _params(sem):
    return pltpu.CompilerParams(dimension_semantics=sem, vmem_limit_bytes=VMEM_LIMIT)


def _mod_kernel(c_ref, w_ref, b_ref, o_ref):
    o_ref[...] = _dot(_silu(c_ref[...]), w_ref[...], HIGHEST) + b_ref[...]


def _modulation(cc, w_mod, b_mod):
    rows, d = cc.shape
    n = w_mod.shape[1]
    tn = 1024
    return pl.pallas_call(
        _mod_kernel,
        grid=(n // tn,),
        in_specs=[pl.BlockSpec((rows, d), lambda j: (0, 0)),
                  pl.BlockSpec((d, tn), lambda j: (0, j)),
                  pl.BlockSpec((1, tn), lambda j: (0, j))],
        out_specs=pl.BlockSpec((rows, tn), lambda j: (0, j)),
        out_shape=jax.ShapeDtypeStruct((rows, n), F32),
        compiler_params=_params(("arbitrary",)),
        name="modulation",
    )(cc, w_mod, b_mod.reshape(1, n))


def _inproj_kernel(x_ref, xp_ref, xn_ref, sh_ref, sc_ref, wm_ref, wg_ref, cw_ref, qkv_ref, z_ref, ret_ref, g_ref,
                   p_ref):
    j, nl = pl.program_id(1), pl.num_programs(1)
    tm = x_ref.shape[1]
    modulate = lambda x: _ln_rows(x) * (1.0 + sc_ref[0]) + sh_ref[0]
    h = modulate(x_ref[0])
    hb = h.astype(BF16)
    h_prev = jnp.where(j > 0, modulate(xp_ref[0]), 0.0).astype(BF16)
    h_next = jnp.where(j < nl - 1, modulate(xn_ref[0]), 0.0).astype(BF16)
    h_ext = jnp.concatenate([h_prev, hb, h_next], axis=0)
    half = CONV_W // 2
    gw = GDN_HEADS * HEAD_DIM
    for part in range(3):
        p_ref[part] = _dot(h_ext, wm_ref[:, part * gw:(part + 1) * gw])

    def conv_part(part):
        c0 = part * gw
        p = p_ref[part]
        acc = p[HALO - half:HALO - half + tm, :] * cw_ref[0:1, c0:c0 + gw]
        for tap in range(1, CONV_W):
            acc = acc + p[HALO - half + tap:HALO - half + tap + tm, :] * cw_ref[tap:tap + 1, c0:c0 + gw]
        y = _silu(acc)
        if part < 2:
            scale = HEAD_DIM ** -0.5 if part == 0 else 1.0
            blocks = [y[:, hd * HEAD_DIM:(hd + 1) * HEAD_DIM] for hd in range(GDN_HEADS)]
            blocks = [blk * (lax.rsqrt(jnp.sum(blk * blk, axis=-1, keepdims=True) + LN_EPS) * scale) for blk in blocks]
            y = jnp.concatenate(blocks, axis=1)
        qkv_ref[0, :, c0:c0 + gw] = y.astype(BF16)

    rest = [(ref, n0) for ref in (z_ref, ret_ref) for n0 in range(0, ref.shape[-1], 512)]
    col = 3 * gw
    for idx, (ref, n0) in enumerate(rest):
        if idx < 3:
            conv_part(idx)
        ref[0, :, n0:n0 + 512] = _dot(hb, wm_ref[:, col:col + 512]).astype(BF16)
        col += 512
    h_lo = (h - hb.astype(F32)).astype(BF16)
    wg = wg_ref[...]
    wg_hi = wg.astype(BF16)
    wg_lo = (wg - wg_hi.astype(F32)).astype(BF16)
    g_ref[0] = (_dot(h_lo, wg_hi) + _dot(hb, wg_lo)) + _dot(hb, wg_hi)


def _inproj(x, sh, sc, w_main, w_gate, conv_w, tm, b0, b):
    _, l, d = x.shape
    widths = (3 * GDN_HEADS * HEAD_DIM, GDN_HEADS * HEAD_DIM, 4 * RET_HEADS * HEAD_DIM)
    n_gate = w_gate.shape[1]
    per_tile, last = tm // HALO, l // HALO - 1
    row = lambda i, j: (i, j, 0)
    const = lambda i, j: (0, 0)
    return pl.pallas_call(
        _inproj_kernel,
        grid=(b, l // tm),
        in_specs=[pl.BlockSpec((1, tm, d), lambda i, j: (i + b0, j, 0)),
                  pl.BlockSpec((1, HALO, d), lambda i, j: (i + b0, jnp.maximum(j * per_tile - 1, 0), 0)),
                  pl.BlockSpec((1, HALO, d), lambda i, j: (i + b0, jnp.minimum((j + 1) * per_tile, last), 0)),
                  pl.BlockSpec((1, 1, d), lambda i, j: (i + b0, 0, 0)),
                  pl.BlockSpec((1, 1, d), lambda i, j: (i + b0, 0, 0)),
                  pl.BlockSpec(w_main.shape, const),
                  pl.BlockSpec(w_gate.shape, const),
                  pl.BlockSpec(conv_w.shape, const)],
        out_specs=[pl.BlockSpec((1, tm, w), row) for w in widths]
                  + [pl.BlockSpec((1, tm, n_gate), row)],
        out_shape=[jax.ShapeDtypeStruct((b, l, w), BF16) for w in widths]
                  + [jax.ShapeDtypeStruct((b, l, n_gate), F32)],
        scratch_shapes=[pltpu.VMEM((3, tm + 2 * HALO, GDN_HEADS * HEAD_DIM), F32)],
        compiler_params=_params(("arbitrary", "arbitrary")),
        name="inproj",
    )(x, x, x, sh, sc, w_main, w_gate, conv_w)


def _bdot(a, b):
    return _dot(a.astype(BF16), b.astype(BF16))


def _pack_bf16_pair(x):
    n = x.shape[1] // 2
    bits = lambda v: pltpu.bitcast(v.astype(BF16).astype(F32), jnp.uint32)
    word = lax.shift_right_logical(bits(x[:, :n]), jnp.uint32(16)) | (bits(x[:, n:]) & jnp.uint32(0xFFFF0000))
    return pltpu.bitcast(word, jnp.int32)


def _unpack_bf16_pair(w):
    u = pltpu.bitcast(w, jnp.uint32)
    lo = pltpu.bitcast(lax.shift_left(u, jnp.uint32(16)), F32)
    hi = pltpu.bitcast(u & jnp.uint32(0xFFFF0000), F32)
    return lo.astype(BF16), hi.astype(BF16)


def _split3(x):
    hi = x.astype(BF16)
    r1 = x - hi.astype(F32)
    mid = r1.astype(BF16)
    lo = (r1 - mid.astype(F32)).astype(BF16)
    return hi, mid, lo


def _dot_split_lhs(x, mask01):
    m = mask01.astype(BF16)
    hi, mid, lo = _split3(x)
    return (_dot(lo, m) + _dot(mid, m)) + _dot(hi, m)


def _each(fn, *lists):
    return [fn(*args) for args in zip(*lists)]


def _inv_unit_triangular(mats, base=INV_BASE):
    n = mats[0].shape[0]
    i = lax.broadcasted_iota(jnp.int32, (n, n), 0)
    j = lax.broadcasted_iota(jnp.int32, (n, n), 1)
    eye = (i == j).astype(F32)
    shift = base.bit_length() - 1
    inside = (i >> shift) == (j >> shift)
    xs = _each(lambda a: jnp.where(inside, a, 0.0), mats)
    ts = _each(lambda d: eye - d, xs)
    for _ in range(shift - 1):
        xs = _each(lambda x: _bdot(x, x), xs)
        yield
        ts = _each(lambda t, x: t + _bdot(t, x), ts, xs)
        yield
    size = base
    while size < GDN_CHUNK:
        shift += 1
        size *= 2
        wider = (i >> shift) == (j >> shift)
        off = wider & jnp.logical_not(inside)
        ots = _each(lambda a, t: _bdot(jnp.where(off, a, 0.0), t), mats, ts)
        yield
        ts = _each(lambda t, ot: t - _bdot(t, ot), ts, ots)
        yield
        inside = wider
    return ts


def _gdn_kernel(alog_ref, dtb_ref,
                q_ref, k_ref, v_ref, qc_ref, kc_ref, vc_ref,
                gt_ref, z_ref, ng_ref,
                o_ref,
                qs_ref, ks_ref, vs_ref, gts_ref, gth_ref,
                wqf_ref, wqb_ref, u0_ref, intra_ref, ket_ref, gef_ref, geb_ref,
                of_ref, ob_ref, *, hb, l_lat, l_ctx, slots_per_group):
    c = GDN_CHUNK
    c2 = 2 * c
    lt = l_lat + l_ctx
    n_ctx = l_ctx // c
    nt = lt // c
    hgrp = pl.program_id(1)

    i2 = lax.broadcasted_iota(jnp.int32, (c2, c2), 0)
    j2 = lax.broadcasted_iota(jnp.int32, (c2, c2), 1)
    same_blk = (i2 & c) == (j2 & c)
    sgn_i = jnp.where(i2 < c, 1, -1)
    sgn_j = jnp.where(j2 < c, 1, -1)
    incl = same_blk & ((j2 - i2) * sgn_i <= 0)
    strict = same_blk & ((j2 - i2) * sgn_i < 0)
    tri_row = (same_blk & ((i2 - j2) * sgn_j <= 0)).astype(F32)
    eye_m = i2 == j2
    eye_b = eye_m.astype(BF16)
    lane_lt_c = j2 < c

    for hh in range(hb):
        head = hgrp * hb + hh
        lane0 = hh * HEAD_DIM
        for (src_c, src_l, dst) in ((qc_ref, q_ref, qs_ref), (kc_ref, k_ref, ks_ref), (vc_ref, v_ref, vs_ref)):
            dst[hh, 0:l_ctx, :] = src_c[0, :, lane0:lane0 + HEAD_DIM]
            dst[hh, l_ctx:lt, :] = src_l[0, :, lane0:lane0 + HEAD_DIM]

        a_f, a_b = alog_ref[head], alog_ref[GDN_HEADS + head]
        d_f, d_b = dtb_ref[head], dtb_ref[GDN_HEADS + head]
        compr = lax.broadcasted_iota(jnp.int32, (nt * 8, 1), 0) & 7
        neg_ar = -jnp.exp(jnp.where(compr == 2, a_f, a_b))
        dtbr = jnp.where(compr == 2, d_f, d_b)
        grow_raw = gt_ref[0, hh]
        grow = jnp.where(compr < 2, _sigmoid(grow_raw), neg_ar * _softplus(grow_raw + dtbr))
        gts_ref[hh] = jnp.zeros(gts_ref.shape[1:], F32)
        gts_ref[hh, :, 0:c] = grow
        gth_ref[hh] = jnp.zeros(gth_ref.shape[1:], F32)
        gth_ref[hh, :, c:c2] = grow

    spg = slots_per_group
    heads = list(range(hb))

    def bwd_chunk(i):
        return jnp.where(i < n_ctx, n_ctx - 1 - i, nt + n_ctx - 1 - i)

    def slot_load(i, hh):
        cbk = bwd_chunk(i)
        rf = pl.multiple_of(i * c, c)
        rb = pl.multiple_of(cbk * c, c)
        two = lambda ref: jnp.concatenate([ref[hh, pl.ds(rf, c), :], ref[hh, pl.ds(rb, c), :]], axis=0).astype(F32)
        g_r = (gts_ref[hh, pl.ds(pl.multiple_of(i * 8, 8), 8), :]
               + gth_ref[hh, pl.ds(pl.multiple_of(cbk * 8, 8), 8), :])
        return two(ks_ref), two(vs_ref), two(qs_ref), g_r

    def slot_stages(loaded):
        k2, v2, q2, g_r = [list(col) for col in zip(*loaded)]
        rcs = _each(lambda g: _dot_split_lhs(g, tri_row), g_r)
        yield
        k2b = _each(lambda x: x.astype(BF16), k2)
        kk = _each(_dot_nt, k2b, k2b)
        qk = _each(lambda q, kb: _dot_nt(q.astype(BF16), kb), q2, k2b)
        yield
        by_dir = lambda g, k: jnp.where(lane_lt_c[0:1, :], g[k:k + 1, :], g[k + 1:k + 2, :])
        lane_sum = lambda mask, row: jnp.sum(jnp.where(mask, row, 0.0), axis=1, keepdims=True)
        gcc = _each(lambda g: lane_sum(incl, by_dir(g, 2)), g_r)
        gcr = _each(lambda r: by_dir(r, 2), rcs)
        beta = _each(lambda g: lane_sum(eye_m, by_dir(g, 0)), g_r)
        gend = _each(lambda s: jnp.concatenate([jnp.broadcast_to(s[c - 1:c, :], (c, 1)),
                                                jnp.broadcast_to(s[c:c + 1, :], (c, 1))], axis=0), gcc)
        decay = _each(lambda gc, gr: jnp.exp(jnp.where(incl, gc - gr, -jnp.inf)), gcc, gcr)
        a_mat = _each(lambda m, d, b: jnp.where(strict, m * d, 0.0) * b, kk, decay, beta)
        t_f32 = yield from _inv_unit_triangular(a_mat)
        t_mat = _each(lambda t: t.astype(BF16), t_f32)
        egc = _each(jnp.exp, gcc)
        u0 = _each(lambda t, v, b: _dot(t, (v * b).astype(BF16)), t_mat, v2, beta)
        w = _each(lambda t, k, b, e: _dot(t, (k * (b * e)).astype(BF16)), t_mat, k2, beta, egc)
        ket = _each(lambda k, ge_, gc: _dot_nt(eye_b, (k * jnp.exp(ge_ - gc)).astype(BF16)), k2, gend, gcc)
        yield
        qd = _each(lambda q, e: (q * e).astype(BF16), q2, egc)
        intra = _each(lambda m, d: (m * d).astype(BF16), qk, decay)
        ge = _each(jnp.exp, gend)
        return [(jnp.concatenate([w_[0:c].astype(BF16), qd_[0:c]], axis=0),
                 jnp.concatenate([w_[c:c2].astype(BF16), qd_[c:c2]], axis=0), u0_, in_,
                 jnp.concatenate([jnp.where(lane_lt_c, kt, 0.0), jnp.where(lane_lt_c, 0.0, kt)], axis=0).astype(BF16),
                 jnp.broadcast_to(g_[0:1, :], (8, HEAD_DIM)), jnp.broadcast_to(g_[c:c + 1, :], (8, HEAD_DIM)))
                for w_, qd_, u0_, in_, kt, g_ in zip(w, qd, u0, intra, ket, ge)]

    def transform_group(g):
        par = lax.rem(g, 2)
        jobs = [(s, hh) for s in range(spg) for hh in heads]
        results = yield from slot_stages([slot_load(g * spg + s, hh) for s, hh in jobs])
        for (s, hh), (wq_f, wq_b, u0, intra, ket, ge_f, ge_b) in zip(jobs, results):
            wqf_ref[par, hh, s] = wq_f
            wqb_ref[par, hh, s] = wq_b
            u0_ref[par, hh, s] = u0
            intra_ref[par, hh, s] = intra
            ket_ref[par, hh, s] = ket
            gef_ref[par, hh, s] = ge_f
            geb_ref[par, hh, s] = ge_b

    def recur_group(g, states):
        par = lax.rem(g, 2)
        sf, sb = list(states[0::2]), list(states[1::2])
        for s in range(spg):
            i = g * spg + s
            rf = pl.multiple_of(i * c, c)
            rb = pl.multiple_of(bwd_chunk(i) * c, c)
            r_f = _each(lambda hh, st: _dot(wqf_ref[par, hh, s], st.astype(BF16)), heads, sf)
            r_b = _each(lambda hh, st: _dot(wqb_ref[par, hh, s], st.astype(BF16)), heads, sb)
            u0 = _each(lambda hh: u0_ref[par, hh, s], heads)
            yield
            u2 = _each(lambda u, f, b_: jnp.concatenate([u[0:c] - f[0:c], u[c:c2] - b_[0:c]], axis=0).astype(BF16),
                       u0, r_f, r_b)
            ket = _each(lambda hh: ket_ref[par, hh, s], heads)
            df = _each(lambda kt, u: _dot(kt[0:HEAD_DIM], u), ket, u2)
            db = _each(lambda kt, u: _dot(kt[HEAD_DIM:2 * HEAD_DIM], u), ket, u2)
            iu = _each(lambda hh, u: _dot(intra_ref[par, hh, s], u), heads, u2)
            sf = _each(lambda hh, st, d: gef_ref[par, hh, s][0:1, :] * st + d, heads, sf, df)
            sb = _each(lambda hh, st, d: geb_ref[par, hh, s][0:1, :] * st + d, heads, sb, db)
            for hh in heads:
                of_ref[hh, pl.ds(rf, c), :] = r_f[hh][c:c2] + iu[hh][0:c]
                ob_ref[hh, pl.ds(rb, c), :] = r_b[hh][c:c2] + iu[hh][c:c2]
            yield
        return tuple(x for pair in zip(sf, sb) for x in pair)

    def drive(transform, recur, transforms_per_recur=1):
        states = None
        while transform is not None or recur is not None:
            if transform is not None:
                for _ in range(transforms_per_recur):
                    try:
                        next(transform)
                    except StopIteration:
                        transform = None
                        break
            if recur is not None:
                try:
                    next(recur)
                except StopIteration as stop:
                    states, recur = stop.value, None
        return states

    n_groups = nt // spg
    zero_state = tuple(jnp.zeros((HEAD_DIM, HEAD_DIM), F32) for _ in range(2 * hb))
    drive(transform_group(0), None)
    states = lax.fori_loop(0, n_groups - 1,
                           lambda g, st: drive(transform_group(g + 1), recur_group(g, st)), zero_state)

    ng = ng_ref[...]
    blk = 256

    def finish_block(hh, b):
        r, lane0 = b * blk, hh * HEAD_DIM
        o = of_ref[hh, l_ctx + r:l_ctx + r + blk, :] + ob_ref[hh, l_ctx + r:l_ctx + r + blk, :]
        y = o * lax.rsqrt(jnp.mean(o * o, axis=-1, keepdims=True) + LN_EPS) * ng
        z = z_ref[0, r:r + blk, lane0:lane0 + HEAD_DIM].astype(F32)
        o_ref[0, r:r + blk, lane0:lane0 + HEAD_DIM] = (y * _silu(z)).astype(BF16)

    def finish_blocks(jobs):
        for hh, b in jobs:
            finish_block(hh, b)
            yield

    first = -(-(spg * c) // blk)
    last = (nt - spg - n_ctx) * c // blk
    early = [(hh, b) for b in range(first, last) for hh in heads]
    late = [(hh, b) for b in range(l_lat // blk) if not first <= b < last for hh in heads]
    drive(finish_blocks(early), recur_group(n_groups - 1, states))
    drive(finish_blocks(late), None)


def _gdn_mixer(qkv_lat, qkv_ctx, z_lat, gate_rows, a_log, dt_bias, norm_g, hb):
    b, l_lat, _ = qkv_lat.shape
    l_ctx = qkv_ctx.shape[1]
    lt = l_lat + l_ctx
    nt = lt // GDN_CHUNK
    hw = hb * HEAD_DIM
    ngrp = GDN_HEADS // hb
    c2 = 2 * GDN_CHUNK
    spg = GDN_SLOTS_PER_GROUP
    assert nt % spg == 0 and nt // spg >= 2 and GDN_HEADS % hb == 0

    def seq_spec(length, part):
        return pl.BlockSpec((1, length, hw), lambda i, j, *_: (i, 0, part * ngrp + j))

    grid_spec = pltpu.PrefetchScalarGridSpec(
        num_scalar_prefetch=2,
        grid=(b, ngrp),
        in_specs=[seq_spec(l_lat, 0), seq_spec(l_lat, 1), seq_spec(l_lat, 2),
                  seq_spec(l_ctx, 0), seq_spec(l_ctx, 1), seq_spec(l_ctx, 2),
                  pl.BlockSpec((1, hb, nt * 8, GDN_CHUNK), lambda i, j, *_: (i, j, 0, 0)),
                  pl.BlockSpec((1, l_lat, hw), lambda i, j, *_: (i, 0, j)),
                  pl.BlockSpec((1, HEAD_DIM), lambda i, j, *_: (0, 0))],
        out_specs=pl.BlockSpec((1, l_lat, hw), lambda i, j, *_: (i, 0, j)),
        scratch_shapes=[
            pltpu.VMEM((hb, lt, HEAD_DIM), BF16),
            pltpu.VMEM((hb, lt, HEAD_DIM), BF16),
            pltpu.VMEM((hb, lt, HEAD_DIM), BF16),
            pltpu.VMEM((hb, nt * 8, c2), F32),
            pltpu.VMEM((hb, nt * 8, c2), F32),
            pltpu.VMEM((2, hb, spg, c2, HEAD_DIM), BF16),
            pltpu.VMEM((2, hb, spg, c2, HEAD_DIM), BF16),
            pltpu.VMEM((2, hb, spg, c2, HEAD_DIM), F32),
            pltpu.VMEM((2, hb, spg, c2, c2), BF16),
            pltpu.VMEM((2, hb, spg, 2 * HEAD_DIM, c2), BF16),
            pltpu.VMEM((2, hb, spg, 8, HEAD_DIM), F32),
            pltpu.VMEM((2, hb, spg, 8, HEAD_DIM), F32),
            pltpu.VMEM((hb, lt, HEAD_DIM), F32),
            pltpu.VMEM((hb, lt, HEAD_DIM), F32),
        ])
    kern = functools.partial(_gdn_kernel, hb=hb, l_lat=l_lat, l_ctx=l_ctx, slots_per_group=spg)
    return pl.pallas_call(
        kern,
        grid_spec=grid_spec,
        out_shape=jax.ShapeDtypeStruct((b, l_lat, GDN_HEADS * HEAD_DIM), BF16),
        compiler_params=_params(("arbitrary", "arbitrary")),
        name="gdn_mixer",
    )(a_log, dt_bias, qkv_lat, qkv_lat, qkv_lat, qkv_ctx, qkv_ctx, qkv_ctx,
      gate_rows, z_lat, norm_g.reshape(1, HEAD_DIM))


def _ret_kernel(lg_ref, q_ref, k_ref, v_ref, gate_ref, qc_ref, kc_ref, vc_ref, cos_ref, sin_ref, ng_ref,
                o_ref, q_s, k_s, rf_s, rb_s, *, l_lat, l_ctx, hb):
    c = RET_CHUNK
    n_lat = l_lat // c
    n_ctx = l_ctx // c
    heads = list(range(hb))
    lanes = [slice(hh * HEAD_DIM, (hh + 1) * HEAD_DIM) for hh in heads]
    pos_c = lax.broadcasted_iota(jnp.int32, (c, 1), 0).astype(F32)
    ii = lax.broadcasted_iota(jnp.int32, (c, c), 0)
    jj = lax.broadcasted_iota(jnp.int32, (c, c), 1)
    dif = (ii - jj).astype(F32)
    e2 = lax.broadcasted_iota(jnp.int32, (HEAD_DIM, HEAD_DIM), 0)
    f2 = lax.broadcasted_iota(jnp.int32, (HEAD_DIM, HEAD_DIM), 1)
    eye_b = (e2 == f2).astype(BF16)
    kscale = HEAD_DIM ** -0.5
    lg_f = [lg_ref[pl.program_id(1) * hb + hh] for hh in heads]
    lg_b = [lg_ref[RET_HEADS + pl.program_id(1) * hb + hh] for hh in heads]
    dmat = _each(lambda f, b_: (jnp.exp(jnp.where(ii >= jj, f * dif, -jnp.inf))
                                + jnp.exp(jnp.where(jj >= ii, -b_ * dif, -jnp.inf))), lg_f, lg_b)
    kdec_f = _each(lambda f: jnp.exp(f * (c - 1 - pos_c)), lg_f)
    kdec_b = _each(lambda b_: jnp.exp(b_ * pos_c), lg_b)
    qdec_f = _each(lambda f: jnp.exp(f * (pos_c + 1.0)), lg_f)
    qdec_b = _each(lambda b_: jnp.exp(b_ * (c - pos_c)), lg_b)
    cd_f = _each(lambda f: jnp.exp(jnp.full((1, HEAD_DIM), f * c, F32)), lg_f)
    cd_b = _each(lambda b_: jnp.exp(jnp.full((1, HEAD_DIM), b_ * c, F32)), lg_b)

    def transposed(kd):
        return _dot_nt(eye_b, kd.astype(BF16)).astype(BF16)

    r_f = [jnp.zeros((HEAD_DIM, HEAD_DIM), F32) for _ in heads]
    r_b = [jnp.zeros((HEAD_DIM, HEAD_DIM), F32) for _ in heads]
    for n in range(n_ctx):
        m = n_ctx - 1 - n
        kf = _each(lambda ln, d: transposed(kc_ref[0, n * c:(n + 1) * c, ln].astype(F32) * kscale * d), lanes, kdec_f)
        kb = _each(lambda ln, d: transposed(kc_ref[0, m * c:(m + 1) * c, ln].astype(F32) * kscale * d), lanes, kdec_b)
        pf = _each(lambda kt, ln: _dot(kt, vc_ref[0, n * c:(n + 1) * c, ln]), kf, lanes)
        pb = _each(lambda kt, ln: _dot(kt, vc_ref[0, m * c:(m + 1) * c, ln]), kb, lanes)
        r_f = _each(lambda d, r, p: d * r + p, cd_f, r_f, pf)
        r_b = _each(lambda d, r, p: d * r + p, cd_b, r_b, pb)

    def rope_body(n, carry):
        r = pl.multiple_of(n * c, c)
        cs = cos_ref[pl.ds(r, c), :]
        sn = sin_ref[pl.ds(r, c), :]
        for hh, ln in zip(heads, lanes):
            q = q_ref[0, pl.ds(r, c), ln].astype(F32)
            k = k_ref[0, pl.ds(r, c), ln].astype(F32) * kscale
            q_s[hh, pl.ds(r, c), :] = q * cs + pltpu.roll(q, HEAD_DIM // 2, 1) * sn
            k_s[hh, pl.ds(r, c), :] = k * cs + pltpu.roll(k, HEAD_DIM // 2, 1) * sn
        return carry

    lax.fori_loop(0, n_lat, rope_body, 0)

    chunks = list(range(n_lat))
    rows = [slice(n * c, (n + 1) * c) for n in chunks]
    jobs = [(hh, n) for hh in heads for n in chunks]
    k_c = [k_s[hh, rows[n], :] for hh, n in jobs]
    kft = [transposed(k * kdec_f[hh]) for k, (hh, n) in zip(k_c, jobs)]
    kbt = [transposed(k * kdec_b[hh]) for k, (hh, n) in zip(k_c, jobs)]
    kvf = [_dot(kt, v_ref[0, rows[n], lanes[hh]]) for kt, (hh, n) in zip(kft, jobs)]
    kvb = [_dot(kt, v_ref[0, rows[n], lanes[hh]]) for kt, (hh, n) in zip(kbt, jobs)]
    for hh in heads:
        r = r_f[hh]
        for n in chunks:
            rf_s[hh, n] = r
            r = cd_f[hh] * r + kvf[hh * n_lat + n]
        r = r_b[hh]
        for n in reversed(chunks):
            rb_s[hh, n] = r
            r = cd_b[hh] * r + kvb[hh * n_lat + n]

    group = 4
    for g0 in range(0, len(jobs), group):
        part = jobs[g0:g0 + group]
        q_c = [q_s[hh, rows[n], :] for hh, n in part]
        s = [_dot_nt(q.astype(BF16), k_s[hh, rows[n], :].astype(BF16)) for q, (hh, n) in zip(q_c, part)]
        att = [(s_ * dmat[hh]).astype(BF16) for s_, (hh, n) in zip(s, part)]
        o = [_dot(a, v_ref[0, rows[n], lanes[hh]]) for a, (hh, n) in zip(att, part)]
        qd = [jnp.concatenate([(q * qdec_f[hh]).astype(BF16), (q * qdec_b[hh]).astype(BF16)], axis=1)
              for q, (hh, n) in zip(q_c, part)]
        st = [jnp.concatenate([rf_s[hh, n], rb_s[hh, n]], axis=0).astype(BF16) for hh, n in part]
        o = _each(lambda o_, qd_, st_: o_ + _dot(qd_, st_), o, qd, st)
        for o_, (hh, n) in zip(o, part):
            y = _ln_rows(o_) * ng_ref[:, lanes[hh]]
            g = gate_ref[0, rows[n], lanes[hh]].astype(F32)
            o_ref[0, rows[n], lanes[hh]] = (y * _silu(g)).astype(BF16)


def _ret_mixer(ret_lat, ret_ctx, log_gamma, norm_g, cos_t, sin_t):
    b, l_lat, _ = ret_lat.shape
    l_ctx = ret_ctx.shape[1]
    n_lat = l_lat // RET_CHUNK
    hb = RET_HEADS
    hw = hb * HEAD_DIM
    ngrp = RET_HEADS // hb

    def seq_spec(length, part):
        return pl.BlockSpec((1, length, hw), lambda i, j, *_: (i, 0, part * ngrp + j))

    grid_spec = pltpu.PrefetchScalarGridSpec(
        num_scalar_prefetch=1,
        grid=(b, ngrp),
        in_specs=[seq_spec(l_lat, 0), seq_spec(l_lat, 1), seq_spec(l_lat, 2), seq_spec(l_lat, 3),
                  seq_spec(l_ctx, 0), seq_spec(l_ctx, 1), seq_spec(l_ctx, 2),
                  pl.BlockSpec((l_lat, HEAD_DIM), lambda i, j, *_: (0, 0)),
                  pl.BlockSpec((l_lat, HEAD_DIM), lambda i, j, *_: (0, 0)),
                  pl.BlockSpec((1, hw), lambda i, j, *_: (0, j))],
        out_specs=pl.BlockSpec((1, l_lat, hw), lambda i, j, *_: (i, 0, j)),
        scratch_shapes=[pltpu.VMEM((hb, l_lat, HEAD_DIM), F32),
                        pltpu.VMEM((hb, l_lat, HEAD_DIM), F32),
                        pltpu.VMEM((hb, n_lat, HEAD_DIM, HEAD_DIM), F32),
                        pltpu.VMEM((hb, n_lat, HEAD_DIM, HEAD_DIM), F32)])
    kern = functools.partial(_ret_kernel, l_lat=l_lat, l_ctx=l_ctx, hb=hb)
    return pl.pallas_call(
        kern,
        grid_spec=grid_spec,
        out_shape=jax.ShapeDtypeStruct((b, l_lat, RET_HEADS * HEAD_DIM), BF16),
        compiler_params=_params(("arbitrary", "arbitrary")),
        name="ret_mixer",
    )(log_gamma.reshape(-1), ret_lat, ret_lat, ret_lat, ret_lat, ret_ctx, ret_ctx, ret_ctx,
      cos_t, sin_t, norm_g.reshape(1, -1))


def _rope_tables(l_lat):
    rows = l_lat // GRID_W
    row = jnp.repeat(jnp.arange(rows, dtype=F32), GRID_W)
    col = jnp.tile(jnp.arange(GRID_W, dtype=F32), rows)
    quarter = HEAD_DIM // 4
    inv = ROPE_THETA ** (-jnp.arange(quarter, dtype=F32) / quarter)
    ang = jnp.concatenate([row[:, None] * inv, col[:, None] * inv], -1)
    cos, sin = jnp.cos(ang), jnp.sin(ang)
    return jnp.concatenate([cos, cos], -1), jnp.concatenate([-sin, sin], -1)


def _top_rows(vals, k):
    n = vals.shape[0]
    idx = lax.broadcasted_iota(jnp.int32, vals.shape, 0)
    taken = jnp.zeros(vals.shape, jnp.int32)
    firsts = []
    for _ in range(k):
        live = jnp.where(taken == 0, vals, -jnp.inf)
        top = jnp.max(live, axis=0, keepdims=True)
        cand = jnp.where((live == top) & (taken == 0), idx, n)
        first = jnp.min(cand, axis=0, keepdims=True)
        taken = taken + (idx == first).astype(jnp.int32)
        firsts.append(first)
    return firsts, taken


def _outproj_kernel(a_ref, r_ref, x_ref, wa_ref, wr_ref, g1_ref, sh2_ref, sc2_ref, lg_ref, lb_ref,
                    wrt_ref, rb_ref, x1_ref, ha_ref, hb_ref, ek_ref, rk_ref, wk_ref, cnt_ref, carry_ref):
    y = _dot(a_ref[0], wa_ref[...]) + _dot(r_ref[0], wr_ref[...])
    x1 = _ln_rows(DEEPNORM_ALPHA * x_ref[0] + g1_ref[0] * y) * lg_ref[...] + lb_ref[...]
    x1_ref[0] = x1
    h = _ln_rows(x1) * (1.0 + sc2_ref[0]) + sh2_ref[0]
    for ref, part in zip((ha_ref, hb_ref), _pack_rows(h)):
        ref[0] = part
    s = _sigmoid(_dot_nt(wrt_ref[...], h, HIGHEST))
    sb = s + rb_ref[...]
    tm = s.shape[1]
    per = N_EXPERTS // N_GROUPS
    sub = lax.broadcasted_iota(jnp.int32, (per, tm), 0)
    gs_rows = []
    for g in range(N_GROUPS):
        blk = sb[g * per:(g + 1) * per, :]
        m1 = jnp.max(blk, axis=0, keepdims=True)
        first = jnp.min(jnp.where(blk == m1, sub, per), axis=0, keepdims=True)
        m2 = jnp.max(jnp.where(sub == first, -jnp.inf, blk), axis=0, keepdims=True)
        gs_rows.append(m1 + m2)
    gscore = jnp.concatenate(gs_rows, axis=0)
    _, gtaken = _top_rows(gscore, TOPK_GROUPS)
    emask = jnp.concatenate([jnp.broadcast_to(gtaken[g:g + 1, :], (per, tm)) for g in range(N_GROUPS)], axis=0)
    masked = jnp.where(emask > 0, sb, -jnp.inf)
    firsts, taken = _top_rows(masked, TOP_K)
    first_step = (pl.program_id(0) == 0) & (pl.program_id(1) == 0)

    @pl.when(first_step)
    def _():
        carry_ref[...] = jnp.zeros(carry_ref.shape, F32)

    sel_f = taken.astype(F32)
    ti = lax.broadcasted_iota(jnp.int32, (tm, tm), 0)
    tj = lax.broadcasted_iota(jnp.int32, (tm, tm), 1)
    rank = _dot(sel_f.astype(BF16), (ti < tj).astype(BF16)) + carry_ref[:, 0:1]
    carry = carry_ref[...] + jnp.sum(sel_f, axis=1, keepdims=True)
    carry_ref[...] = carry
    cnt_ref[...] = carry
    eidx = lax.broadcasted_iota(jnp.int32, (N_EXPERTS, tm), 0)
    picked = []
    for k in range(TOP_K):
        hit = eidx == firsts[k]
        pick = lambda v: jnp.sum(jnp.where(hit, v, 0.0), axis=0, keepdims=True)
        ek_ref[k:k + 1, :] = firsts[k]
        rk_ref[k:k + 1, :] = pick(rank).astype(jnp.int32)
        picked.append(pick(s))
    total = picked[0]
    for k in range(1, TOP_K):
        total = total + picked[k]
    for k in range(TOP_K):
        wk_ref[k:k + 1, :] = picked[k] / total * ROUTED_SCALE


def _outproj(a_lat, r_lat, x, w_a, w_r, g1, sh2, sc2, ln_g, ln_b, w_router_t, router_bias, tm, b0):
    b, l, half = a_lat.shape
    d = x.shape[-1]
    nl = l // tm
    row = lambda i, j: (i, j, 0)
    per_b = lambda i, j: (i + b0, 0, 0)
    const = lambda i, j: (0, 0)
    return pl.pallas_call(
        _outproj_kernel,
        grid=(b, nl),
        in_specs=[pl.BlockSpec((1, tm, half), row),
                  pl.BlockSpec((1, tm, half), row),
                  pl.BlockSpec((1, tm, d), lambda i, j: (i + b0, j, 0)),
                  pl.BlockSpec((half, d), const),
                  pl.BlockSpec((half, d), const),
                  pl.BlockSpec((1, 1, d), per_b),
                  pl.BlockSpec((1, 1, d), per_b),
                  pl.BlockSpec((1, 1, d), per_b),
                  pl.BlockSpec((1, d), const),
                  pl.BlockSpec((1, d), const),
                  pl.BlockSpec((N_EXPERTS, d), const),
                  pl.BlockSpec((N_EXPERTS, 1), const)],
        out_specs=[pl.BlockSpec((1, tm, d), row),
                   pl.BlockSpec((1, tm, d // (2 * PACK_GROUPS)), row),
                   pl.BlockSpec((1, tm, d // (2 * PACK_GROUPS)), row),
                   pl.BlockSpec((TOP_K, tm), lambda i, j: (0, i * nl + j)),
                   pl.BlockSpec((TOP_K, tm), lambda i, j: (0, i * nl + j)),
                   pl.BlockSpec((TOP_K, tm), lambda i, j: (0, i * nl + j)),
                   pl.BlockSpec((N_EXPERTS, HEAD_DIM), const)],
        out_shape=[jax.ShapeDtypeStruct((b, l, d), F32),
                   jax.ShapeDtypeStruct((b, l, d // (2 * PACK_GROUPS)), jnp.int32),
                   jax.ShapeDtypeStruct((b, l, d // (2 * PACK_GROUPS)), jnp.int32),
                   jax.ShapeDtypeStruct((TOP_K, b * l), jnp.int32),
                   jax.ShapeDtypeStruct((TOP_K, b * l), jnp.int32),
                   jax.ShapeDtypeStruct((TOP_K, b * l), F32),
                   jax.ShapeDtypeStruct((N_EXPERTS, HEAD_DIM), F32)],
        scratch_shapes=[pltpu.VMEM((N_EXPERTS, HEAD_DIM), F32)],
        compiler_params=_params(("arbitrary", "arbitrary")),
        name="outproj_router",
    )(a_lat, r_lat, x, w_a, w_r, g1, sh2, sc2, ln_g.reshape(1, d), ln_b.reshape(1, d),
      w_router_t, router_bias.reshape(N_EXPERTS, 1))


def _pack_rows(x):
    n = x.shape[1] // PACK_GROUPS
    return [_pack_bf16_pair(x[:, g * n:(g + 1) * n]) for g in range(PACK_GROUPS)]


def _unpack_rows(parts):
    cols = []
    for p in parts:
        cols += list(_unpack_bf16_pair(p))
    return cols


def _glu_ffn(cols, w_gate_up, w_down, ff):
    n = cols[0].shape[1]
    ab = _dot(cols[0], w_gate_up[0:n])
    for i in range(1, len(cols)):
        ab = ab + _dot(cols[i], w_gate_up[i * n:(i + 1) * n])
    act = (_silu(ab[:, :ff]) * ab[:, ff:]).astype(BF16)
    return _dot(act, w_down)


def _expert_kernel(te_ref, used_ref, *refs, ff):
    tps = EXPERT_TILES_PER_STEP
    xs_refs, w_refs = refs[:PACK_GROUPS], refs[PACK_GROUPS:PACK_GROUPS + 2 * tps]
    ys_refs = refs[PACK_GROUPS + 2 * tps:2 * PACK_GROUPS + 2 * tps]
    wbf_refs = refs[2 * PACK_GROUPS + 2 * tps:]
    i = pl.program_id(0)
    used = used_ref[0]

    def run(n_active):
        for t in range(n_active):
            tile = i * tps + t
            prev = jnp.maximum(tile - tps, 0)

            @pl.when((i == 0) | (te_ref[tile] != te_ref[prev]))
            def _(t=t):
                wbf_refs[2 * t][...] = w_refs[2 * t][0].astype(BF16)
                wbf_refs[2 * t + 1][...] = w_refs[2 * t + 1][0].astype(BF16)

        sub = EXPERT_TILE // EXPERT_SUBTILES
        blocks = [(t, slice(t * EXPERT_TILE + s * sub, t * EXPERT_TILE + (s + 1) * sub))
                  for t in range(n_active) for s in range(EXPERT_SUBTILES)]
        wgu = [wbf_refs[2 * t][...] for t in range(n_active)]
        wd = [wbf_refs[2 * t + 1][...] for t in range(n_active)]
        cols = [_unpack_rows([r[b, :] for r in xs_refs]) for t, b in blocks]
        n = cols[0][0].shape[1]
        ab = [_dot(c[0], wgu[t][0:n]) for c, (t, b) in zip(cols, blocks)]
        for j in range(1, 2 * PACK_GROUPS):
            ab = [acc + _dot(c[j], wgu[t][j * n:(j + 1) * n]) for acc, c, (t, b) in zip(ab, cols, blocks)]
        act = [(_silu(a[:, :ff]) * a[:, ff:]).astype(BF16) for a in ab]
        y = [_dot(a, wd[t]) for a, (t, b) in zip(act, blocks)]
        for (t, b), y_b in zip(blocks, y):
            for ref, part in zip(ys_refs, _pack_rows(y_b)):
                ref[b, :] = part

    for n_active in range(1, tps + 1):
        full = (i * tps + n_active <= used) if n_active == tps else (i * tps + n_active == used)
        pl.when(full)(functools.partial(run, n_active))


def _expert_ffn(xs, tile_expert, used_tiles, w_gu, w_d):
    n_rows, dp = xs[0].shape
    n_e, d, ff2 = w_gu.shape
    ff = ff2 // 2
    tps = EXPERT_TILES_PER_STEP
    step_rows = tps * EXPERT_TILE
    assert n_rows % step_rows == 0
    rows_spec = pl.BlockSpec((step_rows, dp), lambda i, te, used: (jnp.minimum(i, (used[0] - 1) // tps), 0))
    w_specs = []
    for t in range(tps):
        w_specs += [pl.BlockSpec((1, d, ff2), lambda i, te, used, t=t: (te[i * tps + t], 0, 0)),
                    pl.BlockSpec((1, ff, d), lambda i, te, used, t=t: (te[i * tps + t], 0, 0))]
    grid_spec = pltpu.PrefetchScalarGridSpec(
        num_scalar_prefetch=2,
        grid=(n_rows // step_rows,),
        in_specs=[rows_spec] * PACK_GROUPS + w_specs,
        out_specs=[rows_spec] * PACK_GROUPS,
        scratch_shapes=[pltpu.VMEM((d, ff2), BF16), pltpu.VMEM((ff, d), BF16)] * tps)
    return pl.pallas_call(
        functools.partial(_expert_kernel, ff=ff),
        grid_spec=grid_spec,
        out_shape=[jax.ShapeDtypeStruct((n_rows, dp), jnp.int32)] * PACK_GROUPS,
        compiler_params=_params(("arbitrary",)),
        name="expert_ffn",
    )(tile_expert, used_tiles, *xs, *([w_gu, w_d] * tps))


def _combine_kernel(*refs, ff):
    h_refs, y_refs = refs[:PACK_GROUPS], refs[PACK_GROUPS:2 * PACK_GROUPS]
    x1_ref, wk_ref, wsg_ref, wsd_ref, g2_ref, lg_ref, lb_ref, o_ref = refs[2 * PACK_GROUPS:]
    shared = _glu_ffn(_unpack_rows([r[...] for r in h_refs]), wsg_ref[...], wsd_ref[...], ff)
    wk = wk_ref[...]
    n = y_refs[0].shape[2]
    blocks = [shared[:, i * n:(i + 1) * n] for i in range(2 * PACK_GROUPS)]
    for k in range(TOP_K):
        cols = _unpack_rows([r[k] for r in y_refs])
        blocks = [acc + wk[:, k:k + 1] * c.astype(F32) for acc, c in zip(blocks, cols)]
    f = jnp.concatenate(blocks, axis=1)
    o_ref[...] = _ln_rows(DEEPNORM_ALPHA * x1_ref[...] + g2_ref[0] * f) * lg_ref[...] + lb_ref[...]


def _moe_combine(h, x1, y_sel, wk_tok, w_sg, w_sd, g2, ln_g, ln_b, tm, b0, l, prev_out):
    t_part, d = x1.shape
    t = g2.shape[0] * l
    dp = h[0].shape[1]
    tiles_per_batch = l // tm
    off = b0 * tiles_per_batch
    ff = w_sd.shape[0]
    row = lambda i: (i, 0)
    const = lambda i: (0, 0)
    in_specs = ([pl.BlockSpec((tm, dp), row)] * PACK_GROUPS
                + [pl.BlockSpec((TOP_K, tm, dp), lambda i: (0, i, 0))] * PACK_GROUPS
                + [pl.BlockSpec((tm, d), row),
                   pl.BlockSpec((tm, TOP_K), row),
                   pl.BlockSpec(w_sg.shape, const),
                   pl.BlockSpec(w_sd.shape, const),
                   pl.BlockSpec((1, 1, d), lambda i: (i // tiles_per_batch + b0, 0, 0)),
                   pl.BlockSpec((1, d), const),
                   pl.BlockSpec((1, d), const)])
    args = [*h, *y_sel, x1, wk_tok, w_sg, w_sd, g2, ln_g.reshape(1, d), ln_b.reshape(1, d)]
    kern = functools.partial(_combine_kernel, ff=ff)
    aliases = {}
    if prev_out is not None:
        in_specs.append(pl.BlockSpec(memory_space=pl.ANY))
        aliases = {len(args): 0}
        args.append(prev_out)
        kern = lambda *refs: _combine_kernel(*refs[:-2], refs[-1], ff=ff)
    return pl.pallas_call(
        kern,
        grid=(t_part // tm,),
        in_specs=in_specs,
        out_specs=pl.BlockSpec((tm, d), lambda i: (i + off, 0)),
        out_shape=jax.ShapeDtypeStruct((t, d), F32),
        input_output_aliases=aliases,
        compiler_params=_params(("arbitrary",)),
        name="moe_combine",
    )(*args)


def _routing_tables(ek, rk, counts, n_rows):
    padded = (counts + EXPERT_TILE - 1) // EXPERT_TILE * EXPERT_TILE
    ends = jnp.cumsum(padded)
    starts = ends - padded
    experts = jnp.arange(N_EXPERTS, dtype=jnp.int32)
    start_of = jnp.sum(jnp.where(ek[None] == experts[:, None, None], starts[:, None, None], 0), axis=0)
    pos = start_of + rk
    tile_start = jnp.arange(n_rows // EXPERT_TILE, dtype=jnp.int32) * EXPERT_TILE
    tile_expert = jnp.minimum(jnp.sum(ends[None, :] <= tile_start[:, None], axis=1), N_EXPERTS - 1).astype(jnp.int32)
    used_tiles = (ends[-1:] // EXPERT_TILE).astype(jnp.int32)
    return pos.astype(jnp.int32), tile_expert, used_tiles


def _sc_mesh():
    return plsc.VectorSubcoreMesh(core_axis_name="core", subcore_axis_name="subcore")


def _dispatch_rows(rows, pos, n_rows):
    t, dp = rows.shape
    n_k = pos.shape[0]

    @functools.partial(pl.kernel, mesh=_sc_mesh(), out_type=jax.ShapeDtypeStruct((n_rows, dp), rows.dtype),
                       scratch_types=[])
    def scatter(rows_hbm, pos_hbm, out_hbm):
        def body(rows_vmem, idx_vmem):
            pltpu.sync_copy(rows_vmem, out_hbm.at[idx_vmem.at[0]])

        pltpu.emit_pipeline(
            body,
            grid=(t // SC_WINDOW, n_k),
            in_specs=[pl.BlockSpec((SC_WINDOW, dp), lambda i, k: (i, 0)),
                      pl.BlockSpec((1, SC_WINDOW), lambda i, k: (k, i))],
            out_specs=[],
            core_axis_name=("core", "subcore"),
            dimension_semantics=(pltpu.PARALLEL, pltpu.ARBITRARY),
        )(rows_hbm, pos_hbm)

    return scatter(rows, pos)


def _gather_rows(table, pos):
    n_k, t = pos.shape
    dp = table.shape[1]

    @functools.partial(pl.kernel, mesh=_sc_mesh(), out_type=jax.ShapeDtypeStruct((n_k * t, dp), table.dtype),
                       scratch_types=[])
    def gather(table_hbm, pos_hbm, out_hbm):
        def body(idx_vmem, out_vmem):
            pltpu.sync_copy(table_hbm.at[idx_vmem.at[0]], out_vmem)

        pltpu.emit_pipeline(
            body,
            grid=(n_k * t // SC_WINDOW,),
            in_specs=[pl.BlockSpec((1, SC_WINDOW), lambda i: (0, i))],
            out_specs=[pl.BlockSpec((SC_WINDOW, dp), lambda i: (i, 0))],
            core_axis_name=("core", "subcore"),
            dimension_semantics=(pltpu.PARALLEL,),
        )(pos_hbm, out_hbm)

    return gather(table, pos.reshape(1, n_k * t)).reshape(n_k, t, dp)


def _routed_moe(h, x1, ek, rk, wk, cnt, w_sg, w_sd, w_gu, w_d, g2, ln_g, ln_b, b0, l, prev_out):
    t = x1.shape[0]
    n_rows = t * TOP_K + N_EXPERTS * EXPERT_TILE
    pos, tile_expert, used_tiles = _routing_tables(ek, rk, cnt[:, 0].astype(jnp.int32), n_rows)
    xs = [_dispatch_rows(rows, pos, n_rows) for rows in h]
    ys = _expert_ffn(xs, tile_expert, used_tiles, w_gu, w_d)
    y_sel = [_gather_rows(table, pos) for table in ys]
    return _moe_combine(h, x1, y_sel, wk.T, w_sg, w_sd, g2, ln_g, ln_b, ROW_TILE, b0, l, prev_out)


def _gate_layouts(g_lat, g_ctx):
    g = jnp.concatenate([g_ctx, g_lat], axis=1)[..., :4 * GDN_HEADS]
    b, lt, _ = g.shape
    nt = lt // GDN_CHUNK
    rows = jnp.transpose(g.reshape(b, nt, GDN_CHUNK, 4, GDN_HEADS), (0, 4, 1, 3, 2))
    return jnp.pad(rows, ((0, 0), (0, 0), (0, 0), (0, 4), (0, 0))).reshape(b, GDN_HEADS, nt * 8, GDN_CHUNK)


def kernel(x, c, ctx, c_ctx, w_mod, b_mod, w_in, conv_w, gdn_a_log, gdn_dt_bias, gdn_norm_g, ret_log_gamma,
           ret_norm_g, w_out, ln1_g, ln1_b, w_router, router_bias, w_gate_up, w_down, w_shared_gate_up,
           w_shared_down, ln2_g, ln2_b):
    b, l, d = x.shape
    lc = ctx.shape[1]
    gw = GDN_HEADS * HEAD_DIM
    assert d == 2 * gw and w_in.shape[-1] == 8 * gw + 4 * GDN_HEADS and w_gate_up.shape[1] == N_EXPERTS
    assert b % BATCH_PARTS == 0 and l % ROW_TILE == 0 and l % RET_CHUNK == 0 and l % GRID_W == 0
    assert lc % RET_CHUNK == 0 and lc % HALO == 0 and (b // BATCH_PARTS * l) % EXPERT_TILE == 0

    n_mod = -(-(b + 1) // 8) * 8
    cc = jnp.zeros((n_mod, d), F32).at[:b].set(c).at[b].set(c_ctx)
    mod = _modulation(cc, w_mod[0], b_mod[0])
    sh1, sc1, g1, sh2, sc2, g2 = [mod[:b, i * d:(i + 1) * d].reshape(b, 1, d) for i in range(6)]
    csh1 = jnp.broadcast_to(mod[b, 0:d].reshape(1, 1, d), (b, 1, d))
    csc1 = jnp.broadcast_to(mod[b, d:2 * d].reshape(1, 1, d), (b, 1, d))

    w = w_in[0]
    o_gate = 3 * gw + gw
    o_ret = o_gate + 4 * GDN_HEADS
    w_main = jnp.concatenate([w[:, :o_gate], w[:, o_ret:]], axis=1).astype(BF16)
    w_gate = w[:, o_gate:o_ret]
    cos_t, sin_t = _rope_tables(l)
    wo = w_out[0].astype(BF16)
    w_sg, w_sd = w_shared_gate_up[0].astype(BF16), w_shared_down[0].astype(BF16)
    w_router_t = w_router[0].T

    nb = b // BATCH_PARTS
    out = None
    for part in range(BATCH_PARTS):
        b0 = part * nb
        qkv_lat, z_lat, ret_lat, g_lat = _inproj(x, sh1, sc1, w_main, w_gate, conv_w[0], ROW_TILE, b0, nb)
        qkv_ctx, _, ret_ctx, g_ctx = _inproj(ctx, csh1, csc1, w_main, w_gate, conv_w[0], lc, b0, nb)
        gate_rows = _gate_layouts(g_lat, g_ctx)
        a_lat = _gdn_mixer(qkv_lat, qkv_ctx, z_lat, gate_rows, gdn_a_log[0],
                           gdn_dt_bias[0], gdn_norm_g[0], GDN_HEADS_PER_STEP)
        r_lat = _ret_mixer(ret_lat, ret_ctx, ret_log_gamma[0], ret_norm_g[0], cos_t, sin_t)
        x1, h_a, h_b, ek, rk, wk, cnt = _outproj(a_lat, r_lat, x, wo[:gw], wo[gw:], g1, sh2, sc2, ln1_g[0], ln1_b[0],
                                                 w_router_t, router_bias[0], ROW_TILE, b0)
        h = [arr.reshape(nb * l, arr.shape[-1]) for arr in (h_a, h_b)]
        out = _routed_moe(h, x1.reshape(nb * l, d), ek, rk, wk, cnt, w_sg, w_sd, w_gate_up[0], w_down[0],
                          g2, ln2_g[0], ln2_b[0], b0, l, out)
    return out.reshape(b, l, d)
```

```python
import functools

import jax
import jax.numpy as jnp
from jax import lax
from jax.experimental import pallas as pl
from jax.experimental.pallas import tpu as pltpu
from jax.experimental.pallas import tpu_sc as plsc

F32 = jnp.float32
BF16 = jnp.bfloat16

HEAD_DIM = 128
GDN_HEADS = 4
RET_HEADS = 4
CONV_W = 5
HALO = 16
GDN_CHUNK = 64
INV_BASE = 16
RET_CHUNK = 256
GRID_W = 64
ROPE_THETA = 10000.0
N_EXPERTS = 64
TOP_K = 8
N_GROUPS = 8
TOPK_GROUPS = 4
ROUTED_SCALE = 2.5
SC_WINDOW = 128
PACK_GROUPS = 2
ROW_TILE = 512
GDN_HEADS_PER_STEP = 2
GDN_SLOTS_PER_GROUP = 6
BATCH_PARTS = 2
ROW_STREAM_BUFFERS = 3
EXPERT_TILES_PER_STEP = 2
EXPERT_SUBTILES = 4
EXPERT_TILE = 512
LN_EPS = 1e-6
DEPTH = 1
DEEPNORM_ALPHA = (2 * DEPTH) ** 0.25

VMEM_LIMIT = 56 * 1024 * 1024
HIGHEST = lax.Precision.HIGHEST
NT_DIMS = (((1,), (1,)), ((), ()))


def _dot(a, b, precision=None):
    return jnp.dot(a, b, preferred_element_type=F32, precision=precision)


def _dot_nt(a, b, precision=None):
    return lax.dot_general(a, b, NT_DIMS, preferred_element_type=F32, precision=precision)


def _silu(x):
    return x * (1.0 / (1.0 + jnp.exp(-x)))


def _sigmoid(x):
    return 1.0 / (1.0 + jnp.exp(-x))


def _softplus(x):
    return jnp.maximum(x, 0.0) + jnp.log(1.0 + jnp.exp(-jnp.abs(x)))


def _ln_rows(x):
    mu = jnp.mean(x, axis=-1, keepdims=True)
    xc = x - mu
    var = jnp.mean(xc * xc, axis=-1, keepdims=True)
    return xc * lax.rsqrt(var + LN_EPS)


def _params(sem):
    return pltpu.CompilerParams(dimension_semantics=sem, vmem_limit_bytes=VMEM_LIMIT)


def _mod_kernel(c_ref, w_ref, b_ref, o_ref):
    o_ref[...] = _dot(_silu(c_ref[...]), w_ref[...], HIGHEST) + b_ref[...]


def _modulation(cc, w_mod, b_mod):
    rows, d = cc.shape
    n = w_mod.shape[1]
    tn = 1024
    return pl.pallas_call(
        _mod_kernel,
        grid=(n // tn,),
        in_specs=[pl.BlockSpec((rows, d), lambda j: (0, 0)),
                  pl.BlockSpec((d, tn), lambda j: (0, j)),
                  pl.BlockSpec((1, tn), lambda j: (0, j))],
        out_specs=pl.BlockSpec((rows, tn), lambda j: (0, j)),
        out_shape=jax.ShapeDtypeStruct((rows, n), F32),
        compiler_params=_params(("arbitrary",)),
        name="modulation",
    )(cc, w_mod, b_mod.reshape(1, n))


def _inproj_kernel(x_ref, xp_ref, xn_ref, sh_ref, sc_ref, wm_ref, wg_ref, cw_ref, qkv_ref, z_ref, ret_ref, g_ref,
                   p_ref):
    j, nl = pl.program_id(1), pl.num_programs(1)
    tm = x_ref.shape[1]
    modulate = lambda x: _ln_rows(x) * (1.0 + sc_ref[0]) + sh_ref[0]
    h = modulate(x_ref[0])
    hb = h.astype(BF16)
    h_prev = jnp.where(j > 0, modulate(xp_ref[0]), 0.0).astype(BF16)
    h_next = jnp.where(j < nl - 1, modulate(xn_ref[0]), 0.0).astype(BF16)
    h_ext = jnp.concatenate([h_prev, hb, h_next], axis=0)
    half = CONV_W // 2
    gw = GDN_HEADS * HEAD_DIM
    for part in range(3):
        p_ref[part] = _dot(h_ext, wm_ref[:, part * gw:(part + 1) * gw])

    def conv_part(part):
        c0 = part * gw
        p = p_ref[part]
        acc = p[HALO - half:HALO - half + tm, :] * cw_ref[0:1, c0:c0 + gw]
        for tap in range(1, CONV_W):
            acc = acc + p[HALO - half + tap:HALO - half + tap + tm, :] * cw_ref[tap:tap + 1, c0:c0 + gw]
        y = _silu(acc)
        if part < 2:
            scale = HEAD_DIM ** -0.5 if part == 0 else 1.0
            blocks = [y[:, hd * HEAD_DIM:(hd + 1) * HEAD_DIM] for hd in range(GDN_HEADS)]
            blocks = [blk * (lax.rsqrt(jnp.sum(blk * blk, axis=-1, keepdims=True) + LN_EPS) * scale) for blk in blocks]
            y = jnp.concatenate(blocks, axis=1)
        qkv_ref[0, :, c0:c0 + gw] = y.astype(BF16)

    rest = [(ref, n0) for ref in (z_ref, ret_ref) for n0 in range(0, ref.shape[-1], 512)]
    col = 3 * gw
    for idx, (ref, n0) in enumerate(rest):
        if idx < 3:
            conv_part(idx)
        ref[0, :, n0:n0 + 512] = _dot(hb, wm_ref[:, col:col + 512]).astype(BF16)
        col += 512
    h_lo = (h - hb.astype(F32)).astype(BF16)
    wg = wg_ref[...]
    wg_hi = wg.astype(BF16)
    wg_lo = (wg - wg_hi.astype(F32)).astype(BF16)
    g_ref[0] = (_dot(h_lo, wg_hi) + _dot(hb, wg_lo)) + _dot(hb, wg_hi)


def _inproj(x, sh, sc, w_main, w_gate, conv_w, tm, b0, b):
    _, l, d = x.shape
    widths = (3 * GDN_HEADS * HEAD_DIM, GDN_HEADS * HEAD_DIM, 4 * RET_HEADS * HEAD_DIM)
    n_gate = w_gate.shape[1]
    per_tile, last = tm // HALO, l // HALO - 1
    row = lambda i, j: (i, j, 0)
    const = lambda i, j: (0, 0)
    return pl.pallas_call(
        _inproj_kernel,
        grid=(b, l // tm),
        in_specs=[pl.BlockSpec((1, tm, d), lambda i, j: (i + b0, j, 0)),
                  pl.BlockSpec((1, HALO, d), lambda i, j: (i + b0, jnp.maximum(j * per_tile - 1, 0), 0)),
                  pl.BlockSpec((1, HALO, d), lambda i, j: (i + b0, jnp.minimum((j + 1) * per_tile, last), 0)),
                  pl.BlockSpec((1, 1, d), lambda i, j: (i + b0, 0, 0)),
                  pl.BlockSpec((1, 1, d), lambda i, j: (i + b0, 0, 0)),
                  pl.BlockSpec(w_main.shape, const),
                  pl.BlockSpec(w_gate.shape, const),
                  pl.BlockSpec(conv_w.shape, const)],
        out_specs=[pl.BlockSpec((1, tm, w), row) for w in widths]
                  + [pl.BlockSpec((1, tm, n_gate), row)],
        out_shape=[jax.ShapeDtypeStruct((b, l, w), BF16) for w in widths]
                  + [jax.ShapeDtypeStruct((b, l, n_gate), F32)],
        scratch_shapes=[pltpu.VMEM((3, tm + 2 * HALO, GDN_HEADS * HEAD_DIM), F32)],
        compiler_params=_params(("arbitrary", "arbitrary")),
        name="inproj",
    )(x, x, x, sh, sc, w_main, w_gate, conv_w)


def _bdot(a, b):
    return _dot(a.astype(BF16), b.astype(BF16))


def _pack_bf16_pair(x):
    n = x.shape[1] // 2
    bits = lambda v: pltpu.bitcast(v.astype(BF16).astype(F32), jnp.uint32)
    word = lax.shift_right_logical(bits(x[:, :n]), jnp.uint32(16)) | (bits(x[:, n:]) & jnp.uint32(0xFFFF0000))
    return pltpu.bitcast(word, jnp.int32)


def _unpack_bf16_pair(w):
    u = pltpu.bitcast(w, jnp.uint32)
    lo = pltpu.bitcast(lax.shift_left(u, jnp.uint32(16)), F32)
    hi = pltpu.bitcast(u & jnp.uint32(0xFFFF0000), F32)
    return lo.astype(BF16), hi.astype(BF16)


def _split3(x):
    hi = x.astype(BF16)
    r1 = x - hi.astype(F32)
    mid = r1.astype(BF16)
    lo = (r1 - mid.astype(F32)).astype(BF16)
    return hi, mid, lo


def _dot_split_lhs(x, mask01):
    m = mask01.astype(BF16)
    hi, mid, lo = _split3(x)
    return (_dot(lo, m) + _dot(mid, m)) + _dot(hi, m)


def _each(fn, *lists):
    return [fn(*args) for args in zip(*lists)]


def _inv_unit_triangular(mats, base=INV_BASE):
    n = mats[0].shape[0]
    i = lax.broadcasted_iota(jnp.int32, (n, n), 0)
    j = lax.broadcasted_iota(jnp.int32, (n, n), 1)
    eye = (i == j).astype(F32)
    shift = base.bit_length() - 1
    inside = (i >> shift) == (j >> shift)
    xs = _each(lambda a: jnp.where(inside, a, 0.0), mats)
    ts = _each(lambda d: eye - d, xs)
    for _ in range(shift - 1):
        xs = _each(lambda x: _bdot(x, x), xs)
        yield
        ts = _each(lambda t, x: t + _bdot(t, x), ts, xs)
        yield
    size = base
    while size < GDN_CHUNK:
        shift += 1
        size *= 2
        wider = (i >> shift) == (j >> shift)
        off = wider & jnp.logical_not(inside)
        ots = _each(lambda a, t: _bdot(jnp.where(off, a, 0.0), t), mats, ts)
        yield
        ts = _each(lambda t, ot: t - _bdot(t, ot), ts, ots)
        yield
        inside = wider
    return ts


def _gdn_kernel(alog_ref, dtb_ref,
                q_ref, k_ref, v_ref, qc_ref, kc_ref, vc_ref,
                gt_ref, z_ref, ng_ref,
                o_ref,
                qs_ref, ks_ref, vs_ref, gts_ref, gth_ref,
                wqf_ref, wqb_ref, u0_ref, intra_ref, ket_ref, gef_ref, geb_ref,
                of_ref, ob_ref, *, hb, l_lat, l_ctx, slots_per_group):
    c = GDN_CHUNK
    c2 = 2 * c
    lt = l_lat + l_ctx
    n_ctx = l_ctx // c
    nt = lt // c
    hgrp = pl.program_id(1)

    i2 = lax.broadcasted_iota(jnp.int32, (c2, c2), 0)
    j2 = lax.broadcasted_iota(jnp.int32, (c2, c2), 1)
    same_blk = (i2 & c) == (j2 & c)
    sgn_i = jnp.where(i2 < c, 1, -1)
    sgn_j = jnp.where(j2 < c, 1, -1)
    incl = same_blk & ((j2 - i2) * sgn_i <= 0)
    strict = same_blk & ((j2 - i2) * sgn_i < 0)
    tri_row = (same_blk & ((i2 - j2) * sgn_j <= 0)).astype(F32)
    eye_m = i2 == j2
    eye_b = eye_m.astype(BF16)
    lane_lt_c = j2 < c

    for hh in range(hb):
        head = hgrp * hb + hh
        lane0 = hh * HEAD_DIM
        for (src_c, src_l, dst) in ((qc_ref, q_ref, qs_ref), (kc_ref, k_ref, ks_ref), (vc_ref, v_ref, vs_ref)):
            dst[hh, 0:l_ctx, :] = src_c[0, :, lane0:lane0 + HEAD_DIM]
            dst[hh, l_ctx:lt, :] = src_l[0, :, lane0:lane0 + HEAD_DIM]

        a_f, a_b = alog_ref[head], alog_ref[GDN_HEADS + head]
        d_f, d_b = dtb_ref[head], dtb_ref[GDN_HEADS + head]
        compr = lax.broadcasted_iota(jnp.int32, (nt * 8, 1), 0) & 7
        neg_ar = -jnp.exp(jnp.where(compr == 2, a_f, a_b))
        dtbr = jnp.where(compr == 2, d_f, d_b)
        grow_raw = gt_ref[0, hh]
        grow = jnp.where(compr < 2, _sigmoid(grow_raw), neg_ar * _softplus(grow_raw + dtbr))
        gts_ref[hh] = jnp.zeros(gts_ref.shape[1:], F32)
        gts_ref[hh, :, 0:c] = grow
        gth_ref[hh] = jnp.zeros(gth_ref.shape[1:], F32)
        gth_ref[hh, :, c:c2] = grow

    spg = slots_per_group
    heads = list(range(hb))

    def bwd_chunk(i):
        return jnp.where(i < n_ctx, n_ctx - 1 - i, nt + n_ctx - 1 - i)

    def slot_load(i, hh):
        cbk = bwd_chunk(i)
        rf = pl.multiple_of(i * c, c)
        rb = pl.multiple_of(cbk * c, c)
        two = lambda ref: jnp.concatenate([ref[hh, pl.ds(rf, c), :], ref[hh, pl.ds(rb, c), :]], axis=0).astype(F32)
        g_r = (gts_ref[hh, pl.ds(pl.multiple_of(i * 8, 8), 8), :]
               + gth_ref[hh, pl.ds(pl.multiple_of(cbk * 8, 8), 8), :])
        return two(ks_ref), two(vs_ref), two(qs_ref), g_r

    def slot_stages(loaded):
        k2, v2, q2, g_r = [list(col) for col in zip(*loaded)]
        rcs = _each(lambda g: _dot_split_lhs(g, tri_row), g_r)
        yield
        k2b = _each(lambda x: x.astype(BF16), k2)
        kk = _each(_dot_nt, k2b, k2b)
        qk = _each(lambda q, kb: _dot_nt(q.astype(BF16), kb), q2, k2b)
        yield
        by_dir = lambda g, k: jnp.where(lane_lt_c[0:1, :], g[k:k + 1, :], g[k + 1:k + 2, :])
        lane_sum = lambda mask, row: jnp.sum(jnp.where(mask, row, 0.0), axis=1, keepdims=True)
        gcc = _each(lambda g: lane_sum(incl, by_dir(g, 2)), g_r)
        gcr = _each(lambda r: by_dir(r, 2), rcs)
        beta = _each(lambda g: lane_sum(eye_m, by_dir(g, 0)), g_r)
        gend = _each(lambda s: jnp.concatenate([jnp.broadcast_to(s[c - 1:c, :], (c, 1)),
                                                jnp.broadcast_to(s[c:c + 1, :], (c, 1))], axis=0), gcc)
        decay = _each(lambda gc, gr: jnp.exp(jnp.where(incl, gc - gr, -jnp.inf)), gcc, gcr)
        a_mat = _each(lambda m, d, b: jnp.where(strict, m * d, 0.0) * b, kk, decay, beta)
        t_f32 = yield from _inv_unit_triangular(a_mat)
        t_mat = _each(lambda t: t.astype(BF16), t_f32)
        egc = _each(jnp.exp, gcc)
        u0 = _each(lambda t, v, b: _dot(t, (v * b).astype(BF16)), t_mat, v2, beta)
        w = _each(lambda t, k, b, e: _dot(t, (k * (b * e)).astype(BF16)), t_mat, k2, beta, egc)
        ket = _each(lambda k, ge_, gc: _dot_nt(eye_b, (k * jnp.exp(ge_ - gc)).astype(BF16)), k2, gend, gcc)
        yield
        qd = _each(lambda q, e: (q * e).astype(BF16), q2, egc)
        intra = _each(lambda m, d: (m * d).astype(BF16), qk, decay)
        ge = _each(jnp.exp, gend)
        return [(jnp.concatenate([w_[0:c].astype(BF16), qd_[0:c]], axis=0),
                 jnp.concatenate([w_[c:c2].astype(BF16), qd_[c:c2]], axis=0), u0_, in_,
                 jnp.concatenate([jnp.where(lane_lt_c, kt, 0.0), jnp.where(lane_lt_c, 0.0, kt)], axis=0).astype(BF16),
                 jnp.broadcast_to(g_[0:1, :], (8, HEAD_DIM)), jnp.broadcast_to(g_[c:c + 1, :], (8, HEAD_DIM)))
                for w_, qd_, u0_, in_, kt, g_ in zip(w, qd, u0, intra, ket, ge)]

    def transform_group(g):
        par = lax.rem(g, 2)
        jobs = [(s, hh) for s in range(spg) for hh in heads]
        results = yield from slot_stages([slot_load(g * spg + s, hh) for s, hh in jobs])
        for (s, hh), (wq_f, wq_b, u0, intra, ket, ge_f, ge_b) in zip(jobs, results):
            wqf_ref[par, hh, s] = wq_f
            wqb_ref[par, hh, s] = wq_b
            u0_ref[par, hh, s] = u0
            intra_ref[par, hh, s] = intra
            ket_ref[par, hh, s] = ket
            gef_ref[par, hh, s] = ge_f
            geb_ref[par, hh, s] = ge_b

    def recur_group(g, states):
        par = lax.rem(g, 2)
        sf, sb = list(states[0::2]), list(states[1::2])
        for s in range(spg):
            i = g * spg + s
            rf = pl.multiple_of(i * c, c)
            rb = pl.multiple_of(bwd_chunk(i) * c, c)
            r_f = _each(lambda hh, st: _dot(wqf_ref[par, hh, s], st.astype(BF16)), heads, sf)
            r_b = _each(lambda hh, st: _dot(wqb_ref[par, hh, s], st.astype(BF16)), heads, sb)
            u0 = _each(lambda hh: u0_ref[par, hh, s], heads)
            yield
            u2 = _each(lambda u, f, b_: jnp.concatenate([u[0:c] - f[0:c], u[c:c2] - b_[0:c]], axis=0).astype(BF16),
                       u0, r_f, r_b)
            ket = _each(lambda hh: ket_ref[par, hh, s], heads)
            df = _each(lambda kt, u: _dot(kt[0:HEAD_DIM], u), ket, u2)
            db = _each(lambda kt, u: _dot(kt[HEAD_DIM:2 * HEAD_DIM], u), ket, u2)
            iu = _each(lambda hh, u: _dot(intra_ref[par, hh, s], u), heads, u2)
            sf = _each(lambda hh, st, d: gef_ref[par, hh, s][0:1, :] * st + d, heads, sf, df)
            sb = _each(lambda hh, st, d: geb_ref[par, hh, s][0:1, :] * st + d, heads, sb, db)
            for hh in heads:
                of_ref[hh, pl.ds(rf, c), :] = r_f[hh][c:c2] + iu[hh][0:c]
                ob_ref[hh, pl.ds(rb, c), :] = r_b[hh][c:c2] + iu[hh][c:c2]
            yield
        return tuple(x for pair in zip(sf, sb) for x in pair)

    def drive(transform, recur, transforms_per_recur=1):
        states = None
        while transform is not None or recur is not None:
            if transform is not None:
                for _ in range(transforms_per_recur):
                    try:
                        next(transform)
                    except StopIteration:
                        transform = None
                        break
            if recur is not None:
                try:
                    next(recur)
                except StopIteration as stop:
                    states, recur = stop.value, None
        return states

    n_groups = nt // spg
    zero_state = tuple(jnp.zeros((HEAD_DIM, HEAD_DIM), F32) for _ in range(2 * hb))
    drive(transform_group(0), None)
    states = lax.fori_loop(0, n_groups - 1,
                           lambda g, st: drive(transform_group(g + 1), recur_group(g, st)), zero_state)

    ng = ng_ref[...]
    blk = 256

    def finish_block(hh, b):
        r, lane0 = b * blk, hh * HEAD_DIM
        o = of_ref[hh, l_ctx + r:l_ctx + r + blk, :] + ob_ref[hh, l_ctx + r:l_ctx + r + blk, :]
        y = o * lax.rsqrt(jnp.mean(o * o, axis=-1, keepdims=True) + LN_EPS) * ng
        z = z_ref[0, r:r + blk, lane0:lane0 + HEAD_DIM].astype(F32)
        o_ref[0, r:r + blk, lane0:lane0 + HEAD_DIM] = (y * _silu(z)).astype(BF16)

    def finish_blocks(jobs):
        for hh, b in jobs:
            finish_block(hh, b)
            yield

    first = -(-(spg * c) // blk)
    last = (nt - spg - n_ctx) * c // blk
    early = [(hh, b) for b in range(first, last) for hh in heads]
    late = [(hh, b) for b in range(l_lat // blk) if not first <= b < last for hh in heads]
    drive(finish_blocks(early), recur_group(n_groups - 1, states))
    drive(finish_blocks(late), None)


def _gdn_mixer(qkv_lat, qkv_ctx, z_lat, gate_rows, a_log, dt_bias, norm_g, hb):
    b, l_lat, _ = qkv_lat.shape
    l_ctx = qkv_ctx.shape[1]
    lt = l_lat + l_ctx
    nt = lt // GDN_CHUNK
    hw = hb * HEAD_DIM
    ngrp = GDN_HEADS // hb
    c2 = 2 * GDN_CHUNK
    spg = GDN_SLOTS_PER_GROUP
    assert nt % spg == 0 and nt // spg >= 2 and GDN_HEADS % hb == 0

    def seq_spec(length, part):
        return pl.BlockSpec((1, length, hw), lambda i, j, *_: (i, 0, part * ngrp + j))

    grid_spec = pltpu.PrefetchScalarGridSpec(
        num_scalar_prefetch=2,
        grid=(b, ngrp),
        in_specs=[seq_spec(l_lat, 0), seq_spec(l_lat, 1), seq_spec(l_lat, 2),
                  seq_spec(l_ctx, 0), seq_spec(l_ctx, 1), seq_spec(l_ctx, 2),
                  pl.BlockSpec((1, hb, nt * 8, GDN_CHUNK), lambda i, j, *_: (i, j, 0, 0)),
                  pl.BlockSpec((1, l_lat, hw), lambda i, j, *_: (i, 0, j)),
                  pl.BlockSpec((1, HEAD_DIM), lambda i, j, *_: (0, 0))],
        out_specs=pl.BlockSpec((1, l_lat, hw), lambda i, j, *_: (i, 0, j)),
        scratch_shapes=[
            pltpu.VMEM((hb, lt, HEAD_DIM), BF16),
            pltpu.VMEM((hb, lt, HEAD_DIM), BF16),
            pltpu.VMEM((hb, lt, HEAD_DIM), BF16),
            pltpu.VMEM((hb, nt * 8, c2), F32),
            pltpu.VMEM((hb, nt * 8, c2), F32),
            pltpu.VMEM((2, hb, spg, c2, HEAD_DIM), BF16),
            pltpu.VMEM((2, hb, spg, c2, HEAD_DIM), BF16),
            pltpu.VMEM((2, hb, spg, c2, HEAD_DIM), F32),
            pltpu.VMEM((2, hb, spg, c2, c2), BF16),
            pltpu.VMEM((2, hb, spg, 2 * HEAD_DIM, c2), BF16),
            pltpu.VMEM((2, hb, spg, 8, HEAD_DIM), F32),
            pltpu.VMEM((2, hb, spg, 8, HEAD_DIM), F32),
            pltpu.VMEM((hb, lt, HEAD_DIM), F32),
            pltpu.VMEM((hb, lt, HEAD_DIM), F32),
        ])
    kern = functools.partial(_gdn_kernel, hb=hb, l_lat=l_lat, l_ctx=l_ctx, slots_per_group=spg)
    return pl.pallas_call(
        kern,
        grid_spec=grid_spec,
        out_shape=jax.ShapeDtypeStruct((b, l_lat, GDN_HEADS * HEAD_DIM), BF16),
        compiler_params=_params(("arbitrary", "arbitrary")),
        name="gdn_mixer",
    )(a_log, dt_bias, qkv_lat, qkv_lat, qkv_lat, qkv_ctx, qkv_ctx, qkv_ctx,
      gate_rows, z_lat, norm_g.reshape(1, HEAD_DIM))


def _ret_kernel(lg_ref, q_ref, k_ref, v_ref, gate_ref, qc_ref, kc_ref, vc_ref, cos_ref, sin_ref, ng_ref,
                o_ref, q_s, k_s, rf_s, rb_s, *, l_lat, l_ctx, hb):
    c = RET_CHUNK
    n_lat = l_lat // c
    n_ctx = l_ctx // c
    heads = list(range(hb))
    lanes = [slice(hh * HEAD_DIM, (hh + 1) * HEAD_DIM) for hh in heads]
    pos_c = lax.broadcasted_iota(jnp.int32, (c, 1), 0).astype(F32)
    ii = lax.broadcasted_iota(jnp.int32, (c, c), 0)
    jj = lax.broadcasted_iota(jnp.int32, (c, c), 1)
    dif = (ii - jj).astype(F32)
    e2 = lax.broadcasted_iota(jnp.int32, (HEAD_DIM, HEAD_DIM), 0)
    f2 = lax.broadcasted_iota(jnp.int32, (HEAD_DIM, HEAD_DIM), 1)
    eye_b = (e2 == f2).astype(BF16)
    kscale = HEAD_DIM ** -0.5
    lg_f = [lg_ref[pl.program_id(1) * hb + hh] for hh in heads]
    lg_b = [lg_ref[RET_HEADS + pl.program_id(1) * hb + hh] for hh in heads]
    dmat = _each(lambda f, b_: (jnp.exp(jnp.where(ii >= jj, f * dif, -jnp.inf))
                                + jnp.exp(jnp.where(jj >= ii, -b_ * dif, -jnp.inf))), lg_f, lg_b)
    kdec_f = _each(lambda f: jnp.exp(f * (c - 1 - pos_c)), lg_f)
    kdec_b = _each(lambda b_: jnp.exp(b_ * pos_c), lg_b)
    qdec_f = _each(lambda f: jnp.exp(f * (pos_c + 1.0)), lg_f)
    qdec_b = _each(lambda b_: jnp.exp(b_ * (c - pos_c)), lg_b)
    cd_f = _each(lambda f: jnp.exp(jnp.full((1, HEAD_DIM), f * c, F32)), lg_f)
    cd_b = _each(lambda b_: jnp.exp(jnp.full((1, HEAD_DIM), b_ * c, F32)), lg_b)

    def transposed(kd):
        return _dot_nt(eye_b, kd.astype(BF16)).astype(BF16)

    r_f = [jnp.zeros((HEAD_DIM, HEAD_DIM), F32) for _ in heads]
    r_b = [jnp.zeros((HEAD_DIM, HEAD_DIM), F32) for _ in heads]
    for n in range(n_ctx):
        m = n_ctx - 1 - n
        kf = _each(lambda ln, d: transposed(kc_ref[0, n * c:(n + 1) * c, ln].astype(F32) * kscale * d), lanes, kdec_f)
        kb = _each(lambda ln, d: transposed(kc_ref[0, m * c:(m + 1) * c, ln].astype(F32) * kscale * d), lanes, kdec_b)
        pf = _each(lambda kt, ln: _dot(kt, vc_ref[0, n * c:(n + 1) * c, ln]), kf, lanes)
        pb = _each(lambda kt, ln: _dot(kt, vc_ref[0, m * c:(m + 1) * c, ln]), kb, lanes)
        r_f = _each(lambda d, r, p: d * r + p, cd_f, r_f, pf)
        r_b = _each(lambda d, r, p: d * r + p, cd_b, r_b, pb)

    def rope_body(n, carry):
        r = pl.multiple_of(n * c, c)
        cs = cos_ref[pl.ds(r, c), :]
        sn = sin_ref[pl.ds(r, c), :]
        for hh, ln in zip(heads, lanes):
            q = q_ref[0, pl.ds(r, c), ln].astype(F32)
            k = k_ref[0, pl.ds(r, c), ln].astype(F32) * kscale
            q_s[hh, pl.ds(r, c), :] = q * cs + pltpu.roll(q, HEAD_DIM // 2, 1) * sn
            k_s[hh, pl.ds(r, c), :] = k * cs + pltpu.roll(k, HEAD_DIM // 2, 1) * sn
        return carry

    lax.fori_loop(0, n_lat, rope_body, 0)

    chunks = list(range(n_lat))
    rows = [slice(n * c, (n + 1) * c) for n in chunks]
    jobs = [(hh, n) for hh in heads for n in chunks]
    k_c = [k_s[hh, rows[n], :] for hh, n in jobs]
    kft = [transposed(k * kdec_f[hh]) for k, (hh, n) in zip(k_c, jobs)]
    kbt = [transposed(k * kdec_b[hh]) for k, (hh, n) in zip(k_c, jobs)]
    kvf = [_dot(kt, v_ref[0, rows[n], lanes[hh]]) for kt, (hh, n) in zip(kft, jobs)]
    kvb = [_dot(kt, v_ref[0, rows[n], lanes[hh]]) for kt, (hh, n) in zip(kbt, jobs)]
    for hh in heads:
        r = r_f[hh]
        for n in chunks:
            rf_s[hh, n] = r
            r = cd_f[hh] * r + kvf[hh * n_lat + n]
        r = r_b[hh]
        for n in reversed(chunks):
            rb_s[hh, n] = r
            r = cd_b[hh] * r + kvb[hh * n_lat + n]

    group = 4
    for g0 in range(0, len(jobs), group):
        part = jobs[g0:g0 + group]
        q_c = [q_s[hh, rows[n], :] for hh, n in part]
        s = [_dot_nt(q.astype(BF16), k_s[hh, rows[n], :].astype(BF16)) for q, (hh, n) in zip(q_c, part)]
        att = [(s_ * dmat[hh]).astype(BF16) for s_, (hh, n) in zip(s, part)]
        o = [_dot(a, v_ref[0, rows[n], lanes[hh]]) for a, (hh, n) in zip(att, part)]
        qd = [jnp.concatenate([(q * qdec_f[hh]).astype(BF16), (q * qdec_b[hh]).astype(BF16)], axis=1)
              for q, (hh, n) in zip(q_c, part)]
        st = [jnp.concatenate([rf_s[hh, n], rb_s[hh, n]], axis=0).astype(BF16) for hh, n in part]
        o = _each(lambda o_, qd_, st_: o_ + _dot(qd_, st_), o, qd, st)
        for o_, (hh, n) in zip(o, part):
            y = _ln_rows(o_) * ng_ref[:, lanes[hh]]
            g = gate_ref[0, rows[n], lanes[hh]].astype(F32)
            o_ref[0, rows[n], lanes[hh]] = (y * _silu(g)).astype(BF16)


def _ret_mixer(ret_lat, ret_ctx, log_gamma, norm_g, cos_t, sin_t):
    b, l_lat, _ = ret_lat.shape
    l_ctx = ret_ctx.shape[1]
    n_lat = l_lat // RET_CHUNK
    hb = RET_HEADS
    hw = hb * HEAD_DIM
    ngrp = RET_HEADS // hb

    def seq_spec(length, part):
        return pl.BlockSpec((1, length, hw), lambda i, j, *_: (i, 0, part * ngrp + j))

    grid_spec = pltpu.PrefetchScalarGridSpec(
        num_scalar_prefetch=1,
        grid=(b, ngrp),
        in_specs=[seq_spec(l_lat, 0), seq_spec(l_lat, 1), seq_spec(l_lat, 2), seq_spec(l_lat, 3),
                  seq_spec(l_ctx, 0), seq_spec(l_ctx, 1), seq_spec(l_ctx, 2),
                  pl.BlockSpec((l_lat, HEAD_DIM), lambda i, j, *_: (0, 0)),
                  pl.BlockSpec((l_lat, HEAD_DIM), lambda i, j, *_: (0, 0)),
                  pl.BlockSpec((1, hw), lambda i, j, *_: (0, j))],
        out_specs=pl.BlockSpec((1, l_lat, hw), lambda i, j, *_: (i, 0, j)),
        scratch_shapes=[pltpu.VMEM((hb, l_lat, HEAD_DIM), F32),
                        pltpu.VMEM((hb, l_lat, HEAD_DIM), F32),
                        pltpu.VMEM((hb, n_lat, HEAD_DIM, HEAD_DIM), F32),
                        pltpu.VMEM((hb, n_lat, HEAD_DIM, HEAD_DIM), F32)])
    kern = functools.partial(_ret_kernel, l_lat=l_lat, l_ctx=l_ctx, hb=hb)
    return pl.pallas_call(
        kern,
        grid_spec=grid_spec,
        out_shape=jax.ShapeDtypeStruct((b, l_lat, RET_HEADS * HEAD_DIM), BF16),
        compiler_params=_params(("arbitrary", "arbitrary")),
        name="ret_mixer",
    )(log_gamma.reshape(-1), ret_lat, ret_lat, ret_lat, ret_lat, ret_ctx, ret_ctx, ret_ctx,
      cos_t, sin_t, norm_g.reshape(1, -1))


def _rope_tables(l_lat):
    rows = l_lat // GRID_W
    row = jnp.repeat(jnp.arange(rows, dtype=F32), GRID_W)
    col = jnp.tile(jnp.arange(GRID_W, dtype=F32), rows)
    quarter = HEAD_DIM // 4
    inv = ROPE_THETA ** (-jnp.arange(quarter, dtype=F32) / quarter)
    ang = jnp.concatenate([row[:, None] * inv, col[:, None] * inv], -1)
    cos, sin = jnp.cos(ang), jnp.sin(ang)
    return jnp.concatenate([cos, cos], -1), jnp.concatenate([-sin, sin], -1)


def _top_rows(vals, k):
    n = vals.shape[0]
    idx = lax.broadcasted_iota(jnp.int32, vals.shape, 0)
    taken = jnp.zeros(vals.shape, jnp.int32)
    firsts = []
    for _ in range(k):
        live = jnp.where(taken == 0, vals, -jnp.inf)
        top = jnp.max(live, axis=0, keepdims=True)
        cand = jnp.where((live == top) & (taken == 0), idx, n)
        first = jnp.min(cand, axis=0, keepdims=True)
        taken = taken + (idx == first).astype(jnp.int32)
        firsts.append(first)
    return firsts, taken


def _outproj_kernel(a_ref, r_ref, x_ref, wa_ref, wr_ref, g1_ref, sh2_ref, sc2_ref, lg_ref, lb_ref,
                    wrt_ref, rb_ref, x1_ref, ha_ref, hb_ref, ek_ref, rk_ref, wk_ref, cnt_ref, carry_ref):
    y = _dot(a_ref[0], wa_ref[...]) + _dot(r_ref[0], wr_ref[...])
    x1 = _ln_rows(DEEPNORM_ALPHA * x_ref[0] + g1_ref[0] * y) * lg_ref[...] + lb_ref[...]
    x1_ref[0] = x1
    h = _ln_rows(x1) * (1.0 + sc2_ref[0]) + sh2_ref[0]
    for ref, part in zip((ha_ref, hb_ref), _pack_rows(h)):
        ref[0] = part
    s = _sigmoid(_dot_nt(wrt_ref[...], h, HIGHEST))
    sb = s + rb_ref[...]
    tm = s.shape[1]
    per = N_EXPERTS // N_GROUPS
    sub = lax.broadcasted_iota(jnp.int32, (per, tm), 0)
    gs_rows = []
    for g in range(N_GROUPS):
        blk = sb[g * per:(g + 1) * per, :]
        m1 = jnp.max(blk, axis=0, keepdims=True)
        first = jnp.min(jnp.where(blk == m1, sub, per), axis=0, keepdims=True)
        m2 = jnp.max(jnp.where(sub == first, -jnp.inf, blk), axis=0, keepdims=True)
        gs_rows.append(m1 + m2)
    gscore = jnp.concatenate(gs_rows, axis=0)
    _, gtaken = _top_rows(gscore, TOPK_GROUPS)
    emask = jnp.concatenate([jnp.broadcast_to(gtaken[g:g + 1, :], (per, tm)) for g in range(N_GROUPS)], axis=0)
    masked = jnp.where(emask > 0, sb, -jnp.inf)
    firsts, taken = _top_rows(masked, TOP_K)
    first_step = (pl.program_id(0) == 0) & (pl.program_id(1) == 0)

    @pl.when(first_step)
    def _():
        carry_ref[...] = jnp.zeros(carry_ref.shape, F32)

    sel_f = taken.astype(F32)
    ti = lax.broadcasted_iota(jnp.int32, (tm, tm), 0)
    tj = lax.broadcasted_iota(jnp.int32, (tm, tm), 1)
    rank = _dot(sel_f.astype(BF16), (ti < tj).astype(BF16)) + carry_ref[:, 0:1]
    carry = carry_ref[...] + jnp.sum(sel_f, axis=1, keepdims=True)
    carry_ref[...] = carry
    cnt_ref[...] = carry
    eidx = lax.broadcasted_iota(jnp.int32, (N_EXPERTS, tm), 0)
    picked = []
    for k in range(TOP_K):
        hit = eidx == firsts[k]
        pick = lambda v: jnp.sum(jnp.where(hit, v, 0.0), axis=0, keepdims=True)
        ek_ref[k:k + 1, :] = firsts[k]
        rk_ref[k:k + 1, :] = pick(rank).astype(jnp.int32)
        picked.append(pick(s))
    total = picked[0]
    for k in range(1, TOP_K):
        total = total + picked[k]
    for k in range(TOP_K):
        wk_ref[k:k + 1, :] = picked[k] / total * ROUTED_SCALE


def _outproj(a_lat, r_lat, x, w_a, w_r, g1, sh2, sc2, ln_g, ln_b, w_router_t, router_bias, tm, b0):
    b, l, half = a_lat.shape
    d = x.shape[-1]
    nl = l // tm
    row = lambda i, j: (i, j, 0)
    per_b = lambda i, j: (i + b0, 0, 0)
    const = lambda i, j: (0, 0)
    return pl.pallas_call(
        _outproj_kernel,
        grid=(b, nl),
        in_specs=[pl.BlockSpec((1, tm, half), row),
                  pl.BlockSpec((1, tm, half), row),
                  pl.BlockSpec((1, tm, d), lambda i, j: (i + b0, j, 0)),
                  pl.BlockSpec((half, d), const),
                  pl.BlockSpec((half, d), const),
                  pl.BlockSpec((1, 1, d), per_b),
                  pl.BlockSpec((1, 1, d), per_b),
                  pl.BlockSpec((1, 1, d), per_b),
                  pl.BlockSpec((1, d), const),
                  pl.BlockSpec((1, d), const),
                  pl.BlockSpec((N_EXPERTS, d), const),
                  pl.BlockSpec((N_EXPERTS, 1), const)],
        out_specs=[pl.BlockSpec((1, tm, d), row),
                   pl.BlockSpec((1, tm, d // (2 * PACK_GROUPS)), row),
                   pl.BlockSpec((1, tm, d // (2 * PACK_GROUPS)), row),
                   pl.BlockSpec((TOP_K, tm), lambda i, j: (0, i * nl + j)),
                   pl.BlockSpec((TOP_K, tm), lambda i, j: (0, i * nl + j)),
                   pl.BlockSpec((TOP_K, tm), lambda i, j: (0, i * nl + j)),
                   pl.BlockSpec((N_EXPERTS, HEAD_DIM), const)],
        out_shape=[jax.ShapeDtypeStruct((b, l, d), F32),
                   jax.ShapeDtypeStruct((b, l, d // (2 * PACK_GROUPS)), jnp.int32),
                   jax.ShapeDtypeStruct((b, l, d // (2 * PACK_GROUPS)), jnp.int32),
                   jax.ShapeDtypeStruct((TOP_K, b * l), jnp.int32),
                   jax.ShapeDtypeStruct((TOP_K, b * l), jnp.int32),
                   jax.ShapeDtypeStruct((TOP_K, b * l), F32),
                   jax.ShapeDtypeStruct((N_EXPERTS, HEAD_DIM), F32)],
        scratch_shapes=[pltpu.VMEM((N_EXPERTS, HEAD_DIM), F32)],
        compiler_params=_params(("arbitrary", "arbitrary")),
        name="outproj_router",
    )(a_lat, r_lat, x, w_a, w_r, g1, sh2, sc2, ln_g.reshape(1, d), ln_b.reshape(1, d),
      w_router_t, router_bias.reshape(N_EXPERTS, 1))


def _pack_rows(x):
    n = x.shape[1] // PACK_GROUPS
    return [_pack_bf16_pair(x[:, g * n:(g + 1) * n]) for g in range(PACK_GROUPS)]


def _unpack_rows(parts):
    cols = []
    for p in parts:
        cols += list(_unpack_bf16_pair(p))
    return cols


def _glu_ffn(cols, w_gate_up, w_down, ff):
    n = cols[0].shape[1]
    ab = _dot(cols[0], w_gate_up[0:n])
    for i in range(1, len(cols)):
        ab = ab + _dot(cols[i], w_gate_up[i * n:(i + 1) * n])
    act = (_silu(ab[:, :ff]) * ab[:, ff:]).astype(BF16)
    return _dot(act, w_down)


def _expert_kernel(te_ref, used_ref, *refs, ff):
    tps = EXPERT_TILES_PER_STEP
    xs_refs, w_refs = refs[:PACK_GROUPS], refs[PACK_GROUPS:PACK_GROUPS + 2 * tps]
    ys_refs = refs[PACK_GROUPS + 2 * tps:2 * PACK_GROUPS + 2 * tps]
    wbf_refs = refs[2 * PACK_GROUPS + 2 * tps:]
    i = pl.program_id(0)
    used = used_ref[0]

    def run(n_active):
        for t in range(n_active):
            tile = i * tps + t
            prev = jnp.maximum(tile - tps, 0)

            @pl.when((i == 0) | (te_ref[tile] != te_ref[prev]))
            def _(t=t):
                wbf_refs[2 * t][...] = w_refs[2 * t][0].astype(BF16)
                wbf_refs[2 * t + 1][...] = w_refs[2 * t + 1][0].astype(BF16)

        sub = EXPERT_TILE // EXPERT_SUBTILES
        blocks = [(t, slice(t * EXPERT_TILE + s * sub, t * EXPERT_TILE + (s + 1) * sub))
                  for t in range(n_active) for s in range(EXPERT_SUBTILES)]
        wgu = [wbf_refs[2 * t][...] for t in range(n_active)]
        wd = [wbf_refs[2 * t + 1][...] for t in range(n_active)]
        cols = [_unpack_rows([r[b, :] for r in xs_refs]) for t, b in blocks]
        n = cols[0][0].shape[1]
        ab = [_dot(c[0], wgu[t][0:n]) for c, (t, b) in zip(cols, blocks)]
        for j in range(1, 2 * PACK_GROUPS):
            ab = [acc + _dot(c[j], wgu[t][j * n:(j + 1) * n]) for acc, c, (t, b) in zip(ab, cols, blocks)]
        act = [(_silu(a[:, :ff]) * a[:, ff:]).astype(BF16) for a in ab]
        y = [_dot(a, wd[t]) for a, (t, b) in zip(act, blocks)]
        for (t, b), y_b in zip(blocks, y):
            for ref, part in zip(ys_refs, _pack_rows(y_b)):
                ref[b, :] = part

    for n_active in range(1, tps + 1):
        full = (i * tps + n_active <= used) if n_active == tps else (i * tps + n_active == used)
        pl.when(full)(functools.partial(run, n_active))


def _expert_ffn(xs, tile_expert, used_tiles, w_gu, w_d):
    n_rows, dp = xs[0].shape
    n_e, d, ff2 = w_gu.shape
    ff = ff2 // 2
    tps = EXPERT_TILES_PER_STEP
    step_rows = tps * EXPERT_TILE
    assert n_rows % step_rows == 0
    rows_spec = pl.BlockSpec((step_rows, dp), lambda i, te, used: (jnp.minimum(i, (used[0] - 1) // tps), 0))
    w_specs = []
    for t in range(tps):
        w_specs += [pl.BlockSpec((1, d, ff2), lambda i, te, used, t=t: (te[i * tps + t], 0, 0)),
                    pl.BlockSpec((1, ff, d), lambda i, te, used, t=t: (te[i * tps + t], 0, 0))]
    grid_spec = pltpu.PrefetchScalarGridSpec(
        num_scalar_prefetch=2,
        grid=(n_rows // step_rows,),
        in_specs=[rows_spec] * PACK_GROUPS + w_specs,
        out_specs=[rows_spec] * PACK_GROUPS,
        scratch_shapes=[pltpu.VMEM((d, ff2), BF16), pltpu.VMEM((ff, d), BF16)] * tps)
    return pl.pallas_call(
        functools.partial(_expert_kernel, ff=ff),
        grid_spec=grid_spec,
        out_shape=[jax.ShapeDtypeStruct((n_rows, dp), jnp.int32)] * PACK_GROUPS,
        compiler_params=_params(("arbitrary",)),
        name="expert_ffn",
    )(tile_expert, used_tiles, *xs, *([w_gu, w_d] * tps))


def _combine_kernel(*refs, ff):
    h_refs, y_refs = refs[:PACK_GROUPS], refs[PACK_GROUPS:2 * PACK_GROUPS]
    x1_ref, wk_ref, wsg_ref, wsd_ref, g2_ref, lg_ref, lb_ref, o_ref = refs[2 * PACK_GROUPS:]
    shared = _glu_ffn(_unpack_rows([r[...] for r in h_refs]), wsg_ref[...], wsd_ref[...], ff)
    wk = wk_ref[...]
    n = y_refs[0].shape[2]
    blocks = [shared[:, i * n:(i + 1) * n] for i in range(2 * PACK_GROUPS)]
    for k in range(TOP_K):
        cols = _unpack_rows([r[k] for r in y_refs])
        blocks = [acc + wk[:, k:k + 1] * c.astype(F32) for acc, c in zip(blocks, cols)]
    f = jnp.concatenate(blocks, axis=1)
    o_ref[...] = _ln_rows(DEEPNORM_ALPHA * x1_ref[...] + g2_ref[0] * f) * lg_ref[...] + lb_ref[...]


def _moe_combine(h, x1, y_sel, wk_tok, w_sg, w_sd, g2, ln_g, ln_b, tm, b0, l, prev_out):
    t_part, d = x1.shape
    t = g2.shape[0] * l
    dp = h[0].shape[1]
    tiles_per_batch = l // tm
    off = b0 * tiles_per_batch
    ff = w_sd.shape[0]
    row = lambda i: (i, 0)
    stream_specs = ([pl.BlockSpec((tm, dp), row)] * PACK_GROUPS
                    + [pl.BlockSpec((TOP_K, tm, dp), lambda i: (0, i, 0),
                                    pipeline_mode=pl.Buffered(ROW_STREAM_BUFFERS))] * PACK_GROUPS
                    + [pl.BlockSpec((tm, d), row),
                       pl.BlockSpec((tm, TOP_K), row),
                       pl.BlockSpec((1, 1, d), lambda i: (i // tiles_per_batch + b0, 0, 0))])
    out_spec = pl.BlockSpec((tm, d), lambda i: (i + off, 0))
    streams = [*h, *y_sel, x1, wk_tok, g2]
    n_stream = len(streams)
    whole = [w_sg, w_sd, ln_g.reshape(1, d), ln_b.reshape(1, d)]
    args = streams + whole
    in_specs = ([pl.BlockSpec(memory_space=pl.ANY)] * n_stream
                + [pl.BlockSpec(memory_space=pltpu.VMEM)] * len(whole))
    aliases = {}
    if prev_out is not None:
        in_specs.append(pl.BlockSpec(memory_space=pl.ANY))
        aliases = {len(args): 0}
        args.append(prev_out)

    def kern(*refs):
        stream_refs = refs[:n_stream]
        wsg_ref, wsd_ref, lg_ref, lb_ref = refs[n_stream:n_stream + 4]
        o_hbm = refs[-1]

        def body(*blk):
            hs, ys = blk[:PACK_GROUPS], blk[PACK_GROUPS:2 * PACK_GROUPS]
            x1_blk, wk_blk, g2_blk, o_blk = blk[2 * PACK_GROUPS:]
            _combine_kernel(*hs, *ys, x1_blk, wk_blk, wsg_ref, wsd_ref, g2_blk, lg_ref, lb_ref, o_blk, ff=ff)

        pltpu.emit_pipeline(body, grid=(t_part // tm,), in_specs=stream_specs,
                            out_specs=[out_spec])(*stream_refs, o_hbm)

    return pl.pallas_call(
        kern,
        in_specs=in_specs,
        out_specs=pl.BlockSpec(memory_space=pl.ANY),
        out_shape=jax.ShapeDtypeStruct((t, d), F32),
        input_output_aliases=aliases,
        compiler_params=pltpu.CompilerParams(vmem_limit_bytes=VMEM_LIMIT),
        name="moe_combine",
    )(*args)


def _routing_tables(ek, rk, counts, n_rows):
    padded = (counts + EXPERT_TILE - 1) // EXPERT_TILE * EXPERT_TILE
    ends = jnp.cumsum(padded)
    starts = ends - padded
    experts = jnp.arange(N_EXPERTS, dtype=jnp.int32)
    start_of = jnp.sum(jnp.where(ek[None] == experts[:, None, None], starts[:, None, None], 0), axis=0)
    pos = start_of + rk
    tile_start = jnp.arange(n_rows // EXPERT_TILE, dtype=jnp.int32) * EXPERT_TILE
    tile_expert = jnp.minimum(jnp.sum(ends[None, :] <= tile_start[:, None], axis=1), N_EXPERTS - 1).astype(jnp.int32)
    used_tiles = (ends[-1:] // EXPERT_TILE).astype(jnp.int32)
    return pos.astype(jnp.int32), tile_expert, used_tiles


def _sc_mesh():
    return plsc.VectorSubcoreMesh(core_axis_name="core", subcore_axis_name="subcore")


def _dispatch_rows(rows, pos, n_rows):
    t, dp = rows.shape
    n_k = pos.shape[0]

    @functools.partial(pl.kernel, mesh=_sc_mesh(), out_type=jax.ShapeDtypeStruct((n_rows, dp), rows.dtype),
                       scratch_types=[])
    def scatter(rows_hbm, pos_hbm, out_hbm):
        def body(rows_vmem, idx_vmem):
            pltpu.sync_copy(rows_vmem, out_hbm.at[idx_vmem.at[0]])

        pltpu.emit_pipeline(
            body,
            grid=(t // SC_WINDOW, n_k),
            in_specs=[pl.BlockSpec((SC_WINDOW, dp), lambda i, k: (i, 0)),
                      pl.BlockSpec((1, SC_WINDOW), lambda i, k: (k, i))],
            out_specs=[],
            core_axis_name=("core", "subcore"),
            dimension_semantics=(pltpu.PARALLEL, pltpu.ARBITRARY),
        )(rows_hbm, pos_hbm)

    return scatter(rows, pos)


def _gather_rows(table, pos):
    n_k, t = pos.shape
    dp = table.shape[1]

    @functools.partial(pl.kernel, mesh=_sc_mesh(), out_type=jax.ShapeDtypeStruct((n_k * t, dp), table.dtype),
                       scratch_types=[])
    def gather(table_hbm, pos_hbm, out_hbm):
        def body(idx_vmem, out_vmem):
            pltpu.sync_copy(table_hbm.at[idx_vmem.at[0]], out_vmem)

        pltpu.emit_pipeline(
            body,
            grid=(n_k * t // SC_WINDOW,),
            in_specs=[pl.BlockSpec((1, SC_WINDOW), lambda i: (0, i))],
            out_specs=[pl.BlockSpec((SC_WINDOW, dp), lambda i: (i, 0))],
            core_axis_name=("core", "subcore"),
            dimension_semantics=(pltpu.PARALLEL,),
        )(pos_hbm, out_hbm)

    return gather(table, pos.reshape(1, n_k * t)).reshape(n_k, t, dp)


def _routed_moe(h, x1, ek, rk, wk, cnt, w_sg, w_sd, w_gu, w_d, g2, ln_g, ln_b, b0, l, prev_out):
    t = x1.shape[0]
    n_rows = t * TOP_K + N_EXPERTS * EXPERT_TILE
    pos, tile_expert, used_tiles = _routing_tables(ek, rk, cnt[:, 0].astype(jnp.int32), n_rows)
    xs = [_dispatch_rows(rows, pos, n_rows) for rows in h]
    ys = _expert_ffn(xs, tile_expert, used_tiles, w_gu, w_d)
    y_sel = [_gather_rows(table, pos) for table in ys]
    return _moe_combine(h, x1, y_sel, wk.T, w_sg, w_sd, g2, ln_g, ln_b, ROW_TILE, b0, l, prev_out)


def _gate_layouts(g_lat, g_ctx):
    g = jnp.concatenate([g_ctx, g_lat], axis=1)[..., :4 * GDN_HEADS]
    b, lt, _ = g.shape
    nt = lt // GDN_CHUNK
    rows = jnp.transpose(g.reshape(b, nt, GDN_CHUNK, 4, GDN_HEADS), (0, 4, 1, 3, 2))
    return jnp.pad(rows, ((0, 0), (0, 0), (0, 0), (0, 4), (0, 0))).reshape(b, GDN_HEADS, nt * 8, GDN_CHUNK)


def kernel(x, c, ctx, c_ctx, w_mod, b_mod, w_in, conv_w, gdn_a_log, gdn_dt_bias, gdn_norm_g, ret_log_gamma,
           ret_norm_g, w_out, ln1_g, ln1_b, w_router, router_bias, w_gate_up, w_down, w_shared_gate_up,
           w_shared_down, ln2_g, ln2_b):
    b, l, d = x.shape
    lc = ctx.shape[1]
    gw = GDN_HEADS * HEAD_DIM
    assert d == 2 * gw and w_in.shape[-1] == 8 * gw + 4 * GDN_HEADS and w_gate_up.shape[1] == N_EXPERTS
    assert b % BATCH_PARTS == 0 and l % ROW_TILE == 0 and l % RET_CHUNK == 0 and l % GRID_W == 0
    assert lc % RET_CHUNK == 0 and lc % HALO == 0 and (b // BATCH_PARTS * l) % EXPERT_TILE == 0

    n_mod = -(-(b + 1) // 8) * 8
    cc = jnp.zeros((n_mod, d), F32).at[:b].set(c).at[b].set(c_ctx)
    mod = _modulation(cc, w_mod[0], b_mod[0])
    sh1, sc1, g1, sh2, sc2, g2 = [mod[:b, i * d:(i + 1) * d].reshape(b, 1, d) for i in range(6)]
    csh1 = jnp.broadcast_to(mod[b, 0:d].reshape(1, 1, d), (b, 1, d))
    csc1 = jnp.broadcast_to(mod[b, d:2 * d].reshape(1, 1, d), (b, 1, d))

    w = w_in[0]
    o_gate = 3 * gw + gw
    o_ret = o_gate + 4 * GDN_HEADS
    w_main = jnp.concatenate([w[:, :o_gate], w[:, o_ret:]], axis=1).astype(BF16)
    w_gate = w[:, o_gate:o_ret]
    cos_t, sin_t = _rope_tables(l)
    wo = w_out[0].astype(BF16)
    w_sg, w_sd = w_shared_gate_up[0].astype(BF16), w_shared_down[0].astype(BF16)
    w_router_t = w_router[0].T

    nb = b // BATCH_PARTS
    out = None
    for part in range(BATCH_PARTS):
        b0 = part * nb
        qkv_lat, z_lat, ret_lat, g_lat = _inproj(x, sh1, sc1, w_main, w_gate, conv_w[0], ROW_TILE, b0, nb)
        qkv_ctx, _, ret_ctx, g_ctx = _inproj(ctx, csh1, csc1, w_main, w_gate, conv_w[0], lc, b0, nb)
        gate_rows = _gate_layouts(g_lat, g_ctx)
        a_lat = _gdn_mixer(qkv_lat, qkv_ctx, z_lat, gate_rows, gdn_a_log[0],
                           gdn_dt_bias[0], gdn_norm_g[0], GDN_HEADS_PER_STEP)
        r_lat = _ret_mixer(ret_lat, ret_ctx, ret_log_gamma[0], ret_norm_g[0], cos_t, sin_t)
        x1, h_a, h_b, ek, rk, wk, cnt = _outproj(a_lat, r_lat, x, wo[:gw], wo[gw:], g1, sh2, sc2, ln1_g[0], ln1_b[0],
                                                 w_router_t, router_bias[0], ROW_TILE, b0)
        h = [arr.reshape(nb * l, arr.shape[-1]) for arr in (h_a, h_b)]
        out = _routed_moe(h, x1.reshape(nb * l, d), ek, rk, wk, cnt, w_sg, w_sd, w_gate_up[0], w_down[0],
                          g2, ln2_g[0], ln2_b[0], b0, l, out)
    return out.reshape(b, l, d)
```

```python
import functools

import jax
import jax.numpy as jnp
from jax import lax
from jax.experimental import pallas as pl
from jax.experimental.pallas import tpu as pltpu
from jax.experimental.pallas import tpu_sc as plsc

F32 = jnp.float32
BF16 = jnp.bfloat16

HEAD_DIM = 128
GDN_HEADS = 4
RET_HEADS = 4
CONV_W = 5
HALO = 16
GDN_CHUNK = 64
INV_BASE = 16
RET_CHUNK = 256
GRID_W = 64
ROPE_THETA = 10000.0
N_EXPERTS = 64
TOP_K = 8
N_GROUPS = 8
TOPK_GROUPS = 4
ROUTED_SCALE = 2.5
SC_WINDOW = 128
PACK_GROUPS = 2
ROW_TILE = 512
PROJ_TILE = 1024
GDN_HEADS_PER_STEP = 2
GDN_SLOTS_PER_GROUP = 6
BATCH_PARTS = 2
EXPERT_TILES_PER_STEP = 2
EXPERT_SUBTILES = 4
EXPERT_TILE = 512
LN_EPS = 1e-6
DEPTH = 1
DEEPNORM_ALPHA = (2 * DEPTH) ** 0.25

VMEM_LIMIT = 56 * 1024 * 1024
HIGHEST = lax.Precision.HIGHEST
NT_DIMS = (((1,), (1,)), ((), ()))


def _dot(a, b, precision=None):
    return jnp.dot(a, b, preferred_element_type=F32, precision=precision)


def _dot_nt(a, b, precision=None):
    return lax.dot_general(a, b, NT_DIMS, preferred_element_type=F32, precision=precision)


def _silu(x):
    return x * (1.0 / (1.0 + jnp.exp(-x)))


def _sigmoid(x):
    return 1.0 / (1.0 + jnp.exp(-x))


def _softplus(x):
    return jnp.maximum(x, 0.0) + jnp.log(1.0 + jnp.exp(-jnp.abs(x)))


def _ln_rows(x):
    mu = jnp.mean(x, axis=-1, keepdims=True)
    xc = x - mu
    var = jnp.mean(xc * xc, axis=-1, keepdims=True)
    return xc * lax.rsqrt(var + LN_EPS)


def _params(sem):
    return pltpu.CompilerParams(dimension_semantics=sem, vmem_limit_bytes=VMEM_LIMIT)


def _mod_kernel(c_ref, w_ref, b_ref, o_ref):
    o_ref[...] = _dot(_silu(c_ref[...]), w_ref[...], HIGHEST) + b_ref[...]


def _modulation(cc, w_mod, b_mod):
    rows, d = cc.shape
    n = w_mod.shape[1]
    tn = 1024
    return pl.pallas_call(
        _mod_kernel,
        grid=(n // tn,),
        in_specs=[pl.BlockSpec((rows, d), lambda j: (0, 0)),
                  pl.BlockSpec((d, tn), lambda j: (0, j)),
                  pl.BlockSpec((1, tn), lambda j: (0, j))],
        out_specs=pl.BlockSpec((rows, tn), lambda j: (0, j)),
        out_shape=jax.ShapeDtypeStruct((rows, n), F32),
        compiler_params=_params(("arbitrary",)),
        name="modulation",
    )(cc, w_mod, b_mod.reshape(1, n))


def _inproj_kernel(x_ref, xp_ref, xn_ref, sh_ref, sc_ref, wm_ref, wg_ref, cw_ref, qkv_ref, z_ref, ret_ref, g_ref,
                   p_ref):
    j, nl = pl.program_id(1), pl.num_programs(1)
    tm = x_ref.shape[1]
    modulate = lambda x: _ln_rows(x) * (1.0 + sc_ref[0]) + sh_ref[0]
    h = modulate(x_ref[0])
    hb = h.astype(BF16)
    h_prev = jnp.where(j > 0, modulate(xp_ref[0]), 0.0).astype(BF16)
    h_next = jnp.where(j < nl - 1, modulate(xn_ref[0]), 0.0).astype(BF16)
    h_ext = jnp.concatenate([h_prev, hb, h_next], axis=0)
    half = CONV_W // 2
    gw = GDN_HEADS * HEAD_DIM
    for part in range(3):
        p_ref[part] = _dot(h_ext, wm_ref[:, part * gw:(part + 1) * gw])

    def conv_part(part):
        c0 = part * gw
        p = p_ref[part]
        acc = p[HALO - half:HALO - half + tm, :] * cw_ref[0:1, c0:c0 + gw]
        for tap in range(1, CONV_W):
            acc = acc + p[HALO - half + tap:HALO - half + tap + tm, :] * cw_ref[tap:tap + 1, c0:c0 + gw]
        y = _silu(acc)
        if part < 2:
            scale = HEAD_DIM ** -0.5 if part == 0 else 1.0
            blocks = [y[:, hd * HEAD_DIM:(hd + 1) * HEAD_DIM] for hd in range(GDN_HEADS)]
            blocks = [blk * (lax.rsqrt(jnp.sum(blk * blk, axis=-1, keepdims=True) + LN_EPS) * scale) for blk in blocks]
            y = jnp.concatenate(blocks, axis=1)
        qkv_ref[0, :, c0:c0 + gw] = y.astype(BF16)

    rest = [(ref, n0) for ref in (z_ref, ret_ref) for n0 in range(0, ref.shape[-1], 512)]
    col = 3 * gw
    for idx, (ref, n0) in enumerate(rest):
        if idx < 3:
            conv_part(idx)
        ref[0, :, n0:n0 + 512] = _dot(hb, wm_ref[:, col:col + 512]).astype(BF16)
        col += 512
    h_lo = (h - hb.astype(F32)).astype(BF16)
    wg = wg_ref[...]
    wg_hi = wg.astype(BF16)
    wg_lo = (wg - wg_hi.astype(F32)).astype(BF16)
    g_ref[0] = (_dot(h_lo, wg_hi) + _dot(hb, wg_lo)) + _dot(hb, wg_hi)


def _inproj(x, sh, sc, w_main, w_gate, conv_w, tm, b0, b):
    _, l, d = x.shape
    widths = (3 * GDN_HEADS * HEAD_DIM, GDN_HEADS * HEAD_DIM, 4 * RET_HEADS * HEAD_DIM)
    n_gate = w_gate.shape[1]
    per_tile, last = tm // HALO, l // HALO - 1
    row = lambda i, j: (i, j, 0)
    const = lambda i, j: (0, 0)
    return pl.pallas_call(
        _inproj_kernel,
        grid=(b, l // tm),
        in_specs=[pl.BlockSpec((1, tm, d), lambda i, j: (i + b0, j, 0)),
                  pl.BlockSpec((1, HALO, d), lambda i, j: (i + b0, jnp.maximum(j * per_tile - 1, 0), 0)),
                  pl.BlockSpec((1, HALO, d), lambda i, j: (i + b0, jnp.minimum((j + 1) * per_tile, last), 0)),
                  pl.BlockSpec((1, 1, d), lambda i, j: (i + b0, 0, 0)),
                  pl.BlockSpec((1, 1, d), lambda i, j: (i + b0, 0, 0)),
                  pl.BlockSpec(w_main.shape, const),
                  pl.BlockSpec(w_gate.shape, const),
                  pl.BlockSpec(conv_w.shape, const)],
        out_specs=[pl.BlockSpec((1, tm, w), row) for w in widths]
                  + [pl.BlockSpec((1, tm, n_gate), row)],
        out_shape=[jax.ShapeDtypeStruct((b, l, w), BF16) for w in widths]
                  + [jax.ShapeDtypeStruct((b, l, n_gate), F32)],
        scratch_shapes=[pltpu.VMEM((3, tm + 2 * HALO, GDN_HEADS * HEAD_DIM), F32)],
        compiler_params=_params(("arbitrary", "arbitrary")),
        name="inproj",
    )(x, x, x, sh, sc, w_main, w_gate, conv_w)


def _bdot(a, b):
    return _dot(a.astype(BF16), b.astype(BF16))


def _pack_bf16_pair(x):
    n = x.shape[1] // 2
    bits = lambda v: pltpu.bitcast(v.astype(BF16).astype(F32), jnp.uint32)
    word = lax.shift_right_logical(bits(x[:, :n]), jnp.uint32(16)) | (bits(x[:, n:]) & jnp.uint32(0xFFFF0000))
    return pltpu.bitcast(word, jnp.int32)


def _unpack_bf16_pair(w):
    u = pltpu.bitcast(w, jnp.uint32)
    lo = pltpu.bitcast(lax.shift_left(u, jnp.uint32(16)), F32)
    hi = pltpu.bitcast(u & jnp.uint32(0xFFFF0000), F32)
    return lo.astype(BF16), hi.astype(BF16)


def _split3(x):
    hi = x.astype(BF16)
    r1 = x - hi.astype(F32)
    mid = r1.astype(BF16)
    lo = (r1 - mid.astype(F32)).astype(BF16)
    return hi, mid, lo


def _dot_split_lhs(x, mask01):
    m = mask01.astype(BF16)
    hi, mid, lo = _split3(x)
    return (_dot(lo, m) + _dot(mid, m)) + _dot(hi, m)


def _each(fn, *lists):
    return [fn(*args) for args in zip(*lists)]


def _inv_unit_triangular(mats, base=INV_BASE):
    n = mats[0].shape[0]
    i = lax.broadcasted_iota(jnp.int32, (n, n), 0)
    j = lax.broadcasted_iota(jnp.int32, (n, n), 1)
    eye = (i == j).astype(F32)
    shift = base.bit_length() - 1
    inside = (i >> shift) == (j >> shift)
    xs = _each(lambda a: jnp.where(inside, a, 0.0), mats)
    ts = _each(lambda d: eye - d, xs)
    for _ in range(shift - 1):
        xs = _each(lambda x: _bdot(x, x), xs)
        yield
        ts = _each(lambda t, x: t + _bdot(t, x), ts, xs)
        yield
    size = base
    while size < GDN_CHUNK:
        shift += 1
        size *= 2
        wider = (i >> shift) == (j >> shift)
        off = wider & jnp.logical_not(inside)
        ots = _each(lambda a, t: _bdot(jnp.where(off, a, 0.0), t), mats, ts)
        yield
        ts = _each(lambda t, ot: t - _bdot(t, ot), ts, ots)
        yield
        inside = wider
    return ts


def _gdn_kernel(alog_ref, dtb_ref,
                q_ref, k_ref, v_ref, qc_ref, kc_ref, vc_ref,
                gt_ref, z_ref, ng_ref,
                o_ref,
                qs_ref, ks_ref, vs_ref, gts_ref, gth_ref,
                wqf_ref, wqb_ref, u0_ref, intra_ref, ket_ref, gef_ref, geb_ref,
                of_ref, ob_ref, *, hb, l_lat, l_ctx, slots_per_group):
    c = GDN_CHUNK
    c2 = 2 * c
    lt = l_lat + l_ctx
    n_ctx = l_ctx // c
    nt = lt // c
    hgrp = pl.program_id(1)

    i2 = lax.broadcasted_iota(jnp.int32, (c2, c2), 0)
    j2 = lax.broadcasted_iota(jnp.int32, (c2, c2), 1)
    same_blk = (i2 & c) == (j2 & c)
    sgn_i = jnp.where(i2 < c, 1, -1)
    sgn_j = jnp.where(j2 < c, 1, -1)
    incl = same_blk & ((j2 - i2) * sgn_i <= 0)
    strict = same_blk & ((j2 - i2) * sgn_i < 0)
    tri_row = (same_blk & ((i2 - j2) * sgn_j <= 0)).astype(F32)
    eye_m = i2 == j2
    eye_b = eye_m.astype(BF16)
    lane_lt_c = j2 < c

    for hh in range(hb):
        head = hgrp * hb + hh
        lane0 = hh * HEAD_DIM
        for (src_c, src_l, dst) in ((qc_ref, q_ref, qs_ref), (kc_ref, k_ref, ks_ref), (vc_ref, v_ref, vs_ref)):
            dst[hh, 0:l_ctx, :] = src_c[0, :, lane0:lane0 + HEAD_DIM]
            dst[hh, l_ctx:lt, :] = src_l[0, :, lane0:lane0 + HEAD_DIM]

        a_f, a_b = alog_ref[head], alog_ref[GDN_HEADS + head]
        d_f, d_b = dtb_ref[head], dtb_ref[GDN_HEADS + head]
        compr = lax.broadcasted_iota(jnp.int32, (nt * 8, 1), 0) & 7
        neg_ar = -jnp.exp(jnp.where(compr == 2, a_f, a_b))
        dtbr = jnp.where(compr == 2, d_f, d_b)
        grow_raw = gt_ref[0, hh]
        grow = jnp.where(compr < 2, _sigmoid(grow_raw), neg_ar * _softplus(grow_raw + dtbr))
        gts_ref[hh] = jnp.zeros(gts_ref.shape[1:], F32)
        gts_ref[hh, :, 0:c] = grow
        gth_ref[hh] = jnp.zeros(gth_ref.shape[1:], F32)
        gth_ref[hh, :, c:c2] = grow

    spg = slots_per_group
    heads = list(range(hb))

    def bwd_chunk(i):
        return jnp.where(i < n_ctx, n_ctx - 1 - i, nt + n_ctx - 1 - i)

    def slot_load(i, hh):
        cbk = bwd_chunk(i)
        rf = pl.multiple_of(i * c, c)
        rb = pl.multiple_of(cbk * c, c)
        two = lambda ref: jnp.concatenate([ref[hh, pl.ds(rf, c), :], ref[hh, pl.ds(rb, c), :]], axis=0).astype(F32)
        g_r = (gts_ref[hh, pl.ds(pl.multiple_of(i * 8, 8), 8), :]
               + gth_ref[hh, pl.ds(pl.multiple_of(cbk * 8, 8), 8), :])
        return two(ks_ref), two(vs_ref), two(qs_ref), g_r

    def slot_stages(loaded):
        k2, v2, q2, g_r = [list(col) for col in zip(*loaded)]
        rcs = _each(lambda g: _dot_split_lhs(g, tri_row), g_r)
        yield
        k2b = _each(lambda x: x.astype(BF16), k2)
        kk = _each(_dot_nt, k2b, k2b)
        qk = _each(lambda q, kb: _dot_nt(q.astype(BF16), kb), q2, k2b)
        yield
        by_dir = lambda g, k: jnp.where(lane_lt_c[0:1, :], g[k:k + 1, :], g[k + 1:k + 2, :])
        lane_sum = lambda mask, row: jnp.sum(jnp.where(mask, row, 0.0), axis=1, keepdims=True)
        gcc = _each(lambda g: lane_sum(incl, by_dir(g, 2)), g_r)
        gcr = _each(lambda r: by_dir(r, 2), rcs)
        beta = _each(lambda g: lane_sum(eye_m, by_dir(g, 0)), g_r)
        gend = _each(lambda s: jnp.concatenate([jnp.broadcast_to(s[c - 1:c, :], (c, 1)),
                                                jnp.broadcast_to(s[c:c + 1, :], (c, 1))], axis=0), gcc)
        decay = _each(lambda gc, gr: jnp.exp(jnp.where(incl, gc - gr, -jnp.inf)), gcc, gcr)
        a_mat = _each(lambda m, d, b: jnp.where(strict, m * d, 0.0) * b, kk, decay, beta)
        t_f32 = yield from _inv_unit_triangular(a_mat)
        t_mat = _each(lambda t: t.astype(BF16), t_f32)
        egc = _each(jnp.exp, gcc)
        u0 = _each(lambda t, v, b: _dot(t, (v * b).astype(BF16)), t_mat, v2, beta)
        w = _each(lambda t, k, b, e: _dot(t, (k * (b * e)).astype(BF16)), t_mat, k2, beta, egc)
        ket = _each(lambda k, ge_, gc: _dot_nt(eye_b, (k * jnp.exp(ge_ - gc)).astype(BF16)), k2, gend, gcc)
        yield
        qd = _each(lambda q, e: (q * e).astype(BF16), q2, egc)
        intra = _each(lambda m, d: (m * d).astype(BF16), qk, decay)
        ge = _each(jnp.exp, gend)
        return [(jnp.concatenate([w_[0:c].astype(BF16), qd_[0:c]], axis=0),
                 jnp.concatenate([w_[c:c2].astype(BF16), qd_[c:c2]], axis=0), u0_, in_,
                 jnp.concatenate([jnp.where(lane_lt_c, kt, 0.0), jnp.where(lane_lt_c, 0.0, kt)], axis=0).astype(BF16),
                 jnp.broadcast_to(g_[0:1, :], (8, HEAD_DIM)), jnp.broadcast_to(g_[c:c + 1, :], (8, HEAD_DIM)))
                for w_, qd_, u0_, in_, kt, g_ in zip(w, qd, u0, intra, ket, ge)]

    def transform_group(g):
        par = lax.rem(g, 2)
        jobs = [(s, hh) for s in range(spg) for hh in heads]
        results = yield from slot_stages([slot_load(g * spg + s, hh) for s, hh in jobs])
        for (s, hh), (wq_f, wq_b, u0, intra, ket, ge_f, ge_b) in zip(jobs, results):
            wqf_ref[par, hh, s] = wq_f
            wqb_ref[par, hh, s] = wq_b
            u0_ref[par, hh, s] = u0
            intra_ref[par, hh, s] = intra
            ket_ref[par, hh, s] = ket
            gef_ref[par, hh, s] = ge_f
            geb_ref[par, hh, s] = ge_b

    def recur_group(g, states):
        par = lax.rem(g, 2)
        sf, sb = list(states[0::2]), list(states[1::2])
        for s in range(spg):
            i = g * spg + s
            rf = pl.multiple_of(i * c, c)
            rb = pl.multiple_of(bwd_chunk(i) * c, c)
            r_f = _each(lambda hh, st: _dot(wqf_ref[par, hh, s], st.astype(BF16)), heads, sf)
            r_b = _each(lambda hh, st: _dot(wqb_ref[par, hh, s], st.astype(BF16)), heads, sb)
            u0 = _each(lambda hh: u0_ref[par, hh, s], heads)
            yield
            u2 = _each(lambda u, f, b_: jnp.concatenate([u[0:c] - f[0:c], u[c:c2] - b_[0:c]], axis=0).astype(BF16),
                       u0, r_f, r_b)
            ket = _each(lambda hh: ket_ref[par, hh, s], heads)
            df = _each(lambda kt, u: _dot(kt[0:HEAD_DIM], u), ket, u2)
            db = _each(lambda kt, u: _dot(kt[HEAD_DIM:2 * HEAD_DIM], u), ket, u2)
            iu = _each(lambda hh, u: _dot(intra_ref[par, hh, s], u), heads, u2)
            sf = _each(lambda hh, st, d: gef_ref[par, hh, s][0:1, :] * st + d, heads, sf, df)
            sb = _each(lambda hh, st, d: geb_ref[par, hh, s][0:1, :] * st + d, heads, sb, db)
            for hh in heads:
                of_ref[hh, pl.ds(rf, c), :] = r_f[hh][c:c2] + iu[hh][0:c]
                ob_ref[hh, pl.ds(rb, c), :] = r_b[hh][c:c2] + iu[hh][c:c2]
            yield
        return tuple(x for pair in zip(sf, sb) for x in pair)

    def drive(transform, recur, transforms_per_recur=1):
        states = None
        while transform is not None or recur is not None:
            if transform is not None:
                for _ in range(transforms_per_recur):
                    try:
                        next(transform)
                    except StopIteration:
                        transform = None
                        break
            if recur is not None:
                try:
                    next(recur)
                except StopIteration as stop:
                    states, recur = stop.value, None
        return states

    n_groups = nt // spg
    zero_state = tuple(jnp.zeros((HEAD_DIM, HEAD_DIM), F32) for _ in range(2 * hb))
    drive(transform_group(0), None)
    states = lax.fori_loop(0, n_groups - 1,
                           lambda g, st: drive(transform_group(g + 1), recur_group(g, st)), zero_state)

    ng = ng_ref[...]
    blk = 256

    def finish_block(hh, b):
        r, lane0 = b * blk, hh * HEAD_DIM
        o = of_ref[hh, l_ctx + r:l_ctx + r + blk, :] + ob_ref[hh, l_ctx + r:l_ctx + r + blk, :]
        y = o * lax.rsqrt(jnp.mean(o * o, axis=-1, keepdims=True) + LN_EPS) * ng
        z = z_ref[0, r:r + blk, lane0:lane0 + HEAD_DIM].astype(F32)
        o_ref[0, r:r + blk, lane0:lane0 + HEAD_DIM] = (y * _silu(z)).astype(BF16)

    def finish_blocks(jobs):
        for hh, b in jobs:
            finish_block(hh, b)
            yield

    first = -(-(spg * c) // blk)
    last = (nt - spg - n_ctx) * c // blk
    early = [(hh, b) for b in range(first, last) for hh in heads]
    late = [(hh, b) for b in range(l_lat // blk) if not first <= b < last for hh in heads]
    drive(finish_blocks(early), recur_group(n_groups - 1, states))
    drive(finish_blocks(late), None)


def _gdn_mixer(qkv_lat, qkv_ctx, z_lat, gate_rows, a_log, dt_bias, norm_g, hb):
    b, l_lat, _ = qkv_lat.shape
    l_ctx = qkv_ctx.shape[1]
    lt = l_lat + l_ctx
    nt = lt // GDN_CHUNK
    hw = hb * HEAD_DIM
    ngrp = GDN_HEADS // hb
    c2 = 2 * GDN_CHUNK
    spg = GDN_SLOTS_PER_GROUP
    assert nt % spg == 0 and nt // spg >= 2 and GDN_HEADS % hb == 0

    def seq_spec(length, part):
        return pl.BlockSpec((1, length, hw), lambda i, j, *_: (i, 0, part * ngrp + j))

    grid_spec = pltpu.PrefetchScalarGridSpec(
        num_scalar_prefetch=2,
        grid=(b, ngrp),
        in_specs=[seq_spec(l_lat, 0), seq_spec(l_lat, 1), seq_spec(l_lat, 2),
                  seq_spec(l_ctx, 0), seq_spec(l_ctx, 1), seq_spec(l_ctx, 2),
                  pl.BlockSpec((1, hb, nt * 8, GDN_CHUNK), lambda i, j, *_: (i, j, 0, 0)),
                  pl.BlockSpec((1, l_lat, hw), lambda i, j, *_: (i, 0, j)),
                  pl.BlockSpec((1, HEAD_DIM), lambda i, j, *_: (0, 0))],
        out_specs=pl.BlockSpec((1, l_lat, hw), lambda i, j, *_: (i, 0, j)),
        scratch_shapes=[
            pltpu.VMEM((hb, lt, HEAD_DIM), BF16),
            pltpu.VMEM((hb, lt, HEAD_DIM), BF16),
            pltpu.VMEM((hb, lt, HEAD_DIM), BF16),
            pltpu.VMEM((hb, nt * 8, c2), F32),
            pltpu.VMEM((hb, nt * 8, c2), F32),
            pltpu.VMEM((2, hb, spg, c2, HEAD_DIM), BF16),
            pltpu.VMEM((2, hb, spg, c2, HEAD_DIM), BF16),
            pltpu.VMEM((2, hb, spg, c2, HEAD_DIM), F32),
            pltpu.VMEM((2, hb, spg, c2, c2), BF16),
            pltpu.VMEM((2, hb, spg, 2 * HEAD_DIM, c2), BF16),
            pltpu.VMEM((2, hb, spg, 8, HEAD_DIM), F32),
            pltpu.VMEM((2, hb, spg, 8, HEAD_DIM), F32),
            pltpu.VMEM((hb, lt, HEAD_DIM), F32),
            pltpu.VMEM((hb, lt, HEAD_DIM), F32),
        ])
    kern = functools.partial(_gdn_kernel, hb=hb, l_lat=l_lat, l_ctx=l_ctx, slots_per_group=spg)
    return pl.pallas_call(
        kern,
        grid_spec=grid_spec,
        out_shape=jax.ShapeDtypeStruct((b, l_lat, GDN_HEADS * HEAD_DIM), BF16),
        compiler_params=_params(("arbitrary", "arbitrary")),
        name="gdn_mixer",
    )(a_log, dt_bias, qkv_lat, qkv_lat, qkv_lat, qkv_ctx, qkv_ctx, qkv_ctx,
      gate_rows, z_lat, norm_g.reshape(1, HEAD_DIM))


def _ret_kernel(lg_ref, q_ref, k_ref, v_ref, gate_ref, qc_ref, kc_ref, vc_ref, cos_ref, sin_ref, ng_ref,
                o_ref, q_s, k_s, rf_s, rb_s, *, l_lat, l_ctx, hb):
    c = RET_CHUNK
    n_lat = l_lat // c
    n_ctx = l_ctx // c
    heads = list(range(hb))
    lanes = [slice(hh * HEAD_DIM, (hh + 1) * HEAD_DIM) for hh in heads]
    pos_c = lax.broadcasted_iota(jnp.int32, (c, 1), 0).astype(F32)
    ii = lax.broadcasted_iota(jnp.int32, (c, c), 0)
    jj = lax.broadcasted_iota(jnp.int32, (c, c), 1)
    dif = (ii - jj).astype(F32)
    e2 = lax.broadcasted_iota(jnp.int32, (HEAD_DIM, HEAD_DIM), 0)
    f2 = lax.broadcasted_iota(jnp.int32, (HEAD_DIM, HEAD_DIM), 1)
    eye_b = (e2 == f2).astype(BF16)
    kscale = HEAD_DIM ** -0.5
    lg_f = [lg_ref[pl.program_id(1) * hb + hh] for hh in heads]
    lg_b = [lg_ref[RET_HEADS + pl.program_id(1) * hb + hh] for hh in heads]
    dmat = _each(lambda f, b_: (jnp.exp(jnp.where(ii >= jj, f * dif, -jnp.inf))
                                + jnp.exp(jnp.where(jj >= ii, -b_ * dif, -jnp.inf))), lg_f, lg_b)
    kdec_f = _each(lambda f: jnp.exp(f * (c - 1 - pos_c)), lg_f)
    kdec_b = _each(lambda b_: jnp.exp(b_ * pos_c), lg_b)
    qdec_f = _each(lambda f: jnp.exp(f * (pos_c + 1.0)), lg_f)
    qdec_b = _each(lambda b_: jnp.exp(b_ * (c - pos_c)), lg_b)
    cd_f = _each(lambda f: jnp.exp(jnp.full((1, HEAD_DIM), f * c, F32)), lg_f)
    cd_b = _each(lambda b_: jnp.exp(jnp.full((1, HEAD_DIM), b_ * c, F32)), lg_b)

    def transposed(kd):
        return _dot_nt(eye_b, kd.astype(BF16)).astype(BF16)

    r_f = [jnp.zeros((HEAD_DIM, HEAD_DIM), F32) for _ in heads]
    r_b = [jnp.zeros((HEAD_DIM, HEAD_DIM), F32) for _ in heads]
    for n in range(n_ctx):
        m = n_ctx - 1 - n
        kf = _each(lambda ln, d: transposed(kc_ref[0, n * c:(n + 1) * c, ln].astype(F32) * kscale * d), lanes, kdec_f)
        kb = _each(lambda ln, d: transposed(kc_ref[0, m * c:(m + 1) * c, ln].astype(F32) * kscale * d), lanes, kdec_b)
        pf = _each(lambda kt, ln: _dot(kt, vc_ref[0, n * c:(n + 1) * c, ln]), kf, lanes)
        pb = _each(lambda kt, ln: _dot(kt, vc_ref[0, m * c:(m + 1) * c, ln]), kb, lanes)
        r_f = _each(lambda d, r, p: d * r + p, cd_f, r_f, pf)
        r_b = _each(lambda d, r, p: d * r + p, cd_b, r_b, pb)

    def rope_body(n, carry):
        r = pl.multiple_of(n * c, c)
        cs = cos_ref[pl.ds(r, c), :]
        sn = sin_ref[pl.ds(r, c), :]
        for hh, ln in zip(heads, lanes):
            q = q_ref[0, pl.ds(r, c), ln].astype(F32)
            k = k_ref[0, pl.ds(r, c), ln].astype(F32) * kscale
            q_s[hh, pl.ds(r, c), :] = q * cs + pltpu.roll(q, HEAD_DIM // 2, 1) * sn
            k_s[hh, pl.ds(r, c), :] = k * cs + pltpu.roll(k, HEAD_DIM // 2, 1) * sn
        return carry

    lax.fori_loop(0, n_lat, rope_body, 0)

    chunks = list(range(n_lat))
    rows = [slice(n * c, (n + 1) * c) for n in chunks]
    jobs = [(hh, n) for hh in heads for n in chunks]
    k_c = [k_s[hh, rows[n], :] for hh, n in jobs]
    kft = [transposed(k * kdec_f[hh]) for k, (hh, n) in zip(k_c, jobs)]
    kbt = [transposed(k * kdec_b[hh]) for k, (hh, n) in zip(k_c, jobs)]
    kvf = [_dot(kt, v_ref[0, rows[n], lanes[hh]]) for kt, (hh, n) in zip(kft, jobs)]
    kvb = [_dot(kt, v_ref[0, rows[n], lanes[hh]]) for kt, (hh, n) in zip(kbt, jobs)]
    for hh in heads:
        r = r_f[hh]
        for n in chunks:
            rf_s[hh, n] = r
            r = cd_f[hh] * r + kvf[hh * n_lat + n]
        r = r_b[hh]
        for n in reversed(chunks):
            rb_s[hh, n] = r
            r = cd_b[hh] * r + kvb[hh * n_lat + n]

    group = 4
    for g0 in range(0, len(jobs), group):
        part = jobs[g0:g0 + group]
        q_c = [q_s[hh, rows[n], :] for hh, n in part]
        s = [_dot_nt(q.astype(BF16), k_s[hh, rows[n], :].astype(BF16)) for q, (hh, n) in zip(q_c, part)]
        att = [(s_ * dmat[hh]).astype(BF16) for s_, (hh, n) in zip(s, part)]
        o = [_dot(a, v_ref[0, rows[n], lanes[hh]]) for a, (hh, n) in zip(att, part)]
        qd = [jnp.concatenate([(q * qdec_f[hh]).astype(BF16), (q * qdec_b[hh]).astype(BF16)], axis=1)
              for q, (hh, n) in zip(q_c, part)]
        st = [jnp.concatenate([rf_s[hh, n], rb_s[hh, n]], axis=0).astype(BF16) for hh, n in part]
        o = _each(lambda o_, qd_, st_: o_ + _dot(qd_, st_), o, qd, st)
        for o_, (hh, n) in zip(o, part):
            y = _ln_rows(o_) * ng_ref[:, lanes[hh]]
            g = gate_ref[0, rows[n], lanes[hh]].astype(F32)
            o_ref[0, rows[n], lanes[hh]] = (y * _silu(g)).astype(BF16)


def _ret_mixer(ret_lat, ret_ctx, log_gamma, norm_g, cos_t, sin_t):
    b, l_lat, _ = ret_lat.shape
    l_ctx = ret_ctx.shape[1]
    n_lat = l_lat // RET_CHUNK
    hb = RET_HEADS
    hw = hb * HEAD_DIM
    ngrp = RET_HEADS // hb

    def seq_spec(length, part):
        return pl.BlockSpec((1, length, hw), lambda i, j, *_: (i, 0, part * ngrp + j))

    grid_spec = pltpu.PrefetchScalarGridSpec(
        num_scalar_prefetch=1,
        grid=(b, ngrp),
        in_specs=[seq_spec(l_lat, 0), seq_spec(l_lat, 1), seq_spec(l_lat, 2), seq_spec(l_lat, 3),
                  seq_spec(l_ctx, 0), seq_spec(l_ctx, 1), seq_spec(l_ctx, 2),
                  pl.BlockSpec((l_lat, HEAD_DIM), lambda i, j, *_: (0, 0)),
                  pl.BlockSpec((l_lat, HEAD_DIM), lambda i, j, *_: (0, 0)),
                  pl.BlockSpec((1, hw), lambda i, j, *_: (0, j))],
        out_specs=pl.BlockSpec((1, l_lat, hw), lambda i, j, *_: (i, 0, j)),
        scratch_shapes=[pltpu.VMEM((hb, l_lat, HEAD_DIM), F32),
                        pltpu.VMEM((hb, l_lat, HEAD_DIM), F32),
                        pltpu.VMEM((hb, n_lat, HEAD_DIM, HEAD_DIM), F32),
                        pltpu.VMEM((hb, n_lat, HEAD_DIM, HEAD_DIM), F32)])
    kern = functools.partial(_ret_kernel, l_lat=l_lat, l_ctx=l_ctx, hb=hb)
    return pl.pallas_call(
        kern,
        grid_spec=grid_spec,
        out_shape=jax.ShapeDtypeStruct((b, l_lat, RET_HEADS * HEAD_DIM), BF16),
        compiler_params=_params(("arbitrary", "arbitrary")),
        name="ret_mixer",
    )(log_gamma.reshape(-1), ret_lat, ret_lat, ret_lat, ret_lat, ret_ctx, ret_ctx, ret_ctx,
      cos_t, sin_t, norm_g.reshape(1, -1))


def _rope_tables(l_lat):
    rows = l_lat // GRID_W
    row = jnp.repeat(jnp.arange(rows, dtype=F32), GRID_W)
    col = jnp.tile(jnp.arange(GRID_W, dtype=F32), rows)
    quarter = HEAD_DIM // 4
    inv = ROPE_THETA ** (-jnp.arange(quarter, dtype=F32) / quarter)
    ang = jnp.concatenate([row[:, None] * inv, col[:, None] * inv], -1)
    cos, sin = jnp.cos(ang), jnp.sin(ang)
    return jnp.concatenate([cos, cos], -1), jnp.concatenate([-sin, sin], -1)


def _top_rows(vals, k):
    n = vals.shape[0]
    idx = lax.broadcasted_iota(jnp.int32, vals.shape, 0)
    taken = jnp.zeros(vals.shape, jnp.int32)
    firsts = []
    for _ in range(k):
        live = jnp.where(taken == 0, vals, -jnp.inf)
        top = jnp.max(live, axis=0, keepdims=True)
        cand = jnp.where((live == top) & (taken == 0), idx, n)
        first = jnp.min(cand, axis=0, keepdims=True)
        taken = taken + (idx == first).astype(jnp.int32)
        firsts.append(first)
    return firsts, taken


def _outproj_kernel(a_ref, r_ref, x_ref, wa_ref, wr_ref, g1_ref, sh2_ref, sc2_ref, lg_ref, lb_ref,
                    wrt_ref, rb_ref, x1_ref, ha_ref, hb_ref, ek_ref, rk_ref, wk_ref, cnt_ref, carry_ref):
    y = _dot(a_ref[0], wa_ref[...]) + _dot(r_ref[0], wr_ref[...])
    x1 = _ln_rows(DEEPNORM_ALPHA * x_ref[0] + g1_ref[0] * y) * lg_ref[...] + lb_ref[...]
    x1_ref[0] = x1
    h = _ln_rows(x1) * (1.0 + sc2_ref[0]) + sh2_ref[0]
    for ref, part in zip((ha_ref, hb_ref), _pack_rows(h)):
        ref[0] = part
    s = _sigmoid(_dot_nt(wrt_ref[...], h, HIGHEST))
    sb = s + rb_ref[...]
    tm = s.shape[1]
    per = N_EXPERTS // N_GROUPS
    sub = lax.broadcasted_iota(jnp.int32, (per, tm), 0)
    gs_rows = []
    for g in range(N_GROUPS):
        blk = sb[g * per:(g + 1) * per, :]
        m1 = jnp.max(blk, axis=0, keepdims=True)
        first = jnp.min(jnp.where(blk == m1, sub, per), axis=0, keepdims=True)
        m2 = jnp.max(jnp.where(sub == first, -jnp.inf, blk), axis=0, keepdims=True)
        gs_rows.append(m1 + m2)
    gscore = jnp.concatenate(gs_rows, axis=0)
    _, gtaken = _top_rows(gscore, TOPK_GROUPS)
    emask = jnp.concatenate([jnp.broadcast_to(gtaken[g:g + 1, :], (per, tm)) for g in range(N_GROUPS)], axis=0)
    masked = jnp.where(emask > 0, sb, -jnp.inf)
    firsts, taken = _top_rows(masked, TOP_K)
    first_step = (pl.program_id(0) == 0) & (pl.program_id(1) == 0)

    @pl.when(first_step)
    def _():
        carry_ref[...] = jnp.zeros(carry_ref.shape, F32)

    sel_f = taken.astype(F32)
    ti = lax.broadcasted_iota(jnp.int32, (tm, tm), 0)
    tj = lax.broadcasted_iota(jnp.int32, (tm, tm), 1)
    rank = _dot(sel_f.astype(BF16), (ti < tj).astype(BF16)) + carry_ref[:, 0:1]
    carry = carry_ref[...] + jnp.sum(sel_f, axis=1, keepdims=True)
    carry_ref[...] = carry
    cnt_ref[...] = carry
    eidx = lax.broadcasted_iota(jnp.int32, (N_EXPERTS, tm), 0)
    picked = []
    for k in range(TOP_K):
        hit = eidx == firsts[k]
        pick = lambda v: jnp.sum(jnp.where(hit, v, 0.0), axis=0, keepdims=True)
        ek_ref[k:k + 1, :] = firsts[k]
        rk_ref[k:k + 1, :] = pick(rank).astype(jnp.int32)
        picked.append(pick(s))
    total = picked[0]
    for k in range(1, TOP_K):
        total = total + picked[k]
    for k in range(TOP_K):
        wk_ref[k:k + 1, :] = picked[k] / total * ROUTED_SCALE


def _outproj(a_lat, r_lat, x, w_a, w_r, g1, sh2, sc2, ln_g, ln_b, w_router_t, router_bias, tm, b0):
    b, l, half = a_lat.shape
    d = x.shape[-1]
    nl = l // tm
    row = lambda i, j: (i, j, 0)
    per_b = lambda i, j: (i + b0, 0, 0)
    const = lambda i, j: (0, 0)
    return pl.pallas_call(
        _outproj_kernel,
        grid=(b, nl),
        in_specs=[pl.BlockSpec((1, tm, half), row),
                  pl.BlockSpec((1, tm, half), row),
                  pl.BlockSpec((1, tm, d), lambda i, j: (i + b0, j, 0)),
                  pl.BlockSpec((half, d), const),
                  pl.BlockSpec((half, d), const),
                  pl.BlockSpec((1, 1, d), per_b),
                  pl.BlockSpec((1, 1, d), per_b),
                  pl.BlockSpec((1, 1, d), per_b),
                  pl.BlockSpec((1, d), const),
                  pl.BlockSpec((1, d), const),
                  pl.BlockSpec((N_EXPERTS, d), const),
                  pl.BlockSpec((N_EXPERTS, 1), const)],
        out_specs=[pl.BlockSpec((1, tm, d), row),
                   pl.BlockSpec((1, tm, d // (2 * PACK_GROUPS)), row),
                   pl.BlockSpec((1, tm, d // (2 * PACK_GROUPS)), row),
                   pl.BlockSpec((TOP_K, tm), lambda i, j: (0, i * nl + j)),
                   pl.BlockSpec((TOP_K, tm), lambda i, j: (0, i * nl + j)),
                   pl.BlockSpec((TOP_K, tm), lambda i, j: (0, i * nl + j)),
                   pl.BlockSpec((N_EXPERTS, HEAD_DIM), const)],
        out_shape=[jax.ShapeDtypeStruct((b, l, d), F32),
                   jax.ShapeDtypeStruct((b, l, d // (2 * PACK_GROUPS)), jnp.int32),
                   jax.ShapeDtypeStruct((b, l, d // (2 * PACK_GROUPS)), jnp.int32),
                   jax.ShapeDtypeStruct((TOP_K, b * l), jnp.int32),
                   jax.ShapeDtypeStruct((TOP_K, b * l), jnp.int32),
                   jax.ShapeDtypeStruct((TOP_K, b * l), F32),
                   jax.ShapeDtypeStruct((N_EXPERTS, HEAD_DIM), F32)],
        scratch_shapes=[pltpu.VMEM((N_EXPERTS, HEAD_DIM), F32)],
        compiler_params=_params(("arbitrary", "arbitrary")),
        name="outproj_router",
    )(a_lat, r_lat, x, w_a, w_r, g1, sh2, sc2, ln_g.reshape(1, d), ln_b.reshape(1, d),
      w_router_t, router_bias.reshape(N_EXPERTS, 1))


def _pack_rows(x):
    n = x.shape[1] // PACK_GROUPS
    return [_pack_bf16_pair(x[:, g * n:(g + 1) * n]) for g in range(PACK_GROUPS)]


def _unpack_rows(parts):
    cols = []
    for p in parts:
        cols += list(_unpack_bf16_pair(p))
    return cols


def _glu_ffn(cols, w_gate_up, w_down, ff):
    n = cols[0].shape[1]
    ab = _dot(cols[0], w_gate_up[0:n])
    for i in range(1, len(cols)):
        ab = ab + _dot(cols[i], w_gate_up[i * n:(i + 1) * n])
    act = (_silu(ab[:, :ff]) * ab[:, ff:]).astype(BF16)
    return _dot(act, w_down)


def _expert_kernel(te_ref, used_ref, *refs, ff):
    tps = EXPERT_TILES_PER_STEP
    xs_refs, w_refs = refs[:PACK_GROUPS], refs[PACK_GROUPS:PACK_GROUPS + 2 * tps]
    ys_refs = refs[PACK_GROUPS + 2 * tps:2 * PACK_GROUPS + 2 * tps]
    wbf_refs = refs[2 * PACK_GROUPS + 2 * tps:]
    i = pl.program_id(0)
    used = used_ref[0]

    def run(n_active):
        for t in range(n_active):
            tile = i * tps + t
            prev = jnp.maximum(tile - tps, 0)

            @pl.when((i == 0) | (te_ref[tile] != te_ref[prev]))
            def _(t=t):
                wbf_refs[2 * t][...] = w_refs[2 * t][0].astype(BF16)
                wbf_refs[2 * t + 1][...] = w_refs[2 * t + 1][0].astype(BF16)

        sub = EXPERT_TILE // EXPERT_SUBTILES
        blocks = [(t, slice(t * EXPERT_TILE + s * sub, t * EXPERT_TILE + (s + 1) * sub))
                  for t in range(n_active) for s in range(EXPERT_SUBTILES)]
        wgu = [wbf_refs[2 * t][...] for t in range(n_active)]
        wd = [wbf_refs[2 * t + 1][...] for t in range(n_active)]
        cols = [_unpack_rows([r[b, :] for r in xs_refs]) for t, b in blocks]
        n = cols[0][0].shape[1]
        ab = [_dot(c[0], wgu[t][0:n]) for c, (t, b) in zip(cols, blocks)]
        for j in range(1, 2 * PACK_GROUPS):
            ab = [acc + _dot(c[j], wgu[t][j * n:(j + 1) * n]) for acc, c, (t, b) in zip(ab, cols, blocks)]
        act = [(_silu(a[:, :ff]) * a[:, ff:]).astype(BF16) for a in ab]
        y = [_dot(a, wd[t]) for a, (t, b) in zip(act, blocks)]
        for (t, b), y_b in zip(blocks, y):
            for ref, part in zip(ys_refs, _pack_rows(y_b)):
                ref[b, :] = part

    for n_active in range(1, tps + 1):
        full = (i * tps + n_active <= used) if n_active == tps else (i * tps + n_active == used)
        pl.when(full)(functools.partial(run, n_active))


def _expert_ffn(xs, tile_expert, used_tiles, w_gu, w_d):
    n_rows, dp = xs[0].shape
    n_e, d, ff2 = w_gu.shape
    ff = ff2 // 2
    tps = EXPERT_TILES_PER_STEP
    step_rows = tps * EXPERT_TILE
    assert n_rows % step_rows == 0
    rows_spec = pl.BlockSpec((step_rows, dp), lambda i, te, used: (jnp.minimum(i, (used[0] - 1) // tps), 0))
    w_specs = []
    for t in range(tps):
        w_specs += [pl.BlockSpec((1, d, ff2), lambda i, te, used, t=t: (te[i * tps + t], 0, 0)),
                    pl.BlockSpec((1, ff, d), lambda i, te, used, t=t: (te[i * tps + t], 0, 0))]
    grid_spec = pltpu.PrefetchScalarGridSpec(
        num_scalar_prefetch=2,
        grid=(n_rows // step_rows,),
        in_specs=[rows_spec] * PACK_GROUPS + w_specs,
        out_specs=[rows_spec] * PACK_GROUPS,
        scratch_shapes=[pltpu.VMEM((d, ff2), BF16), pltpu.VMEM((ff, d), BF16)] * tps)
    return pl.pallas_call(
        functools.partial(_expert_kernel, ff=ff),
        grid_spec=grid_spec,
        out_shape=[jax.ShapeDtypeStruct((n_rows, dp), jnp.int32)] * PACK_GROUPS,
        compiler_params=_params(("arbitrary",)),
        name="expert_ffn",
    )(tile_expert, used_tiles, *xs, *([w_gu, w_d] * tps))


def _combine_kernel(*refs, ff):
    h_refs, y_refs = refs[:PACK_GROUPS], refs[PACK_GROUPS:2 * PACK_GROUPS]
    x1_ref, wk_ref, wsg_ref, wsd_ref, g2_ref, lg_ref, lb_ref, o_ref = refs[2 * PACK_GROUPS:]
    shared = _glu_ffn(_unpack_rows([r[...] for r in h_refs]), wsg_ref[...], wsd_ref[...], ff)
    wk = wk_ref[...]
    n = y_refs[0].shape[2]
    blocks = [shared[:, i * n:(i + 1) * n] for i in range(2 * PACK_GROUPS)]
    for k in range(TOP_K):
        cols = _unpack_rows([r[k] for r in y_refs])
        blocks = [acc + wk[:, k:k + 1] * c.astype(F32) for acc, c in zip(blocks, cols)]
    f = jnp.concatenate(blocks, axis=1)
    o_ref[...] = _ln_rows(DEEPNORM_ALPHA * x1_ref[...] + g2_ref[0] * f) * lg_ref[...] + lb_ref[...]


def _moe_combine(h, x1, y_sel, wk_tok, w_sg, w_sd, g2, ln_g, ln_b, tm, b0, l, prev_out):
    t_part, d = x1.shape
    t = g2.shape[0] * l
    dp = h[0].shape[1]
    tiles_per_batch = l // tm
    off = b0 * tiles_per_batch
    ff = w_sd.shape[0]
    row = lambda i: (i, 0)
    const = lambda i: (0, 0)
    in_specs = ([pl.BlockSpec((tm, dp), row)] * PACK_GROUPS
                + [pl.BlockSpec((TOP_K, tm, dp), lambda i: (0, i, 0))] * PACK_GROUPS
                + [pl.BlockSpec((tm, d), row),
                   pl.BlockSpec((tm, TOP_K), row),
                   pl.BlockSpec(w_sg.shape, const),
                   pl.BlockSpec(w_sd.shape, const),
                   pl.BlockSpec((1, 1, d), lambda i: (i // tiles_per_batch + b0, 0, 0)),
                   pl.BlockSpec((1, d), const),
                   pl.BlockSpec((1, d), const)])
    args = [*h, *y_sel, x1, wk_tok, w_sg, w_sd, g2, ln_g.reshape(1, d), ln_b.reshape(1, d)]
    kern = functools.partial(_combine_kernel, ff=ff)
    aliases = {}
    if prev_out is not None:
        in_specs.append(pl.BlockSpec(memory_space=pl.ANY))
        aliases = {len(args): 0}
        args.append(prev_out)
        kern = lambda *refs: _combine_kernel(*refs[:-2], refs[-1], ff=ff)
    return pl.pallas_call(
        kern,
        grid=(t_part // tm,),
        in_specs=in_specs,
        out_specs=pl.BlockSpec((tm, d), lambda i: (i + off, 0)),
        out_shape=jax.ShapeDtypeStruct((t, d), F32),
        input_output_aliases=aliases,
        compiler_params=_params(("arbitrary",)),
        name="moe_combine",
    )(*args)


def _routing_tables(ek, rk, counts, n_rows):
    padded = (counts + EXPERT_TILE - 1) // EXPERT_TILE * EXPERT_TILE
    ends = jnp.cumsum(padded)
    starts = ends - padded
    experts = jnp.arange(N_EXPERTS, dtype=jnp.int32)
    start_of = jnp.sum(jnp.where(ek[None] == experts[:, None, None], starts[:, None, None], 0), axis=0)
    pos = start_of + rk
    tile_start = jnp.arange(n_rows // EXPERT_TILE, dtype=jnp.int32) * EXPERT_TILE
    tile_expert = jnp.minimum(jnp.sum(ends[None, :] <= tile_start[:, None], axis=1), N_EXPERTS - 1).astype(jnp.int32)
    used_tiles = (ends[-1:] // EXPERT_TILE).astype(jnp.int32)
    return pos.astype(jnp.int32), tile_expert, used_tiles


def _sc_mesh():
    return plsc.VectorSubcoreMesh(core_axis_name="core", subcore_axis_name="subcore")


def _dispatch_rows(rows, pos, n_rows):
    t, dp = rows.shape
    n_k = pos.shape[0]

    @functools.partial(pl.kernel, mesh=_sc_mesh(), out_type=jax.ShapeDtypeStruct((n_rows, dp), rows.dtype),
                       scratch_types=[])
    def scatter(rows_hbm, pos_hbm, out_hbm):
        def body(rows_vmem, idx_vmem):
            pltpu.sync_copy(rows_vmem, out_hbm.at[idx_vmem.at[0]])

        pltpu.emit_pipeline(
            body,
            grid=(t // SC_WINDOW, n_k),
            in_specs=[pl.BlockSpec((SC_WINDOW, dp), lambda i, k: (i, 0)),
                      pl.BlockSpec((1, SC_WINDOW), lambda i, k: (k, i))],
            out_specs=[],
            core_axis_name=("core", "subcore"),
            dimension_semantics=(pltpu.PARALLEL, pltpu.ARBITRARY),
        )(rows_hbm, pos_hbm)

    return scatter(rows, pos)


def _gather_rows(table, pos):
    n_k, t = pos.shape
    dp = table.shape[1]

    @functools.partial(pl.kernel, mesh=_sc_mesh(), out_type=jax.ShapeDtypeStruct((n_k * t, dp), table.dtype),
                       scratch_types=[])
    def gather(table_hbm, pos_hbm, out_hbm):
        def body(idx_vmem, out_vmem):
            pltpu.sync_copy(table_hbm.at[idx_vmem.at[0]], out_vmem)

        pltpu.emit_pipeline(
            body,
            grid=(n_k * t // SC_WINDOW,),
            in_specs=[pl.BlockSpec((1, SC_WINDOW), lambda i: (0, i))],
            out_specs=[pl.BlockSpec((SC_WINDOW, dp), lambda i: (i, 0))],
            core_axis_name=("core", "subcore"),
            dimension_semantics=(pltpu.PARALLEL,),
        )(pos_hbm, out_hbm)

    return gather(table, pos.reshape(1, n_k * t)).reshape(n_k, t, dp)


def _routed_moe(h, x1, ek, rk, wk, cnt, w_sg, w_sd, w_gu, w_d, g2, ln_g, ln_b, b0, l, prev_out):
    t = x1.shape[0]
    n_rows = t * TOP_K + N_EXPERTS * EXPERT_TILE
    pos, tile_expert, used_tiles = _routing_tables(ek, rk, cnt[:, 0].astype(jnp.int32), n_rows)
    xs = [_dispatch_rows(rows, pos, n_rows) for rows in h]
    ys = _expert_ffn(xs, tile_expert, used_tiles, w_gu, w_d)
    y_sel = [_gather_rows(table, pos) for table in ys]
    return _moe_combine(h, x1, y_sel, wk.T, w_sg, w_sd, g2, ln_g, ln_b, ROW_TILE, b0, l, prev_out)


def _gate_layouts(g_lat, g_ctx):
    g = jnp.concatenate([g_ctx, g_lat], axis=1)[..., :4 * GDN_HEADS]
    b, lt, _ = g.shape
    nt = lt // GDN_CHUNK
    rows = jnp.transpose(g.reshape(b, nt, GDN_CHUNK, 4, GDN_HEADS), (0, 4, 1, 3, 2))
    return jnp.pad(rows, ((0, 0), (0, 0), (0, 0), (0, 4), (0, 0))).reshape(b, GDN_HEADS, nt * 8, GDN_CHUNK)


def kernel(x, c, ctx, c_ctx, w_mod, b_mod, w_in, conv_w, gdn_a_log, gdn_dt_bias, gdn_norm_g, ret_log_gamma,
           ret_norm_g, w_out, ln1_g, ln1_b, w_router, router_bias, w_gate_up, w_down, w_shared_gate_up,
           w_shared_down, ln2_g, ln2_b):
    b, l, d = x.shape
    lc = ctx.shape[1]
    gw = GDN_HEADS * HEAD_DIM
    assert d == 2 * gw and w_in.shape[-1] == 8 * gw + 4 * GDN_HEADS and w_gate_up.shape[1] == N_EXPERTS
    assert b % BATCH_PARTS == 0 and l % ROW_TILE == 0 and l % RET_CHUNK == 0 and l % GRID_W == 0
    assert lc % RET_CHUNK == 0 and lc % HALO == 0 and (b // BATCH_PARTS * l) % EXPERT_TILE == 0

    n_mod = -(-(b + 1) // 8) * 8
    cc = jnp.zeros((n_mod, d), F32).at[:b].set(c).at[b].set(c_ctx)
    mod = _modulation(cc, w_mod[0], b_mod[0])
    sh1, sc1, g1, sh2, sc2, g2 = [mod[:b, i * d:(i + 1) * d].reshape(b, 1, d) for i in range(6)]
    csh1 = jnp.broadcast_to(mod[b, 0:d].reshape(1, 1, d), (b, 1, d))
    csc1 = jnp.broadcast_to(mod[b, d:2 * d].reshape(1, 1, d), (b, 1, d))

    w = w_in[0]
    o_gate = 3 * gw + gw
    o_ret = o_gate + 4 * GDN_HEADS
    w_main = jnp.concatenate([w[:, :o_gate], w[:, o_ret:]], axis=1).astype(BF16)
    w_gate = w[:, o_gate:o_ret]
    cos_t, sin_t = _rope_tables(l)
    wo = w_out[0].astype(BF16)
    w_sg, w_sd = w_shared_gate_up[0].astype(BF16), w_shared_down[0].astype(BF16)
    w_router_t = w_router[0].T

    nb = b // BATCH_PARTS
    out = None
    for part in range(BATCH_PARTS):
        b0 = part * nb
        qkv_lat, z_lat, ret_lat, g_lat = _inproj(x, sh1, sc1, w_main, w_gate, conv_w[0], PROJ_TILE, b0, nb)
        qkv_ctx, _, ret_ctx, g_ctx = _inproj(ctx, csh1, csc1, w_main, w_gate, conv_w[0], lc, b0, nb)
        gate_rows = _gate_layouts(g_lat, g_ctx)
        a_lat = _gdn_mixer(qkv_lat, qkv_ctx, z_lat, gate_rows, gdn_a_log[0],
                           gdn_dt_bias[0], gdn_norm_g[0], GDN_HEADS_PER_STEP)
        r_lat = _ret_mixer(ret_lat, ret_ctx, ret_log_gamma[0], ret_norm_g[0], cos_t, sin_t)
        x1, h_a, h_b, ek, rk, wk, cnt = _outproj(a_lat, r_lat, x, wo[:gw], wo[gw:], g1, sh2, sc2, ln1_g[0], ln1_b[0],
                                                 w_router_t, router_bias[0], ROW_TILE, b0)
        h = [arr.reshape(nb * l, arr.shape[-1]) for arr in (h_a, h_b)]
        out = _routed_moe(h, x1.reshape(nb * l, d), ek, rk, wk, cnt, w_sg, w_sd, w_gate_up[0], w_down[0],
                          g2, ln2_g[0], ln2_b[0], b0, l, out)
    return out.reshape(b, l, d)
```

```python
import functools

import jax
import jax.numpy as jnp
from jax import lax
from jax.experimental import pallas as pl
from jax.experimental.pallas import tpu as pltpu
from jax.experimental.pallas import tpu_sc as plsc

F32 = jnp.float32
BF16 = jnp.bfloat16

HEAD_DIM = 128
GDN_HEADS = 4
RET_HEADS = 4
CONV_W = 5
HALO = 16
GDN_CHUNK = 64
INV_BASE = 16
RET_CHUNK = 256
GRID_W = 64
ROPE_THETA = 10000.0
N_EXPERTS = 64
TOP_K = 8
N_GROUPS = 8
TOPK_GROUPS = 4
ROUTED_SCALE = 2.5
SC_WINDOW = 128
PACK_GROUPS = 2
ROW_TILE = 512
PROJ_TILE = 1024
GDN_HEADS_PER_STEP = 2
GDN_SLOTS_PER_GROUP = 6
BATCH_PARTS = 2
EXPERT_TILES_PER_STEP = 2
EXPERT_SUBTILES = 4
EXPERT_TILE = 512
LN_EPS = 1e-6
DEPTH = 1
DEEPNORM_ALPHA = (2 * DEPTH) ** 0.25

VMEM_LIMIT = 56 * 1024 * 1024
HIGHEST = lax.Precision.HIGHEST
NT_DIMS = (((1,), (1,)), ((), ()))


def _dot(a, b, precision=None):
    return jnp.dot(a, b, preferred_element_type=F32, precision=precision)


def _dot_nt(a, b, precision=None):
    return lax.dot_general(a, b, NT_DIMS, preferred_element_type=F32, precision=precision)


def _silu(x):
    return x * (1.0 / (1.0 + jnp.exp(-x)))


def _sigmoid(x):
    return 1.0 / (1.0 + jnp.exp(-x))


def _softplus(x):
    return jnp.maximum(x, 0.0) + jnp.log(1.0 + jnp.exp(-jnp.abs(x)))


def _ln_rows(x):
    mu = jnp.mean(x, axis=-1, keepdims=True)
    xc = x - mu
    var = jnp.mean(xc * xc, axis=-1, keepdims=True)
    return xc * lax.rsqrt(var + LN_EPS)


def _params(sem):
    return pltpu.CompilerParams(dimension_semantics=sem, vmem_limit_bytes=VMEM_LIMIT)


def _mod_kernel(c_ref, w_ref, b_ref, o_ref):
    o_ref[...] = _dot(_silu(c_ref[...]), w_ref[...], HIGHEST) + b_ref[...]


def _modulation(cc, w_mod, b_mod):
    rows, d = cc.shape
    n = w_mod.shape[1]
    tn = 1024
    return pl.pallas_call(
        _mod_kernel,
        grid=(n // tn,),
        in_specs=[pl.BlockSpec((rows, d), lambda j: (0, 0)),
                  pl.BlockSpec((d, tn), lambda j: (0, j)),
                  pl.BlockSpec((1, tn), lambda j: (0, j))],
        out_specs=pl.BlockSpec((rows, tn), lambda j: (0, j)),
        out_shape=jax.ShapeDtypeStruct((rows, n), F32),
        compiler_params=_params(("arbitrary",)),
        name="modulation",
    )(cc, w_mod, b_mod.reshape(1, n))


def _inproj_kernel(x_ref, xp_ref, xn_ref, sh_ref, sc_ref, wm_ref, wg_ref, cw_ref, qkv_ref, z_ref, ret_ref, g_ref,
                   p_ref):
    j, nl = pl.program_id(1), pl.num_programs(1)
    tm = x_ref.shape[1]
    modulate = lambda x: _ln_rows(x) * (1.0 + sc_ref[0]) + sh_ref[0]
    h = modulate(x_ref[0])
    hb = h.astype(BF16)
    h_prev = jnp.where(j > 0, modulate(xp_ref[0]), 0.0).astype(BF16)
    h_next = jnp.where(j < nl - 1, modulate(xn_ref[0]), 0.0).astype(BF16)
    h_ext = jnp.concatenate([h_prev, hb, h_next], axis=0)
    half = CONV_W // 2
    gw = GDN_HEADS * HEAD_DIM
    for part in range(3):
        p_ref[part] = _dot(h_ext, wm_ref[:, part * gw:(part + 1) * gw])

    def conv_part(part):
        c0 = part * gw
        p = p_ref[part]
        acc = p[HALO - half:HALO - half + tm, :] * cw_ref[0:1, c0:c0 + gw]
        for tap in range(1, CONV_W):
            acc = acc + p[HALO - half + tap:HALO - half + tap + tm, :] * cw_ref[tap:tap + 1, c0:c0 + gw]
        y = _silu(acc)
        if part < 2:
            scale = HEAD_DIM ** -0.5 if part == 0 else 1.0
            blocks = [y[:, hd * HEAD_DIM:(hd + 1) * HEAD_DIM] for hd in range(GDN_HEADS)]
            blocks = [blk * (lax.rsqrt(jnp.sum(blk * blk, axis=-1, keepdims=True) + LN_EPS) * scale) for blk in blocks]
            y = jnp.concatenate(blocks, axis=1)
        qkv_ref[0, :, c0:c0 + gw] = y.astype(BF16)

    rest = [(ref, n0) for ref in (z_ref, ret_ref) for n0 in range(0, ref.shape[-1], 512)]
    col = 3 * gw
    for idx, (ref, n0) in enumerate(rest):
        if idx < 3:
            conv_part(idx)
        ref[0, :, n0:n0 + 512] = _dot(hb, wm_ref[:, col:col + 512]).astype(BF16)
        col += 512
    h_lo = (h - hb.astype(F32)).astype(BF16)
    wg = wg_ref[...]
    wg_hi = wg.astype(BF16)
    wg_lo = (wg - wg_hi.astype(F32)).astype(BF16)
    g_ref[0] = (_dot(h_lo, wg_hi) + _dot(hb, wg_lo)) + _dot(hb, wg_hi)


def _inproj(x, sh, sc, w_main, w_gate, conv_w, tm, b0, b):
    _, l, d = x.shape
    widths = (3 * GDN_HEADS * HEAD_DIM, GDN_HEADS * HEAD_DIM, 4 * RET_HEADS * HEAD_DIM)
    n_gate = w_gate.shape[1]
    per_tile, last = tm // HALO, l // HALO - 1
    row = lambda i, j: (i, j, 0)
    const = lambda i, j: (0, 0)
    return pl.pallas_call(
        _inproj_kernel,
        grid=(b, l // tm),
        in_specs=[pl.BlockSpec((1, tm, d), lambda i, j: (i + b0, j, 0)),
                  pl.BlockSpec((1, HALO, d), lambda i, j: (i + b0, jnp.maximum(j * per_tile - 1, 0), 0)),
                  pl.BlockSpec((1, HALO, d), lambda i, j: (i + b0, jnp.minimum((j + 1) * per_tile, last), 0)),
                  pl.BlockSpec((1, 1, d), lambda i, j: (i + b0, 0, 0)),
                  pl.BlockSpec((1, 1, d), lambda i, j: (i + b0, 0, 0)),
                  pl.BlockSpec(w_main.shape, const),
                  pl.BlockSpec(w_gate.shape, const),
                  pl.BlockSpec(conv_w.shape, const)],
        out_specs=[pl.BlockSpec((1, tm, w), row) for w in widths]
                  + [pl.BlockSpec((1, tm, n_gate), row)],
        out_shape=[jax.ShapeDtypeStruct((b, l, w), BF16) for w in widths]
                  + [jax.ShapeDtypeStruct((b, l, n_gate), F32)],
        scratch_shapes=[pltpu.VMEM((3, tm + 2 * HALO, GDN_HEADS * HEAD_DIM), F32)],
        compiler_params=_params(("arbitrary", "arbitrary")),
        name="inproj",
    )(x, x, x, sh, sc, w_main, w_gate, conv_w)


def _bdot(a, b):
    return _dot(a.astype(BF16), b.astype(BF16))


def _pack_bf16_pair(x):
    n = x.shape[1] // 2
    bits = lambda v: pltpu.bitcast(v.astype(BF16).astype(F32), jnp.uint32)
    word = lax.shift_right_logical(bits(x[:, :n]), jnp.uint32(16)) | (bits(x[:, n:]) & jnp.uint32(0xFFFF0000))
    return pltpu.bitcast(word, jnp.int32)


def _unpack_bf16_pair(w):
    u = pltpu.bitcast(w, jnp.uint32)
    lo = pltpu.bitcast(lax.shift_left(u, jnp.uint32(16)), F32)
    hi = pltpu.bitcast(u & jnp.uint32(0xFFFF0000), F32)
    return lo.astype(BF16), hi.astype(BF16)


def _split3(x):
    hi = x.astype(BF16)
    r1 = x - hi.astype(F32)
    mid = r1.astype(BF16)
    lo = (r1 - mid.astype(F32)).astype(BF16)
    return hi, mid, lo


def _dot_split_lhs(x, mask01):
    m = mask01.astype(BF16)
    hi, mid, lo = _split3(x)
    return (_dot(lo, m) + _dot(mid, m)) + _dot(hi, m)


def _each(fn, *lists):
    return [fn(*args) for args in zip(*lists)]


def _inv_unit_triangular(mats, base=INV_BASE):
    n = mats[0].shape[0]
    i = lax.broadcasted_iota(jnp.int32, (n, n), 0)
    j = lax.broadcasted_iota(jnp.int32, (n, n), 1)
    eye = (i == j).astype(F32)
    shift = base.bit_length() - 1
    inside = (i >> shift) == (j >> shift)
    xs = _each(lambda a: jnp.where(inside, a, 0.0), mats)
    ts = _each(lambda d: eye - d, xs)
    for _ in range(shift - 1):
        xs = _each(lambda x: _bdot(x, x), xs)
        yield
        ts = _each(lambda t, x: t + _bdot(t, x), ts, xs)
        yield
    size = base
    while size < GDN_CHUNK:
        shift += 1
        size *= 2
        wider = (i >> shift) == (j >> shift)
        off = wider & jnp.logical_not(inside)
        ots = _each(lambda a, t: _bdot(jnp.where(off, a, 0.0), t), mats, ts)
        yield
        ts = _each(lambda t, ot: t - _bdot(t, ot), ts, ots)
        yield
        inside = wider
    return ts


def _gdn_kernel(alog_ref, dtb_ref,
                q_ref, k_ref, v_ref, qc_ref, kc_ref, vc_ref,
                gt_ref, z_ref, ng_ref,
                o_ref,
                qs_ref, ks_ref, vs_ref, gts_ref, gth_ref,
                wqf_ref, wqb_ref, u0_ref, intra_ref, ket_ref, gef_ref, geb_ref,
                of_ref, ob_ref, *, hb, l_lat, l_ctx, slots_per_group):
    c = GDN_CHUNK
    c2 = 2 * c
    lt = l_lat + l_ctx
    n_ctx = l_ctx // c
    nt = lt // c
    hgrp = pl.program_id(1)

    i2 = lax.broadcasted_iota(jnp.int32, (c2, c2), 0)
    j2 = lax.broadcasted_iota(jnp.int32, (c2, c2), 1)
    same_blk = (i2 & c) == (j2 & c)
    sgn_i = jnp.where(i2 < c, 1, -1)
    sgn_j = jnp.where(j2 < c, 1, -1)
    incl = same_blk & ((j2 - i2) * sgn_i <= 0)
    strict = same_blk & ((j2 - i2) * sgn_i < 0)
    tri_row = (same_blk & ((i2 - j2) * sgn_j <= 0)).astype(F32)
    eye_m = i2 == j2
    eye_b = eye_m.astype(BF16)
    lane_lt_c = j2 < c

    for hh in range(hb):
        head = hgrp * hb + hh
        lane0 = hh * HEAD_DIM
        for (src_c, src_l, dst) in ((qc_ref, q_ref, qs_ref), (kc_ref, k_ref, ks_ref), (vc_ref, v_ref, vs_ref)):
            dst[hh, 0:l_ctx, :] = src_c[0, :, lane0:lane0 + HEAD_DIM]
            dst[hh, l_ctx:lt, :] = src_l[0, :, lane0:lane0 + HEAD_DIM]

        a_f, a_b = alog_ref[head], alog_ref[GDN_HEADS + head]
        d_f, d_b = dtb_ref[head], dtb_ref[GDN_HEADS + head]
        compr = lax.broadcasted_iota(jnp.int32, (nt * 8, 1), 0) & 7
        neg_ar = -jnp.exp(jnp.where(compr == 2, a_f, a_b))
        dtbr = jnp.where(compr == 2, d_f, d_b)
        grow_raw = gt_ref[0, hh]
        grow = jnp.where(compr < 2, _sigmoid(grow_raw), neg_ar * _softplus(grow_raw + dtbr))
        gts_ref[hh] = jnp.zeros(gts_ref.shape[1:], F32)
        gts_ref[hh, :, 0:c] = grow
        gth_ref[hh] = jnp.zeros(gth_ref.shape[1:], F32)
        gth_ref[hh, :, c:c2] = grow

    spg = slots_per_group
    heads = list(range(hb))

    def bwd_chunk(i):
        return jnp.where(i < n_ctx, n_ctx - 1 - i, nt + n_ctx - 1 - i)

    def slot_load(i, hh):
        cbk = bwd_chunk(i)
        rf = pl.multiple_of(i * c, c)
        rb = pl.multiple_of(cbk * c, c)
        two = lambda ref: jnp.concatenate([ref[hh, pl.ds(rf, c), :], ref[hh, pl.ds(rb, c), :]], axis=0).astype(F32)
        g_r = (gts_ref[hh, pl.ds(pl.multiple_of(i * 8, 8), 8), :]
               + gth_ref[hh, pl.ds(pl.multiple_of(cbk * 8, 8), 8), :])
        return two(ks_ref), two(vs_ref), two(qs_ref), g_r

    def slot_stages(loaded):
        k2, v2, q2, g_r = [list(col) for col in zip(*loaded)]
        rcs = _each(lambda g: _dot_split_lhs(g, tri_row), g_r)
        yield
        k2b = _each(lambda x: x.astype(BF16), k2)
        kk = _each(_dot_nt, k2b, k2b)
        qk = _each(lambda q, kb: _dot_nt(q.astype(BF16), kb), q2, k2b)
        yield
        by_dir = lambda g, k: jnp.where(lane_lt_c[0:1, :], g[k:k + 1, :], g[k + 1:k + 2, :])
        lane_sum = lambda mask, row: jnp.sum(jnp.where(mask, row, 0.0), axis=1, keepdims=True)
        gcc = _each(lambda g: lane_sum(incl, by_dir(g, 2)), g_r)
        gcr = _each(lambda r: by_dir(r, 2), rcs)
        beta = _each(lambda g: lane_sum(eye_m, by_dir(g, 0)), g_r)
        gend = _each(lambda s: jnp.concatenate([jnp.broadcast_to(s[c - 1:c, :], (c, 1)),
                                                jnp.broadcast_to(s[c:c + 1, :], (c, 1))], axis=0), gcc)
        decay = _each(lambda gc, gr: jnp.exp(jnp.where(incl, gc - gr, -jnp.inf)), gcc, gcr)
        a_mat = _each(lambda m, d, b: jnp.where(strict, m * d, 0.0) * b, kk, decay, beta)
        t_f32 = yield from _inv_unit_triangular(a_mat)
        t_mat = _each(lambda t: t.astype(BF16), t_f32)
        egc = _each(jnp.exp, gcc)
        u0 = _each(lambda t, v, b: _dot(t, (v * b).astype(BF16)), t_mat, v2, beta)
        w = _each(lambda t, k, b, e: _dot(t, (k * (b * e)).astype(BF16)), t_mat, k2, beta, egc)
        ket = _each(lambda k, ge_, gc: _dot_nt(eye_b, (k * jnp.exp(ge_ - gc)).astype(BF16)), k2, gend, gcc)
        yield
        qd = _each(lambda q, e: (q * e).astype(BF16), q2, egc)
        intra = _each(lambda m, d: (m * d).astype(BF16), qk, decay)
        ge = _each(jnp.exp, gend)
        return [(jnp.concatenate([w_[0:c].astype(BF16), qd_[0:c]], axis=0),
                 jnp.concatenate([w_[c:c2].astype(BF16), qd_[c:c2]], axis=0), u0_, in_,
                 jnp.concatenate([jnp.where(lane_lt_c, kt, 0.0), jnp.where(lane_lt_c, 0.0, kt)], axis=0).astype(BF16),
                 jnp.broadcast_to(g_[0:1, :], (8, HEAD_DIM)), jnp.broadcast_to(g_[c:c + 1, :], (8, HEAD_DIM)))
                for w_, qd_, u0_, in_, kt, g_ in zip(w, qd, u0, intra, ket, ge)]

    def transform_group(g):
        par = lax.rem(g, 2)
        jobs = [(s, hh) for s in range(spg) for hh in heads]
        results = yield from slot_stages([slot_load(g * spg + s, hh) for s, hh in jobs])
        for (s, hh), (wq_f, wq_b, u0, intra, ket, ge_f, ge_b) in zip(jobs, results):
            wqf_ref[par, hh, s] = wq_f
            wqb_ref[par, hh, s] = wq_b
            u0_ref[par, hh, s] = u0
            intra_ref[par, hh, s] = intra
            ket_ref[par, hh, s] = ket
            gef_ref[par, hh, s] = ge_f
            geb_ref[par, hh, s] = ge_b

    def recur_group(g, states):
        par = lax.rem(g, 2)
        sf, sb = list(states[0::2]), list(states[1::2])
        for s in range(spg):
            i = g * spg + s
            rf = pl.multiple_of(i * c, c)
            rb = pl.multiple_of(bwd_chunk(i) * c, c)
            r_f = _each(lambda hh, st: _dot(wqf_ref[par, hh, s], st.astype(BF16)), heads, sf)
            r_b = _each(lambda hh, st: _dot(wqb_ref[par, hh, s], st.astype(BF16)), heads, sb)
            u0 = _each(lambda hh: u0_ref[par, hh, s], heads)
            yield
            u2 = _each(lambda u, f, b_: jnp.concatenate([u[0:c] - f[0:c], u[c:c2] - b_[0:c]], axis=0).astype(BF16),
                       u0, r_f, r_b)
            ket = _each(lambda hh: ket_ref[par, hh, s], heads)
            df = _each(lambda kt, u: _dot(kt[0:HEAD_DIM], u), ket, u2)
            db = _each(lambda kt, u: _dot(kt[HEAD_DIM:2 * HEAD_DIM], u), ket, u2)
            iu = _each(lambda hh, u: _dot(intra_ref[par, hh, s], u), heads, u2)
            sf = _each(lambda hh, st, d: gef_ref[par, hh, s][0:1, :] * st + d, heads, sf, df)
            sb = _each(lambda hh, st, d: geb_ref[par, hh, s][0:1, :] * st + d, heads, sb, db)
            for hh in heads:
                of_ref[hh, pl.ds(rf, c), :] = r_f[hh][c:c2] + iu[hh][0:c]
                ob_ref[hh, pl.ds(rb, c), :] = r_b[hh][c:c2] + iu[hh][c:c2]
            yield
        return tuple(x for pair in zip(sf, sb) for x in pair)

    def drive(transform, recur, transforms_per_recur=1):
        states = None
        while transform is not None or recur is not None:
            if transform is not None:
                for _ in range(transforms_per_recur):
                    try:
                        next(transform)
                    except StopIteration:
                        transform = None
                        break
            if recur is not None:
                try:
                    next(recur)
                except StopIteration as stop:
                    states, recur = stop.value, None
        return states

    n_groups = nt // spg
    zero_state = tuple(jnp.zeros((HEAD_DIM, HEAD_DIM), F32) for _ in range(2 * hb))
    drive(transform_group(0), None)
    states = lax.fori_loop(0, n_groups - 1,
                           lambda g, st: drive(transform_group(g + 1), recur_group(g, st)), zero_state)

    ng = ng_ref[...]
    blk = 256

    def finish_block(hh, b):
        r, lane0 = b * blk, hh * HEAD_DIM
        o = of_ref[hh, l_ctx + r:l_ctx + r + blk, :] + ob_ref[hh, l_ctx + r:l_ctx + r + blk, :]
        y = o * lax.rsqrt(jnp.mean(o * o, axis=-1, keepdims=True) + LN_EPS) * ng
        z = z_ref[0, r:r + blk, lane0:lane0 + HEAD_DIM].astype(F32)
        o_ref[0, r:r + blk, lane0:lane0 + HEAD_DIM] = (y * _silu(z)).astype(BF16)

    def finish_blocks(jobs):
        for hh, b in jobs:
            finish_block(hh, b)
            yield

    first = -(-(spg * c) // blk)
    last = (nt - spg - n_ctx) * c // blk
    early = [(hh, b) for b in range(first, last) for hh in heads]
    late = [(hh, b) for b in range(l_lat // blk) if not first <= b < last for hh in heads]
    drive(finish_blocks(early), recur_group(n_groups - 1, states))
    drive(finish_blocks(late), None)


def _gdn_mixer(qkv_lat, qkv_ctx, z_lat, gate_rows, a_log, dt_bias, norm_g, hb):
    b, l_lat, _ = qkv_lat.shape
    l_ctx = qkv_ctx.shape[1]
    lt = l_lat + l_ctx
    nt = lt // GDN_CHUNK
    hw = hb * HEAD_DIM
    ngrp = GDN_HEADS // hb
    c2 = 2 * GDN_CHUNK
    spg = GDN_SLOTS_PER_GROUP
    assert nt % spg == 0 and nt // spg >= 2 and GDN_HEADS % hb == 0

    def seq_spec(length, part):
        return pl.BlockSpec((1, length, hw), lambda i, j, *_: (i, 0, part * ngrp + j))

    grid_spec = pltpu.PrefetchScalarGridSpec(
        num_scalar_prefetch=2,
        grid=(b, ngrp),
        in_specs=[seq_spec(l_lat, 0), seq_spec(l_lat, 1), seq_spec(l_lat, 2),
                  seq_spec(l_ctx, 0), seq_spec(l_ctx, 1), seq_spec(l_ctx, 2),
                  pl.BlockSpec((1, hb, nt * 8, GDN_CHUNK), lambda i, j, *_: (i, j, 0, 0)),
                  pl.BlockSpec((1, l_lat, hw), lambda i, j, *_: (i, 0, j)),
                  pl.BlockSpec((1, HEAD_DIM), lambda i, j, *_: (0, 0))],
        out_specs=pl.BlockSpec((1, l_lat, hw), lambda i, j, *_: (i, 0, j)),
        scratch_shapes=[
            pltpu.VMEM((hb, lt, HEAD_DIM), BF16),
            pltpu.VMEM((hb, lt, HEAD_DIM), BF16),
            pltpu.VMEM((hb, lt, HEAD_DIM), BF16),
            pltpu.VMEM((hb, nt * 8, c2), F32),
            pltpu.VMEM((hb, nt * 8, c2), F32),
            pltpu.VMEM((2, hb, spg, c2, HEAD_DIM), BF16),
            pltpu.VMEM((2, hb, spg, c2, HEAD_DIM), BF16),
            pltpu.VMEM((2, hb, spg, c2, HEAD_DIM), F32),
            pltpu.VMEM((2, hb, spg, c2, c2), BF16),
            pltpu.VMEM((2, hb, spg, 2 * HEAD_DIM, c2), BF16),
            pltpu.VMEM((2, hb, spg, 8, HEAD_DIM), F32),
            pltpu.VMEM((2, hb, spg, 8, HEAD_DIM), F32),
            pltpu.VMEM((hb, lt, HEAD_DIM), F32),
            pltpu.VMEM((hb, lt, HEAD_DIM), F32),
        ])
    kern = functools.partial(_gdn_kernel, hb=hb, l_lat=l_lat, l_ctx=l_ctx, slots_per_group=spg)
    return pl.pallas_call(
        kern,
        grid_spec=grid_spec,
        out_shape=jax.ShapeDtypeStruct((b, l_lat, GDN_HEADS * HEAD_DIM), BF16),
        compiler_params=_params(("arbitrary", "arbitrary")),
        name="gdn_mixer",
    )(a_log, dt_bias, qkv_lat, qkv_lat, qkv_lat, qkv_ctx, qkv_ctx, qkv_ctx,
      gate_rows, z_lat, norm_g.reshape(1, HEAD_DIM))


def _ret_kernel(lg_ref, q_ref, k_ref, v_ref, gate_ref, qc_ref, kc_ref, vc_ref, cos_ref, sin_ref, ng_ref,
                o_ref, q_s, k_s, rf_s, rb_s, *, l_lat, l_ctx, hb):
    c = RET_CHUNK
    n_lat = l_lat // c
    n_ctx = l_ctx // c
    heads = list(range(hb))
    lanes = [slice(hh * HEAD_DIM, (hh + 1) * HEAD_DIM) for hh in heads]
    pos_c = lax.broadcasted_iota(jnp.int32, (c, 1), 0).astype(F32)
    ii = lax.broadcasted_iota(jnp.int32, (c, c), 0)
    jj = lax.broadcasted_iota(jnp.int32, (c, c), 1)
    dif = (ii - jj).astype(F32)
    e2 = lax.broadcasted_iota(jnp.int32, (HEAD_DIM, HEAD_DIM), 0)
    f2 = lax.broadcasted_iota(jnp.int32, (HEAD_DIM, HEAD_DIM), 1)
    eye_b = (e2 == f2).astype(BF16)
    kscale = HEAD_DIM ** -0.5
    lg_f = [lg_ref[pl.program_id(1) * hb + hh] for hh in heads]
    lg_b = [lg_ref[RET_HEADS + pl.program_id(1) * hb + hh] for hh in heads]
    dmat = _each(lambda f, b_: (jnp.exp(jnp.where(ii >= jj, f * dif, -jnp.inf))
                                + jnp.exp(jnp.where(jj >= ii, -b_ * dif, -jnp.inf))), lg_f, lg_b)
    kdec_f = _each(lambda f: jnp.exp(f * (c - 1 - pos_c)), lg_f)
    kdec_b = _each(lambda b_: jnp.exp(b_ * pos_c), lg_b)
    qdec_f = _each(lambda f: jnp.exp(f * (pos_c + 1.0)), lg_f)
    qdec_b = _each(lambda b_: jnp.exp(b_ * (c - pos_c)), lg_b)
    cd_f = _each(lambda f: jnp.exp(jnp.full((1, HEAD_DIM), f * c, F32)), lg_f)
    cd_b = _each(lambda b_: jnp.exp(jnp.full((1, HEAD_DIM), b_ * c, F32)), lg_b)

    def transposed(kd):
        return _dot_nt(eye_b, kd.astype(BF16)).astype(BF16)

    r_f = [jnp.zeros((HEAD_DIM, HEAD_DIM), F32) for _ in heads]
    r_b = [jnp.zeros((HEAD_DIM, HEAD_DIM), F32) for _ in heads]
    for n in range(n_ctx):
        m = n_ctx - 1 - n
        kf = _each(lambda ln, d: transposed(kc_ref[0, n * c:(n + 1) * c, ln].astype(F32) * kscale * d), lanes, kdec_f)
        kb = _each(lambda ln, d: transposed(kc_ref[0, m * c:(m + 1) * c, ln].astype(F32) * kscale * d), lanes, kdec_b)
        pf = _each(lambda kt, ln: _dot(kt, vc_ref[0, n * c:(n + 1) * c, ln]), kf, lanes)
        pb = _each(lambda kt, ln: _dot(kt, vc_ref[0, m * c:(m + 1) * c, ln]), kb, lanes)
        r_f = _each(lambda d, r, p: d * r + p, cd_f, r_f, pf)
        r_b = _each(lambda d, r, p: d * r + p, cd_b, r_b, pb)

    def rope_body(n, carry):
        r = pl.multiple_of(n * c, c)
        cs = cos_ref[pl.ds(r, c), :]
        sn = sin_ref[pl.ds(r, c), :]
        for hh, ln in zip(heads, lanes):
            q = q_ref[0, pl.ds(r, c), ln].astype(F32)
            k = k_ref[0, pl.ds(r, c), ln].astype(F32) * kscale
            q_s[hh, pl.ds(r, c), :] = q * cs + pltpu.roll(q, HEAD_DIM // 2, 1) * sn
            k_s[hh, pl.ds(r, c), :] = k * cs + pltpu.roll(k, HEAD_DIM // 2, 1) * sn
        return carry

    lax.fori_loop(0, n_lat, rope_body, 0)

    chunks = list(range(n_lat))
    rows = [slice(n * c, (n + 1) * c) for n in chunks]
    jobs = [(hh, n) for hh in heads for n in chunks]
    k_c = [k_s[hh, rows[n], :] for hh, n in jobs]
    kft = [transposed(k * kdec_f[hh]) for k, (hh, n) in zip(k_c, jobs)]
    kbt = [transposed(k * kdec_b[hh]) for k, (hh, n) in zip(k_c, jobs)]
    kvf = [_dot(kt, v_ref[0, rows[n], lanes[hh]]) for kt, (hh, n) in zip(kft, jobs)]
    kvb = [_dot(kt, v_ref[0, rows[n], lanes[hh]]) for kt, (hh, n) in zip(kbt, jobs)]
    for hh in heads:
        r = r_f[hh]
        for n in chunks:
            rf_s[hh, n] = r
            r = cd_f[hh] * r + kvf[hh * n_lat + n]
        r = r_b[hh]
        for n in reversed(chunks):
            rb_s[hh, n] = r
            r = cd_b[hh] * r + kvb[hh * n_lat + n]

    group = 4
    for g0 in range(0, len(jobs), group):
        part = jobs[g0:g0 + group]
        q_c = [q_s[hh, rows[n], :] for hh, n in part]
        s = [_dot_nt(q.astype(BF16), k_s[hh, rows[n], :].astype(BF16)) for q, (hh, n) in zip(q_c, part)]
        att = [(s_ * dmat[hh]).astype(BF16) for s_, (hh, n) in zip(s, part)]
        o = [_dot(a, v_ref[0, rows[n], lanes[hh]]) for a, (hh, n) in zip(att, part)]
        qd = [jnp.concatenate([(q * qdec_f[hh]).astype(BF16), (q * qdec_b[hh]).astype(BF16)], axis=1)
              for q, (hh, n) in zip(q_c, part)]
        st = [jnp.concatenate([rf_s[hh, n], rb_s[hh, n]], axis=0).astype(BF16) for hh, n in part]
        o = _each(lambda o_, qd_, st_: o_ + _dot(qd_, st_), o, qd, st)
        for o_, (hh, n) in zip(o, part):
            y = _ln_rows(o_) * ng_ref[:, lanes[hh]]
            g = gate_ref[0, rows[n], lanes[hh]].astype(F32)
            o_ref[0, rows[n], lanes[hh]] = (y * _silu(g)).astype(BF16)


def _ret_mixer(ret_lat, ret_ctx, log_gamma, norm_g, cos_t, sin_t):
    b, l_lat, _ = ret_lat.shape
    l_ctx = ret_ctx.shape[1]
    n_lat = l_lat // RET_CHUNK
    hb = RET_HEADS
    hw = hb * HEAD_DIM
    ngrp = RET_HEADS // hb

    def seq_spec(length, part):
        return pl.BlockSpec((1, length, hw), lambda i, j, *_: (i, 0, part * ngrp + j))

    grid_spec = pltpu.PrefetchScalarGridSpec(
        num_scalar_prefetch=1,
        grid=(b, ngrp),
        in_specs=[seq_spec(l_lat, 0), seq_spec(l_lat, 1), seq_spec(l_lat, 2), seq_spec(l_lat, 3),
                  seq_spec(l_ctx, 0), seq_spec(l_ctx, 1), seq_spec(l_ctx, 2),
                  pl.BlockSpec((l_lat, HEAD_DIM), lambda i, j, *_: (0, 0)),
                  pl.BlockSpec((l_lat, HEAD_DIM), lambda i, j, *_: (0, 0)),
                  pl.BlockSpec((1, hw), lambda i, j, *_: (0, j))],
        out_specs=pl.BlockSpec((1, l_lat, hw), lambda i, j, *_: (i, 0, j)),
        scratch_shapes=[pltpu.VMEM((hb, l_lat, HEAD_DIM), F32),
                        pltpu.VMEM((hb, l_lat, HEAD_DIM), F32),
                        pltpu.VMEM((hb, n_lat, HEAD_DIM, HEAD_DIM), F32),
                        pltpu.VMEM((hb, n_lat, HEAD_DIM, HEAD_DIM), F32)])
    kern = functools.partial(_ret_kernel, l_lat=l_lat, l_ctx=l_ctx, hb=hb)
    return pl.pallas_call(
        kern,
        grid_spec=grid_spec,
        out_shape=jax.ShapeDtypeStruct((b, l_lat, RET_HEADS * HEAD_DIM), BF16),
        compiler_params=_params(("arbitrary", "arbitrary")),
        name="ret_mixer",
    )(log_gamma.reshape(-1), ret_lat, ret_lat, ret_lat, ret_lat, ret_ctx, ret_ctx, ret_ctx,
      cos_t, sin_t, norm_g.reshape(1, -1))


def _rope_tables(l_lat):
    rows = l_lat // GRID_W
    row = jnp.repeat(jnp.arange(rows, dtype=F32), GRID_W)
    col = jnp.tile(jnp.arange(GRID_W, dtype=F32), rows)
    quarter = HEAD_DIM // 4
    inv = ROPE_THETA ** (-jnp.arange(quarter, dtype=F32) / quarter)
    ang = jnp.concatenate([row[:, None] * inv, col[:, None] * inv], -1)
    cos, sin = jnp.cos(ang), jnp.sin(ang)
    return jnp.concatenate([cos, cos], -1), jnp.concatenate([-sin, sin], -1)


def _top_rows(vals, k):
    n = vals.shape[0]
    idx = lax.broadcasted_iota(jnp.int32, vals.shape, 0)
    taken = jnp.zeros(vals.shape, jnp.int32)
    firsts = []
    for _ in range(k):
        live = jnp.where(taken == 0, vals, -jnp.inf)
        top = jnp.max(live, axis=0, keepdims=True)
        cand = jnp.where((live == top) & (taken == 0), idx, n)
        first = jnp.min(cand, axis=0, keepdims=True)
        taken = taken + (idx == first).astype(jnp.int32)
        firsts.append(first)
    return firsts, taken


def _outproj_kernel(a_ref, r_ref, x_ref, wa_ref, wr_ref, g1_ref, sh2_ref, sc2_ref, lg_ref, lb_ref,
                    wrt_ref, rb_ref, x1_ref, ha_ref, hb_ref, ek_ref, rk_ref, wk_ref, cnt_ref, carry_ref):
    y = _dot(a_ref[0], wa_ref[...]) + _dot(r_ref[0], wr_ref[...])
    x1 = _ln_rows(DEEPNORM_ALPHA * x_ref[0] + g1_ref[0] * y) * lg_ref[...] + lb_ref[...]
    x1_ref[0] = x1
    h = _ln_rows(x1) * (1.0 + sc2_ref[0]) + sh2_ref[0]
    for ref, part in zip((ha_ref, hb_ref), _pack_rows(h)):
        ref[0] = part
    s = _sigmoid(_dot_nt(wrt_ref[...], h, HIGHEST))
    sb = s + rb_ref[...]
    tm = s.shape[1]
    per = N_EXPERTS // N_GROUPS
    sub = lax.broadcasted_iota(jnp.int32, (per, tm), 0)
    gs_rows = []
    for g in range(N_GROUPS):
        blk = sb[g * per:(g + 1) * per, :]
        m1 = jnp.max(blk, axis=0, keepdims=True)
        first = jnp.min(jnp.where(blk == m1, sub, per), axis=0, keepdims=True)
        m2 = jnp.max(jnp.where(sub == first, -jnp.inf, blk), axis=0, keepdims=True)
        gs_rows.append(m1 + m2)
    gscore = jnp.concatenate(gs_rows, axis=0)
    _, gtaken = _top_rows(gscore, TOPK_GROUPS)
    emask = jnp.concatenate([jnp.broadcast_to(gtaken[g:g + 1, :], (per, tm)) for g in range(N_GROUPS)], axis=0)
    masked = jnp.where(emask > 0, sb, -jnp.inf)
    firsts, taken = _top_rows(masked, TOP_K)
    first_step = (pl.program_id(0) == 0) & (pl.program_id(1) == 0)

    @pl.when(first_step)
    def _():
        carry_ref[...] = jnp.zeros(carry_ref.shape, F32)

    sel_f = taken.astype(F32)
    ti = lax.broadcasted_iota(jnp.int32, (tm, tm), 0)
    tj = lax.broadcasted_iota(jnp.int32, (tm, tm), 1)
    rank = _dot(sel_f.astype(BF16), (ti < tj).astype(BF16)) + carry_ref[:, 0:1]
    carry = carry_ref[...] + jnp.sum(sel_f, axis=1, keepdims=True)
    carry_ref[...] = carry
    cnt_ref[...] = carry
    eidx = lax.broadcasted_iota(jnp.int32, (N_EXPERTS, tm), 0)
    picked = []
    for k in range(TOP_K):
        hit = eidx == firsts[k]
        pick = lambda v: jnp.sum(jnp.where(hit, v, 0.0), axis=0, keepdims=True)
        ek_ref[k:k + 1, :] = firsts[k]
        rk_ref[k:k + 1, :] = pick(rank).astype(jnp.int32)
        picked.append(pick(s))
    total = picked[0]
    for k in range(1, TOP_K):
        total = total + picked[k]
    for k in range(TOP_K):
        wk_ref[k:k + 1, :] = picked[k] / total * ROUTED_SCALE


def _outproj(a_lat, r_lat, x, w_a, w_r, g1, sh2, sc2, ln_g, ln_b, w_router_t, router_bias, tm, b0):
    b, l, half = a_lat.shape
    d = x.shape[-1]
    nl = l // tm
    row = lambda i, j: (i, j, 0)
    per_b = lambda i, j: (i + b0, 0, 0)
    const = lambda i, j: (0, 0)
    return pl.pallas_call(
        _outproj_kernel,
        grid=(b, nl),
        in_specs=[pl.BlockSpec((1, tm, half), row),
                  pl.BlockSpec((1, tm, half), row),
                  pl.BlockSpec((1, tm, d), lambda i, j: (i + b0, j, 0)),
                  pl.BlockSpec((half, d), const),
                  pl.BlockSpec((half, d), const),
                  pl.BlockSpec((1, 1, d), per_b),
                  pl.BlockSpec((1, 1, d), per_b),
                  pl.BlockSpec((1, 1, d), per_b),
                  pl.BlockSpec((1, d), const),
                  pl.BlockSpec((1, d), const),
                  pl.BlockSpec((N_EXPERTS, d), const),
                  pl.BlockSpec((N_EXPERTS, 1), const)],
        out_specs=[pl.BlockSpec((1, tm, d), row),
                   pl.BlockSpec((1, tm, d // (2 * PACK_GROUPS)), row),
                   pl.BlockSpec((1, tm, d // (2 * PACK_GROUPS)), row),
                   pl.BlockSpec((TOP_K, tm), lambda i, j: (0, i * nl + j)),
                   pl.BlockSpec((TOP_K, tm), lambda i, j: (0, i * nl + j)),
                   pl.BlockSpec((TOP_K, tm), lambda i, j: (0, i * nl + j)),
                   pl.BlockSpec((N_EXPERTS, HEAD_DIM), const)],
        out_shape=[jax.ShapeDtypeStruct((b, l, d), F32),
                   jax.ShapeDtypeStruct((b, l, d // (2 * PACK_GROUPS)), jnp.int32),
                   jax.ShapeDtypeStruct((b, l, d // (2 * PACK_GROUPS)), jnp.int32),
                   jax.ShapeDtypeStruct((TOP_K, b * l), jnp.int32),
                   jax.ShapeDtypeStruct((TOP_K, b * l), jnp.int32),
                   jax.ShapeDtypeStruct((TOP_K, b * l), F32),
                   jax.ShapeDtypeStruct((N_EXPERTS, HEAD_DIM), F32)],
        scratch_shapes=[pltpu.VMEM((N_EXPERTS, HEAD_DIM), F32)],
        compiler_params=_params(("arbitrary", "arbitrary")),
        name="outproj_router",
    )(a_lat, r_lat, x, w_a, w_r, g1, sh2, sc2, ln_g.reshape(1, d), ln_b.reshape(1, d),
      w_router_t, router_bias.reshape(N_EXPERTS, 1))


def _pack_rows(x):
    n = x.shape[1] // PACK_GROUPS
    return [_pack_bf16_pair(x[:, g * n:(g + 1) * n]) for g in range(PACK_GROUPS)]


def _unpack_rows(parts):
    cols = []
    for p in parts:
        cols += list(_unpack_bf16_pair(p))
    return cols


def _glu_ffn(cols, w_gate_up, w_down, ff):
    n = cols[0].shape[1]
    ab = _dot(cols[0], w_gate_up[0:n])
    for i in range(1, len(cols)):
        ab = ab + _dot(cols[i], w_gate_up[i * n:(i + 1) * n])
    act = (_silu(ab[:, :ff]) * ab[:, ff:]).astype(BF16)
    return _dot(act, w_down)


def _expert_kernel(te_ref, used_ref, wblk_ref, *refs, ff):
    tps = EXPERT_TILES_PER_STEP
    xs_refs, w_refs = refs[:PACK_GROUPS], refs[PACK_GROUPS:PACK_GROUPS + 2 * tps]
    ys_refs = refs[PACK_GROUPS + 2 * tps:2 * PACK_GROUPS + 2 * tps]
    wbf_refs = refs[2 * PACK_GROUPS + 2 * tps:]
    i = pl.program_id(0)
    used = used_ref[0]

    def run(n_active):
        for t in range(n_active):
            tile = i * tps + t
            prev = jnp.maximum(tile - tps, 0)

            @pl.when((i == 0) | (te_ref[tile] != te_ref[prev]))
            def _(t=t, tile=tile):
                def cast_from(s):
                    wbf_refs[2 * t][...] = w_refs[2 * s][0].astype(BF16)
                    wbf_refs[2 * t + 1][...] = w_refs[2 * s + 1][0].astype(BF16)

                if t == 0:
                    cast_from(0)
                else:
                    shared = te_ref[tile] == te_ref[i * tps]
                    pl.when(shared)(functools.partial(cast_from, 0))
                    pl.when(jnp.logical_not(shared))(functools.partial(cast_from, t))

        sub = EXPERT_TILE // EXPERT_SUBTILES
        blocks = [(t, slice(t * EXPERT_TILE + s * sub, t * EXPERT_TILE + (s + 1) * sub))
                  for t in range(n_active) for s in range(EXPERT_SUBTILES)]
        wgu = [wbf_refs[2 * t][...] for t in range(n_active)]
        wd = [wbf_refs[2 * t + 1][...] for t in range(n_active)]
        cols = [_unpack_rows([r[b, :] for r in xs_refs]) for t, b in blocks]
        n = cols[0][0].shape[1]
        ab = [_dot(c[0], wgu[t][0:n]) for c, (t, b) in zip(cols, blocks)]
        for j in range(1, 2 * PACK_GROUPS):
            ab = [acc + _dot(c[j], wgu[t][j * n:(j + 1) * n]) for acc, c, (t, b) in zip(ab, cols, blocks)]
        act = [(_silu(a[:, :ff]) * a[:, ff:]).astype(BF16) for a in ab]
        y = [_dot(a, wd[t]) for a, (t, b) in zip(act, blocks)]
        for (t, b), y_b in zip(blocks, y):
            for ref, part in zip(ys_refs, _pack_rows(y_b)):
                ref[b, :] = part

    for n_active in range(1, tps + 1):
        full = (i * tps + n_active <= used) if n_active == tps else (i * tps + n_active == used)
        pl.when(full)(functools.partial(run, n_active))


def _expert_ffn(xs, tile_expert, used_tiles, w_gu, w_d):
    n_rows, dp = xs[0].shape
    n_e, d, ff2 = w_gu.shape
    ff = ff2 // 2
    tps = EXPERT_TILES_PER_STEP
    step_rows = tps * EXPERT_TILE
    assert n_rows % step_rows == 0
    n_steps = n_rows // step_rows
    te2 = tile_expert.reshape(n_steps, tps)
    own = (te2 != te2[:, :1]) | (jnp.arange(tps)[None, :] == 0)
    last = jax.lax.cummax(jnp.where(own, jnp.arange(n_steps, dtype=jnp.int32)[:, None], -1), axis=0)
    w_block = jnp.where(last >= 0, jnp.take_along_axis(te2, jnp.maximum(last, 0), axis=0), te2[:1]).reshape(-1)
    rows_spec = pl.BlockSpec((step_rows, dp), lambda i, te, used, wb: (jnp.minimum(i, (used[0] - 1) // tps), 0))
    w_specs = []
    for t in range(tps):
        w_specs += [pl.BlockSpec((1, d, ff2), lambda i, te, used, wb, t=t: (wb[i * tps + t], 0, 0)),
                    pl.BlockSpec((1, ff, d), lambda i, te, used, wb, t=t: (wb[i * tps + t], 0, 0))]
    grid_spec = pltpu.PrefetchScalarGridSpec(
        num_scalar_prefetch=3,
        grid=(n_rows // step_rows,),
        in_specs=[rows_spec] * PACK_GROUPS + w_specs,
        out_specs=[rows_spec] * PACK_GROUPS,
        scratch_shapes=[pltpu.VMEM((d, ff2), BF16), pltpu.VMEM((ff, d), BF16)] * tps)
    return pl.pallas_call(
        functools.partial(_expert_kernel, ff=ff),
        grid_spec=grid_spec,
        out_shape=[jax.ShapeDtypeStruct((n_rows, dp), jnp.int32)] * PACK_GROUPS,
        compiler_params=_params(("arbitrary",)),
        name="expert_ffn",
    )(tile_expert, used_tiles, w_block.astype(jnp.int32), *xs, *([w_gu, w_d] * tps))


def _combine_kernel(*refs, ff):
    h_refs, y_refs = refs[:PACK_GROUPS], refs[PACK_GROUPS:2 * PACK_GROUPS]
    x1_ref, wk_ref, wsg_ref, wsd_ref, g2_ref, lg_ref, lb_ref, o_ref = refs[2 * PACK_GROUPS:]
    shared = _glu_ffn(_unpack_rows([r[...] for r in h_refs]), wsg_ref[...], wsd_ref[...], ff)
    wk = wk_ref[...]
    n = y_refs[0].shape[2]
    blocks = [shared[:, i * n:(i + 1) * n] for i in range(2 * PACK_GROUPS)]
    for k in range(TOP_K):
        cols = _unpack_rows([r[k] for r in y_refs])
        blocks = [acc + wk[:, k:k + 1] * c.astype(F32) for acc, c in zip(blocks, cols)]
    f = jnp.concatenate(blocks, axis=1)
    o_ref[...] = _ln_rows(DEEPNORM_ALPHA * x1_ref[...] + g2_ref[0] * f) * lg_ref[...] + lb_ref[...]


def _moe_combine(h, x1, y_sel, wk_tok, w_sg, w_sd, g2, ln_g, ln_b, tm, b0, l, prev_out):
    t_part, d = x1.shape
    t = g2.shape[0] * l
    dp = h[0].shape[1]
    tiles_per_batch = l // tm
    off = b0 * tiles_per_batch
    ff = w_sd.shape[0]
    row = lambda i: (i, 0)
    const = lambda i: (0, 0)
    in_specs = ([pl.BlockSpec((tm, dp), row)] * PACK_GROUPS
                + [pl.BlockSpec((TOP_K, tm, dp), lambda i: (0, i, 0))] * PACK_GROUPS
                + [pl.BlockSpec((tm, d), row),
                   pl.BlockSpec((tm, TOP_K), row),
                   pl.BlockSpec(w_sg.shape, const),
                   pl.BlockSpec(w_sd.shape, const),
                   pl.BlockSpec((1, 1, d), lambda i: (i // tiles_per_batch + b0, 0, 0)),
                   pl.BlockSpec((1, d), const),
                   pl.BlockSpec((1, d), const)])
    args = [*h, *y_sel, x1, wk_tok, w_sg, w_sd, g2, ln_g.reshape(1, d), ln_b.reshape(1, d)]
    kern = functools.partial(_combine_kernel, ff=ff)
    aliases = {}
    if prev_out is not None:
        in_specs.append(pl.BlockSpec(memory_space=pl.ANY))
        aliases = {len(args): 0}
        args.append(prev_out)
        kern = lambda *refs: _combine_kernel(*refs[:-2], refs[-1], ff=ff)
    return pl.pallas_call(
        kern,
        grid=(t_part // tm,),
        in_specs=in_specs,
        out_specs=pl.BlockSpec((tm, d), lambda i: (i + off, 0)),
        out_shape=jax.ShapeDtypeStruct((t, d), F32),
        input_output_aliases=aliases,
        compiler_params=_params(("arbitrary",)),
        name="moe_combine",
    )(*args)


def _routing_tables(ek, rk, counts, n_rows):
    padded = (counts + EXPERT_TILE - 1) // EXPERT_TILE * EXPERT_TILE
    ends = jnp.cumsum(padded)
    starts = ends - padded
    experts = jnp.arange(N_EXPERTS, dtype=jnp.int32)
    start_of = jnp.sum(jnp.where(ek[None] == experts[:, None, None], starts[:, None, None], 0), axis=0)
    pos = start_of + rk
    tile_start = jnp.arange(n_rows // EXPERT_TILE, dtype=jnp.int32) * EXPERT_TILE
    tile_expert = jnp.minimum(jnp.sum(ends[None, :] <= tile_start[:, None], axis=1), N_EXPERTS - 1).astype(jnp.int32)
    used_tiles = (ends[-1:] // EXPERT_TILE).astype(jnp.int32)
    return pos.astype(jnp.int32), tile_expert, used_tiles


def _sc_mesh():
    return plsc.VectorSubcoreMesh(core_axis_name="core", subcore_axis_name="subcore")


def _dispatch_rows(rows, pos, n_rows):
    t, dp = rows.shape
    n_k = pos.shape[0]

    @functools.partial(pl.kernel, mesh=_sc_mesh(), out_type=jax.ShapeDtypeStruct((n_rows, dp), rows.dtype),
                       scratch_types=[])
    def scatter(rows_hbm, pos_hbm, out_hbm):
        def body(rows_vmem, idx_vmem):
            pltpu.sync_copy(rows_vmem, out_hbm.at[idx_vmem.at[0]])

        pltpu.emit_pipeline(
            body,
            grid=(t // SC_WINDOW, n_k),
            in_specs=[pl.BlockSpec((SC_WINDOW, dp), lambda i, k: (i, 0)),
                      pl.BlockSpec((1, SC_WINDOW), lambda i, k: (k, i))],
            out_specs=[],
            core_axis_name=("core", "subcore"),
            dimension_semantics=(pltpu.PARALLEL, pltpu.ARBITRARY),
        )(rows_hbm, pos_hbm)

    return scatter(rows, pos)


def _gather_rows(table, pos):
    n_k, t = pos.shape
    dp = table.shape[1]

    @functools.partial(pl.kernel, mesh=_sc_mesh(), out_type=jax.ShapeDtypeStruct((n_k * t, dp), table.dtype),
                       scratch_types=[])
    def gather(table_hbm, pos_hbm, out_hbm):
        def body(idx_vmem, out_vmem):
            pltpu.sync_copy(table_hbm.at[idx_vmem.at[0]], out_vmem)

        pltpu.emit_pipeline(
            body,
            grid=(n_k * t // SC_WINDOW,),
            in_specs=[pl.BlockSpec((1, SC_WINDOW), lambda i: (0, i))],
            out_specs=[pl.BlockSpec((SC_WINDOW, dp), lambda i: (i, 0))],
            core_axis_name=("core", "subcore"),
            dimension_semantics=(pltpu.PARALLEL,),
        )(pos_hbm, out_hbm)

    return gather(table, pos.reshape(1, n_k * t)).reshape(n_k, t, dp)


def _routed_moe(h, x1, ek, rk, wk, cnt, w_sg, w_sd, w_gu, w_d, g2, ln_g, ln_b, b0, l, prev_out):
    t = x1.shape[0]
    n_rows = t * TOP_K + N_EXPERTS * EXPERT_TILE
    pos, tile_expert, used_tiles = _routing_tables(ek, rk, cnt[:, 0].astype(jnp.int32), n_rows)
    xs = [_dispatch_rows(rows, pos, n_rows) for rows in h]
    ys = _expert_ffn(xs, tile_expert, used_tiles, w_gu, w_d)
    y_sel = [_gather_rows(table, pos) for table in ys]
    return _moe_combine(h, x1, y_sel, wk.T, w_sg, w_sd, g2, ln_g, ln_b, ROW_TILE, b0, l, prev_out)


def _gate_layouts(g_lat, g_ctx):
    g = jnp.concatenate([g_ctx, g_lat], axis=1)[..., :4 * GDN_HEADS]
    b, lt, _ = g.shape
    nt = lt // GDN_CHUNK
    rows = jnp.transpose(g.reshape(b, nt, GDN_CHUNK, 4, GDN_HEADS), (0, 4, 1, 3, 2))
    return jnp.pad(rows, ((0, 0), (0, 0), (0, 0), (0, 4), (0, 0))).reshape(b, GDN_HEADS, nt * 8, GDN_CHUNK)


def kernel(x, c, ctx, c_ctx, w_mod, b_mod, w_in, conv_w, gdn_a_log, gdn_dt_bias, gdn_norm_g, ret_log_gamma,
           ret_norm_g, w_out, ln1_g, ln1_b, w_router, router_bias, w_gate_up, w_down, w_shared_gate_up,
           w_shared_down, ln2_g, ln2_b):
    b, l, d = x.shape
    lc = ctx.shape[1]
    gw = GDN_HEADS * HEAD_DIM
    assert d == 2 * gw and w_in.shape[-1] == 8 * gw + 4 * GDN_HEADS and w_gate_up.shape[1] == N_EXPERTS
    assert b % BATCH_PARTS == 0 and l % ROW_TILE == 0 and l % RET_CHUNK == 0 and l % GRID_W == 0
    assert lc % RET_CHUNK == 0 and lc % HALO == 0 and (b // BATCH_PARTS * l) % EXPERT_TILE == 0

    n_mod = -(-(b + 1) // 8) * 8
    cc = jnp.zeros((n_mod, d), F32).at[:b].set(c).at[b].set(c_ctx)
    mod = _modulation(cc, w_mod[0], b_mod[0])
    sh1, sc1, g1, sh2, sc2, g2 = [mod[:b, i * d:(i + 1) * d].reshape(b, 1, d) for i in range(6)]
    csh1 = jnp.broadcast_to(mod[b, 0:d].reshape(1, 1, d), (b, 1, d))
    csc1 = jnp.broadcast_to(mod[b, d:2 * d].reshape(1, 1, d), (b, 1, d))

    w = w_in[0]
    o_gate = 3 * gw + gw
    o_ret = o_gate + 4 * GDN_HEADS
    w_main = jnp.concatenate([w[:, :o_gate], w[:, o_ret:]], axis=1).astype(BF16)
    w_gate = w[:, o_gate:o_ret]
    cos_t, sin_t = _rope_tables(l)
    wo = w_out[0].astype(BF16)
    w_sg, w_sd = w_shared_gate_up[0].astype(BF16), w_shared_down[0].astype(BF16)
    w_router_t = w_router[0].T

    nb = b // BATCH_PARTS
    out = None
    for part in range(BATCH_PARTS):
        b0 = part * nb
        qkv_lat, z_lat, ret_lat, g_lat = _inproj(x, sh1, sc1, w_main, w_gate, conv_w[0], PROJ_TILE, b0, nb)
        qkv_ctx, _, ret_ctx, g_ctx = _inproj(ctx, csh1, csc1, w_main, w_gate, conv_w[0], lc, b0, nb)
        gate_rows = _gate_layouts(g_lat, g_ctx)
        a_lat = _gdn_mixer(qkv_lat, qkv_ctx, z_lat, gate_rows, gdn_a_log[0],
                           gdn_dt_bias[0], gdn_norm_g[0], GDN_HEADS_PER_STEP)
        r_lat = _ret_mixer(ret_lat, ret_ctx, ret_log_gamma[0], ret_norm_g[0], cos_t, sin_t)
        x1, h_a, h_b, ek, rk, wk, cnt = _outproj(a_lat, r_lat, x, wo[:gw], wo[gw:], g1, sh2, sc2, ln1_g[0], ln1_b[0],
                                                 w_router_t, router_bias[0], ROW_TILE, b0)
        h = [arr.reshape(nb * l, arr.shape[-1]) for arr in (h_a, h_b)]
        out = _routed_moe(h, x1.reshape(nb * l, d), ek, rk, wk, cnt, w_sg, w_sd, w_gate_up[0], w_down[0],
                          g2, ln2_g[0], ln2_b[0], b0, l, out)
    return out.reshape(b, l, d)
```
